```python
import jax, jax.numpy as jnp
from jax import lax
import numpy as np

D_MODEL = 2048
BATCH = 2
SEQ = 4096
DEPTH = 1

CHUNK = 64
QBLOCK = 128
D_PLE = 256
HEAD_DIM = 128
N_HEADS_FOX = 8
N_HEADS_SB = 8
W_FOX = N_HEADS_FOX * HEAD_DIM
W_SB = N_HEADS_SB * HEAD_DIM
N_GROUPS = 4
EXPERTS_PER_GROUP = 8
N_EXPERTS = N_GROUPS * EXPERTS_PER_GROUP
TOP_K_IN_GROUP = 2
D_EXPERT = 512
MOE_BLOCK = 128
EPS = 1e-6
D_IN = 3 * W_FOX + N_HEADS_FOX + 3 * W_SB + 2 * D_MODEL

kernel_name = 'hybrid_fox_stickbreaking_hiermoe_block'


def rmsnorm(x, g):
    xf = x.astype(jnp.float32)
    y = xf * lax.rsqrt(jnp.mean(xf * xf, axis=-1, keepdims=True) + EPS)
    return (y * g.astype(jnp.float32)).astype(x.dtype)


def to_blocks(t):
    b, s, h, d = t.shape
    return t.reshape(b, s // QBLOCK, QBLOCK, h, d).transpose(1, 0, 3, 2, 4)


def from_blocks(o):
    nb, b, h, qb, d = o.shape
    return o.transpose(1, 0, 3, 2, 4).reshape(b, nb * qb, h * d)


def forgetting_attention(q, k, v, log_f):
    b, s, h, d = q.shape
    nb = s // QBLOCK
    c = jnp.cumsum(log_f, axis=1).transpose(0, 2, 1)
    c_blocks = c.reshape(b, h, nb, QBLOCK).transpose(2, 0, 1, 3)
    kh = k.transpose(0, 2, 1, 3)
    vh = v.transpose(0, 2, 1, 3)
    key_pos = jnp.arange(s)
    scale = d ** -0.5

    def block(args):
        qb, cqb, i = args
        logits = jnp.einsum('bhqd,bhkd->bhqk', qb, kh).astype(jnp.float32) * scale
        logits = logits + cqb[..., :, None] - c[:, :, None, :]
        qpos = i * QBLOCK + jnp.arange(QBLOCK)
        mask = key_pos[None, :] <= qpos[:, None]
        probs = jax.nn.softmax(jnp.where(mask, logits, -jnp.inf), axis=-1)
        return jnp.einsum('bhqk,bhkd->bhqd', probs.astype(vh.dtype), vh)

    o = lax.map(block, (to_blocks(q), c_blocks, jnp.arange(nb)))
    return from_blocks(o)


def stick_breaking_attention(q, k, v):
    b, s, h, d = q.shape
    nb = s // QBLOCK
    kh = k.transpose(0, 2, 1, 3)
    vh = v.transpose(0, 2, 1, 3)
    key_pos = jnp.arange(s)
    scale = d ** -0.5

    def block(args):
        qb, i = args
        z = jnp.einsum('bhqd,bhkd->bhqk', qb, kh).astype(jnp.float32) * scale
        qpos = i * QBLOCK + jnp.arange(QBLOCK)
        mask = key_pos[None, :] < qpos[:, None]
        neg = jnp.where(mask, jax.nn.log_sigmoid(-z), 0.0)
        after = lax.cumsum(neg, axis=3, reverse=True) - neg
        a = jnp.where(mask, jnp.exp(jax.nn.log_sigmoid(z) + after), 0.0)
        return jnp.einsum('bhqk,bhkd->bhqd', a.astype(vh.dtype), vh)

    o = lax.map(block, (to_blocks(q), jnp.arange(nb)))
    return from_blocks(o)


def hierarchical_moe(h, w_group, w_expert, w_gate, w_up, w_down):
    b, s, d = h.shape
    t = b * s
    xt = h.reshape(t, d)
    g_logits = (xt @ w_group).astype(jnp.float32)
    g_prob = jax.nn.softmax(g_logits, axis=-1)
    g_sel = jnp.argmax(g_logits, axis=-1).astype(jnp.int32)
    p_group = jnp.take_along_axis(g_prob, g_sel[:, None], axis=1)
    e_logits = (xt @ w_expert).astype(jnp.float32).reshape(t, N_GROUPS, EXPERTS_PER_GROUP)
    e_logits = jnp.take_along_axis(e_logits, g_sel[:, None, None], axis=1)[:, 0]
    top_val, top_idx = lax.top_k(e_logits, TOP_K_IN_GROUP)
    gate = jax.nn.softmax(top_val, axis=-1) * p_group
    expert_id = g_sel[:, None] * EXPERTS_PER_GROUP + top_idx.astype(jnp.int32)

    m = t * TOP_K_IN_GROUP
    flat_e = expert_id.reshape(m)
    flat_tok = jnp.repeat(jnp.arange(t, dtype=jnp.int32), TOP_K_IN_GROUP)
    flat_w = gate.reshape(m)
    order = jnp.argsort(flat_e)
    se = flat_e[order]
    stok = flat_tok[order]
    sw = flat_w[order]
    counts = jnp.bincount(flat_e, length=N_EXPERTS)
    padded = (counts + MOE_BLOCK - 1) // MOE_BLOCK * MOE_BLOCK
    pad_end = jnp.cumsum(padded)
    pad_start = pad_end - padded
    start = jnp.cumsum(counts) - counts
    dest = pad_start[se] + jnp.arange(m) - start[se]
    n_blocks = m // MOE_BLOCK + N_EXPERTS
    p_rows = n_blocks * MOE_BLOCK
    row_tok = jnp.full((p_rows,), t, dtype=jnp.int32).at[dest].set(stok)
    row_w = jnp.zeros((p_rows,), dtype=gate.dtype).at[dest].set(sw)
    block_expert = jnp.minimum(
        jnp.searchsorted(pad_end, jnp.arange(n_blocks) * MOE_BLOCK, side='right'),
        N_EXPERTS - 1).astype(jnp.int32)
    x_pad = jnp.concatenate([xt, jnp.zeros((1, d), xt.dtype)], axis=0)
    xb = x_pad[row_tok].reshape(n_blocks, MOE_BLOCK, d)

    def expert_block(args):
        xe, e = args
        hid = jax.nn.silu(xe @ w_gate[e]) * (xe @ w_up[e])
        return hid @ w_down[e]

    yb = lax.map(expert_block, (xb, block_expert))
    y = jax.ops.segment_sum(yb.reshape(p_rows, d) * row_w[:, None].astype(yb.dtype),
                            row_tok, num_segments=t + 1)[:t]
    return y.reshape(b, s, d)


def setup_inputs(seed: int = 0) -> dict:
    key = jax.random.key(seed)
    ks = jax.random.split(key, 20)
    f32 = jnp.float32

    def nrm(k, shape, fan_in):
        return jax.random.normal(k, shape, f32) * (fan_in ** -0.5)

    def gain(k, shape):
        return 1.0 + 0.02 * jax.random.normal(k, shape, f32)

    return {
        'x': jax.random.normal(ks[0], (BATCH, SEQ, D_MODEL), f32),
        'p': jax.random.normal(ks[1], (DEPTH, BATCH, SEQ, D_PLE), f32),
        'w_in': nrm(ks[2], (DEPTH, D_MODEL, D_IN), D_MODEL),
        'b_forget': 1.0 + 3.0 * jax.random.uniform(ks[3], (DEPTH, N_HEADS_FOX), f32),
        'w_branch_fox': nrm(ks[4], (DEPTH, W_FOX, D_MODEL), W_FOX),
        'w_branch_sb': nrm(ks[5], (DEPTH, W_SB, D_MODEL), W_SB),
        'w_mix_out': nrm(ks[6], (DEPTH, D_MODEL, D_MODEL), D_MODEL),
        'g_mix': gain(ks[7], (DEPTH, D_MODEL)),
        'g_ffn': gain(ks[8], (DEPTH, D_MODEL)),
        'w_group': nrm(ks[9], (DEPTH, D_MODEL, N_GROUPS), D_MODEL),
        'w_expert': nrm(ks[10], (DEPTH, D_MODEL, N_EXPERTS), D_MODEL),
        'w_gate': nrm(ks[11], (DEPTH, N_EXPERTS, D_MODEL, D_EXPERT), D_MODEL),
        'w_up': nrm(ks[12], (DEPTH, N_EXPERTS, D_MODEL, D_EXPERT), D_MODEL),
        'w_down': nrm(ks[13], (DEPTH, N_EXPERTS, D_EXPERT, D_MODEL), D_EXPERT),
        'g_ple': gain(ks[14], (DEPTH, D_MODEL)),
        'w_ple_proj': nrm(ks[15], (DEPTH, D_PLE, D_MODEL), D_PLE),
        'w_ple_gate': nrm(ks[16], (DEPTH, D_MODEL, D_MODEL), D_MODEL),
        'g_final': gain(ks[17], (D_MODEL,)),
    }


def reference(x, p, w_in, b_forget, w_branch_fox, w_branch_sb, w_mix_out, g_mix, g_ffn,
              w_group, w_expert, w_gate, w_up, w_down, g_ple, w_ple_proj, w_ple_gate, g_final):
    b, s, _ = x.shape
    sizes = [W_FOX, W_FOX, W_FOX, N_HEADS_FOX, W_SB, W_SB, W_SB, D_MODEL]
    splits = [int(v) for v in np.cumsum(sizes)]
    for i in range(DEPTH):
        h = rmsnorm(x, g_mix[i])
        proj = h @ w_in[i]
        q_a, k_a, v_a, f_a, q_b, k_b, v_b, gate_a, gate_b = jnp.split(proj, splits, axis=-1)
        heads_a = (b, s, N_HEADS_FOX, HEAD_DIM)
        heads_b = (b, s, N_HEADS_SB, HEAD_DIM)
        log_f = jax.nn.log_sigmoid(f_a.astype(jnp.float32) + b_forget[i].astype(jnp.float32))
        o_a = forgetting_attention(q_a.reshape(heads_a), k_a.reshape(heads_a),
                                   v_a.reshape(heads_a), log_f)
        o_b = stick_breaking_attention(q_b.reshape(heads_b), k_b.reshape(heads_b),
                                       v_b.reshape(heads_b))
        mixed = (jax.nn.sigmoid(gate_a) * (o_a @ w_branch_fox[i])
                 + jax.nn.sigmoid(gate_b) * (o_b @ w_branch_sb[i]))
        x = x + mixed @ w_mix_out[i]
        x = x + hierarchical_moe(rmsnorm(x, g_ffn[i]), w_group[i], w_expert[i],
                                 w_gate[i], w_up[i], w_down[i])
        ple = p[i] @ w_ple_proj[i]
        x = x + jax.nn.sigmoid(rmsnorm(x, g_ple[i]) @ w_ple_gate[i]) * ple
    return rmsnorm(x, g_final)
```

```python
import functools

import jax
import jax.numpy as jnp
from jax import lax
from jax.experimental import pallas as pl
from jax.experimental.pallas import tpu as pltpu

F32 = jnp.float32
BF16 = jnp.bfloat16

HEAD_DIM = 128
N_GROUPS = 4
EXPERTS_PER_GROUP = 8
EPS = 1e-6
LANES = 128
VMEM_LIMIT = 56 * 1024 * 1024

NT_DIMS = (((1,), (1,)), ((), ()))


def _params(sem):
    return pltpu.CompilerParams(dimension_semantics=sem, vmem_limit_bytes=VMEM_LIMIT)


def _rmsnorm_f32(x, g):
    ms = jnp.mean(x * x, axis=-1, keepdims=True)
    return x * lax.rsqrt(ms + EPS) * g


def _pack_bf16_pairs(x):
    n = x.shape[1] // 2
    bits = lax.bitcast_convert_type(x.astype(BF16).astype(F32), jnp.uint32)
    return (bits[:, :n] >> 16) | (bits[:, n:] & jnp.uint32(0xFFFF0000))


def _unpack_bf16_pairs(w):
    lo = lax.bitcast_convert_type(w << 16, F32)
    hi = lax.bitcast_convert_type(w & jnp.uint32(0xFFFF0000), F32)
    return jnp.concatenate([lo, hi], axis=1)


def _inproj_kernel(x_ref, g_ref, w_ref, wf_ref, o_ref, f_ref, h_scr):
    @pl.when(pl.program_id(1) == 0)
    def _():
        hb = _rmsnorm_f32(x_ref[...], g_ref[...]).astype(BF16)
        h_scr[...] = hb
        f_ref[...] = lax.dot_general(wf_ref[...], hb, NT_DIMS, preferred_element_type=F32)

    o_ref[...] = jnp.dot(h_scr[...], w_ref[...], preferred_element_type=F32).astype(o_ref.dtype)


def _inproj(x2d, g, w, wf_t, tm, tn):
    t, d = x2d.shape
    n = w.shape[1]
    nh = wf_t.shape[0]
    return pl.pallas_call(
        _inproj_kernel,
        out_shape=(jax.ShapeDtypeStruct((t, n), BF16), jax.ShapeDtypeStruct((nh, t), F32)),
        grid=(t // tm, n // tn),
        in_specs=[
            pl.BlockSpec((tm, d), lambda m, j: (m, 0)),
            pl.BlockSpec((1, d), lambda m, j: (0, 0)),
            pl.BlockSpec((d, tn), lambda m, j: (0, j)),
            pl.BlockSpec((nh, d), lambda m, j: (0, 0)),
        ],
        out_specs=(
            pl.BlockSpec((tm, tn), lambda m, j: (m, j)),
            pl.BlockSpec((nh, tm), lambda m, j: (0, m)),
        ),
        scratch_shapes=[pltpu.VMEM((tm, d), BF16)],
        compiler_params=_params(("parallel", "arbitrary")),
        name="inproj",
    )(x2d, g, w, wf_t)


def _split3(x):
    p1 = x.astype(BF16)
    r1 = x - p1.astype(F32)
    p2 = r1.astype(BF16)
    p3 = (r1 - p2.astype(F32)).astype(BF16)
    return p1, p2, p3


def _cumsum_kernel(f_ref, b_ref, u_ref, c_ref, *, chunk):
    nh, s = f_ref.shape
    u = u_ref[...]
    carry = jnp.zeros((nh, 1), F32)
    for i in range(s // chunk):
        x = f_ref[:, i * chunk:(i + 1) * chunk] + b_ref[...]
        lf = jax.nn.log_sigmoid(x)
        p1, p2, p3 = _split3(lf)
        cs = (jnp.dot(p1, u, preferred_element_type=F32)
              + jnp.dot(p2, u, preferred_element_type=F32)
              + jnp.dot(p3, u, preferred_element_type=F32))
        c_ref[:, i * chunk:(i + 1) * chunk] = cs + carry
        carry = carry + cs[:, chunk - 1:chunk]


def _forget_cumsum(f_t, b_col, batch):
    nh, t = f_t.shape
    s = t // batch
    chunk = min(256, s)
    u = jnp.triu(jnp.ones((chunk, chunk), BF16))
    return pl.pallas_call(
        functools.partial(_cumsum_kernel, chunk=chunk),
        out_shape=jax.ShapeDtypeStruct((nh, t), F32),
        grid=(batch,),
        in_specs=[
            pl.BlockSpec((nh, s), lambda b: (0, b)),
            pl.BlockSpec((nh, 1), lambda b: (0, 0)),
            pl.BlockSpec((chunk, chunk), lambda b: (0, 0)),
        ],
        out_specs=pl.BlockSpec((nh, s), lambda b: (0, b)),
        compiler_params=_params(("parallel",)),
        name="forget_cumsum",
    )(f_t, b_col, u)


def _fox_kernel(q_ref, k_ref, v_ref, c_ref, o_ref, m_scr, l_scr, acc_scr, *, tq, tk, scale):
    h = pl.program_id(1)
    i = pl.program_id(2)
    q = q_ref[...]
    m_scr[...] = jnp.full(m_scr.shape, -jnp.inf, F32)
    l_scr[...] = jnp.zeros(l_scr.shape, F32)
    acc_scr[...] = jnp.zeros(acc_scr.shape, F32)

    def tile(j, masked):
        start = pl.multiple_of(j * tk, tk)
        kj = k_ref[pl.ds(start, tk), :]
        vj = v_ref[pl.ds(start, tk), :]
        s = lax.dot_general(q, kj, NT_DIMS, preferred_element_type=F32) * scale
        s = s - c_ref[pl.ds(h, 1), pl.ds(start, tk)]
        if masked:
            rows = i * tq + lax.broadcasted_iota(jnp.int32, (tq, tk), 0)
            cols = j * tk + lax.broadcasted_iota(jnp.int32, (tq, tk), 1)
            s = jnp.where(cols <= rows, s, -jnp.inf)
        m_prev = m_scr[...]
        m_new = jnp.maximum(m_prev, jnp.max(s, axis=-1, keepdims=True))
        alpha = jnp.exp(m_prev - m_new)
        p = jnp.exp(s - m_new)
        l_scr[...] = alpha * l_scr[...] + jnp.sum(p, axis=-1, keepdims=True)
        acc_scr[...] = alpha * acc_scr[...] + jnp.dot(p.astype(BF16), vj, preferred_element_type=F32)
        m_scr[...] = m_new

    ratio = tq // tk
    n_full = i * ratio

    def body(j, carry):
        tile(j, False)
        return carry

    lax.fori_loop(0, n_full, body, 0)
    for d in range(ratio):
        tile(n_full + d, True)
    o_ref[...] = (acc_scr[...] / l_scr[...]).astype(o_ref.dtype)


def _fox_attention(proj, c, batch, seq, n_heads, col_q, col_k, col_v, tq, tk):
    t = proj.shape[0]
    nq = seq // tq
    scale = HEAD_DIM ** -0.5
    cq, ck, cv = col_q // HEAD_DIM, col_k // HEAD_DIM, col_v // HEAD_DIM
    return pl.pallas_call(
        functools.partial(_fox_kernel, tq=tq, tk=tk, scale=scale),
        out_shape=jax.ShapeDtypeStruct((t, n_heads * HEAD_DIM), BF16),
        grid=(batch, n_heads, nq),
        in_specs=[
            pl.BlockSpec((tq, HEAD_DIM), lambda b, h, i: (b * nq + i, cq + h)),
            pl.BlockSpec((seq, HEAD_DIM), lambda b, h, i: (b, ck + h)),
            pl.BlockSpec((seq, HEAD_DIM), lambda b, h, i: (b, cv + h)),
            pl.BlockSpec((c.shape[0], seq), lambda b, h, i: (0, b)),
        ],
        out_specs=pl.BlockSpec((tq, HEAD_DIM), lambda b, h, i: (b * nq + i, h)),
        scratch_shapes=[
            pltpu.VMEM((tq, 1), F32),
            pltpu.VMEM((tq, 1), F32),
            pltpu.VMEM((tq, HEAD_DIM), F32),
        ],
        compiler_params=_params(("parallel", "parallel", "arbitrary")),
        name="fox_attention",
    )(proj, proj, proj, c)


def _sb_kernel(q_ref, k_ref, v_ref, tri_ref, o_ref, r_scr, acc_scr, *, tq, tk, scale):
    i = pl.program_id(2)
    q = q_ref[...]
    r_scr[...] = jnp.zeros(r_scr.shape, F32)
    acc_scr[...] = jnp.zeros(acc_scr.shape, F32)

    def tile(j, masked):
        start = pl.multiple_of(j * tk, tk)
        kj = k_ref[pl.ds(start, tk), :]
        vj = v_ref[pl.ds(start, tk), :]
        z = lax.dot_general(q, kj, NT_DIMS, preferred_element_type=F32) * scale
        neg = -(jnp.maximum(z, 0.0) + jnp.log1p(jnp.exp(-jnp.abs(z))))
        if masked:
            rows = i * tq + lax.broadcasted_iota(jnp.int32, (tq, tk), 0)
            cols = j * tk + lax.broadcasted_iota(jnp.int32, (tq, tk), 1)
            mask = cols < rows
            neg = jnp.where(mask, neg, 0.0)
        w = jnp.dot(neg.astype(BF16), tri_ref[...], preferred_element_type=F32)
        a = jnp.exp((z + neg) + (w + r_scr[...]))
        if masked:
            a = jnp.where(mask, a, 0.0)
        acc_scr[...] += jnp.dot(a.astype(BF16), vj, preferred_element_type=F32)
        r_scr[...] += jnp.sum(neg, axis=-1, keepdims=True)

    ratio = tq // tk
    n_full = i * ratio
    for d in reversed(range(ratio)):
        tile(n_full + d, True)

    def body(it, carry):
        tile(n_full - 1 - it, False)
        return carry

    lax.fori_loop(0, n_full, body, 0)
    o_ref[...] = acc_scr[...].astype(o_ref.dtype)


def _sb_attention(proj, batch, seq, n_heads, col_q, col_k, col_v, tq, tk):
    t = proj.shape[0]
    nq = seq // tq
    scale = HEAD_DIM ** -0.5
    cq, ck, cv = col_q // HEAD_DIM, col_k // HEAD_DIM, col_v // HEAD_DIM
    tri = jnp.tril(jnp.ones((tk, tk), BF16), -1)
    return pl.pallas_call(
        functools.partial(_sb_kernel, tq=tq, tk=tk, scale=scale),
        out_shape=jax.ShapeDtypeStruct((t, n_heads * HEAD_DIM), BF16),
        grid=(batch, n_heads, nq),
        in_specs=[
            pl.BlockSpec((tq, HEAD_DIM), lambda b, h, i: (b * nq + i, cq + h)),
            pl.BlockSpec((seq, HEAD_DIM), lambda b, h, i: (b, ck + h)),
            pl.BlockSpec((seq, HEAD_DIM), lambda b, h, i: (b, cv + h)),
            pl.BlockSpec((tk, tk), lambda b, h, i: (0, 0)),
        ],
        out_specs=pl.BlockSpec((tq, HEAD_DIM), lambda b, h, i: (b * nq + i, h)),
        scratch_shapes=[
            pltpu.VMEM((tq, 1), F32),
            pltpu.VMEM((tq, HEAD_DIM), F32),
        ],
        compiler_params=_params(("parallel", "parallel", "arbitrary")),
        name="sb_attention",
    )(proj, proj, proj, tri)


def _merge_kernel(oa_ref, ob_ref, wa_ref, wb_ref, ga_ref, gb_ref, o_ref):
    ya = jnp.dot(oa_ref[...], wa_ref[...], preferred_element_type=F32)
    yb = jnp.dot(ob_ref[...], wb_ref[...], preferred_element_type=F32)
    ga = jax.nn.sigmoid(ga_ref[...].astype(F32))
    gb = jax.nn.sigmoid(gb_ref[...].astype(F32))
    o_ref[...] = (ga * ya + gb * yb).astype(o_ref.dtype)


def _merge(o_a, o_b, wa, wb, proj, col_ga, col_gb, tm, tn):
    t, ka = o_a.shape
    kb = o_b.shape[1]
    d = wa.shape[1]
    ca, cb = col_ga // tn, col_gb // tn
    return pl.pallas_call(
        _merge_kernel,
        out_shape=jax.ShapeDtypeStruct((t, d), BF16),
        grid=(t // tm, d // tn),
        in_specs=[
            pl.BlockSpec((tm, ka), lambda m, j: (m, 0)),
            pl.BlockSpec((tm, kb), lambda m, j: (m, 0)),
            pl.BlockSpec((ka, tn), lambda m, j: (0, j)),
            pl.BlockSpec((kb, tn), lambda m, j: (0, j)),
            pl.BlockSpec((tm, tn), lambda m, j: (m, ca + j)),
            pl.BlockSpec((tm, tn), lambda m, j: (m, cb + j)),
        ],
        out_specs=pl.BlockSpec((tm, tn), lambda m, j: (m, j)),
        compiler_params=_params(("parallel", "arbitrary")),
        name="branch_merge",
    )(o_a, o_b, wa, wb, proj, proj)


def _mixout_kernel(a_ref, w_ref, x_ref, o_ref):
    o_ref[...] = x_ref[...] + jnp.dot(a_ref[...], w_ref[...], preferred_element_type=F32)


def _mixout(mixed, w, x2d, tm, tn):
    t, k = mixed.shape
    d = w.shape[1]
    return pl.pallas_call(
        _mixout_kernel,
        out_shape=jax.ShapeDtypeStruct((t, d), F32),
        grid=(t // tm, d // tn),
        in_specs=[
            pl.BlockSpec((tm, k), lambda m, j: (m, 0)),
            pl.BlockSpec((k, tn), lambda m, j: (0, j)),
            pl.BlockSpec((tm, tn), lambda m, j: (m, j)),
        ],
        out_specs=pl.BlockSpec((tm, tn), lambda m, j: (m, j)),
        compiler_params=_params(("parallel", "arbitrary")),
        name="mix_out",
    )(mixed, w, x2d)


def _router_kernel(x_ref, g_ref, wh_ref, wl_ref, lt_ref, h_ref, ids_ref, gate_ref, cnt_ref, carry_scr,
                   *, n_groups, per_group):
    @pl.when(pl.program_id(0) == 0)
    def _():
        carry_scr[...] = jnp.zeros(carry_scr.shape, F32)

    h = _rmsnorm_f32(x_ref[...], g_ref[...])
    hh = h.astype(BF16)
    h_ref[...] = _pack_bf16_pairs(h)
    hl = (h - hh.astype(F32)).astype(BF16)
    logits = (jnp.dot(hh, wh_ref[...], preferred_element_type=F32)
              + jnp.dot(hl, wh_ref[...], preferred_element_type=F32)
              + jnp.dot(hh, wl_ref[...], preferred_element_type=F32))
    tm = logits.shape[0]
    n_exp = n_groups * per_group
    lane = lax.broadcasted_iota(jnp.int32, (tm, LANES), 1)
    lane_f = lane.astype(F32)
    neg_inf = -jnp.inf

    def first_argmax(vals, valid):
        vmax = jnp.max(jnp.where(valid, vals, neg_inf), axis=-1, keepdims=True)
        idx = jnp.min(jnp.where(valid & (vals == vmax), lane_f, float(LANES)), axis=-1, keepdims=True)
        return vmax, idx.astype(jnp.int32)

    is_group = lane < n_groups
    gmax, gsel = first_argmax(logits, is_group)
    gsum = jnp.sum(jnp.where(is_group, jnp.exp(logits - gmax), 0.0), axis=-1, keepdims=True)
    p_group = 1.0 / gsum
    lo = n_groups + gsel * per_group
    in_group = (lane >= lo) & (lane < lo + per_group)
    v1, i1 = first_argmax(logits, in_group)
    v2, i2 = first_argmax(logits, in_group & (lane != i1))
    e2w = jnp.exp(v2 - v1)
    gate1 = (1.0 / (1.0 + e2w)) * p_group
    gate2 = (e2w / (1.0 + e2w)) * p_group
    e1 = i1 - n_groups
    e2 = i2 - n_groups

    onehot = ((lane == e1) | (lane == e2)).astype(BF16)
    before = jnp.dot(lt_ref[...], onehot, preferred_element_type=F32) + carry_scr[...]
    rank1 = jnp.sum(jnp.where(lane == e1, before, 0.0), axis=-1, keepdims=True).astype(jnp.int32)
    rank2 = jnp.sum(jnp.where(lane == e2, before, 0.0), axis=-1, keepdims=True).astype(jnp.int32)
    carry_new = carry_scr[...] + jnp.sum(onehot.astype(F32), axis=0, keepdims=True)
    carry_scr[...] = carry_new

    ids_ref[...] = jnp.where(lane == 0, e1, jnp.where(lane == 1, e2,
                             jnp.where(lane == 2, rank1, jnp.where(lane == 3, rank2, 0))))
    gate_ref[...] = jnp.where(lane == 0, gate1, jnp.where(lane == 1, gate2, 0.0))
    cnt_ref[...] = jnp.broadcast_to(carry_new, cnt_ref.shape).astype(jnp.int32)
    del n_exp


def _router(x2, g, w_hi, w_lo, tm, n_groups, per_group):
    t, d = x2.shape
    lt = jnp.tril(jnp.ones((tm, tm), BF16), -1)
    return pl.pallas_call(
        functools.partial(_router_kernel, n_groups=n_groups, per_group=per_group),
        out_shape=(
            jax.ShapeDtypeStruct((t, d // 2), jnp.uint32),
            jax.ShapeDtypeStruct((t, LANES), jnp.int32),
            jax.ShapeDtypeStruct((t, LANES), F32),
            jax.ShapeDtypeStruct((8, LANES), jnp.int32),
        ),
        grid=(t // tm,),
        in_specs=[
            pl.BlockSpec((tm, d), lambda m: (m, 0)),
            pl.BlockSpec((1, d), lambda m: (0, 0)),
            pl.BlockSpec((d, LANES), lambda m: (0, 0)),
            pl.BlockSpec((d, LANES), lambda m: (0, 0)),
            pl.BlockSpec((tm, tm), lambda m: (0, 0)),
        ],
        out_specs=(
            pl.BlockSpec((tm, d // 2), lambda m: (m, 0)),
            pl.BlockSpec((tm, LANES), lambda m: (m, 0)),
            pl.BlockSpec((tm, LANES), lambda m: (m, 0)),
            pl.BlockSpec((8, LANES), lambda m: (0, 0)),
        ),
        scratch_shapes=[pltpu.VMEM((1, LANES), F32)],
        compiler_params=_params(("arbitrary",)),
        name="router",
    )(x2, g, w_hi, w_lo, lt)


def _dispatch_kernel(d1_ref, d2_ref, h_ref, xb_in_ref, xb_ref, sem, *, tm):
    del xb_in_ref
    base = pl.program_id(0) * tm

    def row_copy(r, dest):
        return pltpu.make_async_copy(h_ref.at[pl.ds(r, 1)], xb_ref.at[pl.ds(dest, 1)], sem)

    def start(r, carry):
        row_copy(r, d1_ref[base + r]).start()
        row_copy(r, d2_ref[base + r]).start()
        return carry

    lax.fori_loop(0, tm, start, 0)

    def wait(r, carry):
        row_copy(r, d1_ref[base + r]).wait()
        row_copy(r, d2_ref[base + r]).wait()
        return carry

    lax.fori_loop(0, tm, wait, 0)


def _dispatch(h2p, dest1, dest2, p_rows, tm):
    t, dw = h2p.shape
    xb0 = jnp.zeros((p_rows, dw), h2p.dtype)
    return pl.pallas_call(
        functools.partial(_dispatch_kernel, tm=tm),
        out_shape=jax.ShapeDtypeStruct((p_rows, dw), h2p.dtype),
        grid_spec=pltpu.PrefetchScalarGridSpec(
            num_scalar_prefetch=2,
            grid=(t // tm,),
            in_specs=[
                pl.BlockSpec((tm, dw), lambda m, d1, d2: (m, 0)),
                pl.BlockSpec(memory_space=pl.ANY),
            ],
            out_specs=pl.BlockSpec(memory_space=pl.ANY),
            scratch_shapes=[pltpu.SemaphoreType.DMA],
        ),
        input_output_aliases={3: 0},
        compiler_params=_params(("arbitrary",)),
        name="moe_dispatch",
    )(dest1, dest2, h2p, xb0)


def _experts_kernel(be_ref, nu_ref, x_ref, wg_ref, wu_ref, wd_ref, y_ref, wg_scr, wu_scr, wd_scr):
    b = pl.program_id(0)
    prev = be_ref[jnp.maximum(b - 1, 0)]
    fresh = (b == 0) | (be_ref[b] != prev)

    @pl.when(fresh & (b < nu_ref[0]))
    def _():
        wg_scr[...] = wg_ref[0].astype(BF16)
        wu_scr[...] = wu_ref[0].astype(BF16)
        wd_scr[...] = wd_ref[0].astype(BF16)

    @pl.when(b < nu_ref[0])
    def _():
        x = _unpack_bf16_pairs(x_ref[...]).astype(BF16)
        g = jnp.dot(x, wg_scr[...], preferred_element_type=F32)
        u = jnp.dot(x, wu_scr[...], preferred_element_type=F32)
        hid = (g * jax.nn.sigmoid(g) * u).astype(BF16)
        y_ref[...] = _pack_bf16_pairs(jnp.dot(hid, wd_scr[...], preferred_element_type=F32))


def _experts(xb, w_gate, w_up, w_down, block_expert, n_used, bm):
    p_rows, dw = xb.shape
    d, de = w_gate.shape[1:]
    nb = p_rows // bm

    def row_map(b, be, nu):
        return (jnp.minimum(b, nu[0] - 1), 0)

    def w_map(b, be, nu):
        return (be[b], 0, 0)

    return pl.pallas_call(
        _experts_kernel,
        out_shape=jax.ShapeDtypeStruct((p_rows, dw), jnp.uint32),
        grid_spec=pltpu.PrefetchScalarGridSpec(
            num_scalar_prefetch=2,
            grid=(nb,),
            in_specs=[
                pl.BlockSpec((bm, dw), row_map),
                pl.BlockSpec((1, d, de), w_map),
                pl.BlockSpec((1, d, de), w_map),
                pl.BlockSpec((1, de, d), w_map),
            ],
            out_specs=pl.BlockSpec((bm, dw), row_map),
            scratch_shapes=[
                pltpu.VMEM((d, de), BF16),
                pltpu.VMEM((d, de), BF16),
                pltpu.VMEM((de, d), BF16),
            ],
        ),
        input_output_aliases={2: 0},
        compiler_params=_params(("arbitrary",)),
        name="moe_experts",
    )(block_expert, n_used, xb, w_gate, w_up, w_down)


def _combine_kernel(d1_ref, d2_ref, x_ref, gate_ref, yb_ref, o_ref, buf, sem, *, tm):
    base = pl.program_id(0) * tm

    def row_copy(r, dest, slot):
        return pltpu.make_async_copy(yb_ref.at[pl.ds(dest, 1)], buf.at[slot, pl.ds(r, 1)], sem)

    def start(r, carry):
        row_copy(r, d1_ref[base + r], 0).start()
        row_copy(r, d2_ref[base + r], 1).start()
        return carry

    lax.fori_loop(0, tm, start, 0)

    def wait(r, carry):
        row_copy(r, d1_ref[base + r], 0).wait()
        row_copy(r, d2_ref[base + r], 1).wait()
        return carry

    lax.fori_loop(0, tm, wait, 0)
    gates = gate_ref[...]
    y1 = _unpack_bf16_pairs(buf[0])
    y2 = _unpack_bf16_pairs(buf[1])
    o_ref[...] = x_ref[...] + gates[:, 0:1] * y1 + gates[:, 1:2] * y2


def _combine(x2, gates, yb, dest1, dest2, tm):
    t, d = x2.shape
    dw = yb.shape[1]
    return pl.pallas_call(
        functools.partial(_combine_kernel, tm=tm),
        out_shape=jax.ShapeDtypeStruct((t, d), F32),
        grid_spec=pltpu.PrefetchScalarGridSpec(
            num_scalar_prefetch=2,
            grid=(t // tm,),
            in_specs=[
                pl.BlockSpec((tm, d), lambda m, d1, d2: (m, 0)),
                pl.BlockSpec((tm, LANES), lambda m, d1, d2: (m, 0)),
                pl.BlockSpec(memory_space=pl.ANY),
            ],
            out_specs=pl.BlockSpec((tm, d), lambda m, d1, d2: (m, 0)),
            scratch_shapes=[pltpu.VMEM((2, tm, dw), yb.dtype), pltpu.SemaphoreType.DMA],
        ),
        compiler_params=_params(("arbitrary",)),
        name="moe_combine",
    )(dest1, dest2, x2, gates, yb)


def _ple_kernel(x_ref, g_ref, p_ref, wg_ref, wp_ref, gf_ref, o_ref, h_scr, p_scr, *, tn):
    j = pl.program_id(1)

    @pl.when(j == 0)
    def _():
        h_scr[...] = _rmsnorm_f32(x_ref[...], g_ref[...]).astype(BF16)
        p_scr[...] = p_ref[...].astype(BF16)

    col = pl.multiple_of(j * tn, tn)
    gate = jnp.dot(h_scr[...], wg_ref[...], preferred_element_type=F32)
    ple = jnp.dot(p_scr[...], wp_ref[...], preferred_element_type=F32)
    o_ref[:, pl.ds(col, tn)] = x_ref[:, pl.ds(col, tn)] + jax.nn.sigmoid(gate) * ple

    @pl.when(j == pl.num_programs(1) - 1)
    def _():
        o_ref[...] = _rmsnorm_f32(o_ref[...], gf_ref[...])


def _ple_final(x3, g_ple, p2d, w_gate, w_proj, g_final, tm, tn):
    t, d = x3.shape
    dp = p2d.shape[1]
    return pl.pallas_call(
        functools.partial(_ple_kernel, tn=tn),
        out_shape=jax.ShapeDtypeStruct((t, d), F32),
        grid=(t // tm, d // tn),
        in_specs=[
            pl.BlockSpec((tm, d), lambda m, j: (m, 0)),
            pl.BlockSpec((1, d), lambda m, j: (0, 0)),
            pl.BlockSpec((tm, dp), lambda m, j: (m, 0)),
            pl.BlockSpec((d, tn), lambda m, j: (0, j)),
            pl.BlockSpec((dp, tn), lambda m, j: (0, j)),
            pl.BlockSpec((1, d), lambda m, j: (0, 0)),
        ],
        out_specs=pl.BlockSpec((tm, d), lambda m, j: (m, 0)),
        scratch_shapes=[pltpu.VMEM((tm, d), BF16), pltpu.VMEM((tm, dp), BF16)],
        compiler_params=_params(("parallel", "arbitrary")),
        name="ple_final",
    )(x3, g_ple, p2d, w_gate, w_proj, g_final)


def _tile(n, pref):
    return pref if n % pref == 0 else n


def _layer(x2d, p2d, batch, seq, w_in, b_forget, w_branch_fox, w_branch_sb, w_mix_out, g_mix, g_ffn,
           w_group, w_expert, w_gate, w_up, w_down, g_ple, w_ple_proj, w_ple_gate, g_final):
    t, d = x2d.shape
    n_heads = b_forget.shape[0]
    w_att = n_heads * HEAD_DIM
    n_exp = w_expert.shape[1]

    w_main = jnp.concatenate([w_in[:, :3 * w_att], w_in[:, 3 * w_att + n_heads:]], axis=1).astype(BF16)
    wf_t = w_in[:, 3 * w_att:3 * w_att + n_heads].T.astype(BF16)
    col_qa, col_ka, col_va = 0, w_att, 2 * w_att
    col_qb, col_kb, col_vb = 3 * w_att, 4 * w_att, 5 * w_att
    col_ga, col_gb = 6 * w_att, 6 * w_att + d

    tm = _tile(t, 1024)
    tn = _tile(d, 512)
    proj, f_t = _inproj(x2d, g_mix.reshape(1, d), w_main, wf_t, tm, _tile(w_main.shape[1], 512))
    c = _forget_cumsum(f_t, b_forget.reshape(n_heads, 1), batch)

    tq = _tile(seq, 512)
    o_a = _fox_attention(proj, c, batch, seq, n_heads, col_qa, col_ka, col_va, tq, _tile(seq, 512))
    o_b = _sb_attention(proj, batch, seq, n_heads, col_qb, col_kb, col_vb, tq, _tile(seq, 256))

    mixed = _merge(o_a, o_b, w_branch_fox.astype(BF16), w_branch_sb.astype(BF16), proj, col_ga, col_gb, tm, tn)
    x2 = _mixout(mixed, w_mix_out.astype(BF16), x2d, tm, tn)

    w_r = jnp.concatenate([w_group, w_expert], axis=1)
    w_r = jnp.pad(w_r, ((0, 0), (0, LANES - w_r.shape[1])))
    w_r_hi = w_r.astype(BF16)
    w_r_lo = (w_r - w_r_hi.astype(F32)).astype(BF16)
    tr = _tile(t, 512)
    h2, ids, gates, counts = _router(x2, g_ffn.reshape(1, d), w_r_hi, w_r_lo, tr, N_GROUPS, EXPERTS_PER_GROUP)

    bm = 256
    n_assign = 2 * t
    nb = n_assign // bm + n_exp
    cnt = counts[0, :n_exp]
    nblk = (cnt + bm - 1) // bm
    blk_end = jnp.cumsum(nblk)
    pad_start = (blk_end - nblk) * bm
    dest1 = pad_start[ids[:, 0]] + ids[:, 2]
    dest2 = pad_start[ids[:, 1]] + ids[:, 3]
    n_used = blk_end[-1:].astype(jnp.int32)
    block_expert = jnp.minimum(
        jnp.searchsorted(blk_end, jnp.arange(nb, dtype=jnp.int32), side='right'), n_exp - 1).astype(jnp.int32)

    td = _tile(t, 256)
    xb = _dispatch(h2, dest1, dest2, nb * bm, td)
    yb = _experts(xb, w_gate, w_up, w_down, block_expert, n_used, bm)
    x3 = _combine(x2, gates, yb, dest1, dest2, td)

    tp = _tile(t, 512)
    return _ple_final(x3, g_ple.reshape(1, d), p2d, w_ple_gate.astype(BF16), w_ple_proj.astype(BF16),
                      g_final.reshape(1, d), tp, tn)


def kernel(x, p, w_in, b_forget, w_branch_fox, w_branch_sb, w_mix_out, g_mix, g_ffn, w_group, w_expert,
           w_gate, w_up, w_down, g_ple, w_ple_proj, w_ple_gate, g_final):
    b, s, d = x.shape
    depth = w_in.shape[0]
    assert depth == 1, "the final norm is fused into the single layer"
    x2d = x.reshape(b * s, d)
    out = _layer(x2d, p[0].reshape(b * s, -1), b, s, w_in[0], b_forget[0], w_branch_fox[0], w_branch_sb[0],
                 w_mix_out[0], g_mix[0], g_ffn[0], w_group[0], w_expert[0], w_gate[0], w_up[0], w_down[0],
                 g_ple[0], w_ple_proj[0], w_ple_gate[0], g_final)
    return out.reshape(b, s, d)
```

```python
import functools

import jax
import jax.numpy as jnp
from jax import lax
from jax.experimental import pallas as pl
from jax.experimental.pallas import tpu as pltpu

F32 = jnp.float32
BF16 = jnp.bfloat16

HEAD_DIM = 128
N_GROUPS = 4
EXPERTS_PER_GROUP = 8
EPS = 1e-6
LANES = 128
VMEM_LIMIT = 56 * 1024 * 1024

LOG2E = 1.4426950408889634

NT_DIMS = (((1,), (1,)), ((), ()))


def _params(sem):
    return pltpu.CompilerParams(dimension_semantics=sem, vmem_limit_bytes=VMEM_LIMIT)


def _rmsnorm_f32(x, g):
    ms = jnp.mean(x * x, axis=-1, keepdims=True)
    return x * lax.rsqrt(ms + EPS) * g


def _pack_bf16_pairs(x):
    n = x.shape[1] // 2
    bits = lax.bitcast_convert_type(x.astype(BF16).astype(F32), jnp.uint32)
    return (bits[:, :n] >> 16) | (bits[:, n:] & jnp.uint32(0xFFFF0000))


def _unpack_bf16_pairs(w):
    lo = lax.bitcast_convert_type(w << 16, F32)
    hi = lax.bitcast_convert_type(w & jnp.uint32(0xFFFF0000), F32)
    return jnp.concatenate([lo, hi], axis=1)


def _inproj_kernel(x_ref, g_ref, w_ref, wv_ref, wf_ref, o_ref, vt_ref, f_ref, h_scr, *, nvt):
    j = pl.program_id(1)

    @pl.when(j == 0)
    def _():
        hb = _rmsnorm_f32(x_ref[...], g_ref[...]).astype(BF16)
        h_scr[...] = hb
        f_ref[...] = jnp.dot(hb, wf_ref[...], preferred_element_type=F32)

    @pl.when(j < nvt)
    def _():
        vt_ref[...] = lax.dot_general(wv_ref[...], h_scr[...], NT_DIMS,
                                      preferred_element_type=F32).astype(vt_ref.dtype)

    @pl.when(j >= nvt)
    def _():
        o_ref[...] = jnp.dot(h_scr[...], w_ref[...], preferred_element_type=F32).astype(o_ref.dtype)


def _inproj(x2d, g, w, wv_t, wf, tm, tn):
    t, d = x2d.shape
    n = w.shape[1]
    nv = wv_t.shape[0]
    nvt = nv // tn
    return pl.pallas_call(
        functools.partial(_inproj_kernel, nvt=nvt),
        out_shape=(jax.ShapeDtypeStruct((t, n), BF16), jax.ShapeDtypeStruct((nv, t), BF16),
                   jax.ShapeDtypeStruct((t, LANES), F32)),
        grid=(t // tm, nvt + n // tn),
        in_specs=[
            pl.BlockSpec((tm, d), lambda m, j: (m, 0)),
            pl.BlockSpec((1, d), lambda m, j: (0, 0)),
            pl.BlockSpec((d, tn), lambda m, j: (0, jnp.maximum(j - nvt, 0))),
            pl.BlockSpec((tn, d), lambda m, j: (jnp.minimum(j, nvt - 1), 0)),
            pl.BlockSpec((d, LANES), lambda m, j: (0, 0)),
        ],
        out_specs=(
            pl.BlockSpec((tm, tn), lambda m, j: (m, jnp.maximum(j - nvt, 0))),
            pl.BlockSpec((tn, tm), lambda m, j: (jnp.minimum(j, nvt - 1), m)),
            pl.BlockSpec((tm, LANES), lambda m, j: (m, 0)),
        ),
        scratch_shapes=[pltpu.VMEM((tm, d), BF16)],
        compiler_params=_params(("parallel", "arbitrary")),
        name="inproj",
    )(x2d, g, w, wv_t, wf)


def _split3(x):
    p1 = x.astype(BF16)
    r1 = x - p1.astype(F32)
    p2 = r1.astype(BF16)
    p3 = (r1 - p2.astype(F32)).astype(BF16)
    return p1, p2, p3


def _cumsum_kernel(f_ref, b_ref, lt_ref, c_ref, carry_scr, *, chunk, n_heads):
    @pl.when(pl.program_id(1) == 0)
    def _():
        carry_scr[...] = jnp.zeros(carry_scr.shape, F32)

    rows = f_ref.shape[0]
    lt = lt_ref[...]
    carry = carry_scr[...]
    for i in range(rows // chunk):
        lf = jax.nn.log_sigmoid(f_ref[i * chunk:(i + 1) * chunk, :] + b_ref[...])
        p1, p2, p3 = _split3(lf)
        cs = (jnp.dot(lt, p1, preferred_element_type=F32)
              + jnp.dot(lt, p2, preferred_element_type=F32)
              + jnp.dot(lt, p3, preferred_element_type=F32)) + carry
        for h in range(n_heads):
            c_ref[h, i * chunk:(i + 1) * chunk, :] = jnp.broadcast_to(cs[:, h:h + 1] * LOG2E, (chunk, LANES))
        carry = cs[chunk - 1:chunk, :]
    carry_scr[...] = carry


def _forget_cumsum(f_tok, b_row, batch, n_heads):
    t = f_tok.shape[0]
    s = t // batch
    rows = min(1024, s)
    chunk = min(256, s)
    lt = jnp.tril(jnp.ones((chunk, chunk), BF16))
    nr = s // rows
    return pl.pallas_call(
        functools.partial(_cumsum_kernel, chunk=chunk, n_heads=n_heads),
        out_shape=jax.ShapeDtypeStruct((n_heads, t, LANES), F32),
        grid=(batch, nr),
        in_specs=[
            pl.BlockSpec((rows, LANES), lambda b, r: (b * nr + r, 0)),
            pl.BlockSpec((1, LANES), lambda b, r: (0, 0)),
            pl.BlockSpec((chunk, chunk), lambda b, r: (0, 0)),
        ],
        out_specs=pl.BlockSpec((n_heads, rows, LANES), lambda b, r: (0, b * nr + r, 0)),
        scratch_shapes=[pltpu.VMEM((1, LANES), F32)],
        compiler_params=_params(("parallel", "arbitrary")),
        name="forget_cumsum",
    )(f_tok, b_row, lt)


def _fox_kernel(q_ref, k_ref, vt_ref, c_ref, o_ref, m_scr, l_scr, acc_scr, *, tq, tk, hp):
    i = pl.program_id(2)
    m_scr[...] = jnp.full(m_scr.shape, -jnp.inf, F32)
    l_scr[...] = jnp.zeros(l_scr.shape, F32)
    acc_scr[...] = jnp.zeros(acc_scr.shape, F32)

    def tile(j, masked):
        start = pl.multiple_of(j * tk, tk)
        if masked:
            key = j * tk + lax.broadcasted_iota(jnp.int32, (tk, tq), 0)
            qry = i * tq + lax.broadcasted_iota(jnp.int32, (tk, tq), 1)
            keep = key <= qry
        for hh in range(hp):
            lanes = slice(hh * HEAD_DIM, (hh + 1) * HEAD_DIM)
            kj = k_ref[pl.ds(start, tk), lanes]
            s = lax.dot_general(kj, q_ref[:, lanes], NT_DIMS, preferred_element_type=F32)
            cj = c_ref[hh, pl.ds(start, tk), :]
            s = s - jnp.concatenate([cj] * (tq // LANES), axis=1)
            if masked:
                s = jnp.where(keep, s, -jnp.inf)
            m_prev = m_scr[hh]
            m_new = jnp.maximum(m_prev, jnp.max(s, axis=0, keepdims=True))
            alpha = jnp.exp2(m_prev - m_new)
            p = jnp.exp2(s - m_new)
            l_scr[hh] = alpha * l_scr[hh] + jnp.sum(p, axis=0, keepdims=True)
            vtj = vt_ref[hh * HEAD_DIM:(hh + 1) * HEAD_DIM, pl.ds(start, tk)]
            acc_scr[hh] = alpha * acc_scr[hh] + jnp.dot(vtj, p.astype(BF16), preferred_element_type=F32)
            m_scr[hh] = m_new

    def body(j, carry):
        tile(j, False)
        return carry

    lax.fori_loop(0, i, body, 0)
    tile(i, True)
    for hh in range(hp):
        o = acc_scr[hh] / l_scr[hh]
        o_ref[:, hh * HEAD_DIM:(hh + 1) * HEAD_DIM] = o.T.astype(o_ref.dtype)


def _fox_attention(proj, vt, c, batch, seq, n_heads, col_q, col_k, row_v, tq, hp):
    t = proj.shape[0]
    nq = seq // tq
    wb = hp * HEAD_DIM
    cq, ck, rv = col_q // wb, col_k // wb, row_v // wb
    return pl.pallas_call(
        functools.partial(_fox_kernel, tq=tq, tk=tq, hp=hp),
        out_shape=jax.ShapeDtypeStruct((t, n_heads * HEAD_DIM), BF16),
        grid=(batch, n_heads // hp, nq),
        in_specs=[
            pl.BlockSpec((tq, wb), lambda b, g, i: (b * nq + i, cq + g)),
            pl.BlockSpec((seq, wb), lambda b, g, i: (b, ck + g)),
            pl.BlockSpec((wb, seq), lambda b, g, i: (rv + g, b)),
            pl.BlockSpec((hp, seq, LANES), lambda b, g, i: (g, b, 0)),
        ],
        out_specs=pl.BlockSpec((tq, wb), lambda b, g, i: (b * nq + i, g)),
        scratch_shapes=[
            pltpu.VMEM((hp, 1, tq), F32),
            pltpu.VMEM((hp, 1, tq), F32),
            pltpu.VMEM((hp, HEAD_DIM, tq), F32),
        ],
        compiler_params=_params(("parallel", "parallel", "arbitrary")),
        name="fox_attention",
    )(proj, proj, vt, c)


def _sb_kernel(q_ref, k_ref, vt_ref, tri_ref, o_ref, r_scr, acc_scr, *, tq, tk, hp):
    i = pl.program_id(2)
    r_scr[...] = jnp.zeros(r_scr.shape, F32)
    acc_scr[...] = jnp.zeros(acc_scr.shape, F32)

    def tile(j, masked):
        start = pl.multiple_of(j * tk, tk)
        if masked:
            key = j * tk + lax.broadcasted_iota(jnp.int32, (tk, tq), 0)
            qry = i * tq + lax.broadcasted_iota(jnp.int32, (tk, tq), 1)
            mask = key < qry
        for hh in range(hp):
            lanes = slice(hh * HEAD_DIM, (hh + 1) * HEAD_DIM)
            kj = k_ref[pl.ds(start, tk), lanes]
            u = lax.dot_general(kj, q_ref[:, lanes], NT_DIMS, preferred_element_type=F32)
            neg = -(jnp.maximum(u, 0.0) + jnp.log2(1.0 + jnp.exp2(-jnp.abs(u))))
            if masked:
                neg = jnp.where(mask, neg, 0.0)
            w = jnp.dot(tri_ref[...], neg.astype(BF16), preferred_element_type=F32)
            a = jnp.exp2((u + neg) + (w + r_scr[hh]))
            if masked:
                a = jnp.where(mask, a, 0.0)
            vtj = vt_ref[hh * HEAD_DIM:(hh + 1) * HEAD_DIM, pl.ds(start, tk)]
            acc_scr[hh] += jnp.dot(vtj, a.astype(BF16), preferred_element_type=F32)
            r_scr[hh] += jnp.sum(neg, axis=0, keepdims=True)

    ratio = tq // tk
    n_full = i * ratio
    for d in reversed(range(ratio)):
        tile(n_full + d, True)

    def body(it, carry):
        tile(n_full - 1 - it, False)
        return carry

    lax.fori_loop(0, n_full, body, 0)
    for hh in range(hp):
        o_ref[:, hh * HEAD_DIM:(hh + 1) * HEAD_DIM] = acc_scr[hh].T.astype(o_ref.dtype)


def _sb_attention(proj, vt, batch, seq, n_heads, col_q, col_k, row_v, tq, tk, hp):
    t = proj.shape[0]
    nq = seq // tq
    wb = hp * HEAD_DIM
    cq, ck, rv = col_q // wb, col_k // wb, row_v // wb
    tri = jnp.triu(jnp.ones((tk, tk), BF16), 1)
    return pl.pallas_call(
        functools.partial(_sb_kernel, tq=tq, tk=tk, hp=hp),
        out_shape=jax.ShapeDtypeStruct((t, n_heads * HEAD_DIM), BF16),
        grid=(batch, n_heads // hp, nq),
        in_specs=[
            pl.BlockSpec((tq, wb), lambda b, g, i: (b * nq + i, cq + g)),
            pl.BlockSpec((seq, wb), lambda b, g, i: (b, ck + g)),
            pl.BlockSpec((wb, seq), lambda b, g, i: (rv + g, b)),
            pl.BlockSpec((tk, tk), lambda b, g, i: (0, 0)),
        ],
        out_specs=pl.BlockSpec((tq, wb), lambda b, g, i: (b * nq + i, g)),
        scratch_shapes=[
            pltpu.VMEM((hp, 1, tq), F32),
            pltpu.VMEM((hp, HEAD_DIM, tq), F32),
        ],
        compiler_params=_params(("parallel", "parallel", "arbitrary")),
        name="sb_attention",
    )(proj, proj, vt, tri)


def _merge_kernel(oa_ref, ob_ref, wa_ref, wb_ref, ga_ref, gb_ref, o_ref):
    ya = jnp.dot(oa_ref[...], wa_ref[...], preferred_element_type=F32)
    yb = jnp.dot(ob_ref[...], wb_ref[...], preferred_element_type=F32)
    ga = jax.nn.sigmoid(ga_ref[...].astype(F32))
    gb = jax.nn.sigmoid(gb_ref[...].astype(F32))
    o_ref[...] = (ga * ya + gb * yb).astype(o_ref.dtype)


def _merge(o_a, o_b, wa, wb, proj, col_ga, col_gb, tm, tn):
    t, ka = o_a.shape
    kb = o_b.shape[1]
    d = wa.shape[1]
    ca, cb = col_ga // tn, col_gb // tn
    return pl.pallas_call(
        _merge_kernel,
        out_shape=jax.ShapeDtypeStruct((t, d), BF16),
        grid=(t // tm, d // tn),
        in_specs=[
            pl.BlockSpec((tm, ka), lambda m, j: (m, 0)),
            pl.BlockSpec((tm, kb), lambda m, j: (m, 0)),
            pl.BlockSpec((ka, tn), lambda m, j: (0, j)),
            pl.BlockSpec((kb, tn), lambda m, j: (0, j)),
            pl.BlockSpec((tm, tn), lambda m, j: (m, ca + j)),
            pl.BlockSpec((tm, tn), lambda m, j: (m, cb + j)),
        ],
        out_specs=pl.BlockSpec((tm, tn), lambda m, j: (m, j)),
        compiler_params=_params(("parallel", "arbitrary")),
        name="branch_merge",
    )(o_a, o_b, wa, wb, proj, proj)


def _mixout_kernel(a_ref, w_ref, x_ref, o_ref):
    o_ref[...] = x_ref[...] + jnp.dot(a_ref[...], w_ref[...], preferred_element_type=F32)


def _mixout(mixed, w, x2d, tm, tn):
    t, k = mixed.shape
    d = w.shape[1]
    return pl.pallas_call(
        _mixout_kernel,
        out_shape=jax.ShapeDtypeStruct((t, d), F32),
        grid=(t // tm, d // tn),
        in_specs=[
            pl.BlockSpec((tm, k), lambda m, j: (m, 0)),
            pl.BlockSpec((k, tn), lambda m, j: (0, j)),
            pl.BlockSpec((tm, tn), lambda m, j: (m, j)),
        ],
        out_specs=pl.BlockSpec((tm, tn), lambda m, j: (m, j)),
        compiler_params=_params(("parallel", "arbitrary")),
        name="mix_out",
    )(mixed, w, x2d)


def _router_kernel(x_ref, g_ref, wh_ref, wl_ref, lt_ref, h_ref, ids_ref, gate_ref, cnt_ref, carry_scr,
                   *, n_groups, per_group):
    @pl.when(pl.program_id(0) == 0)
    def _():
        carry_scr[...] = jnp.zeros(carry_scr.shape, F32)

    h = _rmsnorm_f32(x_ref[...], g_ref[...])
    hh = h.astype(BF16)
    h_ref[...] = _pack_bf16_pairs(h)
    hl = (h - hh.astype(F32)).astype(BF16)
    logits = (jnp.dot(hh, wh_ref[...], preferred_element_type=F32)
              + jnp.dot(hl, wh_ref[...], preferred_element_type=F32)
              + jnp.dot(hh, wl_ref[...], preferred_element_type=F32))
    tm = logits.shape[0]
    n_exp = n_groups * per_group
    lane = lax.broadcasted_iota(jnp.int32, (tm, LANES), 1)
    lane_f = lane.astype(F32)
    neg_inf = -jnp.inf

    def first_argmax(vals, valid):
        vmax = jnp.max(jnp.where(valid, vals, neg_inf), axis=-1, keepdims=True)
        idx = jnp.min(jnp.where(valid & (vals == vmax), lane_f, float(LANES)), axis=-1, keepdims=True)
        return vmax, idx.astype(jnp.int32)

    is_group = lane < n_groups
    gmax, gsel = first_argmax(logits, is_group)
    gsum = jnp.sum(jnp.where(is_group, jnp.exp(logits - gmax), 0.0), axis=-1, keepdims=True)
    p_group = 1.0 / gsum
    lo = n_groups + gsel * per_group
    in_group = (lane >= lo) & (lane < lo + per_group)
    v1, i1 = first_argmax(logits, in_group)
    v2, i2 = first_argmax(logits, in_group & (lane != i1))
    e2w = jnp.exp(v2 - v1)
    gate1 = (1.0 / (1.0 + e2w)) * p_group
    gate2 = (e2w / (1.0 + e2w)) * p_group
    e1 = i1 - n_groups
    e2 = i2 - n_groups

    onehot = ((lane == e1) | (lane == e2)).astype(BF16)
    before = jnp.dot(lt_ref[...], onehot, preferred_element_type=F32) + carry_scr[...]
    rank1 = jnp.sum(jnp.where(lane == e1, before, 0.0), axis=-1, keepdims=True).astype(jnp.int32)
    rank2 = jnp.sum(jnp.where(lane == e2, before, 0.0), axis=-1, keepdims=True).astype(jnp.int32)
    carry_new = carry_scr[...] + jnp.sum(onehot.astype(F32), axis=0, keepdims=True)
    carry_scr[...] = carry_new

    ids_ref[...] = jnp.where(lane == 0, e1, jnp.where(lane == 1, e2,
                             jnp.where(lane == 2, rank1, jnp.where(lane == 3, rank2, 0))))
    gate_ref[...] = jnp.where(lane == 0, gate1, jnp.where(lane == 1, gate2, 0.0))
    cnt_ref[...] = jnp.broadcast_to(carry_new, cnt_ref.shape).astype(jnp.int32)
    del n_exp


def _router(x2, g, w_hi, w_lo, tm, n_groups, per_group):
    t, d = x2.shape
    lt = jnp.tril(jnp.ones((tm, tm), BF16), -1)
    return pl.pallas_call(
        functools.partial(_router_kernel, n_groups=n_groups, per_group=per_group),
        out_shape=(
            jax.ShapeDtypeStruct((t, d // 2), jnp.uint32),
            jax.ShapeDtypeStruct((t, LANES), jnp.int32),
            jax.ShapeDtypeStruct((t, LANES), F32),
            jax.ShapeDtypeStruct((8, LANES), jnp.int32),
        ),
        grid=(t // tm,),
        in_specs=[
            pl.BlockSpec((tm, d), lambda m: (m, 0)),
            pl.BlockSpec((1, d), lambda m: (0, 0)),
            pl.BlockSpec((d, LANES), lambda m: (0, 0)),
            pl.BlockSpec((d, LANES), lambda m: (0, 0)),
            pl.BlockSpec((tm, tm), lambda m: (0, 0)),
        ],
        out_specs=(
            pl.BlockSpec((tm, d // 2), lambda m: (m, 0)),
            pl.BlockSpec((tm, LANES), lambda m: (m, 0)),
            pl.BlockSpec((tm, LANES), lambda m: (m, 0)),
            pl.BlockSpec((8, LANES), lambda m: (0, 0)),
        ),
        scratch_shapes=[pltpu.VMEM((1, LANES), F32)],
        compiler_params=_params(("arbitrary",)),
        name="router",
    )(x2, g, w_hi, w_lo, lt)


def _dispatch_kernel(d1_ref, d2_ref, h_ref, xb_in_ref, xb_ref, sem, *, tm):
    del xb_in_ref
    base = pl.program_id(0) * tm

    def row_copy(r, dest):
        return pltpu.make_async_copy(h_ref.at[pl.ds(r, 1)], xb_ref.at[pl.ds(dest, 1)], sem)

    def start(r, carry):
        row_copy(r, d1_ref[base + r]).start()
        row_copy(r, d2_ref[base + r]).start()
        return carry

    lax.fori_loop(0, tm, start, 0)

    def wait(r, carry):
        row_copy(r, d1_ref[base + r]).wait()
        row_copy(r, d2_ref[base + r]).wait()
        return carry

    lax.fori_loop(0, tm, wait, 0)


def _dispatch(h2p, dest1, dest2, p_rows, tm):
    t, dw = h2p.shape
    xb0 = jnp.zeros((p_rows, dw), h2p.dtype)
    return pl.pallas_call(
        functools.partial(_dispatch_kernel, tm=tm),
        out_shape=jax.ShapeDtypeStruct((p_rows, dw), h2p.dtype),
        grid_spec=pltpu.PrefetchScalarGridSpec(
            num_scalar_prefetch=2,
            grid=(t // tm,),
            in_specs=[
                pl.BlockSpec((tm, dw), lambda m, d1, d2: (m, 0)),
                pl.BlockSpec(memory_space=pl.ANY),
            ],
            out_specs=pl.BlockSpec(memory_space=pl.ANY),
            scratch_shapes=[pltpu.SemaphoreType.DMA],
        ),
        input_output_aliases={3: 0},
        compiler_params=_params(("arbitrary",)),
        name="moe_dispatch",
    )(dest1, dest2, h2p, xb0)


def _experts_kernel(be_ref, nu_ref, x_ref, wg_ref, wu_ref, wd_ref, y_ref, wg_scr, wu_scr, wd_scr):
    b = pl.program_id(0)
    prev = be_ref[jnp.maximum(b - 1, 0)]
    fresh = (b == 0) | (be_ref[b] != prev)

    @pl.when(fresh & (b < nu_ref[0]))
    def _():
        wg_scr[...] = wg_ref[0].astype(BF16)
        wu_scr[...] = wu_ref[0].astype(BF16)
        wd_scr[...] = wd_ref[0].astype(BF16)

    @pl.when(b < nu_ref[0])
    def _():
        x = _unpack_bf16_pairs(x_ref[...]).astype(BF16)
        g = jnp.dot(x, wg_scr[...], preferred_element_type=F32)
        u = jnp.dot(x, wu_scr[...], preferred_element_type=F32)
        hid = (g * jax.nn.sigmoid(g) * u).astype(BF16)
        y_ref[...] = _pack_bf16_pairs(jnp.dot(hid, wd_scr[...], preferred_element_type=F32))


def _experts(xb, w_gate, w_up, w_down, block_expert, n_used, bm):
    p_rows, dw = xb.shape
    d, de = w_gate.shape[1:]
    nb = p_rows // bm

    def row_map(b, be, nu):
        return (jnp.minimum(b, nu[0] - 1), 0)

    def w_map(b, be, nu):
        return (be[b], 0, 0)

    return pl.pallas_call(
        _experts_kernel,
        out_shape=jax.ShapeDtypeStruct((p_rows, dw), jnp.uint32),
        grid_spec=pltpu.PrefetchScalarGridSpec(
            num_scalar_prefetch=2,
            grid=(nb,),
            in_specs=[
                pl.BlockSpec((bm, dw), row_map),
                pl.BlockSpec((1, d, de), w_map),
                pl.BlockSpec((1, d, de), w_map),
                pl.BlockSpec((1, de, d), w_map),
            ],
            out_specs=pl.BlockSpec((bm, dw), row_map),
            scratch_shapes=[
                pltpu.VMEM((d, de), BF16),
                pltpu.VMEM((d, de), BF16),
                pltpu.VMEM((de, d), BF16),
            ],
        ),
        input_output_aliases={2: 0},
        compiler_params=_params(("arbitrary",)),
        name="moe_experts",
    )(block_expert, n_used, xb, w_gate, w_up, w_down)


def _combine_kernel(d1_ref, d2_ref, x_ref, gate_ref, yb_ref, o_ref, buf, sem, *, tm):
    base = pl.program_id(0) * tm

    def row_copy(r, dest, slot):
        return pltpu.make_async_copy(yb_ref.at[pl.ds(dest, 1)], buf.at[slot, pl.ds(r, 1)], sem)

    def start(r, carry):
        row_copy(r, d1_ref[base + r], 0).start()
        row_copy(r, d2_ref[base + r], 1).start()
        return carry

    lax.fori_loop(0, tm, start, 0)

    def wait(r, carry):
        row_copy(r, d1_ref[base + r], 0).wait()
        row_copy(r, d2_ref[base + r], 1).wait()
        return carry

    lax.fori_loop(0, tm, wait, 0)
    gates = gate_ref[...]
    y1 = _unpack_bf16_pairs(buf[0])
    y2 = _unpack_bf16_pairs(buf[1])
    o_ref[...] = x_ref[...] + gates[:, 0:1] * y1 + gates[:, 1:2] * y2


def _combine(x2, gates, yb, dest1, dest2, tm):
    t, d = x2.shape
    dw = yb.shape[1]
    return pl.pallas_call(
        functools.partial(_combine_kernel, tm=tm),
        out_shape=jax.ShapeDtypeStruct((t, d), F32),
        grid_spec=pltpu.PrefetchScalarGridSpec(
            num_scalar_prefetch=2,
            grid=(t // tm,),
            in_specs=[
                pl.BlockSpec((tm, d), lambda m, d1, d2: (m, 0)),
                pl.BlockSpec((tm, LANES), lambda m, d1, d2: (m, 0)),
                pl.BlockSpec(memory_space=pl.ANY),
            ],
            out_specs=pl.BlockSpec((tm, d), lambda m, d1, d2: (m, 0)),
            scratch_shapes=[pltpu.VMEM((2, tm, dw), yb.dtype), pltpu.SemaphoreType.DMA],
        ),
        compiler_params=_params(("arbitrary",)),
        name="moe_combine",
    )(dest1, dest2, x2, gates, yb)


def _ple_kernel(x_ref, g_ref, p_ref, wg_ref, wp_ref, gf_ref, o_ref, h_scr, p_scr, *, tn):
    j = pl.program_id(1)

    @pl.when(j == 0)
    def _():
        h_scr[...] = _rmsnorm_f32(x_ref[...], g_ref[...]).astype(BF16)
        p_scr[...] = p_ref[...].astype(BF16)

    col = pl.multiple_of(j * tn, tn)
    gate = jnp.dot(h_scr[...], wg_ref[...], preferred_element_type=F32)
    ple = jnp.dot(p_scr[...], wp_ref[...], preferred_element_type=F32)
    o_ref[:, pl.ds(col, tn)] = x_ref[:, pl.ds(col, tn)] + jax.nn.sigmoid(gate) * ple

    @pl.when(j == pl.num_programs(1) - 1)
    def _():
        o_ref[...] = _rmsnorm_f32(o_ref[...], gf_ref[...])


def _ple_final(x3, g_ple, p2d, w_gate, w_proj, g_final, tm, tn):
    t, d = x3.shape
    dp = p2d.shape[1]
    return pl.pallas_call(
        functools.partial(_ple_kernel, tn=tn),
        out_shape=jax.ShapeDtypeStruct((t, d), F32),
        grid=(t // tm, d // tn),
        in_specs=[
            pl.BlockSpec((tm, d), lambda m, j: (m, 0)),
            pl.BlockSpec((1, d), lambda m, j: (0, 0)),
            pl.BlockSpec((tm, dp), lambda m, j: (m, 0)),
            pl.BlockSpec((d, tn), lambda m, j: (0, j)),
            pl.BlockSpec((dp, tn), lambda m, j: (0, j)),
            pl.BlockSpec((1, d), lambda m, j: (0, 0)),
        ],
        out_specs=pl.BlockSpec((tm, d), lambda m, j: (m, 0)),
        scratch_shapes=[pltpu.VMEM((tm, d), BF16), pltpu.VMEM((tm, dp), BF16)],
        compiler_params=_params(("parallel", "arbitrary")),
        name="ple_final",
    )(x3, g_ple, p2d, w_gate, w_proj, g_final)


def _tile(n, pref):
    return pref if n % pref == 0 else n


def _layer(x2d, p2d, batch, seq, w_in, b_forget, w_branch_fox, w_branch_sb, w_mix_out, g_mix, g_ffn,
           w_group, w_expert, w_gate, w_up, w_down, g_ple, w_ple_proj, w_ple_gate, g_final):
    t, d = x2d.shape
    n_heads = b_forget.shape[0]
    w_att = n_heads * HEAD_DIM
    n_exp = w_expert.shape[1]

    qs = LOG2E * HEAD_DIM ** -0.5
    off_b = 3 * w_att + n_heads
    w_main = jnp.concatenate([w_in[:, :w_att] * qs, w_in[:, w_att:2 * w_att],
                              w_in[:, off_b:off_b + w_att] * qs, w_in[:, off_b + w_att:off_b + 2 * w_att],
                              w_in[:, off_b + 3 * w_att:]], axis=1).astype(BF16)
    wv_t = jnp.concatenate([w_in[:, 2 * w_att:3 * w_att], w_in[:, off_b + 2 * w_att:off_b + 3 * w_att]],
                           axis=1).T.astype(BF16)
    wf = jnp.pad(w_in[:, 3 * w_att:off_b], ((0, 0), (0, LANES - n_heads))).astype(BF16)
    b_row = jnp.pad(b_forget, (0, LANES - n_heads)).reshape(1, LANES)
    col_qa, col_ka, col_qb, col_kb = 0, w_att, 2 * w_att, 3 * w_att
    col_ga, col_gb = 4 * w_att, 4 * w_att + d
    row_va, row_vb = 0, w_att

    tm = _tile(t, 1024)
    tn = _tile(d, 512)
    proj, vt, f_tok = _inproj(x2d, g_mix.reshape(1, d), w_main, wv_t, wf, tm, 512)
    c = _forget_cumsum(f_tok, b_row, batch, n_heads)

    tq = _tile(seq, 512)
    hp = 4
    o_a = _fox_attention(proj, vt, c, batch, seq, n_heads, col_qa, col_ka, row_va, tq, hp)
    o_b = _sb_attention(proj, vt, batch, seq, n_heads, col_qb, col_kb, row_vb, tq, _tile(seq, 256), hp)

    mixed = _merge(o_a, o_b, w_branch_fox.astype(BF16), w_branch_sb.astype(BF16), proj, col_ga, col_gb, tm, tn)
    x2 = _mixout(mixed, w_mix_out.astype(BF16), x2d, tm, tn)

    w_r = jnp.concatenate([w_group, w_expert], axis=1)
    w_r = jnp.pad(w_r, ((0, 0), (0, LANES - w_r.shape[1])))
    w_r_hi = w_r.astype(BF16)
    w_r_lo = (w_r - w_r_hi.astype(F32)).astype(BF16)
    tr = _tile(t, 512)
    h2, ids, gates, counts = _router(x2, g_ffn.reshape(1, d), w_r_hi, w_r_lo, tr, N_GROUPS, EXPERTS_PER_GROUP)

    bm = 256
    n_assign = 2 * t
    nb = n_assign // bm + n_exp
    cnt = counts[0, :n_exp]
    nblk = (cnt + bm - 1) // bm
    blk_end = jnp.cumsum(nblk)
    pad_start = (blk_end - nblk) * bm
    dest1 = pad_start[ids[:, 0]] + ids[:, 2]
    dest2 = pad_start[ids[:, 1]] + ids[:, 3]
    n_used = blk_end[-1:].astype(jnp.int32)
    block_expert = jnp.minimum(
        jnp.searchsorted(blk_end, jnp.arange(nb, dtype=jnp.int32), side='right'), n_exp - 1).astype(jnp.int32)

    td = _tile(t, 256)
    xb = _dispatch(h2, dest1, dest2, nb * bm, td)
    yb = _experts(xb, w_gate, w_up, w_down, block_expert, n_used, bm)
    x3 = _combine(x2, gates, yb, dest1, dest2, td)

    tp = _tile(t, 512)
    return _ple_final(x3, g_ple.reshape(1, d), p2d, w_ple_gate.astype(BF16), w_ple_proj.astype(BF16),
                      g_final.reshape(1, d), tp, tn)


def kernel(x, p, w_in, b_forget, w_branch_fox, w_branch_sb, w_mix_out, g_mix, g_ffn, w_group, w_expert,
           w_gate, w_up, w_down, g_ple, w_ple_proj, w_ple_gate, g_final):
    b, s, d = x.shape
    depth = w_in.shape[0]
    assert depth == 1, "the final norm is fused into the single layer"
    x2d = x.reshape(b * s, d)
    out = _layer(x2d, p[0].reshape(b * s, -1), b, s, w_in[0], b_forget[0], w_branch_fox[0], w_branch_sb[0],
                 w_mix_out[0], g_mix[0], g_ffn[0], w_group[0], w_expert[0], w_gate[0], w_up[0], w_down[0],
                 g_ple[0], w_ple_proj[0], w_ple_gate[0], g_final)
    return out.reshape(b, s, d)
```

```python
import functools

import jax
import jax.numpy as jnp
from jax import lax
from jax.experimental import pallas as pl
from jax.experimental.pallas import tpu as pltpu

F32 = jnp.float32
BF16 = jnp.bfloat16

HEAD_DIM = 128
N_GROUPS = 4
EXPERTS_PER_GROUP = 8
EPS = 1e-6
LANES = 128
VMEM_LIMIT = 56 * 1024 * 1024

LOG2E = 1.4426950408889634
R_DONE = 160.0

NT_DIMS = (((1,), (1,)), ((), ()))


def _params(sem):
    return pltpu.CompilerParams(dimension_semantics=sem, vmem_limit_bytes=VMEM_LIMIT)


def _rmsnorm_f32(x, g):
    ms = jnp.mean(x * x, axis=-1, keepdims=True)
    return x * lax.rsqrt(ms + EPS) * g


def _pack_bf16_pairs(x):
    n = x.shape[1] // 2
    bits = lax.bitcast_convert_type(x.astype(BF16).astype(F32), jnp.uint32)
    return (bits[:, :n] >> 16) | (bits[:, n:] & jnp.uint32(0xFFFF0000))


def _unpack_bf16_pairs(w):
    lo = lax.bitcast_convert_type(w << 16, F32)
    hi = lax.bitcast_convert_type(w & jnp.uint32(0xFFFF0000), F32)
    return jnp.concatenate([lo, hi], axis=1)


def _inproj_kernel(x_ref, g_ref, w_ref, wv_ref, wf_ref, o_ref, vt_ref, f_ref, h_scr, *, nvt):
    j = pl.program_id(1)

    @pl.when(j == 0)
    def _():
        hb = _rmsnorm_f32(x_ref[...], g_ref[...]).astype(BF16)
        h_scr[...] = hb
        f_ref[...] = jnp.dot(hb, wf_ref[...], preferred_element_type=F32)

    @pl.when(j < nvt)
    def _():
        vt_ref[...] = lax.dot_general(wv_ref[...], h_scr[...], NT_DIMS,
                                      preferred_element_type=F32).astype(vt_ref.dtype)

    @pl.when(j >= nvt)
    def _():
        o_ref[...] = jnp.dot(h_scr[...], w_ref[...], preferred_element_type=F32).astype(o_ref.dtype)


def _inproj(x2d, g, w, wv_t, wf, tm, tn):
    t, d = x2d.shape
    n = w.shape[1]
    nv = wv_t.shape[0]
    nvt = nv // tn
    return pl.pallas_call(
        functools.partial(_inproj_kernel, nvt=nvt),
        out_shape=(jax.ShapeDtypeStruct((t, n), BF16), jax.ShapeDtypeStruct((nv, t), BF16),
                   jax.ShapeDtypeStruct((t, LANES), F32)),
        grid=(t // tm, nvt + n // tn),
        in_specs=[
            pl.BlockSpec((tm, d), lambda m, j: (m, 0)),
            pl.BlockSpec((1, d), lambda m, j: (0, 0)),
            pl.BlockSpec((d, tn), lambda m, j: (0, jnp.maximum(j - nvt, 0))),
            pl.BlockSpec((tn, d), lambda m, j: (jnp.minimum(j, nvt - 1), 0)),
            pl.BlockSpec((d, LANES), lambda m, j: (0, 0)),
        ],
        out_specs=(
            pl.BlockSpec((tm, tn), lambda m, j: (m, jnp.maximum(j - nvt, 0))),
            pl.BlockSpec((tn, tm), lambda m, j: (jnp.minimum(j, nvt - 1), m)),
            pl.BlockSpec((tm, LANES), lambda m, j: (m, 0)),
        ),
        scratch_shapes=[pltpu.VMEM((tm, d), BF16)],
        compiler_params=_params(("parallel", "arbitrary")),
        name="inproj",
    )(x2d, g, w, wv_t, wf)


def _split3(x):
    p1 = x.astype(BF16)
    r1 = x - p1.astype(F32)
    p2 = r1.astype(BF16)
    p3 = (r1 - p2.astype(F32)).astype(BF16)
    return p1, p2, p3


def _cumsum_kernel(f_ref, b_ref, lt_ref, c_ref, carry_scr, *, chunk, n_heads):
    @pl.when(pl.program_id(1) == 0)
    def _():
        carry_scr[...] = jnp.zeros(carry_scr.shape, F32)

    rows = f_ref.shape[0]
    lt = lt_ref[...]
    carry = carry_scr[...]
    for i in range(rows // chunk):
        lf = jax.nn.log_sigmoid(f_ref[i * chunk:(i + 1) * chunk, :] + b_ref[...])
        p1, p2, p3 = _split3(lf)
        cs = (jnp.dot(lt, p1, preferred_element_type=F32)
              + jnp.dot(lt, p2, preferred_element_type=F32)
              + jnp.dot(lt, p3, preferred_element_type=F32)) + carry
        for h in range(n_heads):
            c_ref[h, i * chunk:(i + 1) * chunk, :] = jnp.broadcast_to(cs[:, h:h + 1] * LOG2E, (chunk, LANES))
        carry = cs[chunk - 1:chunk, :]
    carry_scr[...] = carry


def _forget_cumsum(f_tok, b_row, batch, n_heads):
    t = f_tok.shape[0]
    s = t // batch
    rows = min(1024, s)
    chunk = min(256, s)
    lt = jnp.tril(jnp.ones((chunk, chunk), BF16))
    nr = s // rows
    return pl.pallas_call(
        functools.partial(_cumsum_kernel, chunk=chunk, n_heads=n_heads),
        out_shape=jax.ShapeDtypeStruct((n_heads, t, LANES), F32),
        grid=(batch, nr),
        in_specs=[
            pl.BlockSpec((rows, LANES), lambda b, r: (b * nr + r, 0)),
            pl.BlockSpec((1, LANES), lambda b, r: (0, 0)),
            pl.BlockSpec((chunk, chunk), lambda b, r: (0, 0)),
        ],
        out_specs=pl.BlockSpec((n_heads, rows, LANES), lambda b, r: (0, b * nr + r, 0)),
        scratch_shapes=[pltpu.VMEM((1, LANES), F32)],
        compiler_params=_params(("parallel", "arbitrary")),
        name="forget_cumsum",
    )(f_tok, b_row, lt)


def _fox_kernel(q_ref, k_ref, vt_ref, c_ref, o_ref, m_scr, l_scr, acc_scr, *, tq, tk, hp):
    i = pl.program_id(2)
    m_scr[...] = jnp.full(m_scr.shape, -jnp.inf, F32)
    l_scr[...] = jnp.zeros(l_scr.shape, F32)
    acc_scr[...] = jnp.zeros(acc_scr.shape, F32)

    def tile(j, masked):
        start = pl.multiple_of(j * tk, tk)
        if masked:
            key = j * tk + lax.broadcasted_iota(jnp.int32, (tk, tq), 0)
            qry = i * tq + lax.broadcasted_iota(jnp.int32, (tk, tq), 1)
            keep = key <= qry
        def scores(hh):
            lanes = slice(hh * HEAD_DIM, (hh + 1) * HEAD_DIM)
            kj = k_ref[pl.ds(start, tk), lanes]
            return lax.dot_general(kj, q_ref[:, lanes], NT_DIMS, preferred_element_type=F32)

        s_all = [scores(hh) for hh in range(hp)]
        for hh in range(hp):
            s = s_all[hh]
            cj = c_ref[hh, pl.ds(start, tk), :]
            s = s - jnp.concatenate([cj] * (tq // LANES), axis=1)
            if masked:
                s = jnp.where(keep, s, -jnp.inf)
            m_prev = m_scr[hh]
            m_new = jnp.maximum(m_prev, jnp.max(s, axis=0, keepdims=True))
            alpha = jnp.exp2(m_prev - m_new)
            p = jnp.exp2(s - m_new)
            l_scr[hh] = alpha * l_scr[hh] + jnp.sum(p, axis=0, keepdims=True)
            vtj = vt_ref[hh * HEAD_DIM:(hh + 1) * HEAD_DIM, pl.ds(start, tk)]
            acc_scr[hh] = alpha * acc_scr[hh] + jnp.dot(vtj, p.astype(BF16), preferred_element_type=F32)
            m_scr[hh] = m_new

    def body(j, carry):
        tile(j, False)
        return carry

    lax.fori_loop(0, i, body, 0)
    tile(i, True)
    for hh in range(hp):
        o = acc_scr[hh] / l_scr[hh]
        o_ref[:, hh * HEAD_DIM:(hh + 1) * HEAD_DIM] = o.T.astype(o_ref.dtype)


def _fox_attention(proj, vt, c, batch, seq, n_heads, col_q, col_k, row_v, tq, hp):
    t = proj.shape[0]
    nq = seq // tq
    wb = hp * HEAD_DIM
    cq, ck, rv = col_q // wb, col_k // wb, row_v // wb
    return pl.pallas_call(
        functools.partial(_fox_kernel, tq=tq, tk=tq, hp=hp),
        out_shape=jax.ShapeDtypeStruct((t, n_heads * HEAD_DIM), BF16),
        grid=(batch, n_heads // hp, nq),
        in_specs=[
            pl.BlockSpec((tq, wb), lambda b, g, i: (b * nq + i, cq + g)),
            pl.BlockSpec((seq, wb), lambda b, g, i: (b, ck + g)),
            pl.BlockSpec((wb, seq), lambda b, g, i: (rv + g, b)),
            pl.BlockSpec((hp, seq, LANES), lambda b, g, i: (g, b, 0)),
        ],
        out_specs=pl.BlockSpec((tq, wb), lambda b, g, i: (b * nq + i, g)),
        scratch_shapes=[
            pltpu.VMEM((hp, 1, tq), F32),
            pltpu.VMEM((hp, 1, tq), F32),
            pltpu.VMEM((hp, HEAD_DIM, tq), F32),
        ],
        compiler_params=_params(("parallel", "parallel", "arbitrary")),
        name="fox_attention",
    )(proj, proj, vt, c)


def _sb_kernel(q_ref, k_ref, vt_ref, tri_ref, o_ref, r_scr, acc_scr, *, tq, tk, hp):
    i = pl.program_id(2)
    r_scr[...] = jnp.zeros(r_scr.shape, F32)
    acc_scr[...] = jnp.zeros(acc_scr.shape, F32)

    sign_bit = jnp.uint32(0x80000000)

    def tile(j, masked):
        start = pl.multiple_of(j * tk, tk)
        if masked:
            key = j * tk + lax.broadcasted_iota(jnp.int32, (tk, tq), 0)
            qry = i * tq + lax.broadcasted_iota(jnp.int32, (tk, tq), 1)
            mask = key < qry

        def scores(hh):
            lanes = slice(hh * HEAD_DIM, (hh + 1) * HEAD_DIM)
            kj = k_ref[pl.ds(start, tk), lanes]
            return lax.dot_general(kj, q_ref[:, lanes], NT_DIMS, preferred_element_type=F32)

        u_all = [scores(hh) for hh in range(hp)]
        for hh in range(hp):
            u = u_all[hh]
            minus_abs = lax.bitcast_convert_type(lax.bitcast_convert_type(u, jnp.uint32) | sign_bit, F32)
            sp = jnp.maximum(u, 0.0) + jnp.log2(1.0 + jnp.exp2(minus_abs))
            if masked:
                sp = jnp.where(mask, sp, 0.0)
            w = jnp.dot(tri_ref[...], sp.astype(BF16), preferred_element_type=F32)
            a = jnp.exp2(u - ((sp + w) + r_scr[hh]))
            if masked:
                a = jnp.where(mask, a, 0.0)
            vtj = vt_ref[hh * HEAD_DIM:(hh + 1) * HEAD_DIM, pl.ds(start, tk)]
            acc_scr[hh] += jnp.dot(vtj, a.astype(BF16), preferred_element_type=F32)
            r_scr[hh] += jnp.sum(sp, axis=0, keepdims=True)

    ratio = max(tq // tk, 1)
    n_full = (i * tq) // tk
    for d in reversed(range(ratio)):
        tile(n_full + d, True)

    def live():
        return (jnp.min(r_scr[...]) <= R_DONE).astype(jnp.int32)

    def cond(carry):
        it, alive = carry
        return (it < n_full) & (alive > 0)

    def body(carry):
        it, _ = carry
        tile(n_full - 1 - it, False)
        return it + 1, live()

    lax.while_loop(cond, body, (jnp.int32(0), live()))
    for hh in range(hp):
        o_ref[:, hh * HEAD_DIM:(hh + 1) * HEAD_DIM] = acc_scr[hh].T.astype(o_ref.dtype)


def _sb_attention(proj, vt, batch, seq, n_heads, col_q, col_k, row_v, tq, tk, hp):
    t = proj.shape[0]
    nq = seq // tq
    wb = hp * HEAD_DIM
    cq, ck, rv = col_q // wb, col_k // wb, row_v // wb
    tri = jnp.triu(jnp.ones((tk, tk), BF16), 1)
    return pl.pallas_call(
        functools.partial(_sb_kernel, tq=tq, tk=tk, hp=hp),
        out_shape=jax.ShapeDtypeStruct((t, n_heads * HEAD_DIM), BF16),
        grid=(batch, n_heads // hp, nq),
        in_specs=[
            pl.BlockSpec((tq, wb), lambda b, g, i: (b * nq + i, cq + g)),
            pl.BlockSpec((seq, wb), lambda b, g, i: (b, ck + g)),
            pl.BlockSpec((wb, seq), lambda b, g, i: (rv + g, b)),
            pl.BlockSpec((tk, tk), lambda b, g, i: (0, 0)),
        ],
        out_specs=pl.BlockSpec((tq, wb), lambda b, g, i: (b * nq + i, g)),
        scratch_shapes=[
            pltpu.VMEM((hp, 1, tq), F32),
            pltpu.VMEM((hp, HEAD_DIM, tq), F32),
        ],
        compiler_params=_params(("parallel", "parallel", "arbitrary")),
        name="sb_attention",
    )(proj, proj, vt, tri)


def _merge_kernel(oa_ref, ob_ref, wa_ref, wb_ref, ga_ref, gb_ref, o_ref):
    ya = jnp.dot(oa_ref[...], wa_ref[...], preferred_element_type=F32)
    yb = jnp.dot(ob_ref[...], wb_ref[...], preferred_element_type=F32)
    ga = jax.nn.sigmoid(ga_ref[...].astype(F32))
    gb = jax.nn.sigmoid(gb_ref[...].astype(F32))
    o_ref[...] = (ga * ya + gb * yb).astype(o_ref.dtype)


def _merge(o_a, o_b, wa, wb, proj, col_ga, col_gb, tm, tn):
    t, ka = o_a.shape
    kb = o_b.shape[1]
    d = wa.shape[1]
    ca, cb = col_ga // tn, col_gb // tn
    return pl.pallas_call(
        _merge_kernel,
        out_shape=jax.ShapeDtypeStruct((t, d), BF16),
        grid=(t // tm, d // tn),
        in_specs=[
            pl.BlockSpec((tm, ka), lambda m, j: (m, 0)),
            pl.BlockSpec((tm, kb), lambda m, j: (m, 0)),
            pl.BlockSpec((ka, tn), lambda m, j: (0, j)),
            pl.BlockSpec((kb, tn), lambda m, j: (0, j)),
            pl.BlockSpec((tm, tn), lambda m, j: (m, ca + j)),
            pl.BlockSpec((tm, tn), lambda m, j: (m, cb + j)),
        ],
        out_specs=pl.BlockSpec((tm, tn), lambda m, j: (m, j)),
        compiler_params=_params(("parallel", "arbitrary")),
        name="branch_merge",
    )(o_a, o_b, wa, wb, proj, proj)


def _mixout_kernel(a_ref, w_ref, x_ref, o_ref):
    o_ref[...] = x_ref[...] + jnp.dot(a_ref[...], w_ref[...], preferred_element_type=F32)


def _mixout(mixed, w, x2d, tm, tn):
    t, k = mixed.shape
    d = w.shape[1]
    return pl.pallas_call(
        _mixout_kernel,
        out_shape=jax.ShapeDtypeStruct((t, d), F32),
        grid=(t // tm, d // tn),
        in_specs=[
            pl.BlockSpec((tm, k), lambda m, j: (m, 0)),
            pl.BlockSpec((k, tn), lambda m, j: (0, j)),
            pl.BlockSpec((tm, tn), lambda m, j: (m, j)),
        ],
        out_specs=pl.BlockSpec((tm, tn), lambda m, j: (m, j)),
        compiler_params=_params(("parallel", "arbitrary")),
        name="mix_out",
    )(mixed, w, x2d)


def _router_kernel(x_ref, g_ref, wh_ref, wl_ref, lt_ref, h_ref, ids_ref, gate_ref, cnt_ref, carry_scr,
                   *, n_groups, per_group):
    @pl.when(pl.program_id(0) == 0)
    def _():
        carry_scr[...] = jnp.zeros(carry_scr.shape, F32)

    h = _rmsnorm_f32(x_ref[...], g_ref[...])
    hh = h.astype(BF16)
    h_ref[...] = _pack_bf16_pairs(h)
    hl = (h - hh.astype(F32)).astype(BF16)
    logits = (jnp.dot(hh, wh_ref[...], preferred_element_type=F32)
              + jnp.dot(hl, wh_ref[...], preferred_element_type=F32)
              + jnp.dot(hh, wl_ref[...], preferred_element_type=F32))
    tm = logits.shape[0]
    n_exp = n_groups * per_group
    lane = lax.broadcasted_iota(jnp.int32, (tm, LANES), 1)
    lane_f = lane.astype(F32)
    neg_inf = -jnp.inf

    def first_argmax(vals, valid):
        vmax = jnp.max(jnp.where(valid, vals, neg_inf), axis=-1, keepdims=True)
        idx = jnp.min(jnp.where(valid & (vals == vmax), lane_f, float(LANES)), axis=-1, keepdims=True)
        return vmax, idx.astype(jnp.int32)

    is_group = lane < n_groups
    gmax, gsel = first_argmax(logits, is_group)
    gsum = jnp.sum(jnp.where(is_group, jnp.exp(logits - gmax), 0.0), axis=-1, keepdims=True)
    p_group = 1.0 / gsum
    lo = n_groups + gsel * per_group
    in_group = (lane >= lo) & (lane < lo + per_group)
    v1, i1 = first_argmax(logits, in_group)
    v2, i2 = first_argmax(logits, in_group & (lane != i1))
    e2w = jnp.exp(v2 - v1)
    gate1 = (1.0 / (1.0 + e2w)) * p_group
    gate2 = (e2w / (1.0 + e2w)) * p_group
    e1 = i1 - n_groups
    e2 = i2 - n_groups

    onehot = ((lane == e1) | (lane == e2)).astype(BF16)
    before = jnp.dot(lt_ref[...], onehot, preferred_element_type=F32) + carry_scr[...]
    rank1 = jnp.sum(jnp.where(lane == e1, before, 0.0), axis=-1, keepdims=True).astype(jnp.int32)
    rank2 = jnp.sum(jnp.where(lane == e2, before, 0.0), axis=-1, keepdims=True).astype(jnp.int32)
    carry_new = carry_scr[...] + jnp.sum(onehot.astype(F32), axis=0, keepdims=True)
    carry_scr[...] = carry_new

    ids_ref[...] = jnp.where(lane == 0, e1, jnp.where(lane == 1, e2,
                             jnp.where(lane == 2, rank1, jnp.where(lane == 3, rank2, 0))))
    gate_ref[...] = jnp.where(lane == 0, gate1, jnp.where(lane == 1, gate2, 0.0))
    cnt_ref[...] = jnp.broadcast_to(carry_new, cnt_ref.shape).astype(jnp.int32)
    del n_exp


def _router(x2, g, w_hi, w_lo, tm, n_groups, per_group):
    t, d = x2.shape
    lt = jnp.tril(jnp.ones((tm, tm), BF16), -1)
    return pl.pallas_call(
        functools.partial(_router_kernel, n_groups=n_groups, per_group=per_group),
        out_shape=(
            jax.ShapeDtypeStruct((t, d // 2), jnp.uint32),
            jax.ShapeDtypeStruct((t, LANES), jnp.int32),
            jax.ShapeDtypeStruct((t, LANES), F32),
            jax.ShapeDtypeStruct((8, LANES), jnp.int32),
        ),
        grid=(t // tm,),
        in_specs=[
            pl.BlockSpec((tm, d), lambda m: (m, 0)),
            pl.BlockSpec((1, d), lambda m: (0, 0)),
            pl.BlockSpec((d, LANES), lambda m: (0, 0)),
            pl.BlockSpec((d, LANES), lambda m: (0, 0)),
            pl.BlockSpec((tm, tm), lambda m: (0, 0)),
        ],
        out_specs=(
            pl.BlockSpec((tm, d // 2), lambda m: (m, 0)),
            pl.BlockSpec((tm, LANES), lambda m: (m, 0)),
            pl.BlockSpec((tm, LANES), lambda m: (m, 0)),
            pl.BlockSpec((8, LANES), lambda m: (0, 0)),
        ),
        scratch_shapes=[pltpu.VMEM((1, LANES), F32)],
        compiler_params=_params(("arbitrary",)),
        name="router",
    )(x2, g, w_hi, w_lo, lt)


def _dispatch_kernel(d1_ref, d2_ref, h_ref, xb_in_ref, xb_ref, sem, *, tm):
    del xb_in_ref
    base = pl.program_id(0) * tm

    def row_copy(r, dest):
        return pltpu.make_async_copy(h_ref.at[pl.ds(r, 1)], xb_ref.at[pl.ds(dest, 1)], sem)

    def start(r, carry):
        row_copy(r, d1_ref[base + r]).start()
        row_copy(r, d2_ref[base + r]).start()
        return carry

    lax.fori_loop(0, tm, start, 0)

    def wait(r, carry):
        row_copy(r, d1_ref[base + r]).wait()
        row_copy(r, d2_ref[base + r]).wait()
        return carry

    lax.fori_loop(0, tm, wait, 0)


def _dispatch(h2p, dest1, dest2, p_rows, tm):
    t, dw = h2p.shape
    xb0 = jnp.zeros((p_rows, dw), h2p.dtype)
    return pl.pallas_call(
        functools.partial(_dispatch_kernel, tm=tm),
        out_shape=jax.ShapeDtypeStruct((p_rows, dw), h2p.dtype),
        grid_spec=pltpu.PrefetchScalarGridSpec(
            num_scalar_prefetch=2,
            grid=(t // tm,),
            in_specs=[
                pl.BlockSpec((tm, dw), lambda m, d1, d2: (m, 0)),
                pl.BlockSpec(memory_space=pl.ANY),
            ],
            out_specs=pl.BlockSpec(memory_space=pl.ANY),
            scratch_shapes=[pltpu.SemaphoreType.DMA],
        ),
        input_output_aliases={3: 0},
        compiler_params=_params(("arbitrary",)),
        name="moe_dispatch",
    )(dest1, dest2, h2p, xb0)


def _experts_kernel(be_ref, nu_ref, x_ref, wg_ref, wu_ref, wd_ref, y_ref, wg_scr, wu_scr, wd_scr):
    b = pl.program_id(0)
    prev = be_ref[jnp.maximum(b - 1, 0)]
    fresh = (b == 0) | (be_ref[b] != prev)

    @pl.when(fresh & (b < nu_ref[0]))
    def _():
        wg_scr[...] = wg_ref[0].astype(BF16)
        wu_scr[...] = wu_ref[0].astype(BF16)
        wd_scr[...] = wd_ref[0].astype(BF16)

    @pl.when(b < nu_ref[0])
    def _():
        x = _unpack_bf16_pairs(x_ref[...]).astype(BF16)
        g = jnp.dot(x, wg_scr[...], preferred_element_type=F32)
        u = jnp.dot(x, wu_scr[...], preferred_element_type=F32)
        hid = (g * jax.nn.sigmoid(g) * u).astype(BF16)
        y_ref[...] = _pack_bf16_pairs(jnp.dot(hid, wd_scr[...], preferred_element_type=F32))


def _experts(xb, w_gate, w_up, w_down, block_expert, n_used, bm):
    p_rows, dw = xb.shape
    d, de = w_gate.shape[1:]
    nb = p_rows // bm

    def row_map(b, be, nu):
        return (jnp.minimum(b, nu[0] - 1), 0)

    def w_map(b, be, nu):
        return (be[b], 0, 0)

    return pl.pallas_call(
        _experts_kernel,
        out_shape=jax.ShapeDtypeStruct((p_rows, dw), jnp.uint32),
        grid_spec=pltpu.PrefetchScalarGridSpec(
            num_scalar_prefetch=2,
            grid=(nb,),
            in_specs=[
                pl.BlockSpec((bm, dw), row_map),
                pl.BlockSpec((1, d, de), w_map),
                pl.BlockSpec((1, d, de), w_map),
                pl.BlockSpec((1, de, d), w_map),
            ],
            out_specs=pl.BlockSpec((bm, dw), row_map),
            scratch_shapes=[
                pltpu.VMEM((d, de), BF16),
                pltpu.VMEM((d, de), BF16),
                pltpu.VMEM((de, d), BF16),
            ],
        ),
        input_output_aliases={2: 0},
        compiler_params=_params(("arbitrary",)),
        name="moe_experts",
    )(block_expert, n_used, xb, w_gate, w_up, w_down)


def _combine_kernel(d1_ref, d2_ref, x_ref, gate_ref, yb_ref, o_ref, buf, sem, *, tm):
    base = pl.program_id(0) * tm

    def row_copy(r, dest, slot):
        return pltpu.make_async_copy(yb_ref.at[pl.ds(dest, 1)], buf.at[slot, pl.ds(r, 1)], sem)

    def start(r, carry):
        row_copy(r, d1_ref[base + r], 0).start()
        row_copy(r, d2_ref[base + r], 1).start()
        return carry

    lax.fori_loop(0, tm, start, 0)

    def wait(r, carry):
        row_copy(r, d1_ref[base + r], 0).wait()
        row_copy(r, d2_ref[base + r], 1).wait()
        return carry

    lax.fori_loop(0, tm, wait, 0)
    gates = gate_ref[...]
    y1 = _unpack_bf16_pairs(buf[0])
    y2 = _unpack_bf16_pairs(buf[1])
    o_ref[...] = x_ref[...] + gates[:, 0:1] * y1 + gates[:, 1:2] * y2


def _combine(x2, gates, yb, dest1, dest2, tm):
    t, d = x2.shape
    dw = yb.shape[1]
    return pl.pallas_call(
        functools.partial(_combine_kernel, tm=tm),
        out_shape=jax.ShapeDtypeStruct((t, d), F32),
        grid_spec=pltpu.PrefetchScalarGridSpec(
            num_scalar_prefetch=2,
            grid=(t // tm,),
            in_specs=[
                pl.BlockSpec((tm, d), lambda m, d1, d2: (m, 0)),
                pl.BlockSpec((tm, LANES), lambda m, d1, d2: (m, 0)),
                pl.BlockSpec(memory_space=pl.ANY),
            ],
            out_specs=pl.BlockSpec((tm, d), lambda m, d1, d2: (m, 0)),
            scratch_shapes=[pltpu.VMEM((2, tm, dw), yb.dtype), pltpu.SemaphoreType.DMA],
        ),
        compiler_params=_params(("arbitrary",)),
        name="moe_combine",
    )(dest1, dest2, x2, gates, yb)


def _ple_kernel(x_ref, g_ref, p_ref, wg_ref, wp_ref, gf_ref, o_ref, h_scr, p_scr, *, tn):
    j = pl.program_id(1)

    @pl.when(j == 0)
    def _():
        h_scr[...] = _rmsnorm_f32(x_ref[...], g_ref[...]).astype(BF16)
        p_scr[...] = p_ref[...].astype(BF16)

    col = pl.multiple_of(j * tn, tn)
    gate = jnp.dot(h_scr[...], wg_ref[...], preferred_element_type=F32)
    ple = jnp.dot(p_scr[...], wp_ref[...], preferred_element_type=F32)
    o_ref[:, pl.ds(col, tn)] = x_ref[:, pl.ds(col, tn)] + jax.nn.sigmoid(gate) * ple

    @pl.when(j == pl.num_programs(1) - 1)
    def _():
        o_ref[...] = _rmsnorm_f32(o_ref[...], gf_ref[...])


def _ple_final(x3, g_ple, p2d, w_gate, w_proj, g_final, tm, tn):
    t, d = x3.shape
    dp = p2d.shape[1]
    return pl.pallas_call(
        functools.partial(_ple_kernel, tn=tn),
        out_shape=jax.ShapeDtypeStruct((t, d), F32),
        grid=(t // tm, d // tn),
        in_specs=[
            pl.BlockSpec((tm, d), lambda m, j: (m, 0)),
            pl.BlockSpec((1, d), lambda m, j: (0, 0)),
            pl.BlockSpec((tm, dp), lambda m, j: (m, 0)),
            pl.BlockSpec((d, tn), lambda m, j: (0, j)),
            pl.BlockSpec((dp, tn), lambda m, j: (0, j)),
            pl.BlockSpec((1, d), lambda m, j: (0, 0)),
        ],
        out_specs=pl.BlockSpec((tm, d), lambda m, j: (m, 0)),
        scratch_shapes=[pltpu.VMEM((tm, d), BF16), pltpu.VMEM((tm, dp), BF16)],
        compiler_params=_params(("parallel", "arbitrary")),
        name="ple_final",
    )(x3, g_ple, p2d, w_gate, w_proj, g_final)


def _tile(n, pref):
    return pref if n % pref == 0 else n


def _layer(x2d, p2d, batch, seq, w_in, b_forget, w_branch_fox, w_branch_sb, w_mix_out, g_mix, g_ffn,
           w_group, w_expert, w_gate, w_up, w_down, g_ple, w_ple_proj, w_ple_gate, g_final):
    t, d = x2d.shape
    n_heads = b_forget.shape[0]
    w_att = n_heads * HEAD_DIM
    n_exp = w_expert.shape[1]

    qs = LOG2E * HEAD_DIM ** -0.5
    off_b = 3 * w_att + n_heads
    w_main = jnp.concatenate([w_in[:, :w_att] * qs, w_in[:, w_att:2 * w_att],
                              w_in[:, off_b:off_b + w_att] * qs, w_in[:, off_b + w_att:off_b + 2 * w_att],
                              w_in[:, off_b + 3 * w_att:]], axis=1).astype(BF16)
    wv_t = jnp.concatenate([w_in[:, 2 * w_att:3 * w_att], w_in[:, off_b + 2 * w_att:off_b + 3 * w_att]],
                           axis=1).T.astype(BF16)
    wf = jnp.pad(w_in[:, 3 * w_att:off_b], ((0, 0), (0, LANES - n_heads))).astype(BF16)
    b_row = jnp.pad(b_forget, (0, LANES - n_heads)).reshape(1, LANES)
    col_qa, col_ka, col_qb, col_kb = 0, w_att, 2 * w_att, 3 * w_att
    col_ga, col_gb = 4 * w_att, 4 * w_att + d
    row_va, row_vb = 0, w_att

    tm = _tile(t, 1024)
    tn = _tile(d, 512)
    proj, vt, f_tok = _inproj(x2d, g_mix.reshape(1, d), w_main, wv_t, wf, tm, 512)
    c = _forget_cumsum(f_tok, b_row, batch, n_heads)

    tq = _tile(seq, 512)
    hp = 4
    o_a = _fox_attention(proj, vt, c, batch, seq, n_heads, col_qa, col_ka, row_va, tq, hp)
    o_b = _sb_attention(proj, vt, batch, seq, n_heads, col_qb, col_kb, row_vb, tq, _tile(seq, 256), hp)

    mixed = _merge(o_a, o_b, w_branch_fox.astype(BF16), w_branch_sb.astype(BF16), proj, col_ga, col_gb, tm, tn)
    x2 = _mixout(mixed, w_mix_out.astype(BF16), x2d, tm, tn)

    w_r = jnp.concatenate([w_group, w_expert], axis=1)
    w_r = jnp.pad(w_r, ((0, 0), (0, LANES - w_r.shape[1])))
    w_r_hi = w_r.astype(BF16)
    w_r_lo = (w_r - w_r_hi.astype(F32)).astype(BF16)
    tr = _tile(t, 512)
    h2, ids, gates, counts = _router(x2, g_ffn.reshape(1, d), w_r_hi, w_r_lo, tr, N_GROUPS, EXPERTS_PER_GROUP)

    bm = 256
    n_assign = 2 * t
    nb = n_assign // bm + n_exp
    cnt = counts[0, :n_exp]
    nblk = (cnt + bm - 1) // bm
    blk_end = jnp.cumsum(nblk)
    pad_start = (blk_end - nblk) * bm
    dest1 = pad_start[ids[:, 0]] + ids[:, 2]
    dest2 = pad_start[ids[:, 1]] + ids[:, 3]
    n_used = blk_end[-1:].astype(jnp.int32)
    block_expert = jnp.minimum(
        jnp.searchsorted(blk_end, jnp.arange(nb, dtype=jnp.int32), side='right'), n_exp - 1).astype(jnp.int32)

    td = _tile(t, 256)
    xb = _dispatch(h2, dest1, dest2, nb * bm, td)
    yb = _experts(xb, w_gate, w_up, w_down, block_expert, n_used, bm)
    x3 = _combine(x2, gates, yb, dest1, dest2, td)

    tp = _tile(t, 512)
    return _ple_final(x3, g_ple.reshape(1, d), p2d, w_ple_gate.astype(BF16), w_ple_proj.astype(BF16),
                      g_final.reshape(1, d), tp, tn)


def kernel(x, p, w_in, b_forget, w_branch_fox, w_branch_sb, w_mix_out, g_mix, g_ffn, w_group, w_expert,
           w_gate, w_up, w_down, g_ple, w_ple_proj, w_ple_gate, g_final):
    b, s, d = x.shape
    depth = w_in.shape[0]
    assert depth == 1, "the final norm is fused into the single layer"
    x2d = x.reshape(b * s, d)
    out = _layer(x2d, p[0].reshape(b * s, -1), b, s, w_in[0], b_forget[0], w_branch_fox[0], w_branch_sb[0],
                 w_mix_out[0], g_mix[0], g_ffn[0], w_group[0], w_expert[0], w_gate[0], w_up[0], w_down[0],
                 g_ple[0], w_ple_proj[0], w_ple_gate[0], g_final)
    return out.reshape(b, s, d)
```

```python
import functools

import jax
import jax.numpy as jnp
from jax import lax
from jax.experimental import pallas as pl
from jax.experimental.pallas import tpu as pltpu

F32 = jnp.float32
BF16 = jnp.bfloat16

HEAD_DIM = 128
N_GROUPS = 4
EXPERTS_PER_GROUP = 8
EPS = 1e-6
LANES = 128
SUBLANES = 8
VMEM_LIMIT = 56 * 1024 * 1024

LOG2E = 1.4426950408889634
DMA_UNROLL = 8
R_DONE = 160.0

NT_DIMS = (((1,), (1,)), ((), ()))


def _params(sem):
    return pltpu.CompilerParams(dimension_semantics=sem, vmem_limit_bytes=VMEM_LIMIT)


def _rmsnorm_f32(x, g):
    ms = jnp.mean(x * x, axis=-1, keepdims=True)
    return x * lax.rsqrt(ms + EPS) * g


def _pack_bf16_pairs(x):
    n = x.shape[1] // 2
    bits = lax.bitcast_convert_type(x.astype(BF16).astype(F32), jnp.uint32)
    return (bits[:, :n] >> 16) | (bits[:, n:] & jnp.uint32(0xFFFF0000))


def _unpack_bf16_pairs(w):
    lo = lax.bitcast_convert_type(w << 16, F32)
    hi = lax.bitcast_convert_type(w & jnp.uint32(0xFFFF0000), F32)
    return jnp.concatenate([lo, hi], axis=1)


def _store_row_tiles(ref, words):
    n = words.shape[0]
    ns = words.shape[1] // LANES
    for c in range(ns):
        ref[pl.ds(c, n, stride=ns), :] = words[:, c * LANES:(c + 1) * LANES]


def _load_row_tiles(ref, ns):
    n = ref.shape[0] // ns
    return jnp.concatenate([ref[pl.ds(c, n, stride=ns), :] for c in range(ns)], axis=1)


def _inproj_kernel(x_ref, g_ref, w_ref, wv_ref, wf_ref, o_ref, vt_ref, f_ref, h_scr, *, nvt):
    j = pl.program_id(1)

    @pl.when(j == 0)
    def _():
        hb = _rmsnorm_f32(x_ref[...], g_ref[...]).astype(BF16)
        h_scr[...] = hb
        f_ref[...] = jnp.dot(hb, wf_ref[...], preferred_element_type=F32)

    @pl.when(j < nvt)
    def _():
        vt_ref[...] = lax.dot_general(wv_ref[...], h_scr[...], NT_DIMS,
                                      preferred_element_type=F32).astype(vt_ref.dtype)

    @pl.when(j >= nvt)
    def _():
        o_ref[...] = jnp.dot(h_scr[...], w_ref[...], preferred_element_type=F32).astype(o_ref.dtype)


def _inproj(x2d, g, w, wv_t, wf, tm, tn):
    t, d = x2d.shape
    n = w.shape[1]
    nv = wv_t.shape[0]
    nvt = nv // tn
    return pl.pallas_call(
        functools.partial(_inproj_kernel, nvt=nvt),
        out_shape=(jax.ShapeDtypeStruct((t, n), BF16), jax.ShapeDtypeStruct((nv, t), BF16),
                   jax.ShapeDtypeStruct((t, LANES), F32)),
        grid=(t // tm, nvt + n // tn),
        in_specs=[
            pl.BlockSpec((tm, d), lambda m, j: (m, 0)),
            pl.BlockSpec((1, d), lambda m, j: (0, 0)),
            pl.BlockSpec((d, tn), lambda m, j: (0, jnp.maximum(j - nvt, 0))),
            pl.BlockSpec((tn, d), lambda m, j: (jnp.minimum(j, nvt - 1), 0)),
            pl.BlockSpec((d, LANES), lambda m, j: (0, 0)),
        ],
        out_specs=(
            pl.BlockSpec((tm, tn), lambda m, j: (m, jnp.maximum(j - nvt, 0))),
            pl.BlockSpec((tn, tm), lambda m, j: (jnp.minimum(j, nvt - 1), m)),
            pl.BlockSpec((tm, LANES), lambda m, j: (m, 0)),
        ),
        scratch_shapes=[pltpu.VMEM((tm, d), BF16)],
        compiler_params=_params(("parallel", "arbitrary")),
        name="inproj",
    )(x2d, g, w, wv_t, wf)


def _split3(x):
    p1 = x.astype(BF16)
    r1 = x - p1.astype(F32)
    p2 = r1.astype(BF16)
    p3 = (r1 - p2.astype(F32)).astype(BF16)
    return p1, p2, p3


def _cumsum_kernel(f_ref, b_ref, lt_ref, c_ref, carry_scr, *, chunk, n_heads):
    @pl.when(pl.program_id(1) == 0)
    def _():
        carry_scr[...] = jnp.zeros(carry_scr.shape, F32)

    rows = f_ref.shape[0]
    lt = lt_ref[...]
    carry = carry_scr[...]
    for i in range(rows // chunk):
        lf = jax.nn.log_sigmoid(f_ref[i * chunk:(i + 1) * chunk, :] + b_ref[...])
        p1, p2, p3 = _split3(lf)
        cs = (jnp.dot(lt, p1, preferred_element_type=F32)
              + jnp.dot(lt, p2, preferred_element_type=F32)
              + jnp.dot(lt, p3, preferred_element_type=F32)) + carry
        for h in range(n_heads):
            c_ref[h, i * chunk:(i + 1) * chunk, :] = jnp.broadcast_to(cs[:, h:h + 1] * LOG2E, (chunk, LANES))
        carry = cs[chunk - 1:chunk, :]
    carry_scr[...] = carry


def _forget_cumsum(f_tok, b_row, batch, n_heads):
    t = f_tok.shape[0]
    s = t // batch
    rows = min(1024, s)
    chunk = min(256, s)
    lt = jnp.tril(jnp.ones((chunk, chunk), BF16))
    nr = s // rows
    return pl.pallas_call(
        functools.partial(_cumsum_kernel, chunk=chunk, n_heads=n_heads),
        out_shape=jax.ShapeDtypeStruct((n_heads, t, LANES), F32),
        grid=(batch, nr),
        in_specs=[
            pl.BlockSpec((rows, LANES), lambda b, r: (b * nr + r, 0)),
            pl.BlockSpec((1, LANES), lambda b, r: (0, 0)),
            pl.BlockSpec((chunk, chunk), lambda b, r: (0, 0)),
        ],
        out_specs=pl.BlockSpec((n_heads, rows, LANES), lambda b, r: (0, b * nr + r, 0)),
        scratch_shapes=[pltpu.VMEM((1, LANES), F32)],
        compiler_params=_params(("parallel", "arbitrary")),
        name="forget_cumsum",
    )(f_tok, b_row, lt)


def _fox_kernel(q_ref, k_ref, vt_ref, c_ref, o_ref, m_scr, l_scr, acc_scr, *, tq, tk, hp):
    i = pl.program_id(2)
    m_scr[...] = jnp.full(m_scr.shape, -jnp.inf, F32)
    l_scr[...] = jnp.zeros(l_scr.shape, F32)
    acc_scr[...] = jnp.zeros(acc_scr.shape, F32)

    def tile(j, masked):
        start = pl.multiple_of(j * tk, tk)
        if masked:
            key = j * tk + lax.broadcasted_iota(jnp.int32, (tk, tq), 0)
            qry = i * tq + lax.broadcasted_iota(jnp.int32, (tk, tq), 1)
            keep = key <= qry
        def scores(hh):
            lanes = slice(hh * HEAD_DIM, (hh + 1) * HEAD_DIM)
            kj = k_ref[pl.ds(start, tk), lanes]
            return lax.dot_general(kj, q_ref[:, lanes], NT_DIMS, preferred_element_type=F32)

        s_all = [scores(hh) for hh in range(hp)]
        for hh in range(hp):
            s = s_all[hh]
            cj = c_ref[hh, pl.ds(start, tk), :]
            s = s - jnp.concatenate([cj] * (tq // LANES), axis=1)
            if masked:
                s = jnp.where(keep, s, -jnp.inf)
            m_prev = m_scr[hh]
            m_new = jnp.maximum(m_prev, jnp.max(s, axis=0, keepdims=True))
            alpha = jnp.exp2(m_prev - m_new)
            p = jnp.exp2(s - m_new)
            l_scr[hh] = alpha * l_scr[hh] + jnp.sum(p, axis=0, keepdims=True)
            vtj = vt_ref[hh * HEAD_DIM:(hh + 1) * HEAD_DIM, pl.ds(start, tk)]
            acc_scr[hh] = alpha * acc_scr[hh] + jnp.dot(vtj, p.astype(BF16), preferred_element_type=F32)
            m_scr[hh] = m_new

    def body(j, carry):
        tile(j, False)
        return carry

    lax.fori_loop(0, i, body, 0)
    tile(i, True)
    for hh in range(hp):
        o = acc_scr[hh] / l_scr[hh]
        o_ref[:, hh * HEAD_DIM:(hh + 1) * HEAD_DIM] = o.T.astype(o_ref.dtype)


def _fox_attention(proj, vt, c, batch, seq, n_heads, col_q, col_k, row_v, tq, hp):
    t = proj.shape[0]
    nq = seq // tq
    wb = hp * HEAD_DIM
    cq, ck, rv = col_q // wb, col_k // wb, row_v // wb
    return pl.pallas_call(
        functools.partial(_fox_kernel, tq=tq, tk=tq, hp=hp),
        out_shape=jax.ShapeDtypeStruct((t, n_heads * HEAD_DIM), BF16),
        grid=(batch, n_heads // hp, nq),
        in_specs=[
            pl.BlockSpec((tq, wb), lambda b, g, i: (b * nq + i, cq + g)),
            pl.BlockSpec((seq, wb), lambda b, g, i: (b, ck + g)),
            pl.BlockSpec((wb, seq), lambda b, g, i: (rv + g, b)),
            pl.BlockSpec((hp, seq, LANES), lambda b, g, i: (g, b, 0)),
        ],
        out_specs=pl.BlockSpec((tq, wb), lambda b, g, i: (b * nq + i, g)),
        scratch_shapes=[
            pltpu.VMEM((hp, 1, tq), F32),
            pltpu.VMEM((hp, 1, tq), F32),
            pltpu.VMEM((hp, HEAD_DIM, tq), F32),
        ],
        compiler_params=_params(("parallel", "parallel", "arbitrary")),
        name="fox_attention",
    )(proj, proj, vt, c)


def _sb_kernel(q_ref, k_ref, vt_ref, tri_ref, o_ref, r_scr, acc_scr, *, tq, tk, hp):
    i = pl.program_id(2)
    r_scr[...] = jnp.zeros(r_scr.shape, F32)
    acc_scr[...] = jnp.zeros(acc_scr.shape, F32)

    sign_bit = jnp.uint32(0x80000000)

    def tile(j, masked):
        start = pl.multiple_of(j * tk, tk)
        if masked:
            key = j * tk + lax.broadcasted_iota(jnp.int32, (tk, tq), 0)
            qry = i * tq + lax.broadcasted_iota(jnp.int32, (tk, tq), 1)
            mask = key < qry

        def scores(hh):
            lanes = slice(hh * HEAD_DIM, (hh + 1) * HEAD_DIM)
            kj = k_ref[pl.ds(start, tk), lanes]
            return lax.dot_general(kj, q_ref[:, lanes], NT_DIMS, preferred_element_type=F32)

        u_all = [scores(hh) for hh in range(hp)]
        for hh in range(hp):
            u = u_all[hh]
            minus_abs = lax.bitcast_convert_type(lax.bitcast_convert_type(u, jnp.uint32) | sign_bit, F32)
            sp = jnp.maximum(u, 0.0) + jnp.log2(1.0 + jnp.exp2(minus_abs))
            if masked:
                sp = jnp.where(mask, sp, 0.0)
            w = jnp.dot(tri_ref[...], sp.astype(BF16), preferred_element_type=F32)
            a = jnp.exp2(u - ((sp + w) + r_scr[hh]))
            if masked:
                a = jnp.where(mask, a, 0.0)
            vtj = vt_ref[hh * HEAD_DIM:(hh + 1) * HEAD_DIM, pl.ds(start, tk)]
            acc_scr[hh] += jnp.dot(vtj, a.astype(BF16), preferred_element_type=F32)
            r_scr[hh] += jnp.sum(sp, axis=0, keepdims=True)

    ratio = max(tq // tk, 1)
    n_full = (i * tq) // tk
    for d in reversed(range(ratio)):
        tile(n_full + d, True)

    def live():
        return (jnp.min(r_scr[...]) <= R_DONE).astype(jnp.int32)

    def cond(carry):
        it, alive = carry
        return (it < n_full) & (alive > 0)

    def body(carry):
        it, _ = carry
        tile(n_full - 1 - it, False)
        return it + 1, live()

    lax.while_loop(cond, body, (jnp.int32(0), live()))
    for hh in range(hp):
        o_ref[:, hh * HEAD_DIM:(hh + 1) * HEAD_DIM] = acc_scr[hh].T.astype(o_ref.dtype)


def _sb_attention(proj, vt, batch, seq, n_heads, col_q, col_k, row_v, tq, tk, hp):
    t = proj.shape[0]
    nq = seq // tq
    wb = hp * HEAD_DIM
    cq, ck, rv = col_q // wb, col_k // wb, row_v // wb
    tri = jnp.triu(jnp.ones((tk, tk), BF16), 1)
    return pl.pallas_call(
        functools.partial(_sb_kernel, tq=tq, tk=tk, hp=hp),
        out_shape=jax.ShapeDtypeStruct((t, n_heads * HEAD_DIM), BF16),
        grid=(batch, n_heads // hp, nq),
        in_specs=[
            pl.BlockSpec((tq, wb), lambda b, g, i: (b * nq + i, cq + g)),
            pl.BlockSpec((seq, wb), lambda b, g, i: (b, ck + g)),
            pl.BlockSpec((wb, seq), lambda b, g, i: (rv + g, b)),
            pl.BlockSpec((tk, tk), lambda b, g, i: (0, 0)),
        ],
        out_specs=pl.BlockSpec((tq, wb), lambda b, g, i: (b * nq + i, g)),
        scratch_shapes=[
            pltpu.VMEM((hp, 1, tq), F32),
            pltpu.VMEM((hp, HEAD_DIM, tq), F32),
        ],
        compiler_params=_params(("parallel", "parallel", "arbitrary")),
        name="sb_attention",
    )(proj, proj, vt, tri)


def _merge_kernel(oa_ref, ob_ref, wa_ref, wb_ref, ga_ref, gb_ref, o_ref):
    ya = jnp.dot(oa_ref[...], wa_ref[...], preferred_element_type=F32)
    yb = jnp.dot(ob_ref[...], wb_ref[...], preferred_element_type=F32)
    ga = jax.nn.sigmoid(ga_ref[...].astype(F32))
    gb = jax.nn.sigmoid(gb_ref[...].astype(F32))
    o_ref[...] = (ga * ya + gb * yb).astype(o_ref.dtype)


def _merge(o_a, o_b, wa, wb, proj, col_ga, col_gb, tm, tn):
    t, ka = o_a.shape
    kb = o_b.shape[1]
    d = wa.shape[1]
    ca, cb = col_ga // tn, col_gb // tn
    return pl.pallas_call(
        _merge_kernel,
        out_shape=jax.ShapeDtypeStruct((t, d), BF16),
        grid=(t // tm, d // tn),
        in_specs=[
            pl.BlockSpec((tm, ka), lambda m, j: (m, 0)),
            pl.BlockSpec((tm, kb), lambda m, j: (m, 0)),
            pl.BlockSpec((ka, tn), lambda m, j: (0, j)),
            pl.BlockSpec((kb, tn), lambda m, j: (0, j)),
            pl.BlockSpec((tm, tn), lambda m, j: (m, ca + j)),
            pl.BlockSpec((tm, tn), lambda m, j: (m, cb + j)),
        ],
        out_specs=pl.BlockSpec((tm, tn), lambda m, j: (m, j)),
        compiler_params=_params(("parallel", "arbitrary")),
        name="branch_merge",
    )(o_a, o_b, wa, wb, proj, proj)


def _mixout_kernel(a_ref, w_ref, x_ref, o_ref):
    o_ref[...] = x_ref[...] + jnp.dot(a_ref[...], w_ref[...], preferred_element_type=F32)


def _mixout(mixed, w, x2d, tm, tn):
    t, k = mixed.shape
    d = w.shape[1]
    return pl.pallas_call(
        _mixout_kernel,
        out_shape=jax.ShapeDtypeStruct((t, d), F32),
        grid=(t // tm, d // tn),
        in_specs=[
            pl.BlockSpec((tm, k), lambda m, j: (m, 0)),
            pl.BlockSpec((k, tn), lambda m, j: (0, j)),
            pl.BlockSpec((tm, tn), lambda m, j: (m, j)),
        ],
        out_specs=pl.BlockSpec((tm, tn), lambda m, j: (m, j)),
        compiler_params=_params(("parallel", "arbitrary")),
        name="mix_out",
    )(mixed, w, x2d)


def _router_kernel(x_ref, g_ref, wh_ref, wl_ref, lt_ref, sel_ref, h_ref, ids_ref, gate_ref, cnt_ref, carry_scr,
                   *, n_groups, per_group):
    @pl.when(pl.program_id(0) == 0)
    def _():
        carry_scr[...] = jnp.zeros(carry_scr.shape, F32)

    h = _rmsnorm_f32(x_ref[...], g_ref[...])
    hh = h.astype(BF16)
    _store_row_tiles(h_ref, _pack_bf16_pairs(h))
    hl = (h - hh.astype(F32)).astype(BF16)
    logits = (jnp.dot(hh, wh_ref[...], preferred_element_type=F32)
              + jnp.dot(hl, wh_ref[...], preferred_element_type=F32)
              + jnp.dot(hh, wl_ref[...], preferred_element_type=F32))
    tm = logits.shape[0]
    n_exp = n_groups * per_group
    lane = lax.broadcasted_iota(jnp.int32, (tm, LANES), 1)
    lane_f = lane.astype(F32)
    neg_inf = -jnp.inf

    def first_argmax(vals, valid):
        vmax = jnp.max(jnp.where(valid, vals, neg_inf), axis=-1, keepdims=True)
        idx = jnp.min(jnp.where(valid & (vals == vmax), lane_f, float(LANES)), axis=-1, keepdims=True)
        return vmax, idx.astype(jnp.int32)

    is_group = lane < n_groups
    gmax, gsel = first_argmax(logits, is_group)
    gsum = jnp.sum(jnp.where(is_group, jnp.exp(logits - gmax), 0.0), axis=-1, keepdims=True)
    p_group = 1.0 / gsum
    lo = n_groups + gsel * per_group
    in_group = (lane >= lo) & (lane < lo + per_group)
    v1, i1 = first_argmax(logits, in_group)
    v2, i2 = first_argmax(logits, in_group & (lane != i1))
    e2w = jnp.exp(v2 - v1)
    gate1 = (1.0 / (1.0 + e2w)) * p_group
    gate2 = (e2w / (1.0 + e2w)) * p_group
    e1 = i1 - n_groups
    e2 = i2 - n_groups

    onehot = ((lane == e1) | (lane == e2)).astype(BF16)
    before = jnp.dot(lt_ref[...], onehot, preferred_element_type=F32) + carry_scr[...]
    rank1 = jnp.sum(jnp.where(lane == e1, before, 0.0), axis=-1, keepdims=True).astype(jnp.int32)
    rank2 = jnp.sum(jnp.where(lane == e2, before, 0.0), axis=-1, keepdims=True).astype(jnp.int32)
    carry_new = carry_scr[...] + jnp.sum(onehot.astype(F32), axis=0, keepdims=True)
    carry_scr[...] = carry_new

    cols = jnp.where(lane == 0, e1, jnp.where(lane == 1, e2, jnp.where(lane == 2, rank1 >> 7,
           jnp.where(lane == 3, rank1 & 127, jnp.where(lane == 4, rank2 >> 7,
           jnp.where(lane == 5, rank2 & 127, 0))))))
    ids_ref[...] = lax.dot_general(sel_ref[...], cols.astype(F32).astype(BF16), NT_DIMS,
                                   preferred_element_type=F32)
    gate_ref[...] = jnp.where(lane == 0, gate1, jnp.where(lane == 1, gate2, 0.0))
    cnt_ref[...] = jnp.broadcast_to(carry_new, cnt_ref.shape).astype(jnp.int32)
    del n_exp


def _router(x2, g, w_hi, w_lo, tm, n_groups, per_group):
    t, d = x2.shape
    lt = jnp.tril(jnp.ones((tm, tm), BF16), -1)
    sel = jnp.eye(8, LANES, dtype=BF16)
    return pl.pallas_call(
        functools.partial(_router_kernel, n_groups=n_groups, per_group=per_group),
        out_shape=(
            jax.ShapeDtypeStruct((t * (d // 2 // LANES), LANES), jnp.uint32),
            jax.ShapeDtypeStruct((8, t), F32),
            jax.ShapeDtypeStruct((t, LANES), F32),
            jax.ShapeDtypeStruct((8, LANES), jnp.int32),
        ),
        grid=(t // tm,),
        in_specs=[
            pl.BlockSpec((tm, d), lambda m: (m, 0)),
            pl.BlockSpec((1, d), lambda m: (0, 0)),
            pl.BlockSpec((d, LANES), lambda m: (0, 0)),
            pl.BlockSpec((d, LANES), lambda m: (0, 0)),
            pl.BlockSpec((tm, tm), lambda m: (0, 0)),
            pl.BlockSpec((8, LANES), lambda m: (0, 0)),
        ],
        out_specs=(
            pl.BlockSpec((tm * (d // 2 // LANES), LANES), lambda m: (m, 0)),
            pl.BlockSpec((8, tm), lambda m: (0, m)),
            pl.BlockSpec((tm, LANES), lambda m: (m, 0)),
            pl.BlockSpec((8, LANES), lambda m: (0, 0)),
        ),
        scratch_shapes=[pltpu.VMEM((1, LANES), F32)],
        compiler_params=_params(("arbitrary",)),
        name="router",
    )(x2, g, w_hi, w_lo, lt, sel)


def _plan_kernel(cnt_ref, ids_ref, dest_ref, be_ref, nu_ref, ps_scr, *, n_exp, bm, nb):
    shift = bm.bit_length() - 1

    def per_expert(e, carry):
        blk_start, _ = carry
        n = lax.shift_right_logical(cnt_ref[0, e] + (bm - 1), shift)
        ps_scr[e] = blk_start * bm

        def fill(k, c):
            be_ref[blk_start + k] = e
            return c

        lax.fori_loop(0, n, fill, 0)
        return blk_start + n, jnp.where(n > 0, e, carry[1])

    n_used, last = lax.fori_loop(0, n_exp, per_expert, (jnp.int32(0), jnp.int32(0)))
    nu_ref[0] = n_used

    def tail(k, c):
        be_ref[k] = last
        return c

    lax.fori_loop(n_used, nb, tail, 0)

    e1 = ids_ref[0:1, :]
    e2 = ids_ref[1:2, :]
    p1 = jnp.zeros_like(e1)
    p2 = jnp.zeros_like(e2)
    for e in range(n_exp):
        ps = ps_scr[e].astype(F32)
        p1 = jnp.where(e1 == e, ps, p1)
        p2 = jnp.where(e2 == e, ps, p2)
    dest_ref[...] = jnp.zeros(dest_ref.shape, jnp.int32)
    dest_ref[0:1, :] = (p1 + ids_ref[2:3, :] * 128.0 + ids_ref[3:4, :]).astype(jnp.int32)
    dest_ref[1:2, :] = (p2 + ids_ref[4:5, :] * 128.0 + ids_ref[5:6, :]).astype(jnp.int32)


def _plan(counts, ids_rows, n_exp, bm, nb):
    t = ids_rows.shape[1]
    return pl.pallas_call(
        functools.partial(_plan_kernel, n_exp=n_exp, bm=bm, nb=nb),
        out_shape=(
            jax.ShapeDtypeStruct((8, t), jnp.int32),
            jax.ShapeDtypeStruct((nb,), jnp.int32),
            jax.ShapeDtypeStruct((1,), jnp.int32),
        ),
        in_specs=[
            pl.BlockSpec(memory_space=pltpu.SMEM),
            pl.BlockSpec(memory_space=pltpu.VMEM),
        ],
        out_specs=(
            pl.BlockSpec(memory_space=pltpu.VMEM),
            pl.BlockSpec(memory_space=pltpu.SMEM),
            pl.BlockSpec(memory_space=pltpu.SMEM),
        ),
        scratch_shapes=[pltpu.SMEM((n_exp,), jnp.int32)],
        name="moe_plan",
    )(counts, ids_rows)


def _dispatch_kernel(d1_ref, d2_ref, h_ref, xb_in_ref, xb_ref, sem, *, tm, ns):
    del xb_in_ref
    base = pl.program_id(0) * tm

    def row_copy(r, dest):
        src = h_ref.at[pl.ds(pl.multiple_of(r * ns, ns), ns)]
        return pltpu.make_async_copy(src, xb_ref.at[pl.ds(pl.multiple_of(dest * ns, ns), ns)], sem)

    def start(r, carry):
        row_copy(r, d1_ref[base + r]).start()
        row_copy(r, d2_ref[base + r]).start()
        return carry

    lax.fori_loop(0, tm, start, 0, unroll=DMA_UNROLL)
    for _ in range(2):
        pltpu.make_async_copy(h_ref, xb_ref.at[pl.ds(0, tm * ns)], sem).wait()


def _dispatch(h2p, dest1, dest2, p_rows, tm, ns):
    t = h2p.shape[0] // ns
    xb0 = jnp.zeros((p_rows * ns, LANES), h2p.dtype)
    return pl.pallas_call(
        functools.partial(_dispatch_kernel, tm=tm, ns=ns),
        out_shape=jax.ShapeDtypeStruct((p_rows * ns, LANES), h2p.dtype),
        grid_spec=pltpu.PrefetchScalarGridSpec(
            num_scalar_prefetch=2,
            grid=(t // tm,),
            in_specs=[
                pl.BlockSpec((tm * ns, LANES), lambda m, d1, d2: (m, 0)),
                pl.BlockSpec(memory_space=pl.ANY),
            ],
            out_specs=pl.BlockSpec(memory_space=pl.ANY),
            scratch_shapes=[pltpu.SemaphoreType.DMA],
        ),
        input_output_aliases={3: 0},
        compiler_params=_params(("arbitrary",)),
        name="moe_dispatch",
    )(dest1, dest2, h2p, xb0)


def _experts_kernel(be_ref, nu_ref, x_ref, wg_ref, wu_ref, wd_ref, y_ref, wg_scr, wu_scr, wd_scr, *, ns):
    b = pl.program_id(0)
    prev = be_ref[jnp.maximum(b - 1, 0)]
    fresh = (b == 0) | (be_ref[b] != prev)

    @pl.when(fresh & (b < nu_ref[0]))
    def _():
        wg_scr[...] = wg_ref[0].astype(BF16)
        wu_scr[...] = wu_ref[0].astype(BF16)
        wd_scr[...] = wd_ref[0].astype(BF16)

    @pl.when(b < nu_ref[0])
    def _():
        x = _unpack_bf16_pairs(_load_row_tiles(x_ref, ns)).astype(BF16)
        g = jnp.dot(x, wg_scr[...], preferred_element_type=F32)
        u = jnp.dot(x, wu_scr[...], preferred_element_type=F32)
        hid = (g * jax.nn.sigmoid(g) * u).astype(BF16)
        _store_row_tiles(y_ref, _pack_bf16_pairs(jnp.dot(hid, wd_scr[...], preferred_element_type=F32)))


def _experts(xb, w_gate, w_up, w_down, block_expert, n_used, bm, ns):
    d, de = w_gate.shape[1:]
    nb = xb.shape[0] // (bm * ns)

    def row_map(b, be, nu):
        return (jnp.minimum(b, nu[0] - 1), 0)

    def w_map(b, be, nu):
        return (be[b], 0, 0)

    return pl.pallas_call(
        functools.partial(_experts_kernel, ns=ns),
        out_shape=jax.ShapeDtypeStruct(xb.shape, jnp.uint32),
        grid_spec=pltpu.PrefetchScalarGridSpec(
            num_scalar_prefetch=2,
            grid=(nb,),
            in_specs=[
                pl.BlockSpec((bm * ns, LANES), row_map),
                pl.BlockSpec((1, d, de), w_map),
                pl.BlockSpec((1, d, de), w_map),
                pl.BlockSpec((1, de, d), w_map),
            ],
            out_specs=pl.BlockSpec((bm * ns, LANES), row_map),
            scratch_shapes=[
                pltpu.VMEM((d, de), BF16),
                pltpu.VMEM((d, de), BF16),
                pltpu.VMEM((de, d), BF16),
            ],
        ),
        input_output_aliases={2: 0},
        compiler_params=_params(("arbitrary",)),
        name="moe_experts",
    )(block_expert, n_used, xb, w_gate, w_up, w_down)


def _combine_kernel(d1_ref, d2_ref, x_ref, gate_ref, yb_ref, o_ref, buf, sem, *, tm, ns):
    base = pl.program_id(0) * tm

    def row_copy(r, dest, slot):
        dst = buf.at[slot, pl.ds(pl.multiple_of(r * ns, ns), ns)]
        return pltpu.make_async_copy(yb_ref.at[pl.ds(pl.multiple_of(dest * ns, ns), ns)], dst, sem)

    def start(r, carry):
        row_copy(r, d1_ref[base + r], 0).start()
        row_copy(r, d2_ref[base + r], 1).start()
        return carry

    lax.fori_loop(0, tm, start, 0, unroll=DMA_UNROLL)
    for slot in range(2):
        pltpu.make_async_copy(yb_ref.at[pl.ds(0, tm * ns)], buf.at[slot], sem).wait()
    gates = gate_ref[...]
    y1 = _unpack_bf16_pairs(_load_row_tiles(buf.at[0], ns))
    y2 = _unpack_bf16_pairs(_load_row_tiles(buf.at[1], ns))
    o_ref[...] = x_ref[...] + gates[:, 0:1] * y1 + gates[:, 1:2] * y2


def _combine(x2, gates, yb, dest1, dest2, tm, ns):
    t, d = x2.shape
    return pl.pallas_call(
        functools.partial(_combine_kernel, tm=tm, ns=ns),
        out_shape=jax.ShapeDtypeStruct((t, d), F32),
        grid_spec=pltpu.PrefetchScalarGridSpec(
            num_scalar_prefetch=2,
            grid=(t // tm,),
            in_specs=[
                pl.BlockSpec((tm, d), lambda m, d1, d2: (m, 0)),
                pl.BlockSpec((tm, LANES), lambda m, d1, d2: (m, 0)),
                pl.BlockSpec(memory_space=pl.ANY),
            ],
            out_specs=pl.BlockSpec((tm, d), lambda m, d1, d2: (m, 0)),
            scratch_shapes=[pltpu.VMEM((2, tm * ns, LANES), yb.dtype), pltpu.SemaphoreType.DMA],
        ),
        compiler_params=_params(("arbitrary",)),
        name="moe_combine",
    )(dest1, dest2, x2, gates, yb)


def _ple_kernel(x_ref, g_ref, p_ref, wg_ref, wp_ref, gf_ref, o_ref, h_scr, p_scr, *, tn):
    j = pl.program_id(1)

    @pl.when(j == 0)
    def _():
        h_scr[...] = _rmsnorm_f32(x_ref[...], g_ref[...]).astype(BF16)
        p_scr[...] = p_ref[...].astype(BF16)

    col = pl.multiple_of(j * tn, tn)
    gate = jnp.dot(h_scr[...], wg_ref[...], preferred_element_type=F32)
    ple = jnp.dot(p_scr[...], wp_ref[...], preferred_element_type=F32)
    o_ref[:, pl.ds(col, tn)] = x_ref[:, pl.ds(col, tn)] + jax.nn.sigmoid(gate) * ple

    @pl.when(j == pl.num_programs(1) - 1)
    def _():
        o_ref[...] = _rmsnorm_f32(o_ref[...], gf_ref[...])


def _ple_final(x3, g_ple, p2d, w_gate, w_proj, g_final, tm, tn):
    t, d = x3.shape
    dp = p2d.shape[1]
    return pl.pallas_call(
        functools.partial(_ple_kernel, tn=tn),
        out_shape=jax.ShapeDtypeStruct((t, d), F32),
        grid=(t // tm, d // tn),
        in_specs=[
            pl.BlockSpec((tm, d), lambda m, j: (m, 0)),
            pl.BlockSpec((1, d), lambda m, j: (0, 0)),
            pl.BlockSpec((tm, dp), lambda m, j: (m, 0)),
            pl.BlockSpec((d, tn), lambda m, j: (0, j)),
            pl.BlockSpec((dp, tn), lambda m, j: (0, j)),
            pl.BlockSpec((1, d), lambda m, j: (0, 0)),
        ],
        out_specs=pl.BlockSpec((tm, d), lambda m, j: (m, 0)),
        scratch_shapes=[pltpu.VMEM((tm, d), BF16), pltpu.VMEM((tm, dp), BF16)],
        compiler_params=_params(("parallel", "arbitrary")),
        name="ple_final",
    )(x3, g_ple, p2d, w_gate, w_proj, g_final)


def _tile(n, pref):
    return pref if n % pref == 0 else n


def _layer(x2d, p2d, batch, seq, w_in, b_forget, w_branch_fox, w_branch_sb, w_mix_out, g_mix, g_ffn,
           w_group, w_expert, w_gate, w_up, w_down, g_ple, w_ple_proj, w_ple_gate, g_final):
    t, d = x2d.shape
    n_heads = b_forget.shape[0]
    w_att = n_heads * HEAD_DIM
    n_exp = w_expert.shape[1]

    qs = LOG2E * HEAD_DIM ** -0.5
    off_b = 3 * w_att + n_heads
    w_main = jnp.concatenate([(w_in[:, :w_att] * qs).astype(BF16), w_in[:, w_att:2 * w_att].astype(BF16),
                              (w_in[:, off_b:off_b + w_att] * qs).astype(BF16),
                              w_in[:, off_b + w_att:off_b + 2 * w_att].astype(BF16),
                              w_in[:, off_b + 3 * w_att:].astype(BF16)], axis=1)
    wv_t = jnp.concatenate([w_in[:, 2 * w_att:3 * w_att].astype(BF16).T,
                            w_in[:, off_b + 2 * w_att:off_b + 3 * w_att].astype(BF16).T], axis=0)
    wf = jnp.pad(w_in[:, 3 * w_att:off_b], ((0, 0), (0, LANES - n_heads))).astype(BF16)
    b_row = jnp.pad(b_forget, (0, LANES - n_heads)).reshape(1, LANES)
    col_qa, col_ka, col_qb, col_kb = 0, w_att, 2 * w_att, 3 * w_att
    col_ga, col_gb = 4 * w_att, 4 * w_att + d
    row_va, row_vb = 0, w_att

    tm = _tile(t, 1024)
    tn = _tile(d, 512)
    proj, vt, f_tok = _inproj(x2d, g_mix.reshape(1, d), w_main, wv_t, wf, tm, 512)
    c = _forget_cumsum(f_tok, b_row, batch, n_heads)

    tq = _tile(seq, 512)
    hp = 4
    o_a = _fox_attention(proj, vt, c, batch, seq, n_heads, col_qa, col_ka, row_va, tq, hp)
    o_b = _sb_attention(proj, vt, batch, seq, n_heads, col_qb, col_kb, row_vb, tq, _tile(seq, 256), hp)

    mixed = _merge(o_a, o_b, w_branch_fox.astype(BF16), w_branch_sb.astype(BF16), proj, col_ga, col_gb, tm, tn)
    x2 = _mixout(mixed, w_mix_out.astype(BF16), x2d, tm, tn)

    w_r = jnp.concatenate([w_group, w_expert], axis=1)
    w_r = jnp.pad(w_r, ((0, 0), (0, LANES - w_r.shape[1])))
    w_r_hi = w_r.astype(BF16)
    w_r_lo = (w_r - w_r_hi.astype(F32)).astype(BF16)
    tr = _tile(t, 512)
    h2, ids_rows, gates, counts = _router(x2, g_ffn.reshape(1, d), w_r_hi, w_r_lo, tr, N_GROUPS, EXPERTS_PER_GROUP)

    bm = 256
    n_assign = 2 * t
    nb = n_assign // bm + n_exp
    dest, block_expert, n_used = _plan(counts, ids_rows, n_exp, bm, nb)
    dest1, dest2 = dest[0], dest[1]

    td = _tile(t, 256)
    ns = d // 2 // LANES
    xb = _dispatch(h2, dest1, dest2, nb * bm, td, ns)
    yb = _experts(xb, w_gate, w_up, w_down, block_expert, n_used, bm, ns)
    x3 = _combine(x2, gates, yb, dest1, dest2, td, ns)

    tp = _tile(t, 512)
    return _ple_final(x3, g_ple.reshape(1, d), p2d, w_ple_gate.astype(BF16), w_ple_proj.astype(BF16),
                      g_final.reshape(1, d), tp, tn)


def kernel(x, p, w_in, b_forget, w_branch_fox, w_branch_sb, w_mix_out, g_mix, g_ffn, w_group, w_expert,
           w_gate, w_up, w_down, g_ple, w_ple_proj, w_ple_gate, g_final):
    b, s, d = x.shape
    depth = w_in.shape[0]
    assert depth == 1, "the final norm is fused into the single layer"
    x2d = x.reshape(b * s, d)
    out = _layer(x2d, p[0].reshape(b * s, -1), b, s, w_in[0], b_forget[0], w_branch_fox[0], w_branch_sb[0],
                 w_mix_out[0], g_mix[0], g_ffn[0], w_group[0], w_expert[0], w_gate[0], w_up[0], w_down[0],
                 g_ple[0], w_ple_proj[0], w_ple_gate[0], g_final)
    return out.reshape(b, s, d)
```

```python
import functools

import jax
import jax.numpy as jnp
from jax import lax
from jax.experimental import pallas as pl
from jax.experimental.pallas import tpu as pltpu

F32 = jnp.float32
BF16 = jnp.bfloat16

HEAD_DIM = 128
N_GROUPS = 4
EXPERTS_PER_GROUP = 8
EPS = 1e-6
LANES = 128
SUBLANES = 8
VMEM_LIMIT = 56 * 1024 * 1024

LOG2E = 1.4426950408889634
DMA_UNROLL = 8
R_DONE = 160.0

NT_DIMS = (((1,), (1,)), ((), ()))


def _params(sem):
    return pltpu.CompilerParams(dimension_semantics=sem, vmem_limit_bytes=VMEM_LIMIT)


def _rmsnorm_f32(x, g):
    ms = jnp.mean(x * x, axis=-1, keepdims=True)
    return x * lax.rsqrt(ms + EPS) * g


def _pack_bf16_pairs(x):
    n = x.shape[1] // 2
    bits = lax.bitcast_convert_type(x.astype(BF16).astype(F32), jnp.uint32)
    return (bits[:, :n] >> 16) | (bits[:, n:] & jnp.uint32(0xFFFF0000))


def _unpack_bf16_pairs(w):
    lo = lax.bitcast_convert_type(w << 16, F32)
    hi = lax.bitcast_convert_type(w & jnp.uint32(0xFFFF0000), F32)
    return jnp.concatenate([lo, hi], axis=1)


def _store_row_tiles(ref, words):
    n = words.shape[0]
    ns = words.shape[1] // LANES
    for c in range(ns):
        ref[pl.ds(c, n, stride=ns), :] = words[:, c * LANES:(c + 1) * LANES]


def _load_row_tiles(ref, ns):
    n = ref.shape[0] // ns
    return jnp.concatenate([ref[pl.ds(c, n, stride=ns), :] for c in range(ns)], axis=1)


def _inproj_kernel(x_ref, g_ref, w_ref, wv_ref, wf_ref, o_ref, vt_ref, f_ref, h_scr, *, nvt):
    j = pl.program_id(1)

    @pl.when(j == 0)
    def _():
        hb = _rmsnorm_f32(x_ref[...], g_ref[...]).astype(BF16)
        h_scr[...] = hb
        f_ref[...] = jnp.dot(hb, wf_ref[...], preferred_element_type=F32)

    @pl.when(j < nvt)
    def _():
        vt_ref[...] = lax.dot_general(wv_ref[...], h_scr[...], NT_DIMS,
                                      preferred_element_type=F32).astype(vt_ref.dtype)

    @pl.when(j >= nvt)
    def _():
        o_ref[...] = jnp.dot(h_scr[...], w_ref[...], preferred_element_type=F32).astype(o_ref.dtype)


def _inproj(x2d, g, w, wv_t, wf, tm, tn):
    t, d = x2d.shape
    n = w.shape[1]
    nv = wv_t.shape[0]
    nvt = nv // tn
    return pl.pallas_call(
        functools.partial(_inproj_kernel, nvt=nvt),
        out_shape=(jax.ShapeDtypeStruct((t, n), BF16), jax.ShapeDtypeStruct((nv, t), BF16),
                   jax.ShapeDtypeStruct((t, LANES), F32)),
        grid=(t // tm, nvt + n // tn),
        in_specs=[
            pl.BlockSpec((tm, d), lambda m, j: (m, 0)),
            pl.BlockSpec((1, d), lambda m, j: (0, 0)),
            pl.BlockSpec((d, tn), lambda m, j: (0, jnp.maximum(j - nvt, 0))),
            pl.BlockSpec((tn, d), lambda m, j: (jnp.minimum(j, nvt - 1), 0)),
            pl.BlockSpec((d, LANES), lambda m, j: (0, 0)),
        ],
        out_specs=(
            pl.BlockSpec((tm, tn), lambda m, j: (m, jnp.maximum(j - nvt, 0))),
            pl.BlockSpec((tn, tm), lambda m, j: (jnp.minimum(j, nvt - 1), m)),
            pl.BlockSpec((tm, LANES), lambda m, j: (m, 0)),
        ),
        scratch_shapes=[pltpu.VMEM((tm, d), BF16)],
        compiler_params=_params(("parallel", "arbitrary")),
        name="inproj",
    )(x2d, g, w, wv_t, wf)


def _split3(x):
    p1 = x.astype(BF16)
    r1 = x - p1.astype(F32)
    p2 = r1.astype(BF16)
    p3 = (r1 - p2.astype(F32)).astype(BF16)
    return p1, p2, p3


def _cumsum_kernel(f_ref, b_ref, lt_ref, c_ref, carry_scr, *, chunk, n_heads):
    @pl.when(pl.program_id(1) == 0)
    def _():
        carry_scr[...] = jnp.zeros(carry_scr.shape, F32)

    rows = f_ref.shape[0]
    lt = lt_ref[...]
    carry = carry_scr[...]
    for i in range(rows // chunk):
        lf = jax.nn.log_sigmoid(f_ref[i * chunk:(i + 1) * chunk, :] + b_ref[...])
        p1, p2, p3 = _split3(lf)
        cs = (jnp.dot(lt, p1, preferred_element_type=F32)
              + jnp.dot(lt, p2, preferred_element_type=F32)
              + jnp.dot(lt, p3, preferred_element_type=F32)) + carry
        for h in range(n_heads):
            c_ref[h, i * chunk:(i + 1) * chunk, :] = jnp.broadcast_to(cs[:, h:h + 1] * LOG2E, (chunk, LANES))
        carry = cs[chunk - 1:chunk, :]
    carry_scr[...] = carry


def _forget_cumsum(f_tok, b_row, batch, n_heads):
    t = f_tok.shape[0]
    s = t // batch
    rows = min(1024, s)
    chunk = min(256, s)
    lt = jnp.tril(jnp.ones((chunk, chunk), BF16))
    nr = s // rows
    return pl.pallas_call(
        functools.partial(_cumsum_kernel, chunk=chunk, n_heads=n_heads),
        out_shape=jax.ShapeDtypeStruct((n_heads, t, LANES), F32),
        grid=(batch, nr),
        in_specs=[
            pl.BlockSpec((rows, LANES), lambda b, r: (b * nr + r, 0)),
            pl.BlockSpec((1, LANES), lambda b, r: (0, 0)),
            pl.BlockSpec((chunk, chunk), lambda b, r: (0, 0)),
        ],
        out_specs=pl.BlockSpec((n_heads, rows, LANES), lambda b, r: (0, b * nr + r, 0)),
        scratch_shapes=[pltpu.VMEM((1, LANES), F32)],
        compiler_params=_params(("parallel", "arbitrary")),
        name="forget_cumsum",
    )(f_tok, b_row, lt)


def _fox_kernel(q_ref, k_ref, vt_ref, c_ref, o_ref, m_scr, l_scr, acc_scr, *, tq, tk, hp):
    i = pl.program_id(2)
    m_scr[...] = jnp.full(m_scr.shape, -jnp.inf, F32)
    l_scr[...] = jnp.zeros(l_scr.shape, F32)
    acc_scr[...] = jnp.zeros(acc_scr.shape, F32)

    def tile(j, masked):
        start = pl.multiple_of(j * tk, tk)
        if masked:
            key = j * tk + lax.broadcasted_iota(jnp.int32, (tk, tq), 0)
            qry = i * tq + lax.broadcasted_iota(jnp.int32, (tk, tq), 1)
            keep = key <= qry
        def scores(hh):
            lanes = slice(hh * HEAD_DIM, (hh + 1) * HEAD_DIM)
            kj = k_ref[pl.ds(start, tk), lanes]
            return lax.dot_general(kj, q_ref[:, lanes], NT_DIMS, preferred_element_type=F32)

        s_all = [scores(hh) for hh in range(hp)]
        for hh in range(hp):
            s = s_all[hh]
            cj = c_ref[hh, pl.ds(start, tk), :]
            s = s - jnp.concatenate([cj] * (tq // LANES), axis=1)
            if masked:
                s = jnp.where(keep, s, -jnp.inf)
            m_prev = m_scr[hh]
            m_new = jnp.maximum(m_prev, jnp.max(s, axis=0, keepdims=True))
            alpha = jnp.exp2(m_prev - m_new)
            p = jnp.exp2(s - m_new)
            l_scr[hh] = alpha * l_scr[hh] + jnp.sum(p, axis=0, keepdims=True)
            vtj = vt_ref[hh * HEAD_DIM:(hh + 1) * HEAD_DIM, pl.ds(start, tk)]
            acc_scr[hh] = alpha * acc_scr[hh] + jnp.dot(vtj, p.astype(BF16), preferred_element_type=F32)
            m_scr[hh] = m_new

    def body(j, carry):
        tile(j, False)
        return carry

    lax.fori_loop(0, i, body, 0)
    tile(i, True)
    for hh in range(hp):
        o = acc_scr[hh] / l_scr[hh]
        o_ref[:, hh * HEAD_DIM:(hh + 1) * HEAD_DIM] = o.T.astype(o_ref.dtype)


def _fox_attention(proj, vt, c, batch, seq, n_heads, col_q, col_k, row_v, tq, hp):
    t = proj.shape[0]
    nq = seq // tq
    wb = hp * HEAD_DIM
    cq, ck, rv = col_q // wb, col_k // wb, row_v // wb
    return pl.pallas_call(
        functools.partial(_fox_kernel, tq=tq, tk=tq, hp=hp),
        out_shape=jax.ShapeDtypeStruct((t, n_heads * HEAD_DIM), BF16),
        grid=(batch, n_heads // hp, nq),
        in_specs=[
            pl.BlockSpec((tq, wb), lambda b, g, i: (b * nq + i, cq + g)),
            pl.BlockSpec((seq, wb), lambda b, g, i: (b, ck + g)),
            pl.BlockSpec((wb, seq), lambda b, g, i: (rv + g, b)),
            pl.BlockSpec((hp, seq, LANES), lambda b, g, i: (g, b, 0)),
        ],
        out_specs=pl.BlockSpec((tq, wb), lambda b, g, i: (b * nq + i, g)),
        scratch_shapes=[
            pltpu.VMEM((hp, 1, tq), F32),
            pltpu.VMEM((hp, 1, tq), F32),
            pltpu.VMEM((hp, HEAD_DIM, tq), F32),
        ],
        compiler_params=_params(("parallel", "parallel", "arbitrary")),
        name="fox_attention",
    )(proj, proj, vt, c)


def _sb_kernel(q_ref, k_ref, vt_ref, tri_ref, o_ref, r_scr, acc_scr, *, tq, tk, hp):
    i = pl.program_id(2)
    r_scr[...] = jnp.zeros(r_scr.shape, F32)
    acc_scr[...] = jnp.zeros(acc_scr.shape, F32)

    sign_bit = jnp.uint32(0x80000000)

    def tile(j, masked):
        start = pl.multiple_of(j * tk, tk)
        if masked:
            key = j * tk + lax.broadcasted_iota(jnp.int32, (tk, tq), 0)
            qry = i * tq + lax.broadcasted_iota(jnp.int32, (tk, tq), 1)
            mask = key < qry

        def scores(hh):
            lanes = slice(hh * HEAD_DIM, (hh + 1) * HEAD_DIM)
            kj = k_ref[pl.ds(start, tk), lanes]
            return lax.dot_general(kj, q_ref[:, lanes], NT_DIMS, preferred_element_type=F32)

        u_all = [scores(hh) for hh in range(hp)]
        for hh in range(hp):
            u = u_all[hh]
            minus_abs = lax.bitcast_convert_type(lax.bitcast_convert_type(u, jnp.uint32) | sign_bit, F32)
            sp = jnp.maximum(u, 0.0) + jnp.log2(1.0 + jnp.exp2(minus_abs))
            if masked:
                sp = jnp.where(mask, sp, 0.0)
            w = jnp.dot(tri_ref[...], sp.astype(BF16), preferred_element_type=F32)
            a = jnp.exp2(u - ((sp + w) + r_scr[hh]))
            if masked:
                a = jnp.where(mask, a, 0.0)
            vtj = vt_ref[hh * HEAD_DIM:(hh + 1) * HEAD_DIM, pl.ds(start, tk)]
            acc_scr[hh] += jnp.dot(vtj, a.astype(BF16), preferred_element_type=F32)
            r_scr[hh] += jnp.sum(sp, axis=0, keepdims=True)

    ratio = max(tq // tk, 1)
    n_full = (i * tq) // tk
    for d in reversed(range(ratio)):
        tile(n_full + d, True)

    def live():
        return (jnp.min(r_scr[...]) <= R_DONE).astype(jnp.int32)

    def cond(carry):
        it, alive = carry
        return (it < n_full) & (alive > 0)

    def body(carry):
        it, _ = carry
        tile(n_full - 1 - it, False)
        return it + 1, live()

    lax.while_loop(cond, body, (jnp.int32(0), live()))
    for hh in range(hp):
        o_ref[:, hh * HEAD_DIM:(hh + 1) * HEAD_DIM] = acc_scr[hh].T.astype(o_ref.dtype)


def _sb_attention(proj, vt, batch, seq, n_heads, col_q, col_k, row_v, tq, tk, hp):
    t = proj.shape[0]
    nq = seq // tq
    wb = hp * HEAD_DIM
    cq, ck, rv = col_q // wb, col_k // wb, row_v // wb
    tri = jnp.triu(jnp.ones((tk, tk), BF16), 1)
    return pl.pallas_call(
        functools.partial(_sb_kernel, tq=tq, tk=tk, hp=hp),
        out_shape=jax.ShapeDtypeStruct((t, n_heads * HEAD_DIM), BF16),
        grid=(batch, n_heads // hp, nq),
        in_specs=[
            pl.BlockSpec((tq, wb), lambda b, g, i: (b * nq + i, cq + g)),
            pl.BlockSpec((seq, wb), lambda b, g, i: (b, ck + g)),
            pl.BlockSpec((wb, seq), lambda b, g, i: (rv + g, b)),
            pl.BlockSpec((tk, tk), lambda b, g, i: (0, 0)),
        ],
        out_specs=pl.BlockSpec((tq, wb), lambda b, g, i: (b * nq + i, g)),
        scratch_shapes=[
            pltpu.VMEM((hp, 1, tq), F32),
            pltpu.VMEM((hp, HEAD_DIM, tq), F32),
        ],
        compiler_params=_params(("parallel", "parallel", "arbitrary")),
        name="sb_attention",
    )(proj, proj, vt, tri)


def _merge_kernel(oa_ref, ob_ref, wa_ref, wb_ref, ga_ref, gb_ref, o_ref):
    ya = jnp.dot(oa_ref[...], wa_ref[...].astype(BF16), preferred_element_type=F32)
    yb = jnp.dot(ob_ref[...], wb_ref[...].astype(BF16), preferred_element_type=F32)
    ga = jax.nn.sigmoid(ga_ref[...].astype(F32))
    gb = jax.nn.sigmoid(gb_ref[...].astype(F32))
    o_ref[...] = (ga * ya + gb * yb).astype(o_ref.dtype)


def _merge(o_a, o_b, wa, wb, proj, col_ga, col_gb, tm, tn):
    t, ka = o_a.shape
    kb = o_b.shape[1]
    d = wa.shape[1]
    ca, cb = col_ga // tn, col_gb // tn
    return pl.pallas_call(
        _merge_kernel,
        out_shape=jax.ShapeDtypeStruct((t, d), BF16),
        grid=(t // tm, d // tn),
        in_specs=[
            pl.BlockSpec((tm, ka), lambda m, j: (m, 0)),
            pl.BlockSpec((tm, kb), lambda m, j: (m, 0)),
            pl.BlockSpec((ka, tn), lambda m, j: (0, j)),
            pl.BlockSpec((kb, tn), lambda m, j: (0, j)),
            pl.BlockSpec((tm, tn), lambda m, j: (m, ca + j)),
            pl.BlockSpec((tm, tn), lambda m, j: (m, cb + j)),
        ],
        out_specs=pl.BlockSpec((tm, tn), lambda m, j: (m, j)),
        compiler_params=_params(("parallel", "arbitrary")),
        name="branch_merge",
    )(o_a, o_b, wa, wb, proj, proj)


def _mixout_kernel(a_ref, w_ref, x_ref, o_ref):
    o_ref[...] = x_ref[...] + jnp.dot(a_ref[...], w_ref[...].astype(BF16), preferred_element_type=F32)


def _mixout(mixed, w, x2d, tm, tn):
    t, k = mixed.shape
    d = w.shape[1]
    return pl.pallas_call(
        _mixout_kernel,
        out_shape=jax.ShapeDtypeStruct((t, d), F32),
        grid=(t // tm, d // tn),
        in_specs=[
            pl.BlockSpec((tm, k), lambda m, j: (m, 0)),
            pl.BlockSpec((k, tn), lambda m, j: (0, j)),
            pl.BlockSpec((tm, tn), lambda m, j: (m, j)),
        ],
        out_specs=pl.BlockSpec((tm, tn), lambda m, j: (m, j)),
        compiler_params=_params(("parallel", "arbitrary")),
        name="mix_out",
    )(mixed, w, x2d)


def _router_kernel(x_ref, g_ref, wh_ref, wl_ref, lt_ref, sel_ref, h_ref, ids_ref, gate_ref, cnt_ref, carry_scr,
                   *, n_groups, per_group):
    @pl.when(pl.program_id(0) == 0)
    def _():
        carry_scr[...] = jnp.zeros(carry_scr.shape, F32)

    h = _rmsnorm_f32(x_ref[...], g_ref[...])
    hh = h.astype(BF16)
    _store_row_tiles(h_ref, _pack_bf16_pairs(h))
    hl = (h - hh.astype(F32)).astype(BF16)
    logits = (jnp.dot(hh, wh_ref[...], preferred_element_type=F32)
              + jnp.dot(hl, wh_ref[...], preferred_element_type=F32)
              + jnp.dot(hh, wl_ref[...], preferred_element_type=F32))
    tm = logits.shape[0]
    n_exp = n_groups * per_group
    lane = lax.broadcasted_iota(jnp.int32, (tm, LANES), 1)
    lane_f = lane.astype(F32)
    neg_inf = -jnp.inf

    def first_argmax(vals, valid):
        vmax = jnp.max(jnp.where(valid, vals, neg_inf), axis=-1, keepdims=True)
        idx = jnp.min(jnp.where(valid & (vals == vmax), lane_f, float(LANES)), axis=-1, keepdims=True)
        return vmax, idx.astype(jnp.int32)

    is_group = lane < n_groups
    gmax, gsel = first_argmax(logits, is_group)
    gsum = jnp.sum(jnp.where(is_group, jnp.exp(logits - gmax), 0.0), axis=-1, keepdims=True)
    p_group = 1.0 / gsum
    lo = n_groups + gsel * per_group
    in_group = (lane >= lo) & (lane < lo + per_group)
    v1, i1 = first_argmax(logits, in_group)
    v2, i2 = first_argmax(logits, in_group & (lane != i1))
    e2w = jnp.exp(v2 - v1)
    gate1 = (1.0 / (1.0 + e2w)) * p_group
    gate2 = (e2w / (1.0 + e2w)) * p_group
    e1 = i1 - n_groups
    e2 = i2 - n_groups

    onehot = ((lane == e1) | (lane == e2)).astype(BF16)
    before = jnp.dot(lt_ref[...], onehot, preferred_element_type=F32) + carry_scr[...]
    rank1 = jnp.sum(jnp.where(lane == e1, before, 0.0), axis=-1, keepdims=True).astype(jnp.int32)
    rank2 = jnp.sum(jnp.where(lane == e2, before, 0.0), axis=-1, keepdims=True).astype(jnp.int32)
    carry_new = carry_scr[...] + jnp.sum(onehot.astype(F32), axis=0, keepdims=True)
    carry_scr[...] = carry_new

    cols = jnp.where(lane == 0, e1, jnp.where(lane == 1, e2, jnp.where(lane == 2, rank1 >> 7,
           jnp.where(lane == 3, rank1 & 127, jnp.where(lane == 4, rank2 >> 7,
           jnp.where(lane == 5, rank2 & 127, 0))))))
    ids_ref[...] = lax.dot_general(sel_ref[...], cols.astype(F32).astype(BF16), NT_DIMS,
                                   preferred_element_type=F32)
    gate_ref[...] = jnp.where(lane == 0, gate1, jnp.where(lane == 1, gate2, 0.0))
    cnt_ref[...] = jnp.broadcast_to(carry_new, cnt_ref.shape).astype(jnp.int32)
    del n_exp


def _router(x2, g, w_hi, w_lo, tm, n_groups, per_group):
    t, d = x2.shape
    lt = jnp.tril(jnp.ones((tm, tm), BF16), -1)
    sel = jnp.eye(8, LANES, dtype=BF16)
    return pl.pallas_call(
        functools.partial(_router_kernel, n_groups=n_groups, per_group=per_group),
        out_shape=(
            jax.ShapeDtypeStruct((t * (d // 2 // LANES), LANES), jnp.uint32),
            jax.ShapeDtypeStruct((8, t), F32),
            jax.ShapeDtypeStruct((t, LANES), F32),
            jax.ShapeDtypeStruct((8, LANES), jnp.int32),
        ),
        grid=(t // tm,),
        in_specs=[
            pl.BlockSpec((tm, d), lambda m: (m, 0)),
            pl.BlockSpec((1, d), lambda m: (0, 0)),
            pl.BlockSpec((d, LANES), lambda m: (0, 0)),
            pl.BlockSpec((d, LANES), lambda m: (0, 0)),
            pl.BlockSpec((tm, tm), lambda m: (0, 0)),
            pl.BlockSpec((8, LANES), lambda m: (0, 0)),
        ],
        out_specs=(
            pl.BlockSpec((tm * (d // 2 // LANES), LANES), lambda m: (m, 0)),
            pl.BlockSpec((8, tm), lambda m: (0, m)),
            pl.BlockSpec((tm, LANES), lambda m: (m, 0)),
            pl.BlockSpec((8, LANES), lambda m: (0, 0)),
        ),
        scratch_shapes=[pltpu.VMEM((1, LANES), F32)],
        compiler_params=_params(("arbitrary",)),
        name="router",
    )(x2, g, w_hi, w_lo, lt, sel)


def _plan_kernel(cnt_ref, ids_ref, dest_ref, be_ref, first_ref, nxt_ref, nu_ref, ps_scr, *, n_exp, bm, nb):
    shift = bm.bit_length() - 1

    def clear(k, c):
        first_ref[k] = 0
        nxt_ref[k] = -1
        return c

    lax.fori_loop(0, nb, clear, 0)

    def per_expert(e, carry):
        blk_start, last, prev_start = carry
        n = lax.shift_right_logical(cnt_ref[0, e] + (bm - 1), shift)
        ps_scr[e] = blk_start * bm

        def fill(k, c):
            be_ref[blk_start + k] = e
            return c

        lax.fori_loop(0, n, fill, 0)

        @pl.when(n > 0)
        def _():
            first_ref[blk_start] = 1

            @pl.when(prev_start >= 0)
            def _():
                nxt_ref[prev_start] = e

        used = n > 0
        return blk_start + n, jnp.where(used, e, last), jnp.where(used, blk_start, prev_start)

    n_used, last, _ = lax.fori_loop(0, n_exp, per_expert, (jnp.int32(0), jnp.int32(0), jnp.int32(-1)))
    nu_ref[0] = n_used

    def tail(k, c):
        be_ref[k] = last
        return c

    lax.fori_loop(n_used, nb, tail, 0)

    e1 = ids_ref[0:1, :]
    e2 = ids_ref[1:2, :]
    p1 = jnp.zeros_like(e1)
    p2 = jnp.zeros_like(e2)
    for e in range(n_exp):
        ps = ps_scr[e].astype(F32)
        p1 = jnp.where(e1 == e, ps, p1)
        p2 = jnp.where(e2 == e, ps, p2)
    dest_ref[...] = jnp.zeros(dest_ref.shape, jnp.int32)
    dest_ref[0:1, :] = (p1 + ids_ref[2:3, :] * 128.0 + ids_ref[3:4, :]).astype(jnp.int32)
    dest_ref[1:2, :] = (p2 + ids_ref[4:5, :] * 128.0 + ids_ref[5:6, :]).astype(jnp.int32)


def _plan(counts, ids_rows, n_exp, bm, nb):
    t = ids_rows.shape[1]
    return pl.pallas_call(
        functools.partial(_plan_kernel, n_exp=n_exp, bm=bm, nb=nb),
        out_shape=(
            jax.ShapeDtypeStruct((8, t), jnp.int32),
            jax.ShapeDtypeStruct((nb,), jnp.int32),
            jax.ShapeDtypeStruct((nb,), jnp.int32),
            jax.ShapeDtypeStruct((nb,), jnp.int32),
            jax.ShapeDtypeStruct((1,), jnp.int32),
        ),
        in_specs=[
            pl.BlockSpec(memory_space=pltpu.SMEM),
            pl.BlockSpec(memory_space=pltpu.VMEM),
        ],
        out_specs=(
            pl.BlockSpec(memory_space=pltpu.VMEM),
            pl.BlockSpec(memory_space=pltpu.SMEM),
            pl.BlockSpec(memory_space=pltpu.SMEM),
            pl.BlockSpec(memory_space=pltpu.SMEM),
            pl.BlockSpec(memory_space=pltpu.SMEM),
        ),
        scratch_shapes=[pltpu.SMEM((n_exp,), jnp.int32)],
        name="moe_plan",
    )(counts, ids_rows)


def _dispatch_kernel(d1_ref, d2_ref, h_ref, xb_in_ref, xb_ref, sem, *, tm, ns):
    del xb_in_ref
    base = pl.program_id(0) * tm

    def row_copy(r, dest):
        src = h_ref.at[pl.ds(pl.multiple_of(r * ns, ns), ns)]
        return pltpu.make_async_copy(src, xb_ref.at[pl.ds(pl.multiple_of(dest * ns, ns), ns)], sem)

    def start(r, carry):
        row_copy(r, d1_ref[base + r]).start()
        row_copy(r, d2_ref[base + r]).start()
        return carry

    lax.fori_loop(0, tm, start, 0, unroll=DMA_UNROLL)
    for _ in range(2):
        pltpu.make_async_copy(h_ref, xb_ref.at[pl.ds(0, tm * ns)], sem).wait()


def _dispatch(h2p, dest1, dest2, p_rows, tm, ns):
    t = h2p.shape[0] // ns
    xb0 = jnp.zeros((p_rows * ns, LANES), h2p.dtype)
    return pl.pallas_call(
        functools.partial(_dispatch_kernel, tm=tm, ns=ns),
        out_shape=jax.ShapeDtypeStruct((p_rows * ns, LANES), h2p.dtype),
        grid_spec=pltpu.PrefetchScalarGridSpec(
            num_scalar_prefetch=2,
            grid=(t // tm,),
            in_specs=[
                pl.BlockSpec((tm * ns, LANES), lambda m, d1, d2: (m, 0)),
                pl.BlockSpec(memory_space=pl.ANY),
            ],
            out_specs=pl.BlockSpec(memory_space=pl.ANY),
            scratch_shapes=[pltpu.SemaphoreType.DMA],
        ),
        input_output_aliases={3: 0},
        compiler_params=_params(("arbitrary",)),
        name="moe_dispatch",
    )(dest1, dest2, h2p, xb0)


def _experts_kernel(be_ref, first_ref, nxt_ref, nu_ref, x_ref, wg_hbm, wu_hbm, wd_hbm, y_ref,
                    wg_buf, wu_buf, wd_buf, wg_scr, wu_scr, wd_scr, sem, run_scr, *, ns):
    b = pl.program_id(0)

    def weight_copies(e, slot):
        return (pltpu.make_async_copy(wg_hbm.at[e], wg_buf.at[slot], sem.at[slot]),
                pltpu.make_async_copy(wu_hbm.at[e], wu_buf.at[slot], sem.at[slot]),
                pltpu.make_async_copy(wd_hbm.at[e], wd_buf.at[slot], sem.at[slot]))

    @pl.when(b == 0)
    def _():
        run_scr[0] = 0
        for cp in weight_copies(be_ref[0], 0):
            cp.start()

    @pl.when((first_ref[b] == 1) & (b < nu_ref[0]))
    def _():
        run = run_scr[0]
        slot = run & 1
        for cp in weight_copies(be_ref[b], slot):
            cp.wait()
        wg_scr[...] = wg_buf[slot].astype(BF16)
        wu_scr[...] = wu_buf[slot].astype(BF16)
        wd_scr[...] = wd_buf[slot].astype(BF16)

        @pl.when(nxt_ref[b] >= 0)
        def _():
            for cp in weight_copies(nxt_ref[b], 1 - slot):
                cp.start()

        run_scr[0] = run + 1

    @pl.when(b < nu_ref[0])
    def _():
        x = _unpack_bf16_pairs(_load_row_tiles(x_ref, ns)).astype(BF16)
        g = jnp.dot(x, wg_scr[...], preferred_element_type=F32)
        u = jnp.dot(x, wu_scr[...], preferred_element_type=F32)
        hid = (g * jax.nn.sigmoid(g) * u).astype(BF16)
        _store_row_tiles(y_ref, _pack_bf16_pairs(jnp.dot(hid, wd_scr[...], preferred_element_type=F32)))


def _experts(xb, w_gate, w_up, w_down, block_expert, first, nxt, n_used, bm, ns):
    d, de = w_gate.shape[1:]
    nb = xb.shape[0] // (bm * ns)

    def row_map(b, be, fi, nx, nu):
        return (jnp.minimum(b, nu[0] - 1), 0)

    return pl.pallas_call(
        functools.partial(_experts_kernel, ns=ns),
        out_shape=jax.ShapeDtypeStruct(xb.shape, jnp.uint32),
        grid_spec=pltpu.PrefetchScalarGridSpec(
            num_scalar_prefetch=4,
            grid=(nb,),
            in_specs=[
                pl.BlockSpec((bm * ns, LANES), row_map),
                pl.BlockSpec(memory_space=pl.ANY),
                pl.BlockSpec(memory_space=pl.ANY),
                pl.BlockSpec(memory_space=pl.ANY),
            ],
            out_specs=pl.BlockSpec((bm * ns, LANES), row_map),
            scratch_shapes=[
                pltpu.VMEM((2, d, de), F32),
                pltpu.VMEM((2, d, de), F32),
                pltpu.VMEM((2, de, d), F32),
                pltpu.VMEM((d, de), BF16),
                pltpu.VMEM((d, de), BF16),
                pltpu.VMEM((de, d), BF16),
                pltpu.SemaphoreType.DMA((2,)),
                pltpu.SMEM((1,), jnp.int32),
            ],
        ),
        input_output_aliases={4: 0},
        compiler_params=_params(("arbitrary",)),
        name="moe_experts",
    )(block_expert, first, nxt, n_used, xb, w_gate, w_up, w_down)


def _combine_kernel(d1_ref, d2_ref, x_ref, gate_ref, yb_ref, o_ref, buf, sem, *, tm, ns):
    base = pl.program_id(0) * tm

    def row_copy(r, dest, slot):
        dst = buf.at[slot, pl.ds(pl.multiple_of(r * ns, ns), ns)]
        return pltpu.make_async_copy(yb_ref.at[pl.ds(pl.multiple_of(dest * ns, ns), ns)], dst, sem)

    def start(r, carry):
        row_copy(r, d1_ref[base + r], 0).start()
        row_copy(r, d2_ref[base + r], 1).start()
        return carry

    lax.fori_loop(0, tm, start, 0, unroll=DMA_UNROLL)
    for slot in range(2):
        pltpu.make_async_copy(yb_ref.at[pl.ds(0, tm * ns)], buf.at[slot], sem).wait()
    gates = gate_ref[...]
    y1 = _unpack_bf16_pairs(_load_row_tiles(buf.at[0], ns))
    y2 = _unpack_bf16_pairs(_load_row_tiles(buf.at[1], ns))
    o_ref[...] = x_ref[...] + gates[:, 0:1] * y1 + gates[:, 1:2] * y2


def _combine(x2, gates, yb, dest1, dest2, tm, ns):
    t, d = x2.shape
    return pl.pallas_call(
        functools.partial(_combine_kernel, tm=tm, ns=ns),
        out_shape=jax.ShapeDtypeStruct((t, d), F32),
        grid_spec=pltpu.PrefetchScalarGridSpec(
            num_scalar_prefetch=2,
            grid=(t // tm,),
            in_specs=[
                pl.BlockSpec((tm, d), lambda m, d1, d2: (m, 0)),
                pl.BlockSpec((tm, LANES), lambda m, d1, d2: (m, 0)),
                pl.BlockSpec(memory_space=pl.ANY),
            ],
            out_specs=pl.BlockSpec((tm, d), lambda m, d1, d2: (m, 0)),
            scratch_shapes=[pltpu.VMEM((2, tm * ns, LANES), yb.dtype), pltpu.SemaphoreType.DMA],
        ),
        compiler_params=_params(("arbitrary",)),
        name="moe_combine",
    )(dest1, dest2, x2, gates, yb)


def _ple_kernel(x_ref, g_ref, p_ref, wg_ref, wp_ref, gf_ref, o_ref, h_scr, p_scr, *, tn):
    j = pl.program_id(1)

    @pl.when(j == 0)
    def _():
        h_scr[...] = _rmsnorm_f32(x_ref[...], g_ref[...]).astype(BF16)
        p_scr[...] = p_ref[...].astype(BF16)

    col = pl.multiple_of(j * tn, tn)
    gate = jnp.dot(h_scr[...], wg_ref[...].astype(BF16), preferred_element_type=F32)
    ple = jnp.dot(p_scr[...], wp_ref[...].astype(BF16), preferred_element_type=F32)
    o_ref[:, pl.ds(col, tn)] = x_ref[:, pl.ds(col, tn)] + jax.nn.sigmoid(gate) * ple

    @pl.when(j == pl.num_programs(1) - 1)
    def _():
        o_ref[...] = _rmsnorm_f32(o_ref[...], gf_ref[...])


def _ple_final(x3, g_ple, p2d, w_gate, w_proj, g_final, tm, tn):
    t, d = x3.shape
    dp = p2d.shape[1]
    return pl.pallas_call(
        functools.partial(_ple_kernel, tn=tn),
        out_shape=jax.ShapeDtypeStruct((t, d), F32),
        grid=(t // tm, d // tn),
        in_specs=[
            pl.BlockSpec((tm, d), lambda m, j: (m, 0)),
            pl.BlockSpec((1, d), lambda m, j: (0, 0)),
            pl.BlockSpec((tm, dp), lambda m, j: (m, 0)),
            pl.BlockSpec((d, tn), lambda m, j: (0, j)),
            pl.BlockSpec((dp, tn), lambda m, j: (0, j)),
            pl.BlockSpec((1, d), lambda m, j: (0, 0)),
        ],
        out_specs=pl.BlockSpec((tm, d), lambda m, j: (m, 0)),
        scratch_shapes=[pltpu.VMEM((tm, d), BF16), pltpu.VMEM((tm, dp), BF16)],
        compiler_params=_params(("parallel", "arbitrary")),
        name="ple_final",
    )(x3, g_ple, p2d, w_gate, w_proj, g_final)


def _tile(n, pref):
    return pref if n % pref == 0 else n


def _layer(x2d, p2d, batch, seq, w_in, b_forget, w_branch_fox, w_branch_sb, w_mix_out, g_mix, g_ffn,
           w_group, w_expert, w_gate, w_up, w_down, g_ple, w_ple_proj, w_ple_gate, g_final):
    t, d = x2d.shape
    n_heads = b_forget.shape[0]
    w_att = n_heads * HEAD_DIM
    n_exp = w_expert.shape[1]

    qs = LOG2E * HEAD_DIM ** -0.5
    off_b = 3 * w_att + n_heads
    w_main = jnp.concatenate([(w_in[:, :w_att] * qs).astype(BF16), w_in[:, w_att:2 * w_att].astype(BF16),
                              (w_in[:, off_b:off_b + w_att] * qs).astype(BF16),
                              w_in[:, off_b + w_att:off_b + 2 * w_att].astype(BF16),
                              w_in[:, off_b + 3 * w_att:].astype(BF16)], axis=1)
    wv_t = jnp.concatenate([w_in[:, 2 * w_att:3 * w_att].astype(BF16).T,
                            w_in[:, off_b + 2 * w_att:off_b + 3 * w_att].astype(BF16).T], axis=0)
    wf = jnp.pad(w_in[:, 3 * w_att:off_b], ((0, 0), (0, LANES - n_heads))).astype(BF16)
    b_row = jnp.pad(b_forget, (0, LANES - n_heads)).reshape(1, LANES)
    col_qa, col_ka, col_qb, col_kb = 0, w_att, 2 * w_att, 3 * w_att
    col_ga, col_gb = 4 * w_att, 4 * w_att + d
    row_va, row_vb = 0, w_att

    tm = _tile(t, 1024)
    tn = _tile(d, 512)
    proj, vt, f_tok = _inproj(x2d, g_mix.reshape(1, d), w_main, wv_t, wf, tm, 512)
    c = _forget_cumsum(f_tok, b_row, batch, n_heads)

    tq = _tile(seq, 512)
    hp = 4
    o_a = _fox_attention(proj, vt, c, batch, seq, n_heads, col_qa, col_ka, row_va, tq, hp)
    o_b = _sb_attention(proj, vt, batch, seq, n_heads, col_qb, col_kb, row_vb, tq, _tile(seq, 256), hp)

    mixed = _merge(o_a, o_b, w_branch_fox, w_branch_sb, proj, col_ga, col_gb, tm, tn)
    x2 = _mixout(mixed, w_mix_out, x2d, tm, tn)

    w_r = jnp.concatenate([w_group, w_expert], axis=1)
    w_r = jnp.pad(w_r, ((0, 0), (0, LANES - w_r.shape[1])))
    w_r_hi = w_r.astype(BF16)
    w_r_lo = (w_r - w_r_hi.astype(F32)).astype(BF16)
    tr = _tile(t, 512)
    h2, ids_rows, gates, counts = _router(x2, g_ffn.reshape(1, d), w_r_hi, w_r_lo, tr, N_GROUPS, EXPERTS_PER_GROUP)

    bm = 256
    n_assign = 2 * t
    nb = n_assign // bm + n_exp
    dest, block_expert, first, nxt, n_used = _plan(counts, ids_rows, n_exp, bm, nb)
    dest1, dest2 = dest[0], dest[1]

    td = _tile(t, 256)
    ns = d // 2 // LANES
    xb = _dispatch(h2, dest1, dest2, nb * bm, td, ns)
    yb = _experts(xb, w_gate, w_up, w_down, block_expert, first, nxt, n_used, bm, ns)
    x3 = _combine(x2, gates, yb, dest1, dest2, td, ns)

    tp = _tile(t, 512)
    return _ple_final(x3, g_ple.reshape(1, d), p2d, w_ple_gate, w_ple_proj,
                      g_final.reshape(1, d), tp, tn)


def kernel(x, p, w_in, b_forget, w_branch_fox, w_branch_sb, w_mix_out, g_mix, g_ffn, w_group, w_expert,
           w_gate, w_up, w_down, g_ple, w_ple_proj, w_ple_gate, g_final):
    b, s, d = x.shape
    depth = w_in.shape[0]
    assert depth == 1, "the final norm is fused into the single layer"
    x2d = x.reshape(b * s, d)
    out = _layer(x2d, p[0].reshape(b * s, -1), b, s, w_in[0], b_forget[0], w_branch_fox[0], w_branch_sb[0],
                 w_mix_out[0], g_mix[0], g_ffn[0], w_group[0], w_expert[0], w_gate[0], w_up[0], w_down[0],
                 g_ple[0], w_ple_proj[0], w_ple_gate[0], g_final)
    return out.reshape(b, s, d)
```

```python
import functools

import jax
import jax.numpy as jnp
from jax import lax
from jax.experimental import pallas as pl
from jax.experimental.pallas import tpu as pltpu

F32 = jnp.float32
BF16 = jnp.bfloat16

HEAD_DIM = 128
N_GROUPS = 4
EXPERTS_PER_GROUP = 8
EPS = 1e-6
LANES = 128
SUBLANES = 8
VMEM_LIMIT = 56 * 1024 * 1024

LOG2E = 1.4426950408889634
DMA_UNROLL = 8
R_DONE = 160.0

NT_DIMS = (((1,), (1,)), ((), ()))


def _params(sem):
    return pltpu.CompilerParams(dimension_semantics=sem, vmem_limit_bytes=VMEM_LIMIT)


def _rmsnorm_f32(x, g):
    ms = jnp.mean(x * x, axis=-1, keepdims=True)
    return x * lax.rsqrt(ms + EPS) * g


def _pack_bf16_pairs(x):
    n = x.shape[1] // 2
    bits = lax.bitcast_convert_type(x.astype(BF16).astype(F32), jnp.uint32)
    return (bits[:, :n] >> 16) | (bits[:, n:] & jnp.uint32(0xFFFF0000))


def _unpack_bf16_pairs(w):
    lo = lax.bitcast_convert_type(w << 16, F32)
    hi = lax.bitcast_convert_type(w & jnp.uint32(0xFFFF0000), F32)
    return jnp.concatenate([lo, hi], axis=1)


def _store_row_tiles(ref, words):
    n = words.shape[0]
    ns = words.shape[1] // LANES
    for c in range(ns):
        ref[pl.ds(c, n, stride=ns), :] = words[:, c * LANES:(c + 1) * LANES]


def _load_row_tiles(ref, ns):
    n = ref.shape[0] // ns
    return jnp.concatenate([ref[pl.ds(c, n, stride=ns), :] for c in range(ns)], axis=1)


def _wprep_kernel(w_ref, wm_ref, wvt_ref, wf_ref, *, w_att, n_heads, qs):
    off_b = 3 * w_att + n_heads
    n_in = w_ref.shape[1]
    wm_ref[:, 0:w_att] = (w_ref[:, 0:w_att] * qs).astype(BF16)
    wm_ref[:, w_att:2 * w_att] = w_ref[:, w_att:2 * w_att].astype(BF16)
    wm_ref[:, 2 * w_att:3 * w_att] = (w_ref[:, off_b:off_b + w_att] * qs).astype(BF16)
    wm_ref[:, 3 * w_att:4 * w_att] = w_ref[:, off_b + w_att:off_b + 2 * w_att].astype(BF16)
    wm_ref[:, 4 * w_att:] = w_ref[:, off_b + 3 * w_att:n_in].astype(BF16)
    wvt_ref[0:w_att, :] = w_ref[:, 2 * w_att:3 * w_att].T.astype(BF16)
    wvt_ref[w_att:2 * w_att, :] = w_ref[:, off_b + 2 * w_att:off_b + 3 * w_att].T.astype(BF16)
    lane = lax.broadcasted_iota(jnp.int32, wf_ref.shape, 1)
    wf_ref[...] = jnp.where(lane < n_heads, w_ref[:, 3 * w_att:3 * w_att + LANES], 0.0).astype(BF16)


def _wprep(w_in, w_att, n_heads, qs, rb):
    d, n_in = w_in.shape
    n_main = n_in - n_heads - 2 * w_att
    return pl.pallas_call(
        functools.partial(_wprep_kernel, w_att=w_att, n_heads=n_heads, qs=qs),
        out_shape=(jax.ShapeDtypeStruct((d, n_main), BF16), jax.ShapeDtypeStruct((2 * w_att, d), BF16),
                   jax.ShapeDtypeStruct((d, LANES), BF16)),
        grid=(d // rb,),
        in_specs=[pl.BlockSpec((rb, n_in), lambda r: (r, 0))],
        out_specs=(
            pl.BlockSpec((rb, n_main), lambda r: (r, 0)),
            pl.BlockSpec((2 * w_att, rb), lambda r: (0, r)),
            pl.BlockSpec((rb, LANES), lambda r: (r, 0)),
        ),
        compiler_params=_params(("parallel",)),
        name="inproj_weights",
    )(w_in)


def _inproj_kernel(x_ref, g_ref, w_ref, wv_ref, wf_ref, o_ref, vt_ref, f_ref, h_scr, *, nvt):
    j = pl.program_id(1)

    @pl.when(j == 0)
    def _():
        hb = _rmsnorm_f32(x_ref[...], g_ref[...]).astype(BF16)
        h_scr[...] = hb
        f_ref[...] = jnp.dot(hb, wf_ref[...], preferred_element_type=F32)

    @pl.when(j < nvt)
    def _():
        vt_ref[...] = lax.dot_general(wv_ref[...], h_scr[...], NT_DIMS,
                                      preferred_element_type=F32).astype(vt_ref.dtype)

    @pl.when(j >= nvt)
    def _():
        o_ref[...] = jnp.dot(h_scr[...], w_ref[...], preferred_element_type=F32).astype(o_ref.dtype)


def _inproj(x2d, g, w, wv_t, wf, tm, tn):
    t, d = x2d.shape
    n = w.shape[1]
    nv = wv_t.shape[0]
    nvt = nv // tn
    return pl.pallas_call(
        functools.partial(_inproj_kernel, nvt=nvt),
        out_shape=(jax.ShapeDtypeStruct((t, n), BF16), jax.ShapeDtypeStruct((nv, t), BF16),
                   jax.ShapeDtypeStruct((t, LANES), F32)),
        grid=(t // tm, nvt + n // tn),
        in_specs=[
            pl.BlockSpec((tm, d), lambda m, j: (m, 0)),
            pl.BlockSpec((1, d), lambda m, j: (0, 0)),
            pl.BlockSpec((d, tn), lambda m, j: (0, jnp.maximum(j - nvt, 0))),
            pl.BlockSpec((tn, d), lambda m, j: (jnp.minimum(j, nvt - 1), 0)),
            pl.BlockSpec((d, LANES), lambda m, j: (0, 0)),
        ],
        out_specs=(
            pl.BlockSpec((tm, tn), lambda m, j: (m, jnp.maximum(j - nvt, 0))),
            pl.BlockSpec((tn, tm), lambda m, j: (jnp.minimum(j, nvt - 1), m)),
            pl.BlockSpec((tm, LANES), lambda m, j: (m, 0)),
        ),
        scratch_shapes=[pltpu.VMEM((tm, d), BF16)],
        compiler_params=_params(("parallel", "arbitrary")),
        name="inproj",
    )(x2d, g, w, wv_t, wf)


def _split3(x):
    p1 = x.astype(BF16)
    r1 = x - p1.astype(F32)
    p2 = r1.astype(BF16)
    p3 = (r1 - p2.astype(F32)).astype(BF16)
    return p1, p2, p3


def _cumsum_kernel(f_ref, b_ref, lt_ref, c_ref, carry_scr, *, chunk, n_heads):
    @pl.when(pl.program_id(1) == 0)
    def _():
        carry_scr[...] = jnp.zeros(carry_scr.shape, F32)

    rows = f_ref.shape[0]
    lt = lt_ref[...]
    carry = carry_scr[...]
    for i in range(rows // chunk):
        lf = jax.nn.log_sigmoid(f_ref[i * chunk:(i + 1) * chunk, :] + b_ref[...])
        p1, p2, p3 = _split3(lf)
        cs = (jnp.dot(lt, p1, preferred_element_type=F32)
              + jnp.dot(lt, p2, preferred_element_type=F32)
              + jnp.dot(lt, p3, preferred_element_type=F32)) + carry
        for h in range(n_heads):
            c_ref[h, i * chunk:(i + 1) * chunk, :] = jnp.broadcast_to(cs[:, h:h + 1] * LOG2E, (chunk, LANES))
        carry = cs[chunk - 1:chunk, :]
    carry_scr[...] = carry


def _forget_cumsum(f_tok, b_row, batch, n_heads):
    t = f_tok.shape[0]
    s = t // batch
    rows = min(1024, s)
    chunk = min(256, s)
    lt = jnp.tril(jnp.ones((chunk, chunk), BF16))
    nr = s // rows
    return pl.pallas_call(
        functools.partial(_cumsum_kernel, chunk=chunk, n_heads=n_heads),
        out_shape=jax.ShapeDtypeStruct((n_heads, t, LANES), F32),
        grid=(batch, nr),
        in_specs=[
            pl.BlockSpec((rows, LANES), lambda b, r: (b * nr + r, 0)),
            pl.BlockSpec((1, LANES), lambda b, r: (0, 0)),
            pl.BlockSpec((chunk, chunk), lambda b, r: (0, 0)),
        ],
        out_specs=pl.BlockSpec((n_heads, rows, LANES), lambda b, r: (0, b * nr + r, 0)),
        scratch_shapes=[pltpu.VMEM((1, LANES), F32)],
        compiler_params=_params(("parallel", "arbitrary")),
        name="forget_cumsum",
    )(f_tok, b_row, lt)


def _fox_kernel(q_ref, k_ref, vt_ref, c_ref, o_ref, m_scr, l_scr, acc_scr, *, tq, tk, hp):
    i = pl.program_id(2)
    m_scr[...] = jnp.full(m_scr.shape, -jnp.inf, F32)
    l_scr[...] = jnp.zeros(l_scr.shape, F32)
    acc_scr[...] = jnp.zeros(acc_scr.shape, F32)

    def tile(j, masked):
        start = pl.multiple_of(j * tk, tk)
        if masked:
            key = j * tk + lax.broadcasted_iota(jnp.int32, (tk, tq), 0)
            qry = i * tq + lax.broadcasted_iota(jnp.int32, (tk, tq), 1)
            keep = key <= qry
        def scores(hh):
            lanes = slice(hh * HEAD_DIM, (hh + 1) * HEAD_DIM)
            kj = k_ref[pl.ds(start, tk), lanes]
            return lax.dot_general(kj, q_ref[:, lanes], NT_DIMS, preferred_element_type=F32)

        s_all = [scores(hh) for hh in range(hp)]
        for hh in range(hp):
            s = s_all[hh]
            cj = c_ref[hh, pl.ds(start, tk), :]
            s = s - jnp.concatenate([cj] * (tq // LANES), axis=1)
            if masked:
                s = jnp.where(keep, s, -jnp.inf)
            m_prev = m_scr[hh]
            m_new = jnp.maximum(m_prev, jnp.max(s, axis=0, keepdims=True))
            alpha = jnp.exp2(m_prev - m_new)
            p = jnp.exp2(s - m_new)
            l_scr[hh] = alpha * l_scr[hh] + jnp.sum(p, axis=0, keepdims=True)
            vtj = vt_ref[hh * HEAD_DIM:(hh + 1) * HEAD_DIM, pl.ds(start, tk)]
            acc_scr[hh] = alpha * acc_scr[hh] + jnp.dot(vtj, p.astype(BF16), preferred_element_type=F32)
            m_scr[hh] = m_new

    def body(j, carry):
        tile(j, False)
        return carry

    lax.fori_loop(0, i, body, 0)
    tile(i, True)
    for hh in range(hp):
        o = acc_scr[hh] / l_scr[hh]
        o_ref[:, hh * HEAD_DIM:(hh + 1) * HEAD_DIM] = o.T.astype(o_ref.dtype)


def _fox_attention(proj, vt, c, batch, seq, n_heads, col_q, col_k, row_v, tq, hp):
    t = proj.shape[0]
    nq = seq // tq
    wb = hp * HEAD_DIM
    cq, ck, rv = col_q // wb, col_k // wb, row_v // wb
    return pl.pallas_call(
        functools.partial(_fox_kernel, tq=tq, tk=tq, hp=hp),
        out_shape=jax.ShapeDtypeStruct((t, n_heads * HEAD_DIM), BF16),
        grid=(batch, n_heads // hp, nq),
        in_specs=[
            pl.BlockSpec((tq, wb), lambda b, g, i: (b * nq + i, cq + g)),
            pl.BlockSpec((seq, wb), lambda b, g, i: (b, ck + g)),
            pl.BlockSpec((wb, seq), lambda b, g, i: (rv + g, b)),
            pl.BlockSpec((hp, seq, LANES), lambda b, g, i: (g, b, 0)),
        ],
        out_specs=pl.BlockSpec((tq, wb), lambda b, g, i: (b * nq + i, g)),
        scratch_shapes=[
            pltpu.VMEM((hp, 1, tq), F32),
            pltpu.VMEM((hp, 1, tq), F32),
            pltpu.VMEM((hp, HEAD_DIM, tq), F32),
        ],
        compiler_params=_params(("parallel", "parallel", "arbitrary")),
        name="fox_attention",
    )(proj, proj, vt, c)


def _sb_kernel(q_ref, k_ref, vt_ref, tri_ref, o_ref, r_scr, acc_scr, *, tq, tk, hp):
    i = pl.program_id(2)
    r_scr[...] = jnp.zeros(r_scr.shape, F32)
    acc_scr[...] = jnp.zeros(acc_scr.shape, F32)

    sign_bit = jnp.uint32(0x80000000)

    def tile(j, masked):
        start = pl.multiple_of(j * tk, tk)
        if masked:
            key = j * tk + lax.broadcasted_iota(jnp.int32, (tk, tq), 0)
            qry = i * tq + lax.broadcasted_iota(jnp.int32, (tk, tq), 1)
            mask = key < qry

        def scores(hh):
            lanes = slice(hh * HEAD_DIM, (hh + 1) * HEAD_DIM)
            kj = k_ref[pl.ds(start, tk), lanes]
            return lax.dot_general(kj, q_ref[:, lanes], NT_DIMS, preferred_element_type=F32)

        u_all = [scores(hh) for hh in range(hp)]
        for hh in range(hp):
            u = u_all[hh]
            minus_abs = lax.bitcast_convert_type(lax.bitcast_convert_type(u, jnp.uint32) | sign_bit, F32)
            sp = jnp.maximum(u, 0.0) + jnp.log2(1.0 + jnp.exp2(minus_abs))
            if masked:
                sp = jnp.where(mask, sp, 0.0)
            w = jnp.dot(tri_ref[...], sp.astype(BF16), preferred_element_type=F32)
            a = jnp.exp2(u - ((sp + w) + r_scr[hh]))
            if masked:
                a = jnp.where(mask, a, 0.0)
            vtj = vt_ref[hh * HEAD_DIM:(hh + 1) * HEAD_DIM, pl.ds(start, tk)]
            acc_scr[hh] += jnp.dot(vtj, a.astype(BF16), preferred_element_type=F32)
            r_scr[hh] += jnp.sum(sp, axis=0, keepdims=True)

    ratio = max(tq // tk, 1)
    n_full = (i * tq) // tk
    for d in reversed(range(ratio)):
        tile(n_full + d, True)

    def live():
        return (jnp.min(r_scr[...]) <= R_DONE).astype(jnp.int32)

    def cond(carry):
        it, alive = carry
        return (it < n_full) & (alive > 0)

    def body(carry):
        it, _ = carry
        tile(n_full - 1 - it, False)
        return it + 1, live()

    lax.while_loop(cond, body, (jnp.int32(0), live()))
    for hh in range(hp):
        o_ref[:, hh * HEAD_DIM:(hh + 1) * HEAD_DIM] = acc_scr[hh].T.astype(o_ref.dtype)


def _sb_attention(proj, vt, batch, seq, n_heads, col_q, col_k, row_v, tq, tk, hp):
    t = proj.shape[0]
    nq = seq // tq
    wb = hp * HEAD_DIM
    cq, ck, rv = col_q // wb, col_k // wb, row_v // wb
    tri = jnp.triu(jnp.ones((tk, tk), BF16), 1)
    return pl.pallas_call(
        functools.partial(_sb_kernel, tq=tq, tk=tk, hp=hp),
        out_shape=jax.ShapeDtypeStruct((t, n_heads * HEAD_DIM), BF16),
        grid=(batch, n_heads // hp, nq),
        in_specs=[
            pl.BlockSpec((tq, wb), lambda b, g, i: (b * nq + i, cq + g)),
            pl.BlockSpec((seq, wb), lambda b, g, i: (b, ck + g)),
            pl.BlockSpec((wb, seq), lambda b, g, i: (rv + g, b)),
            pl.BlockSpec((tk, tk), lambda b, g, i: (0, 0)),
        ],
        out_specs=pl.BlockSpec((tq, wb), lambda b, g, i: (b * nq + i, g)),
        scratch_shapes=[
            pltpu.VMEM((hp, 1, tq), F32),
            pltpu.VMEM((hp, HEAD_DIM, tq), F32),
        ],
        compiler_params=_params(("parallel", "parallel", "arbitrary")),
        name="sb_attention",
    )(proj, proj, vt, tri)


def _merge_kernel(oa_ref, ob_ref, wa_ref, wb_ref, ga_ref, gb_ref, o_ref):
    ya = jnp.dot(oa_ref[...], wa_ref[...], preferred_element_type=F32)
    yb = jnp.dot(ob_ref[...], wb_ref[...], preferred_element_type=F32)
    ga = jax.nn.sigmoid(ga_ref[...].astype(F32))
    gb = jax.nn.sigmoid(gb_ref[...].astype(F32))
    o_ref[...] = (ga * ya + gb * yb).astype(o_ref.dtype)


def _merge(o_a, o_b, wa, wb, proj, col_ga, col_gb, tm, tn):
    t, ka = o_a.shape
    kb = o_b.shape[1]
    d = wa.shape[1]
    ca, cb = col_ga // tn, col_gb // tn
    return pl.pallas_call(
        _merge_kernel,
        out_shape=jax.ShapeDtypeStruct((t, d), BF16),
        grid=(t // tm, d // tn),
        in_specs=[
            pl.BlockSpec((tm, ka), lambda m, j: (m, 0)),
            pl.BlockSpec((tm, kb), lambda m, j: (m, 0)),
            pl.BlockSpec((ka, tn), lambda m, j: (0, j)),
            pl.BlockSpec((kb, tn), lambda m, j: (0, j)),
            pl.BlockSpec((tm, tn), lambda m, j: (m, ca + j)),
            pl.BlockSpec((tm, tn), lambda m, j: (m, cb + j)),
        ],
        out_specs=pl.BlockSpec((tm, tn), lambda m, j: (m, j)),
        compiler_params=_params(("parallel", "arbitrary")),
        name="branch_merge",
    )(o_a, o_b, wa, wb, proj, proj)


def _mixout_kernel(a_ref, w_ref, x_ref, o_ref):
    o_ref[...] = x_ref[...] + jnp.dot(a_ref[...], w_ref[...], preferred_element_type=F32)


def _mixout(mixed, w, x2d, tm, tn):
    t, k = mixed.shape
    d = w.shape[1]
    return pl.pallas_call(
        _mixout_kernel,
        out_shape=jax.ShapeDtypeStruct((t, d), F32),
        grid=(t // tm, d // tn),
        in_specs=[
            pl.BlockSpec((tm, k), lambda m, j: (m, 0)),
            pl.BlockSpec((k, tn), lambda m, j: (0, j)),
            pl.BlockSpec((tm, tn), lambda m, j: (m, j)),
        ],
        out_specs=pl.BlockSpec((tm, tn), lambda m, j: (m, j)),
        compiler_params=_params(("parallel", "arbitrary")),
        name="mix_out",
    )(mixed, w, x2d)


def _router_kernel(x_ref, g_ref, wh_ref, wl_ref, lt_ref, sel_ref, h_ref, ids_ref, gate_ref, cnt_ref, carry_scr,
                   *, n_groups, per_group):
    @pl.when(pl.program_id(0) == 0)
    def _():
        carry_scr[...] = jnp.zeros(carry_scr.shape, F32)

    h = _rmsnorm_f32(x_ref[...], g_ref[...])
    hh = h.astype(BF16)
    _store_row_tiles(h_ref, _pack_bf16_pairs(h))
    hl = (h - hh.astype(F32)).astype(BF16)
    logits = (jnp.dot(hh, wh_ref[...], preferred_element_type=F32)
              + jnp.dot(hl, wh_ref[...], preferred_element_type=F32)
              + jnp.dot(hh, wl_ref[...], preferred_element_type=F32))
    tm = logits.shape[0]
    n_exp = n_groups * per_group
    lane = lax.broadcasted_iota(jnp.int32, (tm, LANES), 1)
    lane_f = lane.astype(F32)
    neg_inf = -jnp.inf

    def first_argmax(vals, valid):
        vmax = jnp.max(jnp.where(valid, vals, neg_inf), axis=-1, keepdims=True)
        idx = jnp.min(jnp.where(valid & (vals == vmax), lane_f, float(LANES)), axis=-1, keepdims=True)
        return vmax, idx.astype(jnp.int32)

    is_group = lane < n_groups
    gmax, gsel = first_argmax(logits, is_group)
    gsum = jnp.sum(jnp.where(is_group, jnp.exp(logits - gmax), 0.0), axis=-1, keepdims=True)
    p_group = 1.0 / gsum
    lo = n_groups + gsel * per_group
    in_group = (lane >= lo) & (lane < lo + per_group)
    v1, i1 = first_argmax(logits, in_group)
    v2, i2 = first_argmax(logits, in_group & (lane != i1))
    e2w = jnp.exp(v2 - v1)
    gate1 = (1.0 / (1.0 + e2w)) * p_group
    gate2 = (e2w / (1.0 + e2w)) * p_group
    e1 = i1 - n_groups
    e2 = i2 - n_groups

    onehot = ((lane == e1) | (lane == e2)).astype(BF16)
    before = jnp.dot(lt_ref[...], onehot, preferred_element_type=F32) + carry_scr[...]
    rank1 = jnp.sum(jnp.where(lane == e1, before, 0.0), axis=-1, keepdims=True).astype(jnp.int32)
    rank2 = jnp.sum(jnp.where(lane == e2, before, 0.0), axis=-1, keepdims=True).astype(jnp.int32)
    carry_new = carry_scr[...] + jnp.sum(onehot.astype(F32), axis=0, keepdims=True)
    carry_scr[...] = carry_new

    cols = jnp.where(lane == 0, e1, jnp.where(lane == 1, e2, jnp.where(lane == 2, rank1 >> 7,
           jnp.where(lane == 3, rank1 & 127, jnp.where(lane == 4, rank2 >> 7,
           jnp.where(lane == 5, rank2 & 127, 0))))))
    ids_ref[...] = lax.dot_general(sel_ref[...], cols.astype(F32).astype(BF16), NT_DIMS,
                                   preferred_element_type=F32)
    gate_ref[...] = jnp.where(lane == 0, gate1, jnp.where(lane == 1, gate2, 0.0))
    cnt_ref[...] = jnp.broadcast_to(carry_new, cnt_ref.shape).astype(jnp.int32)
    del n_exp


def _router(x2, g, w_hi, w_lo, tm, n_groups, per_group):
    t, d = x2.shape
    lt = jnp.tril(jnp.ones((tm, tm), BF16), -1)
    sel = jnp.eye(8, LANES, dtype=BF16)
    return pl.pallas_call(
        functools.partial(_router_kernel, n_groups=n_groups, per_group=per_group),
        out_shape=(
            jax.ShapeDtypeStruct((t * (d // 2 // LANES), LANES), jnp.uint32),
            jax.ShapeDtypeStruct((8, t), F32),
            jax.ShapeDtypeStruct((t, LANES), F32),
            jax.ShapeDtypeStruct((8, LANES), jnp.int32),
        ),
        grid=(t // tm,),
        in_specs=[
            pl.BlockSpec((tm, d), lambda m: (m, 0)),
            pl.BlockSpec((1, d), lambda m: (0, 0)),
            pl.BlockSpec((d, LANES), lambda m: (0, 0)),
            pl.BlockSpec((d, LANES), lambda m: (0, 0)),
            pl.BlockSpec((tm, tm), lambda m: (0, 0)),
            pl.BlockSpec((8, LANES), lambda m: (0, 0)),
        ],
        out_specs=(
            pl.BlockSpec((tm * (d // 2 // LANES), LANES), lambda m: (m, 0)),
            pl.BlockSpec((8, tm), lambda m: (0, m)),
            pl.BlockSpec((tm, LANES), lambda m: (m, 0)),
            pl.BlockSpec((8, LANES), lambda m: (0, 0)),
        ),
        scratch_shapes=[pltpu.VMEM((1, LANES), F32)],
        compiler_params=_params(("arbitrary",)),
        name="router",
    )(x2, g, w_hi, w_lo, lt, sel)


def _plan_kernel(cnt_ref, ids_ref, dest_ref, be_ref, first_ref, nxt_ref, nu_ref, ps_scr, *, n_exp, bm, nb):
    shift = bm.bit_length() - 1

    def clear(k, c):
        first_ref[k] = 0
        nxt_ref[k] = -1
        return c

    lax.fori_loop(0, nb, clear, 0)

    def per_expert(e, carry):
        blk_start, last, prev_start = carry
        n = lax.shift_right_logical(cnt_ref[0, e] + (bm - 1), shift)
        ps_scr[e] = blk_start * bm

        def fill(k, c):
            be_ref[blk_start + k] = e
            return c

        lax.fori_loop(0, n, fill, 0)

        @pl.when(n > 0)
        def _():
            first_ref[blk_start] = 1

            @pl.when(prev_start >= 0)
            def _():
                nxt_ref[prev_start] = e

        used = n > 0
        return blk_start + n, jnp.where(used, e, last), jnp.where(used, blk_start, prev_start)

    n_used, last, _ = lax.fori_loop(0, n_exp, per_expert, (jnp.int32(0), jnp.int32(0), jnp.int32(-1)))
    nu_ref[0] = n_used

    def tail(k, c):
        be_ref[k] = last
        return c

    lax.fori_loop(n_used, nb, tail, 0)

    e1 = ids_ref[0:1, :]
    e2 = ids_ref[1:2, :]
    p1 = jnp.zeros_like(e1)
    p2 = jnp.zeros_like(e2)
    for e in range(n_exp):
        ps = ps_scr[e].astype(F32)
        p1 = jnp.where(e1 == e, ps, p1)
        p2 = jnp.where(e2 == e, ps, p2)
    dest_ref[...] = jnp.zeros(dest_ref.shape, jnp.int32)
    dest_ref[0:1, :] = (p1 + ids_ref[2:3, :] * 128.0 + ids_ref[3:4, :]).astype(jnp.int32)
    dest_ref[1:2, :] = (p2 + ids_ref[4:5, :] * 128.0 + ids_ref[5:6, :]).astype(jnp.int32)


def _plan(counts, ids_rows, n_exp, bm, nb):
    t = ids_rows.shape[1]
    return pl.pallas_call(
        functools.partial(_plan_kernel, n_exp=n_exp, bm=bm, nb=nb),
        out_shape=(
            jax.ShapeDtypeStruct((8, t), jnp.int32),
            jax.ShapeDtypeStruct((nb,), jnp.int32),
            jax.ShapeDtypeStruct((nb,), jnp.int32),
            jax.ShapeDtypeStruct((nb,), jnp.int32),
            jax.ShapeDtypeStruct((1,), jnp.int32),
        ),
        in_specs=[
            pl.BlockSpec(memory_space=pltpu.SMEM),
            pl.BlockSpec(memory_space=pltpu.VMEM),
        ],
        out_specs=(
            pl.BlockSpec(memory_space=pltpu.VMEM),
            pl.BlockSpec(memory_space=pltpu.SMEM),
            pl.BlockSpec(memory_space=pltpu.SMEM),
            pl.BlockSpec(memory_space=pltpu.SMEM),
            pl.BlockSpec(memory_space=pltpu.SMEM),
        ),
        scratch_shapes=[pltpu.SMEM((n_exp,), jnp.int32)],
        name="moe_plan",
    )(counts, ids_rows)


def _dispatch_kernel(d1_ref, d2_ref, h_ref, xb_in_ref, xb_ref, sem, *, tm, ns):
    del xb_in_ref
    base = pl.program_id(0) * tm

    def row_copy(r, dest):
        src = h_ref.at[pl.ds(pl.multiple_of(r * ns, ns), ns)]
        return pltpu.make_async_copy(src, xb_ref.at[pl.ds(pl.multiple_of(dest * ns, ns), ns)], sem)

    def start(r, carry):
        row_copy(r, d1_ref[base + r]).start()
        row_copy(r, d2_ref[base + r]).start()
        return carry

    lax.fori_loop(0, tm, start, 0, unroll=DMA_UNROLL)
    for _ in range(2):
        pltpu.make_async_copy(h_ref, xb_ref.at[pl.ds(0, tm * ns)], sem).wait()


def _dispatch(h2p, dest1, dest2, p_rows, tm, ns):
    t = h2p.shape[0] // ns
    xb0 = jnp.zeros((p_rows * ns, LANES), h2p.dtype)
    return pl.pallas_call(
        functools.partial(_dispatch_kernel, tm=tm, ns=ns),
        out_shape=jax.ShapeDtypeStruct((p_rows * ns, LANES), h2p.dtype),
        grid_spec=pltpu.PrefetchScalarGridSpec(
            num_scalar_prefetch=2,
            grid=(t // tm,),
            in_specs=[
                pl.BlockSpec((tm * ns, LANES), lambda m, d1, d2: (m, 0)),
                pl.BlockSpec(memory_space=pl.ANY),
            ],
            out_specs=pl.BlockSpec(memory_space=pl.ANY),
            scratch_shapes=[pltpu.SemaphoreType.DMA],
        ),
        input_output_aliases={3: 0},
        compiler_params=_params(("arbitrary",)),
        name="moe_dispatch",
    )(dest1, dest2, h2p, xb0)


def _experts_kernel(be_ref, first_ref, nxt_ref, nu_ref, x_ref, wg_hbm, wu_hbm, wd_hbm, y_ref,
                    wg_buf, wu_buf, wd_buf, wg_scr, wu_scr, wd_scr, sem, run_scr, *, ns):
    b = pl.program_id(0)

    def weight_copies(e, slot):
        return (pltpu.make_async_copy(wg_hbm.at[e], wg_buf.at[slot], sem.at[slot]),
                pltpu.make_async_copy(wu_hbm.at[e], wu_buf.at[slot], sem.at[slot]),
                pltpu.make_async_copy(wd_hbm.at[e], wd_buf.at[slot], sem.at[slot]))

    @pl.when(b == 0)
    def _():
        run_scr[0] = 0
        for cp in weight_copies(be_ref[0], 0):
            cp.start()

    @pl.when((first_ref[b] == 1) & (b < nu_ref[0]))
    def _():
        run = run_scr[0]
        slot = run & 1
        for cp in weight_copies(be_ref[b], slot):
            cp.wait()
        wg_scr[...] = wg_buf[slot].astype(BF16)
        wu_scr[...] = wu_buf[slot].astype(BF16)
        wd_scr[...] = wd_buf[slot].astype(BF16)

        @pl.when(nxt_ref[b] >= 0)
        def _():
            for cp in weight_copies(nxt_ref[b], 1 - slot):
                cp.start()

        run_scr[0] = run + 1

    @pl.when(b < nu_ref[0])
    def _():
        x = _unpack_bf16_pairs(_load_row_tiles(x_ref, ns)).astype(BF16)
        g = jnp.dot(x, wg_scr[...], preferred_element_type=F32)
        u = jnp.dot(x, wu_scr[...], preferred_element_type=F32)
        hid = (g * jax.nn.sigmoid(g) * u).astype(BF16)
        _store_row_tiles(y_ref, _pack_bf16_pairs(jnp.dot(hid, wd_scr[...], preferred_element_type=F32)))


def _experts(xb, w_gate, w_up, w_down, block_expert, first, nxt, n_used, bm, ns):
    d, de = w_gate.shape[1:]
    nb = xb.shape[0] // (bm * ns)

    def row_map(b, be, fi, nx, nu):
        return (jnp.minimum(b, nu[0] - 1), 0)

    return pl.pallas_call(
        functools.partial(_experts_kernel, ns=ns),
        out_shape=jax.ShapeDtypeStruct(xb.shape, jnp.uint32),
        grid_spec=pltpu.PrefetchScalarGridSpec(
            num_scalar_prefetch=4,
            grid=(nb,),
            in_specs=[
                pl.BlockSpec((bm * ns, LANES), row_map),
                pl.BlockSpec(memory_space=pl.ANY),
                pl.BlockSpec(memory_space=pl.ANY),
                pl.BlockSpec(memory_space=pl.ANY),
            ],
            out_specs=pl.BlockSpec((bm * ns, LANES), row_map),
            scratch_shapes=[
                pltpu.VMEM((2, d, de), F32),
                pltpu.VMEM((2, d, de), F32),
                pltpu.VMEM((2, de, d), F32),
                pltpu.VMEM((d, de), BF16),
                pltpu.VMEM((d, de), BF16),
                pltpu.VMEM((de, d), BF16),
                pltpu.SemaphoreType.DMA((2,)),
                pltpu.SMEM((1,), jnp.int32),
            ],
        ),
        input_output_aliases={4: 0},
        compiler_params=_params(("arbitrary",)),
        name="moe_experts",
    )(block_expert, first, nxt, n_used, xb, w_gate, w_up, w_down)


def _ple_kernel(d1_ref, d2_ref, x_ref, mg_ref, yb_ref, g_ref, p_ref, wg_ref, wp_ref, gf_ref, o_ref,
                ybuf, sem, x3_scr, h_scr, p_scr, *, tm, tn, ns):
    m = pl.program_id(0)
    j = pl.program_id(1)

    def gather(blk, slot):
        base = blk * tm

        def row_copy(r, dest, k):
            dst = ybuf.at[slot, k, pl.ds(pl.multiple_of(r * ns, ns), ns)]
            return pltpu.make_async_copy(yb_ref.at[pl.ds(pl.multiple_of(dest * ns, ns), ns)], dst, sem.at[slot])

        def start(r, carry):
            row_copy(r, d1_ref[base + r], 0).start()
            row_copy(r, d2_ref[base + r], 1).start()
            return carry

        lax.fori_loop(0, tm, start, 0, unroll=DMA_UNROLL)

    @pl.when(j == 0)
    def _():
        slot = m & 1

        @pl.when(m == 0)
        def _():
            gather(0, 0)

        @pl.when(m + 1 < pl.num_programs(0))
        def _():
            gather(m + 1, 1 - slot)

        for k in range(2):
            pltpu.make_async_copy(yb_ref.at[pl.ds(0, tm * ns)], ybuf.at[slot, k], sem.at[slot]).wait()
        mg = mg_ref[...]
        y1 = _unpack_bf16_pairs(_load_row_tiles(ybuf.at[slot, 0], ns))
        y2 = _unpack_bf16_pairs(_load_row_tiles(ybuf.at[slot, 1], ns))
        x3 = x_ref[...] + mg[:, 0:1] * y1 + mg[:, 1:2] * y2
        x3_scr[...] = x3
        h_scr[...] = _rmsnorm_f32(x3, g_ref[...]).astype(BF16)
        p_scr[...] = p_ref[...].astype(BF16)

    col = pl.multiple_of(j * tn, tn)
    gate = jnp.dot(h_scr[...], wg_ref[...], preferred_element_type=F32)
    ple = jnp.dot(p_scr[...], wp_ref[...], preferred_element_type=F32)
    o_ref[:, pl.ds(col, tn)] = x3_scr[:, pl.ds(col, tn)] + jax.nn.sigmoid(gate) * ple

    @pl.when(j == pl.num_programs(1) - 1)
    def _():
        o_ref[...] = _rmsnorm_f32(o_ref[...], gf_ref[...])


def _combine_ple_final(x2, moe_gates, yb, dest1, dest2, g_ple, p2d, w_gate, w_proj, g_final, tm, tn, ns):
    t, d = x2.shape
    dp = p2d.shape[1]
    return pl.pallas_call(
        functools.partial(_ple_kernel, tm=tm, tn=tn, ns=ns),
        out_shape=jax.ShapeDtypeStruct((t, d), F32),
        grid_spec=pltpu.PrefetchScalarGridSpec(
            num_scalar_prefetch=2,
            grid=(t // tm, d // tn),
            in_specs=[
                pl.BlockSpec((tm, d), lambda m, j, d1, d2: (m, 0)),
                pl.BlockSpec((tm, LANES), lambda m, j, d1, d2: (m, 0)),
                pl.BlockSpec(memory_space=pl.ANY),
                pl.BlockSpec((1, d), lambda m, j, d1, d2: (0, 0)),
                pl.BlockSpec((tm, dp), lambda m, j, d1, d2: (m, 0)),
                pl.BlockSpec((d, tn), lambda m, j, d1, d2: (0, j)),
                pl.BlockSpec((dp, tn), lambda m, j, d1, d2: (0, j)),
                pl.BlockSpec((1, d), lambda m, j, d1, d2: (0, 0)),
            ],
            out_specs=pl.BlockSpec((tm, d), lambda m, j, d1, d2: (m, 0)),
            scratch_shapes=[
                pltpu.VMEM((2, 2, tm * ns, LANES), yb.dtype),
                pltpu.SemaphoreType.DMA((2,)),
                pltpu.VMEM((tm, d), F32),
                pltpu.VMEM((tm, d), BF16),
                pltpu.VMEM((tm, dp), BF16),
            ],
        ),
        compiler_params=_params(("arbitrary", "arbitrary")),
        name="combine_ple_final",
    )(dest1, dest2, x2, moe_gates, yb, g_ple, p2d, w_gate, w_proj, g_final)


def _tile(n, pref):
    return pref if n % pref == 0 else n


def _layer(x2d, p2d, batch, seq, w_in, b_forget, w_branch_fox, w_branch_sb, w_mix_out, g_mix, g_ffn,
           w_group, w_expert, w_gate, w_up, w_down, g_ple, w_ple_proj, w_ple_gate, g_final):
    t, d = x2d.shape
    n_heads = b_forget.shape[0]
    w_att = n_heads * HEAD_DIM
    n_exp = w_expert.shape[1]

    qs = LOG2E * HEAD_DIM ** -0.5
    w_main, wv_t, wf = _wprep(w_in, w_att, n_heads, qs, _tile(d, 256))
    b_row = jnp.pad(b_forget, (0, LANES - n_heads)).reshape(1, LANES)
    col_qa, col_ka, col_qb, col_kb = 0, w_att, 2 * w_att, 3 * w_att
    col_ga, col_gb = 4 * w_att, 4 * w_att + d
    row_va, row_vb = 0, w_att

    tm = _tile(t, 1024)
    tn = _tile(d, 512)
    proj, vt, f_tok = _inproj(x2d, g_mix.reshape(1, d), w_main, wv_t, wf, tm, 512)
    c = _forget_cumsum(f_tok, b_row, batch, n_heads)

    tq = _tile(seq, 512)
    hp = 4
    o_a = _fox_attention(proj, vt, c, batch, seq, n_heads, col_qa, col_ka, row_va, tq, hp)
    o_b = _sb_attention(proj, vt, batch, seq, n_heads, col_qb, col_kb, row_vb, tq, _tile(seq, 256), hp)

    mixed = _merge(o_a, o_b, w_branch_fox.astype(BF16), w_branch_sb.astype(BF16), proj, col_ga, col_gb, tm, tn)
    x2 = _mixout(mixed, w_mix_out.astype(BF16), x2d, tm, tn)

    w_r = jnp.concatenate([w_group, w_expert], axis=1)
    w_r = jnp.pad(w_r, ((0, 0), (0, LANES - w_r.shape[1])))
    w_r_hi = w_r.astype(BF16)
    w_r_lo = (w_r - w_r_hi.astype(F32)).astype(BF16)
    tr = _tile(t, 512)
    h2, ids_rows, gates, counts = _router(x2, g_ffn.reshape(1, d), w_r_hi, w_r_lo, tr, N_GROUPS, EXPERTS_PER_GROUP)

    bm = 256
    n_assign = 2 * t
    nb = n_assign // bm + n_exp
    dest, block_expert, first, nxt, n_used = _plan(counts, ids_rows, n_exp, bm, nb)
    dest1, dest2 = dest[0], dest[1]

    td = _tile(t, 1024)
    ns = d // 2 // LANES
    xb = _dispatch(h2, dest1, dest2, nb * bm, td, ns)
    yb = _experts(xb, w_gate, w_up, w_down, block_expert, first, nxt, n_used, bm, ns)

    tp = _tile(t, 512)
    return _combine_ple_final(x2, gates, yb, dest1, dest2, g_ple.reshape(1, d), p2d, w_ple_gate.astype(BF16),
                              w_ple_proj.astype(BF16), g_final.reshape(1, d), tp, tn, ns)


def kernel(x, p, w_in, b_forget, w_branch_fox, w_branch_sb, w_mix_out, g_mix, g_ffn, w_group, w_expert,
           w_gate, w_up, w_down, g_ple, w_ple_proj, w_ple_gate, g_final):
    b, s, d = x.shape
    depth = w_in.shape[0]
    assert depth == 1, "the final norm is fused into the single layer"
    x2d = x.reshape(b * s, d)
    out = _layer(x2d, p[0].reshape(b * s, -1), b, s, w_in[0], b_forget[0], w_branch_fox[0], w_branch_sb[0],
                 w_mix_out[0], g_mix[0], g_ffn[0], w_group[0], w_expert[0], w_gate[0], w_up[0], w_down[0],
                 g_ple[0], w_ple_proj[0], w_ple_gate[0], g_final)
    return out.reshape(b, s, d)
```

```python
import functools

import jax
import jax.numpy as jnp
from jax import lax
from jax.experimental import pallas as pl
from jax.experimental.pallas import tpu as pltpu

F32 = jnp.float32
BF16 = jnp.bfloat16

HEAD_DIM = 128
N_GROUPS = 4
EXPERTS_PER_GROUP = 8
EPS = 1e-6
LANES = 128
SUBLANES = 8
VMEM_LIMIT = 56 * 1024 * 1024

LOG2E = 1.4426950408889634
DMA_UNROLL = 8
R_DONE = 160.0

NT_DIMS = (((1,), (1,)), ((), ()))


def _params(sem):
    return pltpu.CompilerParams(dimension_semantics=sem, vmem_limit_bytes=VMEM_LIMIT)


def _rmsnorm_f32(x, g):
    ms = jnp.mean(x * x, axis=-1, keepdims=True)
    return x * lax.rsqrt(ms + EPS) * g


def _pack_bf16_pairs(x):
    n = x.shape[1] // 2
    bits = lax.bitcast_convert_type(x.astype(BF16).astype(F32), jnp.uint32)
    return (bits[:, :n] >> 16) | (bits[:, n:] & jnp.uint32(0xFFFF0000))


def _unpack_bf16_pairs(w):
    lo = lax.bitcast_convert_type(w << 16, F32)
    hi = lax.bitcast_convert_type(w & jnp.uint32(0xFFFF0000), F32)
    return jnp.concatenate([lo, hi], axis=1)


def _store_row_tiles(ref, words):
    n = words.shape[0]
    ns = words.shape[1] // LANES
    for c in range(ns):
        ref[pl.ds(c, n, stride=ns), :] = words[:, c * LANES:(c + 1) * LANES]


def _load_row_tiles(ref, ns):
    n = ref.shape[0] // ns
    return jnp.concatenate([ref[pl.ds(c, n, stride=ns), :] for c in range(ns)], axis=1)


def _wprep_kernel(w_ref, wm_ref, wvt_ref, wf_ref, *, w_att, n_heads, qs):
    off_b = 3 * w_att + n_heads
    n_in = w_ref.shape[1]
    wm_ref[:, 0:w_att] = (w_ref[:, 0:w_att] * qs).astype(BF16)
    wm_ref[:, w_att:2 * w_att] = w_ref[:, w_att:2 * w_att].astype(BF16)
    wm_ref[:, 2 * w_att:3 * w_att] = (w_ref[:, off_b:off_b + w_att] * qs).astype(BF16)
    wm_ref[:, 3 * w_att:4 * w_att] = w_ref[:, off_b + w_att:off_b + 2 * w_att].astype(BF16)
    wm_ref[:, 4 * w_att:] = w_ref[:, off_b + 3 * w_att:n_in].astype(BF16)
    wvt_ref[0:w_att, :] = w_ref[:, 2 * w_att:3 * w_att].T.astype(BF16)
    wvt_ref[w_att:2 * w_att, :] = w_ref[:, off_b + 2 * w_att:off_b + 3 * w_att].T.astype(BF16)
    lane = lax.broadcasted_iota(jnp.int32, wf_ref.shape, 1)
    wf_ref[...] = jnp.where(lane < n_heads, w_ref[:, 3 * w_att:3 * w_att + LANES], 0.0).astype(BF16)


def _wprep(w_in, w_att, n_heads, qs, rb):
    d, n_in = w_in.shape
    n_main = n_in - n_heads - 2 * w_att
    return pl.pallas_call(
        functools.partial(_wprep_kernel, w_att=w_att, n_heads=n_heads, qs=qs),
        out_shape=(jax.ShapeDtypeStruct((d, n_main), BF16), jax.ShapeDtypeStruct((2 * w_att, d), BF16),
                   jax.ShapeDtypeStruct((d, LANES), BF16)),
        grid=(d // rb,),
        in_specs=[pl.BlockSpec((rb, n_in), lambda r: (r, 0))],
        out_specs=(
            pl.BlockSpec((rb, n_main), lambda r: (r, 0)),
            pl.BlockSpec((2 * w_att, rb), lambda r: (0, r)),
            pl.BlockSpec((rb, LANES), lambda r: (r, 0)),
        ),
        compiler_params=_params(("parallel",)),
        name="inproj_weights",
    )(w_in)


def _inproj_kernel(x_ref, g_ref, w_ref, wv_ref, wf_ref, o_ref, vt_ref, f_ref, h_scr, *, nvt):
    j = pl.program_id(1)

    @pl.when(j == 0)
    def _():
        hb = _rmsnorm_f32(x_ref[...], g_ref[...]).astype(BF16)
        h_scr[...] = hb
        f_ref[...] = jnp.dot(hb, wf_ref[...], preferred_element_type=F32)

    @pl.when(j < nvt)
    def _():
        vt_ref[...] = lax.dot_general(wv_ref[...], h_scr[...], NT_DIMS,
                                      preferred_element_type=F32).astype(vt_ref.dtype)

    @pl.when(j >= nvt)
    def _():
        o_ref[...] = jnp.dot(h_scr[...], w_ref[...], preferred_element_type=F32).astype(o_ref.dtype)


def _inproj(x2d, g, w, wv_t, wf, tm, tn):
    t, d = x2d.shape
    n = w.shape[1]
    nv = wv_t.shape[0]
    nvt = nv // tn
    return pl.pallas_call(
        functools.partial(_inproj_kernel, nvt=nvt),
        out_shape=(jax.ShapeDtypeStruct((t, n), BF16), jax.ShapeDtypeStruct((nv, t), BF16),
                   jax.ShapeDtypeStruct((t, LANES), F32)),
        grid=(t // tm, nvt + n // tn),
        in_specs=[
            pl.BlockSpec((tm, d), lambda m, j: (m, 0)),
            pl.BlockSpec((1, d), lambda m, j: (0, 0)),
            pl.BlockSpec((d, tn), lambda m, j: (0, jnp.maximum(j - nvt, 0))),
            pl.BlockSpec((tn, d), lambda m, j: (jnp.minimum(j, nvt - 1), 0)),
            pl.BlockSpec((d, LANES), lambda m, j: (0, 0)),
        ],
        out_specs=(
            pl.BlockSpec((tm, tn), lambda m, j: (m, jnp.maximum(j - nvt, 0))),
            pl.BlockSpec((tn, tm), lambda m, j: (jnp.minimum(j, nvt - 1), m)),
            pl.BlockSpec((tm, LANES), lambda m, j: (m, 0)),
        ),
        scratch_shapes=[pltpu.VMEM((tm, d), BF16)],
        compiler_params=_params(("parallel", "arbitrary")),
        name="inproj",
    )(x2d, g, w, wv_t, wf)


def _split3(x):
    p1 = x.astype(BF16)
    r1 = x - p1.astype(F32)
    p2 = r1.astype(BF16)
    p3 = (r1 - p2.astype(F32)).astype(BF16)
    return p1, p2, p3


def _cumsum_kernel(f_ref, b_ref, lt_ref, c_ref, carry_scr, *, chunk, n_heads):
    @pl.when(pl.program_id(1) == 0)
    def _():
        carry_scr[...] = jnp.zeros(carry_scr.shape, F32)

    rows = f_ref.shape[0]
    lt = lt_ref[...]
    carry = carry_scr[...]
    for i in range(rows // chunk):
        lf = jax.nn.log_sigmoid(f_ref[i * chunk:(i + 1) * chunk, :] + b_ref[...])
        p1, p2, p3 = _split3(lf)
        cs = (jnp.dot(lt, p1, preferred_element_type=F32)
              + jnp.dot(lt, p2, preferred_element_type=F32)
              + jnp.dot(lt, p3, preferred_element_type=F32)) + carry
        for h in range(n_heads):
            c_ref[h, i * chunk:(i + 1) * chunk, :] = jnp.broadcast_to(cs[:, h:h + 1] * LOG2E, (chunk, LANES))
        carry = cs[chunk - 1:chunk, :]
    carry_scr[...] = carry


def _forget_cumsum(f_tok, b_row, batch, n_heads):
    t = f_tok.shape[0]
    s = t // batch
    rows = min(1024, s)
    chunk = min(256, s)
    lt = jnp.tril(jnp.ones((chunk, chunk), BF16))
    nr = s // rows
    return pl.pallas_call(
        functools.partial(_cumsum_kernel, chunk=chunk, n_heads=n_heads),
        out_shape=jax.ShapeDtypeStruct((n_heads, t, LANES), F32),
        grid=(batch, nr),
        in_specs=[
            pl.BlockSpec((rows, LANES), lambda b, r: (b * nr + r, 0)),
            pl.BlockSpec((1, LANES), lambda b, r: (0, 0)),
            pl.BlockSpec((chunk, chunk), lambda b, r: (0, 0)),
        ],
        out_specs=pl.BlockSpec((n_heads, rows, LANES), lambda b, r: (0, b * nr + r, 0)),
        scratch_shapes=[pltpu.VMEM((1, LANES), F32)],
        compiler_params=_params(("parallel", "arbitrary")),
        name="forget_cumsum",
    )(f_tok, b_row, lt)


def _fox_kernel(q_ref, k_ref, vt_ref, c_ref, o_ref, m_scr, l_scr, acc_scr, *, tq, tk, hp):
    i = pl.program_id(2)
    m_scr[...] = jnp.full(m_scr.shape, -jnp.inf, F32)
    l_scr[...] = jnp.zeros(l_scr.shape, F32)
    acc_scr[...] = jnp.zeros(acc_scr.shape, F32)

    def tile(j, masked):
        start = pl.multiple_of(j * tk, tk)
        if masked:
            key = j * tk + lax.broadcasted_iota(jnp.int32, (tk, tq), 0)
            qry = i * tq + lax.broadcasted_iota(jnp.int32, (tk, tq), 1)
            keep = key <= qry
        def scores(hh):
            lanes = slice(hh * HEAD_DIM, (hh + 1) * HEAD_DIM)
            kj = k_ref[pl.ds(start, tk), lanes]
            return lax.dot_general(kj, q_ref[:, lanes], NT_DIMS, preferred_element_type=F32)

        s_all = [scores(hh) for hh in range(hp)]
        for hh in range(hp):
            s = s_all[hh]
            cj = c_ref[hh, pl.ds(start, tk), :]
            s = s - jnp.concatenate([cj] * (tq // LANES), axis=1)
            if masked:
                s = jnp.where(keep, s, -jnp.inf)
            m_prev = m_scr[hh]
            m_new = jnp.maximum(m_prev, jnp.max(s, axis=0, keepdims=True))
            alpha = jnp.exp2(m_prev - m_new)
            p = jnp.exp2(s - m_new)
            l_scr[hh] = alpha * l_scr[hh] + jnp.sum(p, axis=0, keepdims=True)
            vtj = vt_ref[hh * HEAD_DIM:(hh + 1) * HEAD_DIM, pl.ds(start, tk)]
            acc_scr[hh] = alpha * acc_scr[hh] + jnp.dot(vtj, p.astype(BF16), preferred_element_type=F32)
            m_scr[hh] = m_new

    def body(j, carry):
        tile(j, False)
        return carry

    lax.fori_loop(0, i, body, 0)
    tile(i, True)
    for hh in range(hp):
        o = acc_scr[hh] / l_scr[hh]
        o_ref[:, hh * HEAD_DIM:(hh + 1) * HEAD_DIM] = o.T.astype(o_ref.dtype)


def _fox_attention(proj, vt, c, batch, seq, n_heads, col_q, col_k, row_v, tq, hp):
    t = proj.shape[0]
    nq = seq // tq
    wb = hp * HEAD_DIM
    cq, ck, rv = col_q // wb, col_k // wb, row_v // wb
    return pl.pallas_call(
        functools.partial(_fox_kernel, tq=tq, tk=tq, hp=hp),
        out_shape=jax.ShapeDtypeStruct((t, n_heads * HEAD_DIM), BF16),
        grid=(batch, n_heads // hp, nq),
        in_specs=[
            pl.BlockSpec((tq, wb), lambda b, g, i: (b * nq + i, cq + g)),
            pl.BlockSpec((seq, wb), lambda b, g, i: (b, ck + g)),
            pl.BlockSpec((wb, seq), lambda b, g, i: (rv + g, b)),
            pl.BlockSpec((hp, seq, LANES), lambda b, g, i: (g, b, 0)),
        ],
        out_specs=pl.BlockSpec((tq, wb), lambda b, g, i: (b * nq + i, g)),
        scratch_shapes=[
            pltpu.VMEM((hp, 1, tq), F32),
            pltpu.VMEM((hp, 1, tq), F32),
            pltpu.VMEM((hp, HEAD_DIM, tq), F32),
        ],
        compiler_params=_params(("parallel", "parallel", "arbitrary")),
        name="fox_attention",
    )(proj, proj, vt, c)


def _sb_kernel(q_ref, k_ref, vt_ref, tri_ref, o_ref, r_scr, acc_scr, *, tq, tk, hp):
    i = pl.program_id(2)
    r_scr[...] = jnp.zeros(r_scr.shape, F32)
    acc_scr[...] = jnp.zeros(acc_scr.shape, F32)

    sign_bit = jnp.uint32(0x80000000)

    def tile(j, masked):
        start = pl.multiple_of(j * tk, tk)
        if masked:
            key = j * tk + lax.broadcasted_iota(jnp.int32, (tk, tq), 0)
            qry = i * tq + lax.broadcasted_iota(jnp.int32, (tk, tq), 1)
            mask = key < qry

        def scores(hh):
            lanes = slice(hh * HEAD_DIM, (hh + 1) * HEAD_DIM)
            kj = k_ref[pl.ds(start, tk), lanes]
            return lax.dot_general(kj, q_ref[:, lanes], NT_DIMS, preferred_element_type=F32)

        u_all = [scores(hh) for hh in range(hp)]
        for hh in range(hp):
            u = u_all[hh]
            minus_abs = lax.bitcast_convert_type(lax.bitcast_convert_type(u, jnp.uint32) | sign_bit, F32)
            sp = jnp.maximum(u, 0.0) + jnp.log2(1.0 + jnp.exp2(minus_abs))
            if masked:
                sp = jnp.where(mask, sp, 0.0)
            w = jnp.dot(tri_ref[...], sp.astype(BF16), preferred_element_type=F32)
            a = jnp.exp2(u - ((sp + w) + r_scr[hh]))
            if masked:
                a = jnp.where(mask, a, 0.0)
            vtj = vt_ref[hh * HEAD_DIM:(hh + 1) * HEAD_DIM, pl.ds(start, tk)]
            acc_scr[hh] += jnp.dot(vtj, a.astype(BF16), preferred_element_type=F32)
            r_scr[hh] += jnp.sum(sp, axis=0, keepdims=True)

    ratio = max(tq // tk, 1)
    n_full = (i * tq) // tk
    for d in reversed(range(ratio)):
        tile(n_full + d, True)

    def live():
        return (jnp.min(r_scr[...]) <= R_DONE).astype(jnp.int32)

    def cond(carry):
        it, alive = carry
        return (it < n_full) & (alive > 0)

    def body(carry):
        it, _ = carry
        tile(n_full - 1 - it, False)
        return it + 1, live()

    lax.while_loop(cond, body, (jnp.int32(0), live()))
    for hh in range(hp):
        o_ref[:, hh * HEAD_DIM:(hh + 1) * HEAD_DIM] = acc_scr[hh].T.astype(o_ref.dtype)


def _sb_attention(proj, vt, batch, seq, n_heads, col_q, col_k, row_v, tq, tk, hp):
    t = proj.shape[0]
    nq = seq // tq
    wb = hp * HEAD_DIM
    cq, ck, rv = col_q // wb, col_k // wb, row_v // wb
    tri = jnp.triu(jnp.ones((tk, tk), BF16), 1)
    return pl.pallas_call(
        functools.partial(_sb_kernel, tq=tq, tk=tk, hp=hp),
        out_shape=jax.ShapeDtypeStruct((t, n_heads * HEAD_DIM), BF16),
        grid=(batch, n_heads // hp, nq),
        in_specs=[
            pl.BlockSpec((tq, wb), lambda b, g, i: (b * nq + i, cq + g)),
            pl.BlockSpec((seq, wb), lambda b, g, i: (b, ck + g)),
            pl.BlockSpec((wb, seq), lambda b, g, i: (rv + g, b)),
            pl.BlockSpec((tk, tk), lambda b, g, i: (0, 0)),
        ],
        out_specs=pl.BlockSpec((tq, wb), lambda b, g, i: (b * nq + i, g)),
        scratch_shapes=[
            pltpu.VMEM((hp, 1, tq), F32),
            pltpu.VMEM((hp, HEAD_DIM, tq), F32),
        ],
        compiler_params=_params(("parallel", "parallel", "arbitrary")),
        name="sb_attention",
    )(proj, proj, vt, tri)


def _merge_kernel(oa_ref, ob_ref, wa_ref, wb_ref, ga_ref, gb_ref, o_ref):
    ya = jnp.dot(oa_ref[...], wa_ref[...], preferred_element_type=F32)
    yb = jnp.dot(ob_ref[...], wb_ref[...], preferred_element_type=F32)
    ga = jax.nn.sigmoid(ga_ref[...].astype(F32))
    gb = jax.nn.sigmoid(gb_ref[...].astype(F32))
    o_ref[...] = (ga * ya + gb * yb).astype(o_ref.dtype)


def _merge(o_a, o_b, wa, wb, proj, col_ga, col_gb, tm, tn):
    t, ka = o_a.shape
    kb = o_b.shape[1]
    d = wa.shape[1]
    ca, cb = col_ga // tn, col_gb // tn
    return pl.pallas_call(
        _merge_kernel,
        out_shape=jax.ShapeDtypeStruct((t, d), BF16),
        grid=(t // tm, d // tn),
        in_specs=[
            pl.BlockSpec((tm, ka), lambda m, j: (m, 0)),
            pl.BlockSpec((tm, kb), lambda m, j: (m, 0)),
            pl.BlockSpec((ka, tn), lambda m, j: (0, j)),
            pl.BlockSpec((kb, tn), lambda m, j: (0, j)),
            pl.BlockSpec((tm, tn), lambda m, j: (m, ca + j)),
            pl.BlockSpec((tm, tn), lambda m, j: (m, cb + j)),
        ],
        out_specs=pl.BlockSpec((tm, tn), lambda m, j: (m, j)),
        compiler_params=_params(("parallel", "arbitrary")),
        name="branch_merge",
    )(o_a, o_b, wa, wb, proj, proj)


def _mixout_kernel(a_ref, w_ref, x_ref, o_ref):
    o_ref[...] = x_ref[...] + jnp.dot(a_ref[...], w_ref[...], preferred_element_type=F32)


def _mixout(mixed, w, x2d, tm, tn):
    t, k = mixed.shape
    d = w.shape[1]
    return pl.pallas_call(
        _mixout_kernel,
        out_shape=jax.ShapeDtypeStruct((t, d), F32),
        grid=(t // tm, d // tn),
        in_specs=[
            pl.BlockSpec((tm, k), lambda m, j: (m, 0)),
            pl.BlockSpec((k, tn), lambda m, j: (0, j)),
            pl.BlockSpec((tm, tn), lambda m, j: (m, j)),
        ],
        out_specs=pl.BlockSpec((tm, tn), lambda m, j: (m, j)),
        compiler_params=_params(("parallel", "arbitrary")),
        name="mix_out",
    )(mixed, w, x2d)


def _router_kernel(x_ref, g_ref, wh_ref, wl_ref, lt_ref, sel_ref, h_ref, ids_ref, gate_ref, cnt_ref, carry_scr,
                   *, n_groups, per_group):
    @pl.when(pl.program_id(0) == 0)
    def _():
        carry_scr[...] = jnp.zeros(carry_scr.shape, F32)

    h = _rmsnorm_f32(x_ref[...], g_ref[...])
    hh = h.astype(BF16)
    _store_row_tiles(h_ref, _pack_bf16_pairs(h))
    hl = (h - hh.astype(F32)).astype(BF16)
    logits = (jnp.dot(hh, wh_ref[...], preferred_element_type=F32)
              + jnp.dot(hl, wh_ref[...], preferred_element_type=F32)
              + jnp.dot(hh, wl_ref[...], preferred_element_type=F32))
    tm = logits.shape[0]
    n_exp = n_groups * per_group
    lane = lax.broadcasted_iota(jnp.int32, (tm, LANES), 1)
    lane_f = lane.astype(F32)
    neg_inf = -jnp.inf

    def first_argmax(vals, valid):
        vmax = jnp.max(jnp.where(valid, vals, neg_inf), axis=-1, keepdims=True)
        idx = jnp.min(jnp.where(valid & (vals == vmax), lane_f, float(LANES)), axis=-1, keepdims=True)
        return vmax, idx.astype(jnp.int32)

    is_group = lane < n_groups
    gmax, gsel = first_argmax(logits, is_group)
    gsum = jnp.sum(jnp.where(is_group, jnp.exp(logits - gmax), 0.0), axis=-1, keepdims=True)
    p_group = 1.0 / gsum
    lo = n_groups + gsel * per_group
    in_group = (lane >= lo) & (lane < lo + per_group)
    v1, i1 = first_argmax(logits, in_group)
    v2, i2 = first_argmax(logits, in_group & (lane != i1))
    e2w = jnp.exp(v2 - v1)
    gate1 = (1.0 / (1.0 + e2w)) * p_group
    gate2 = (e2w / (1.0 + e2w)) * p_group
    e1 = i1 - n_groups
    e2 = i2 - n_groups

    onehot = ((lane == e1) | (lane == e2)).astype(BF16)
    before = jnp.dot(lt_ref[...], onehot, preferred_element_type=F32) + carry_scr[...]
    rank1 = jnp.sum(jnp.where(lane == e1, before, 0.0), axis=-1, keepdims=True).astype(jnp.int32)
    rank2 = jnp.sum(jnp.where(lane == e2, before, 0.0), axis=-1, keepdims=True).astype(jnp.int32)
    carry_new = carry_scr[...] + jnp.sum(onehot.astype(F32), axis=0, keepdims=True)
    carry_scr[...] = carry_new

    cols = jnp.where(lane == 0, e1, jnp.where(lane == 1, e2, jnp.where(lane == 2, rank1 >> 7,
           jnp.where(lane == 3, rank1 & 127, jnp.where(lane == 4, rank2 >> 7,
           jnp.where(lane == 5, rank2 & 127, 0))))))
    ids_ref[...] = lax.dot_general(sel_ref[...], cols.astype(F32).astype(BF16), NT_DIMS,
                                   preferred_element_type=F32)
    gate_ref[...] = jnp.where(lane == 0, gate1, jnp.where(lane == 1, gate2, 0.0))
    cnt_ref[...] = jnp.broadcast_to(carry_new, cnt_ref.shape).astype(jnp.int32)
    del n_exp


def _router(x2, g, w_hi, w_lo, tm, n_groups, per_group):
    t, d = x2.shape
    lt = jnp.tril(jnp.ones((tm, tm), BF16), -1)
    sel = jnp.eye(8, LANES, dtype=BF16)
    return pl.pallas_call(
        functools.partial(_router_kernel, n_groups=n_groups, per_group=per_group),
        out_shape=(
            jax.ShapeDtypeStruct((t * (d // 2 // LANES), LANES), jnp.uint32),
            jax.ShapeDtypeStruct((8, t), F32),
            jax.ShapeDtypeStruct((t, LANES), F32),
            jax.ShapeDtypeStruct((8, LANES), jnp.int32),
        ),
        grid=(t // tm,),
        in_specs=[
            pl.BlockSpec((tm, d), lambda m: (m, 0)),
            pl.BlockSpec((1, d), lambda m: (0, 0)),
            pl.BlockSpec((d, LANES), lambda m: (0, 0)),
            pl.BlockSpec((d, LANES), lambda m: (0, 0)),
            pl.BlockSpec((tm, tm), lambda m: (0, 0)),
            pl.BlockSpec((8, LANES), lambda m: (0, 0)),
        ],
        out_specs=(
            pl.BlockSpec((tm * (d // 2 // LANES), LANES), lambda m: (m, 0)),
            pl.BlockSpec((8, tm), lambda m: (0, m)),
            pl.BlockSpec((tm, LANES), lambda m: (m, 0)),
            pl.BlockSpec((8, LANES), lambda m: (0, 0)),
        ),
        scratch_shapes=[pltpu.VMEM((1, LANES), F32)],
        compiler_params=_params(("arbitrary",)),
        name="router",
    )(x2, g, w_hi, w_lo, lt, sel)


def _plan_kernel(cnt_ref, ids_ref, dest_ref, be_ref, first_ref, nxt_ref, nu_ref, ps_scr, *, n_exp, bm, nb):
    shift = bm.bit_length() - 1

    def clear(k, c):
        first_ref[k] = 0
        nxt_ref[k] = -1
        return c

    lax.fori_loop(0, nb, clear, 0)

    def per_expert(e, carry):
        blk_start, last, prev_start = carry
        n = lax.shift_right_logical(cnt_ref[0, e] + (bm - 1), shift)
        ps_scr[e] = blk_start * bm

        def fill(k, c):
            be_ref[blk_start + k] = e
            return c

        lax.fori_loop(0, n, fill, 0)

        @pl.when(n > 0)
        def _():
            first_ref[blk_start] = 1

            @pl.when(prev_start >= 0)
            def _():
                nxt_ref[prev_start] = e

        used = n > 0
        return blk_start + n, jnp.where(used, e, last), jnp.where(used, blk_start, prev_start)

    n_used, last, _ = lax.fori_loop(0, n_exp, per_expert, (jnp.int32(0), jnp.int32(0), jnp.int32(-1)))
    nu_ref[0] = n_used

    def tail(k, c):
        be_ref[k] = last
        return c

    lax.fori_loop(n_used, nb, tail, 0)

    e1 = ids_ref[0:1, :]
    e2 = ids_ref[1:2, :]
    p1 = jnp.zeros_like(e1)
    p2 = jnp.zeros_like(e2)
    for e in range(n_exp):
        ps = ps_scr[e].astype(F32)
        p1 = jnp.where(e1 == e, ps, p1)
        p2 = jnp.where(e2 == e, ps, p2)
    dest_ref[...] = jnp.zeros(dest_ref.shape, jnp.int32)
    dest_ref[0:1, :] = (p1 + ids_ref[2:3, :] * 128.0 + ids_ref[3:4, :]).astype(jnp.int32)
    dest_ref[1:2, :] = (p2 + ids_ref[4:5, :] * 128.0 + ids_ref[5:6, :]).astype(jnp.int32)


def _plan(counts, ids_rows, n_exp, bm, nb):
    t = ids_rows.shape[1]
    return pl.pallas_call(
        functools.partial(_plan_kernel, n_exp=n_exp, bm=bm, nb=nb),
        out_shape=(
            jax.ShapeDtypeStruct((8, t), jnp.int32),
            jax.ShapeDtypeStruct((nb,), jnp.int32),
            jax.ShapeDtypeStruct((nb,), jnp.int32),
            jax.ShapeDtypeStruct((nb,), jnp.int32),
            jax.ShapeDtypeStruct((1,), jnp.int32),
        ),
        in_specs=[
            pl.BlockSpec(memory_space=pltpu.SMEM),
            pl.BlockSpec(memory_space=pltpu.VMEM),
        ],
        out_specs=(
            pl.BlockSpec(memory_space=pltpu.VMEM),
            pl.BlockSpec(memory_space=pltpu.SMEM),
            pl.BlockSpec(memory_space=pltpu.SMEM),
            pl.BlockSpec(memory_space=pltpu.SMEM),
            pl.BlockSpec(memory_space=pltpu.SMEM),
        ),
        scratch_shapes=[pltpu.SMEM((n_exp,), jnp.int32)],
        name="moe_plan",
    )(counts, ids_rows)


def _dispatch_kernel(d1_ref, d2_ref, first_ref, nu_ref, h_ref, xb_ref, zbuf, sem, zsem, *, tm, ns, bm, nb):
    base = pl.program_id(0) * tm

    @pl.when(pl.program_id(0) == 0)
    def _():
        zbuf[...] = jnp.zeros(zbuf.shape, zbuf.dtype)
        n_used = nu_ref[0]

        def has_padding(b):
            nxt_first = first_ref[jnp.minimum(b + 1, nb - 1)]
            return (b >= n_used - 1) | (nxt_first == 1)

        def zero_copy(b):
            return pltpu.make_async_copy(zbuf, xb_ref.at[pl.ds(pl.multiple_of(b * (bm * ns), bm * ns), bm * ns)], zsem)

        def zstart(b, carry):
            @pl.when(has_padding(b))
            def _():
                zero_copy(b).start()
            return carry

        def zwait(b, carry):
            @pl.when(has_padding(b))
            def _():
                zero_copy(b).wait()
            return carry

        lax.fori_loop(0, nb, zstart, 0)
        lax.fori_loop(0, nb, zwait, 0)

    def row_copy(r, dest):
        src = h_ref.at[pl.ds(pl.multiple_of(r * ns, ns), ns)]
        return pltpu.make_async_copy(src, xb_ref.at[pl.ds(pl.multiple_of(dest * ns, ns), ns)], sem)

    def start(r, carry):
        row_copy(r, d1_ref[base + r]).start()
        row_copy(r, d2_ref[base + r]).start()
        return carry

    lax.fori_loop(0, tm, start, 0, unroll=DMA_UNROLL)
    for _ in range(2):
        pltpu.make_async_copy(h_ref, xb_ref.at[pl.ds(0, tm * ns)], sem).wait()


def _dispatch(h2p, dest1, dest2, first, n_used, nb, bm, tm, ns):
    t = h2p.shape[0] // ns
    return pl.pallas_call(
        functools.partial(_dispatch_kernel, tm=tm, ns=ns, bm=bm, nb=nb),
        out_shape=jax.ShapeDtypeStruct((nb * bm * ns, LANES), h2p.dtype),
        grid_spec=pltpu.PrefetchScalarGridSpec(
            num_scalar_prefetch=4,
            grid=(t // tm,),
            in_specs=[pl.BlockSpec((tm * ns, LANES), lambda m, d1, d2, fi, nu: (m, 0))],
            out_specs=pl.BlockSpec(memory_space=pl.ANY),
            scratch_shapes=[pltpu.VMEM((bm * ns, LANES), h2p.dtype), pltpu.SemaphoreType.DMA,
                            pltpu.SemaphoreType.DMA],
        ),
        compiler_params=_params(("arbitrary",)),
        name="moe_dispatch",
    )(dest1, dest2, first, n_used, h2p)


def _experts_kernel(be_ref, first_ref, nxt_ref, nu_ref, x_ref, wg_hbm, wu_hbm, wd_hbm, y_ref,
                    wg_buf, wu_buf, wd_buf, wg_scr, wu_scr, wd_scr, sem, run_scr, *, ns):
    b = pl.program_id(0)

    def weight_copies(e, slot):
        return (pltpu.make_async_copy(wg_hbm.at[e], wg_buf.at[slot], sem.at[slot]),
                pltpu.make_async_copy(wu_hbm.at[e], wu_buf.at[slot], sem.at[slot]),
                pltpu.make_async_copy(wd_hbm.at[e], wd_buf.at[slot], sem.at[slot]))

    @pl.when(b == 0)
    def _():
        run_scr[0] = 0
        for cp in weight_copies(be_ref[0], 0):
            cp.start()

    @pl.when((first_ref[b] == 1) & (b < nu_ref[0]))
    def _():
        run = run_scr[0]
        slot = run & 1
        for cp in weight_copies(be_ref[b], slot):
            cp.wait()
        wg_scr[...] = wg_buf[slot].astype(BF16)
        wu_scr[...] = wu_buf[slot].astype(BF16)
        wd_scr[...] = wd_buf[slot].astype(BF16)

        @pl.when(nxt_ref[b] >= 0)
        def _():
            for cp in weight_copies(nxt_ref[b], 1 - slot):
                cp.start()

        run_scr[0] = run + 1

    @pl.when(b < nu_ref[0])
    def _():
        x = _unpack_bf16_pairs(_load_row_tiles(x_ref, ns)).astype(BF16)
        g = jnp.dot(x, wg_scr[...], preferred_element_type=F32)
        u = jnp.dot(x, wu_scr[...], preferred_element_type=F32)
        hid = (g * jax.nn.sigmoid(g) * u).astype(BF16)
        _store_row_tiles(y_ref, _pack_bf16_pairs(jnp.dot(hid, wd_scr[...], preferred_element_type=F32)))


def _experts(xb, w_gate, w_up, w_down, block_expert, first, nxt, n_used, bm, ns):
    d, de = w_gate.shape[1:]
    nb = xb.shape[0] // (bm * ns)

    def row_map(b, be, fi, nx, nu):
        return (jnp.minimum(b, nu[0] - 1), 0)

    return pl.pallas_call(
        functools.partial(_experts_kernel, ns=ns),
        out_shape=jax.ShapeDtypeStruct(xb.shape, jnp.uint32),
        grid_spec=pltpu.PrefetchScalarGridSpec(
            num_scalar_prefetch=4,
            grid=(nb,),
            in_specs=[
                pl.BlockSpec((bm * ns, LANES), row_map),
                pl.BlockSpec(memory_space=pl.ANY),
                pl.BlockSpec(memory_space=pl.ANY),
                pl.BlockSpec(memory_space=pl.ANY),
            ],
            out_specs=pl.BlockSpec((bm * ns, LANES), row_map),
            scratch_shapes=[
                pltpu.VMEM((2, d, de), F32),
                pltpu.VMEM((2, d, de), F32),
                pltpu.VMEM((2, de, d), F32),
                pltpu.VMEM((d, de), BF16),
                pltpu.VMEM((d, de), BF16),
                pltpu.VMEM((de, d), BF16),
                pltpu.SemaphoreType.DMA((2,)),
                pltpu.SMEM((1,), jnp.int32),
            ],
        ),
        input_output_aliases={4: 0},
        compiler_params=_params(("arbitrary",)),
        name="moe_experts",
    )(block_expert, first, nxt, n_used, xb, w_gate, w_up, w_down)


def _ple_kernel(d1_ref, d2_ref, x_ref, mg_ref, yb_ref, g_ref, p_ref, wg_ref, wp_ref, gf_ref, o_ref,
                ybuf, sem, x3_scr, h_scr, p_scr, *, tm, tn, ns):
    m = pl.program_id(0)
    j = pl.program_id(1)

    def gather(blk, slot):
        base = blk * tm

        def row_copy(r, dest, k):
            dst = ybuf.at[slot, k, pl.ds(pl.multiple_of(r * ns, ns), ns)]
            return pltpu.make_async_copy(yb_ref.at[pl.ds(pl.multiple_of(dest * ns, ns), ns)], dst, sem.at[slot])

        def start(r, carry):
            row_copy(r, d1_ref[base + r], 0).start()
            row_copy(r, d2_ref[base + r], 1).start()
            return carry

        lax.fori_loop(0, tm, start, 0, unroll=DMA_UNROLL)

    @pl.when(j == 0)
    def _():
        slot = m & 1

        @pl.when(m == 0)
        def _():
            gather(0, 0)

        @pl.when(m + 1 < pl.num_programs(0))
        def _():
            gather(m + 1, 1 - slot)

        for k in range(2):
            pltpu.make_async_copy(yb_ref.at[pl.ds(0, tm * ns)], ybuf.at[slot, k], sem.at[slot]).wait()
        mg = mg_ref[...]
        y1 = _unpack_bf16_pairs(_load_row_tiles(ybuf.at[slot, 0], ns))
        y2 = _unpack_bf16_pairs(_load_row_tiles(ybuf.at[slot, 1], ns))
        x3 = x_ref[...] + mg[:, 0:1] * y1 + mg[:, 1:2] * y2
        x3_scr[...] = x3
        h_scr[...] = _rmsnorm_f32(x3, g_ref[...]).astype(BF16)
        p_scr[...] = p_ref[...].astype(BF16)

    col = pl.multiple_of(j * tn, tn)
    gate = jnp.dot(h_scr[...], wg_ref[...], preferred_element_type=F32)
    ple = jnp.dot(p_scr[...], wp_ref[...], preferred_element_type=F32)
    o_ref[:, pl.ds(col, tn)] = x3_scr[:, pl.ds(col, tn)] + jax.nn.sigmoid(gate) * ple

    @pl.when(j == pl.num_programs(1) - 1)
    def _():
        o_ref[...] = _rmsnorm_f32(o_ref[...], gf_ref[...])


def _combine_ple_final(x2, moe_gates, yb, dest1, dest2, g_ple, p2d, w_gate, w_proj, g_final, tm, tn, ns):
    t, d = x2.shape
    dp = p2d.shape[1]
    return pl.pallas_call(
        functools.partial(_ple_kernel, tm=tm, tn=tn, ns=ns),
        out_shape=jax.ShapeDtypeStruct((t, d), F32),
        grid_spec=pltpu.PrefetchScalarGridSpec(
            num_scalar_prefetch=2,
            grid=(t // tm, d // tn),
            in_specs=[
                pl.BlockSpec((tm, d), lambda m, j, d1, d2: (m, 0)),
                pl.BlockSpec((tm, LANES), lambda m, j, d1, d2: (m, 0)),
                pl.BlockSpec(memory_space=pl.ANY),
                pl.BlockSpec((1, d), lambda m, j, d1, d2: (0, 0)),
                pl.BlockSpec((tm, dp), lambda m, j, d1, d2: (m, 0)),
                pl.BlockSpec((d, tn), lambda m, j, d1, d2: (0, j)),
                pl.BlockSpec((dp, tn), lambda m, j, d1, d2: (0, j)),
                pl.BlockSpec((1, d), lambda m, j, d1, d2: (0, 0)),
            ],
            out_specs=pl.BlockSpec((tm, d), lambda m, j, d1, d2: (m, 0)),
            scratch_shapes=[
                pltpu.VMEM((2, 2, tm * ns, LANES), yb.dtype),
                pltpu.SemaphoreType.DMA((2,)),
                pltpu.VMEM((tm, d), F32),
                pltpu.VMEM((tm, d), BF16),
                pltpu.VMEM((tm, dp), BF16),
            ],
        ),
        compiler_params=_params(("arbitrary", "arbitrary")),
        name="combine_ple_final",
    )(dest1, dest2, x2, moe_gates, yb, g_ple, p2d, w_gate, w_proj, g_final)


def _tile(n, pref):
    return pref if n % pref == 0 else n


def _layer(x2d, p2d, batch, seq, w_in, b_forget, w_branch_fox, w_branch_sb, w_mix_out, g_mix, g_ffn,
           w_group, w_expert, w_gate, w_up, w_down, g_ple, w_ple_proj, w_ple_gate, g_final):
    t, d = x2d.shape
    n_heads = b_forget.shape[0]
    w_att = n_heads * HEAD_DIM
    n_exp = w_expert.shape[1]

    qs = LOG2E * HEAD_DIM ** -0.5
    w_main, wv_t, wf = _wprep(w_in, w_att, n_heads, qs, _tile(d, 256))
    b_row = jnp.pad(b_forget, (0, LANES - n_heads)).reshape(1, LANES)
    col_qa, col_ka, col_qb, col_kb = 0, w_att, 2 * w_att, 3 * w_att
    col_ga, col_gb = 4 * w_att, 4 * w_att + d
    row_va, row_vb = 0, w_att

    tm = _tile(t, 1024)
    tn = _tile(d, 512)
    proj, vt, f_tok = _inproj(x2d, g_mix.reshape(1, d), w_main, wv_t, wf, tm, 512)
    c = _forget_cumsum(f_tok, b_row, batch, n_heads)

    tq = _tile(seq, 512)
    hp = 4
    o_a = _fox_attention(proj, vt, c, batch, seq, n_heads, col_qa, col_ka, row_va, tq, hp)
    o_b = _sb_attention(proj, vt, batch, seq, n_heads, col_qb, col_kb, row_vb, tq, _tile(seq, 256), hp)

    mixed = _merge(o_a, o_b, w_branch_fox.astype(BF16), w_branch_sb.astype(BF16), proj, col_ga, col_gb, tm, tn)
    x2 = _mixout(mixed, w_mix_out.astype(BF16), x2d, tm, tn)

    w_r = jnp.concatenate([w_group, w_expert], axis=1)
    w_r = jnp.pad(w_r, ((0, 0), (0, LANES - w_r.shape[1])))
    w_r_hi = w_r.astype(BF16)
    w_r_lo = (w_r - w_r_hi.astype(F32)).astype(BF16)
    tr = _tile(t, 512)
    h2, ids_rows, gates, counts = _router(x2, g_ffn.reshape(1, d), w_r_hi, w_r_lo, tr, N_GROUPS, EXPERTS_PER_GROUP)

    bm = 256
    n_assign = 2 * t
    nb = n_assign // bm + n_exp
    dest, block_expert, first, nxt, n_used = _plan(counts, ids_rows, n_exp, bm, nb)
    dest1, dest2 = dest[0], dest[1]

    td = _tile(t, 1024)
    ns = d // 2 // LANES
    xb = _dispatch(h2, dest1, dest2, first, n_used, nb, bm, td, ns)
    yb = _experts(xb, w_gate, w_up, w_down, block_expert, first, nxt, n_used, bm, ns)

    tp = _tile(t, 512)
    return _combine_ple_final(x2, gates, yb, dest1, dest2, g_ple.reshape(1, d), p2d, w_ple_gate.astype(BF16),
                              w_ple_proj.astype(BF16), g_final.reshape(1, d), tp, tn, ns)


def kernel(x, p, w_in, b_forget, w_branch_fox, w_branch_sb, w_mix_out, g_mix, g_ffn, w_group, w_expert,
           w_gate, w_up, w_down, g_ple, w_ple_proj, w_ple_gate, g_final):
    b, s, d = x.shape
    depth = w_in.shape[0]
    assert depth == 1, "the final norm is fused into the single layer"
    x2d = x.reshape(b * s, d)
    out = _layer(x2d, p[0].reshape(b * s, -1), b, s, w_in[0], b_forget[0], w_branch_fox[0], w_branch_sb[0],
                 w_mix_out[0], g_mix[0], g_ffn[0], w_group[0], w_expert[0], w_gate[0], w_up[0], w_down[0],
                 g_ple[0], w_ple_proj[0], w_ple_gate[0], g_final)
    return out.reshape(b, s, d)
```

```python
import functools

import jax
import jax.numpy as jnp
from jax import lax
from jax.experimental import pallas as pl
from jax.experimental.pallas import tpu as pltpu

F32 = jnp.float32
BF16 = jnp.bfloat16

HEAD_DIM = 128
N_GROUPS = 4
EXPERTS_PER_GROUP = 8
EPS = 1e-6
LANES = 128
SUBLANES = 8
VMEM_LIMIT = 56 * 1024 * 1024

LOG2E = 1.4426950408889634
DMA_UNROLL = 8
R_DONE = 160.0

NT_DIMS = (((1,), (1,)), ((), ()))


def _params(sem):
    return pltpu.CompilerParams(dimension_semantics=sem, vmem_limit_bytes=VMEM_LIMIT)


def _rmsnorm_f32(x, g):
    ms = jnp.mean(x * x, axis=-1, keepdims=True)
    return x * lax.rsqrt(ms + EPS) * g


def _pack_bf16_pairs(x):
    n = x.shape[1] // 2
    bits = lax.bitcast_convert_type(x.astype(BF16).astype(F32), jnp.uint32)
    return (bits[:, :n] >> 16) | (bits[:, n:] & jnp.uint32(0xFFFF0000))


def _unpack_bf16_pairs(w):
    lo = lax.bitcast_convert_type(w << 16, F32)
    hi = lax.bitcast_convert_type(w & jnp.uint32(0xFFFF0000), F32)
    return jnp.concatenate([lo, hi], axis=1)


def _store_row_tiles(ref, words):
    n = words.shape[0]
    ns = words.shape[1] // LANES
    for c in range(ns):
        ref[pl.ds(c, n, stride=ns), :] = words[:, c * LANES:(c + 1) * LANES]


def _load_row_tiles(ref, ns):
    n = ref.shape[0] // ns
    return jnp.concatenate([ref[pl.ds(c, n, stride=ns), :] for c in range(ns)], axis=1)


def _wprep_kernel(off_ref, scale_ref, w_ref, f_ref, wt_ref, wf_ref, *, n_heads):
    i = pl.program_id(0)
    wt_ref[...] = (w_ref[...] * scale_ref[i]).astype(BF16)

    @pl.when(i == 0)
    def _():
        row = lax.broadcasted_iota(jnp.int32, wf_ref.shape, 0)
        f_rows = jnp.concatenate([f_ref[...]] * (wf_ref.shape[0] // f_ref.shape[0]), axis=0)
        wf_ref[...] = jnp.where(row < n_heads, f_rows, 0.0).astype(BF16)


def _wprep(w_t, w_att, n_heads, qs, rb):
    n_in, d = w_t.shape
    off_b = 3 * w_att + n_heads
    segments = [(2 * w_att, w_att, 1.0), (off_b + 2 * w_att, w_att, 1.0),
                (0, w_att, qs), (w_att, w_att, 1.0),
                (off_b, w_att, qs), (off_b + w_att, w_att, 1.0),
                (off_b + 3 * w_att, n_in - off_b - 3 * w_att, 1.0)]
    offs, scales = [], []
    for start, length, scale in segments:
        assert length % rb == 0 and start % SUBLANES == 0
        for k in range(length // rb):
            offs.append((start + k * rb) // SUBLANES)
            scales.append(scale)
    n_out = len(offs) * rb
    f_blk = SUBLANES
    assert n_heads <= f_blk and (3 * w_att) % f_blk == 0
    return pl.pallas_call(
        functools.partial(_wprep_kernel, n_heads=n_heads),
        out_shape=(jax.ShapeDtypeStruct((n_out, d), BF16), jax.ShapeDtypeStruct((LANES, d), BF16)),
        grid_spec=pltpu.PrefetchScalarGridSpec(
            num_scalar_prefetch=2,
            grid=(len(offs),),
            in_specs=[
                pl.BlockSpec((pl.Element(rb), pl.Element(d)), lambda i, off, sc: (off[i] * SUBLANES, 0)),
                pl.BlockSpec((pl.Element(f_blk), pl.Element(d)), lambda i, off, sc: (3 * w_att, 0)),
            ],
            out_specs=(
                pl.BlockSpec((rb, d), lambda i, off, sc: (i, 0)),
                pl.BlockSpec((LANES, d), lambda i, off, sc: (0, 0)),
            ),
        ),
        compiler_params=_params(("arbitrary",)),
        name="inproj_weights",
    )(jnp.asarray(offs, jnp.int32), jnp.asarray(scales, F32), w_t, w_t)


def _inproj_kernel(x_ref, g_ref, w_ref, wf_ref, o_ref, vt_ref, f_ref, h_scr, *, nvt):
    j = pl.program_id(1)

    @pl.when(j == 0)
    def _():
        hb = _rmsnorm_f32(x_ref[...], g_ref[...]).astype(BF16)
        h_scr[...] = hb
        f_ref[...] = lax.dot_general(hb, wf_ref[...], NT_DIMS, preferred_element_type=F32)

    @pl.when(j < nvt)
    def _():
        vt_ref[...] = lax.dot_general(w_ref[...], h_scr[...], NT_DIMS,
                                      preferred_element_type=F32).astype(vt_ref.dtype)

    @pl.when(j >= nvt)
    def _():
        o_ref[...] = lax.dot_general(h_scr[...], w_ref[...], NT_DIMS,
                                     preferred_element_type=F32).astype(o_ref.dtype)


def _inproj(x2d, g, w_t, wf_t, n_v, tm, tn):
    t, d = x2d.shape
    n = w_t.shape[0] - n_v
    nvt = n_v // tn
    return pl.pallas_call(
        functools.partial(_inproj_kernel, nvt=nvt),
        out_shape=(jax.ShapeDtypeStruct((t, n), BF16), jax.ShapeDtypeStruct((n_v, t), BF16),
                   jax.ShapeDtypeStruct((t, LANES), F32)),
        grid=(t // tm, nvt + n // tn),
        in_specs=[
            pl.BlockSpec((tm, d), lambda m, j: (m, 0)),
            pl.BlockSpec((1, d), lambda m, j: (0, 0)),
            pl.BlockSpec((tn, d), lambda m, j: (j, 0)),
            pl.BlockSpec((LANES, d), lambda m, j: (0, 0)),
        ],
        out_specs=(
            pl.BlockSpec((tm, tn), lambda m, j: (m, jnp.maximum(j - nvt, 0))),
            pl.BlockSpec((tn, tm), lambda m, j: (jnp.minimum(j, nvt - 1), m)),
            pl.BlockSpec((tm, LANES), lambda m, j: (m, 0)),
        ),
        scratch_shapes=[pltpu.VMEM((tm, d), BF16)],
        compiler_params=_params(("parallel", "arbitrary")),
        name="inproj",
    )(x2d, g, w_t, wf_t)


def _split3(x):
    p1 = x.astype(BF16)
    r1 = x - p1.astype(F32)
    p2 = r1.astype(BF16)
    p3 = (r1 - p2.astype(F32)).astype(BF16)
    return p1, p2, p3


def _cumsum_kernel(f_ref, b_ref, lt_ref, c_ref, carry_scr, *, chunk, n_heads):
    @pl.when(pl.program_id(1) == 0)
    def _():
        carry_scr[...] = jnp.zeros(carry_scr.shape, F32)

    rows = f_ref.shape[0]
    lt = lt_ref[...]
    carry = carry_scr[...]
    for i in range(rows // chunk):
        lf = jax.nn.log_sigmoid(f_ref[i * chunk:(i + 1) * chunk, :] + b_ref[...])
        p1, p2, p3 = _split3(lf)
        cs = (jnp.dot(lt, p1, preferred_element_type=F32)
              + jnp.dot(lt, p2, preferred_element_type=F32)
              + jnp.dot(lt, p3, preferred_element_type=F32)) + carry
        for h in range(n_heads):
            c_ref[h, i * chunk:(i + 1) * chunk, :] = jnp.broadcast_to(cs[:, h:h + 1] * LOG2E, (chunk, LANES))
        carry = cs[chunk - 1:chunk, :]
    carry_scr[...] = carry


def _forget_cumsum(f_tok, b_row, batch, n_heads):
    t = f_tok.shape[0]
    s = t // batch
    rows = min(1024, s)
    chunk = min(256, s)
    lt = jnp.tril(jnp.ones((chunk, chunk), BF16))
    nr = s // rows
    return pl.pallas_call(
        functools.partial(_cumsum_kernel, chunk=chunk, n_heads=n_heads),
        out_shape=jax.ShapeDtypeStruct((n_heads, t, LANES), F32),
        grid=(batch, nr),
        in_specs=[
            pl.BlockSpec((rows, LANES), lambda b, r: (b * nr + r, 0)),
            pl.BlockSpec((1, LANES), lambda b, r: (0, 0)),
            pl.BlockSpec((chunk, chunk), lambda b, r: (0, 0)),
        ],
        out_specs=pl.BlockSpec((n_heads, rows, LANES), lambda b, r: (0, b * nr + r, 0)),
        scratch_shapes=[pltpu.VMEM((1, LANES), F32)],
        compiler_params=_params(("parallel", "arbitrary")),
        name="forget_cumsum",
    )(f_tok, b_row, lt)


def _fox_kernel(q_ref, k_ref, vt_ref, c_ref, o_ref, m_scr, l_scr, acc_scr, *, tq, tk, hp):
    i = pl.program_id(2)
    m_scr[...] = jnp.full(m_scr.shape, -jnp.inf, F32)
    l_scr[...] = jnp.zeros(l_scr.shape, F32)
    acc_scr[...] = jnp.zeros(acc_scr.shape, F32)

    def tile(j, masked):
        start = pl.multiple_of(j * tk, tk)
        if masked:
            key = j * tk + lax.broadcasted_iota(jnp.int32, (tk, tq), 0)
            qry = i * tq + lax.broadcasted_iota(jnp.int32, (tk, tq), 1)
            keep = key <= qry
        def scores(hh):
            lanes = slice(hh * HEAD_DIM, (hh + 1) * HEAD_DIM)
            kj = k_ref[pl.ds(start, tk), lanes]
            return lax.dot_general(kj, q_ref[:, lanes], NT_DIMS, preferred_element_type=F32)

        s_all = [scores(hh) for hh in range(hp)]
        for hh in range(hp):
            s = s_all[hh]
            cj = c_ref[hh, pl.ds(start, tk), :]
            s = s - jnp.concatenate([cj] * (tq // LANES), axis=1)
            if masked:
                s = jnp.where(keep, s, -jnp.inf)
            m_prev = m_scr[hh]
            m_new = jnp.maximum(m_prev, jnp.max(s, axis=0, keepdims=True))
            alpha = jnp.exp2(m_prev - m_new)
            p = jnp.exp2(s - m_new)
            l_scr[hh] = alpha * l_scr[hh] + jnp.sum(p, axis=0, keepdims=True)
            vtj = vt_ref[hh * HEAD_DIM:(hh + 1) * HEAD_DIM, pl.ds(start, tk)]
            acc_scr[hh] = alpha * acc_scr[hh] + jnp.dot(vtj, p.astype(BF16), preferred_element_type=F32)
            m_scr[hh] = m_new

    def body(j, carry):
        tile(j, False)
        return carry

    lax.fori_loop(0, i, body, 0)
    tile(i, True)
    for hh in range(hp):
        o = acc_scr[hh] / l_scr[hh]
        o_ref[:, hh * HEAD_DIM:(hh + 1) * HEAD_DIM] = o.T.astype(o_ref.dtype)


def _fox_attention(proj, vt, c, batch, seq, n_heads, col_q, col_k, row_v, tq, hp):
    t = proj.shape[0]
    nq = seq // tq
    wb = hp * HEAD_DIM
    cq, ck, rv = col_q // wb, col_k // wb, row_v // wb
    return pl.pallas_call(
        functools.partial(_fox_kernel, tq=tq, tk=tq, hp=hp),
        out_shape=jax.ShapeDtypeStruct((t, n_heads * HEAD_DIM), BF16),
        grid=(batch, n_heads // hp, nq),
        in_specs=[
            pl.BlockSpec((tq, wb), lambda b, g, i: (b * nq + i, cq + g)),
            pl.BlockSpec((seq, wb), lambda b, g, i: (b, ck + g)),
            pl.BlockSpec((wb, seq), lambda b, g, i: (rv + g, b)),
            pl.BlockSpec((hp, seq, LANES), lambda b, g, i: (g, b, 0)),
        ],
        out_specs=pl.BlockSpec((tq, wb), lambda b, g, i: (b * nq + i, g)),
        scratch_shapes=[
            pltpu.VMEM((hp, 1, tq), F32),
            pltpu.VMEM((hp, 1, tq), F32),
            pltpu.VMEM((hp, HEAD_DIM, tq), F32),
        ],
        compiler_params=_params(("parallel", "parallel", "arbitrary")),
        name="fox_attention",
    )(proj, proj, vt, c)


def _sb_kernel(q_ref, k_ref, vt_ref, tri_ref, o_ref, r_scr, acc_scr, *, tq, tk, hp):
    i = pl.program_id(2)
    r_scr[...] = jnp.zeros(r_scr.shape, F32)
    acc_scr[...] = jnp.zeros(acc_scr.shape, F32)

    sign_bit = jnp.uint32(0x80000000)

    def tile(j, masked):
        start = pl.multiple_of(j * tk, tk)
        if masked:
            key = j * tk + lax.broadcasted_iota(jnp.int32, (tk, tq), 0)
            qry = i * tq + lax.broadcasted_iota(jnp.int32, (tk, tq), 1)
            mask = key < qry

        def scores(hh):
            lanes = slice(hh * HEAD_DIM, (hh + 1) * HEAD_DIM)
            kj = k_ref[pl.ds(start, tk), lanes]
            return lax.dot_general(kj, q_ref[:, lanes], NT_DIMS, preferred_element_type=F32)

        u_all = [scores(hh) for hh in range(hp)]
        for hh in range(hp):
            u = u_all[hh]
            minus_abs = lax.bitcast_convert_type(lax.bitcast_convert_type(u, jnp.uint32) | sign_bit, F32)
            sp = jnp.maximum(u, 0.0) + jnp.log2(1.0 + jnp.exp2(minus_abs))
            if masked:
                sp = jnp.where(mask, sp, 0.0)
            w = jnp.dot(tri_ref[...], sp.astype(BF16), preferred_element_type=F32)
            a = jnp.exp2(u - ((sp + w) + r_scr[hh]))
            if masked:
                a = jnp.where(mask, a, 0.0)
            vtj = vt_ref[hh * HEAD_DIM:(hh + 1) * HEAD_DIM, pl.ds(start, tk)]
            acc_scr[hh] += jnp.dot(vtj, a.astype(BF16), preferred_element_type=F32)
            r_scr[hh] += jnp.sum(sp, axis=0, keepdims=True)

    ratio = max(tq // tk, 1)
    n_full = (i * tq) // tk
    for d in reversed(range(ratio)):
        tile(n_full + d, True)

    def live():
        return (jnp.min(r_scr[...]) <= R_DONE).astype(jnp.int32)

    def cond(carry):
        it, alive = carry
        return (it < n_full) & (alive > 0)

    def body(carry):
        it, _ = carry
        tile(n_full - 1 - it, False)
        return it + 1, live()

    lax.while_loop(cond, body, (jnp.int32(0), live()))
    for hh in range(hp):
        o_ref[:, hh * HEAD_DIM:(hh + 1) * HEAD_DIM] = acc_scr[hh].T.astype(o_ref.dtype)


def _sb_attention(proj, vt, batch, seq, n_heads, col_q, col_k, row_v, tq, tk, hp):
    t = proj.shape[0]
    nq = seq // tq
    wb = hp * HEAD_DIM
    cq, ck, rv = col_q // wb, col_k // wb, row_v // wb
    tri = jnp.triu(jnp.ones((tk, tk), BF16), 1)
    return pl.pallas_call(
        functools.partial(_sb_kernel, tq=tq, tk=tk, hp=hp),
        out_shape=jax.ShapeDtypeStruct((t, n_heads * HEAD_DIM), BF16),
        grid=(batch, n_heads // hp, nq),
        in_specs=[
            pl.BlockSpec((tq, wb), lambda b, g, i: (b * nq + i, cq + g)),
            pl.BlockSpec((seq, wb), lambda b, g, i: (b, ck + g)),
            pl.BlockSpec((wb, seq), lambda b, g, i: (rv + g, b)),
            pl.BlockSpec((tk, tk), lambda b, g, i: (0, 0)),
        ],
        out_specs=pl.BlockSpec((tq, wb), lambda b, g, i: (b * nq + i, g)),
        scratch_shapes=[
            pltpu.VMEM((hp, 1, tq), F32),
            pltpu.VMEM((hp, HEAD_DIM, tq), F32),
        ],
        compiler_params=_params(("parallel", "parallel", "arbitrary")),
        name="sb_attention",
    )(proj, proj, vt, tri)


def _merge_kernel(oa_ref, ob_ref, wa_ref, wb_ref, ga_ref, gb_ref, o_ref):
    ya = jnp.dot(oa_ref[...], wa_ref[...], preferred_element_type=F32)
    yb = jnp.dot(ob_ref[...], wb_ref[...], preferred_element_type=F32)
    ga = jax.nn.sigmoid(ga_ref[...].astype(F32))
    gb = jax.nn.sigmoid(gb_ref[...].astype(F32))
    o_ref[...] = (ga * ya + gb * yb).astype(o_ref.dtype)


def _merge(o_a, o_b, wa, wb, proj, col_ga, col_gb, tm, tn):
    t, ka = o_a.shape
    kb = o_b.shape[1]
    d = wa.shape[1]
    ca, cb = col_ga // tn, col_gb // tn
    return pl.pallas_call(
        _merge_kernel,
        out_shape=jax.ShapeDtypeStruct((t, d), BF16),
        grid=(t // tm, d // tn),
        in_specs=[
            pl.BlockSpec((tm, ka), lambda m, j: (m, 0)),
            pl.BlockSpec((tm, kb), lambda m, j: (m, 0)),
            pl.BlockSpec((ka, tn), lambda m, j: (0, j)),
            pl.BlockSpec((kb, tn), lambda m, j: (0, j)),
            pl.BlockSpec((tm, tn), lambda m, j: (m, ca + j)),
            pl.BlockSpec((tm, tn), lambda m, j: (m, cb + j)),
        ],
        out_specs=pl.BlockSpec((tm, tn), lambda m, j: (m, j)),
        compiler_params=_params(("parallel", "arbitrary")),
        name="branch_merge",
    )(o_a, o_b, wa, wb, proj, proj)


def _mixout_kernel(a_ref, w_ref, x_ref, o_ref):
    o_ref[...] = x_ref[...] + jnp.dot(a_ref[...], w_ref[...], preferred_element_type=F32)


def _mixout(mixed, w, x2d, tm, tn):
    t, k = mixed.shape
    d = w.shape[1]
    return pl.pallas_call(
        _mixout_kernel,
        out_shape=jax.ShapeDtypeStruct((t, d), F32),
        grid=(t // tm, d // tn),
        in_specs=[
            pl.BlockSpec((tm, k), lambda m, j: (m, 0)),
            pl.BlockSpec((k, tn), lambda m, j: (0, j)),
            pl.BlockSpec((tm, tn), lambda m, j: (m, j)),
        ],
        out_specs=pl.BlockSpec((tm, tn), lambda m, j: (m, j)),
        compiler_params=_params(("parallel", "arbitrary")),
        name="mix_out",
    )(mixed, w, x2d)


def _router_kernel(x_ref, g_ref, wh_ref, wl_ref, lt_ref, sel_ref, h_ref, ids_ref, gate_ref, cnt_ref, carry_scr,
                   *, n_groups, per_group):
    @pl.when(pl.program_id(0) == 0)
    def _():
        carry_scr[...] = jnp.zeros(carry_scr.shape, F32)

    h = _rmsnorm_f32(x_ref[...], g_ref[...])
    hh = h.astype(BF16)
    _store_row_tiles(h_ref, _pack_bf16_pairs(h))
    hl = (h - hh.astype(F32)).astype(BF16)
    logits = (jnp.dot(hh, wh_ref[...], preferred_element_type=F32)
              + jnp.dot(hl, wh_ref[...], preferred_element_type=F32)
              + jnp.dot(hh, wl_ref[...], preferred_element_type=F32))
    tm = logits.shape[0]
    n_exp = n_groups * per_group
    lane = lax.broadcasted_iota(jnp.int32, (tm, LANES), 1)
    lane_f = lane.astype(F32)
    neg_inf = -jnp.inf

    def first_argmax(vals, valid):
        vmax = jnp.max(jnp.where(valid, vals, neg_inf), axis=-1, keepdims=True)
        idx = jnp.min(jnp.where(valid & (vals == vmax), lane_f, float(LANES)), axis=-1, keepdims=True)
        return vmax, idx.astype(jnp.int32)

    is_group = lane < n_groups
    gmax, gsel = first_argmax(logits, is_group)
    gsum = jnp.sum(jnp.where(is_group, jnp.exp(logits - gmax), 0.0), axis=-1, keepdims=True)
    p_group = 1.0 / gsum
    lo = n_groups + gsel * per_group
    in_group = (lane >= lo) & (lane < lo + per_group)
    v1, i1 = first_argmax(logits, in_group)
    v2, i2 = first_argmax(logits, in_group & (lane != i1))
    e2w = jnp.exp(v2 - v1)
    gate1 = (1.0 / (1.0 + e2w)) * p_group
    gate2 = (e2w / (1.0 + e2w)) * p_group
    e1 = i1 - n_groups
    e2 = i2 - n_groups

    onehot = ((lane == e1) | (lane == e2)).astype(BF16)
    before = jnp.dot(lt_ref[...], onehot, preferred_element_type=F32) + carry_scr[...]
    rank1 = jnp.sum(jnp.where(lane == e1, before, 0.0), axis=-1, keepdims=True).astype(jnp.int32)
    rank2 = jnp.sum(jnp.where(lane == e2, before, 0.0), axis=-1, keepdims=True).astype(jnp.int32)
    carry_new = carry_scr[...] + jnp.sum(onehot.astype(F32), axis=0, keepdims=True)
    carry_scr[...] = carry_new

    cols = jnp.where(lane == 0, e1, jnp.where(lane == 1, e2, jnp.where(lane == 2, rank1 >> 7,
           jnp.where(lane == 3, rank1 & 127, jnp.where(lane == 4, rank2 >> 7,
           jnp.where(lane == 5, rank2 & 127, 0))))))
    ids_ref[...] = lax.dot_general(sel_ref[...], cols.astype(F32).astype(BF16), NT_DIMS,
                                   preferred_element_type=F32)
    gate_ref[...] = jnp.where(lane == 0, gate1, jnp.where(lane == 1, gate2, 0.0))
    cnt_ref[...] = jnp.broadcast_to(carry_new, cnt_ref.shape).astype(jnp.int32)
    del n_exp


def _router(x2, g, w_hi, w_lo, tm, n_groups, per_group):
    t, d = x2.shape
    lt = jnp.tril(jnp.ones((tm, tm), BF16), -1)
    sel = jnp.eye(8, LANES, dtype=BF16)
    return pl.pallas_call(
        functools.partial(_router_kernel, n_groups=n_groups, per_group=per_group),
        out_shape=(
            jax.ShapeDtypeStruct((t * (d // 2 // LANES), LANES), jnp.uint32),
            jax.ShapeDtypeStruct((8, t), F32),
            jax.ShapeDtypeStruct((t, LANES), F32),
            jax.ShapeDtypeStruct((8, LANES), jnp.int32),
        ),
        grid=(t // tm,),
        in_specs=[
            pl.BlockSpec((tm, d), lambda m: (m, 0)),
            pl.BlockSpec((1, d), lambda m: (0, 0)),
            pl.BlockSpec((d, LANES), lambda m: (0, 0)),
            pl.BlockSpec((d, LANES), lambda m: (0, 0)),
            pl.BlockSpec((tm, tm), lambda m: (0, 0)),
            pl.BlockSpec((8, LANES), lambda m: (0, 0)),
        ],
        out_specs=(
            pl.BlockSpec((tm * (d // 2 // LANES), LANES), lambda m: (m, 0)),
            pl.BlockSpec((8, tm), lambda m: (0, m)),
            pl.BlockSpec((tm, LANES), lambda m: (m, 0)),
            pl.BlockSpec((8, LANES), lambda m: (0, 0)),
        ),
        scratch_shapes=[pltpu.VMEM((1, LANES), F32)],
        compiler_params=_params(("arbitrary",)),
        name="router",
    )(x2, g, w_hi, w_lo, lt, sel)


def _plan_kernel(cnt_ref, ids_ref, dest_ref, be_ref, first_ref, nxt_ref, nu_ref, ps_scr, *, n_exp, bm, nb):
    shift = bm.bit_length() - 1

    def clear(k, c):
        first_ref[k] = 0
        nxt_ref[k] = -1
        return c

    lax.fori_loop(0, nb, clear, 0)

    def per_expert(e, carry):
        blk_start, last, prev_start = carry
        n = lax.shift_right_logical(cnt_ref[0, e] + (bm - 1), shift)
        ps_scr[e] = blk_start * bm

        def fill(k, c):
            be_ref[blk_start + k] = e
            return c

        lax.fori_loop(0, n, fill, 0)

        @pl.when(n > 0)
        def _():
            first_ref[blk_start] = 1

            @pl.when(prev_start >= 0)
            def _():
                nxt_ref[prev_start] = e

        used = n > 0
        return blk_start + n, jnp.where(used, e, last), jnp.where(used, blk_start, prev_start)

    n_used, last, _ = lax.fori_loop(0, n_exp, per_expert, (jnp.int32(0), jnp.int32(0), jnp.int32(-1)))
    nu_ref[0] = n_used

    def tail(k, c):
        be_ref[k] = last
        return c

    lax.fori_loop(n_used, nb, tail, 0)

    e1 = ids_ref[0:1, :]
    e2 = ids_ref[1:2, :]
    p1 = jnp.zeros_like(e1)
    p2 = jnp.zeros_like(e2)
    for e in range(n_exp):
        ps = ps_scr[e].astype(F32)
        p1 = jnp.where(e1 == e, ps, p1)
        p2 = jnp.where(e2 == e, ps, p2)
    dest_ref[...] = jnp.zeros(dest_ref.shape, jnp.int32)
    dest_ref[0:1, :] = (p1 + ids_ref[2:3, :] * 128.0 + ids_ref[3:4, :]).astype(jnp.int32)
    dest_ref[1:2, :] = (p2 + ids_ref[4:5, :] * 128.0 + ids_ref[5:6, :]).astype(jnp.int32)


def _plan(counts, ids_rows, n_exp, bm, nb):
    t = ids_rows.shape[1]
    return pl.pallas_call(
        functools.partial(_plan_kernel, n_exp=n_exp, bm=bm, nb=nb),
        out_shape=(
            jax.ShapeDtypeStruct((8, t), jnp.int32),
            jax.ShapeDtypeStruct((nb,), jnp.int32),
            jax.ShapeDtypeStruct((nb,), jnp.int32),
            jax.ShapeDtypeStruct((nb,), jnp.int32),
            jax.ShapeDtypeStruct((1,), jnp.int32),
        ),
        in_specs=[
            pl.BlockSpec(memory_space=pltpu.SMEM),
            pl.BlockSpec(memory_space=pltpu.VMEM),
        ],
        out_specs=(
            pl.BlockSpec(memory_space=pltpu.VMEM),
            pl.BlockSpec(memory_space=pltpu.SMEM),
            pl.BlockSpec(memory_space=pltpu.SMEM),
            pl.BlockSpec(memory_space=pltpu.SMEM),
            pl.BlockSpec(memory_space=pltpu.SMEM),
        ),
        scratch_shapes=[pltpu.SMEM((n_exp,), jnp.int32)],
        name="moe_plan",
    )(counts, ids_rows)


def _dispatch_kernel(d1_ref, d2_ref, first_ref, nu_ref, h_ref, xb_ref, zbuf, sem, zsem, *, tm, ns, bm, nb):
    base = pl.program_id(0) * tm

    @pl.when(pl.program_id(0) == 0)
    def _():
        zbuf[...] = jnp.zeros(zbuf.shape, zbuf.dtype)
        n_used = nu_ref[0]

        def has_padding(b):
            nxt_first = first_ref[jnp.minimum(b + 1, nb - 1)]
            return (b >= n_used - 1) | (nxt_first == 1)

        def zero_copy(b):
            return pltpu.make_async_copy(zbuf, xb_ref.at[pl.ds(pl.multiple_of(b * (bm * ns), bm * ns), bm * ns)], zsem)

        def zstart(b, carry):
            @pl.when(has_padding(b))
            def _():
                zero_copy(b).start()
            return carry

        def zwait(b, carry):
            @pl.when(has_padding(b))
            def _():
                zero_copy(b).wait()
            return carry

        lax.fori_loop(0, nb, zstart, 0)
        lax.fori_loop(0, nb, zwait, 0)

    def row_copy(r, dest):
        src = h_ref.at[pl.ds(pl.multiple_of(r * ns, ns), ns)]
        return pltpu.make_async_copy(src, xb_ref.at[pl.ds(pl.multiple_of(dest * ns, ns), ns)], sem)

    def start(r, carry):
        row_copy(r, d1_ref[base + r]).start()
        row_copy(r, d2_ref[base + r]).start()
        return carry

    lax.fori_loop(0, tm, start, 0, unroll=DMA_UNROLL)
    for _ in range(2):
        pltpu.make_async_copy(h_ref, xb_ref.at[pl.ds(0, tm * ns)], sem).wait()


def _dispatch(h2p, dest1, dest2, first, n_used, nb, bm, tm, ns):
    t = h2p.shape[0] // ns
    return pl.pallas_call(
        functools.partial(_dispatch_kernel, tm=tm, ns=ns, bm=bm, nb=nb),
        out_shape=jax.ShapeDtypeStruct((nb * bm * ns, LANES), h2p.dtype),
        grid_spec=pltpu.PrefetchScalarGridSpec(
            num_scalar_prefetch=4,
            grid=(t // tm,),
            in_specs=[pl.BlockSpec((tm * ns, LANES), lambda m, d1, d2, fi, nu: (m, 0))],
            out_specs=pl.BlockSpec(memory_space=pl.ANY),
            scratch_shapes=[pltpu.VMEM((bm * ns, LANES), h2p.dtype), pltpu.SemaphoreType.DMA,
                            pltpu.SemaphoreType.DMA],
        ),
        compiler_params=_params(("arbitrary",)),
        name="moe_dispatch",
    )(dest1, dest2, first, n_used, h2p)


def _experts_kernel(be_ref, first_ref, nxt_ref, nu_ref, x_ref, wg_hbm, wu_hbm, wd_hbm, y_ref,
                    wg_buf, wu_buf, wd_buf, wg_scr, wu_scr, wd_scr, sem, run_scr, *, ns):
    b = pl.program_id(0)

    def weight_copies(e, slot):
        return (pltpu.make_async_copy(wg_hbm.at[e], wg_buf.at[slot], sem.at[slot]),
                pltpu.make_async_copy(wu_hbm.at[e], wu_buf.at[slot], sem.at[slot]),
                pltpu.make_async_copy(wd_hbm.at[e], wd_buf.at[slot], sem.at[slot]))

    @pl.when(b == 0)
    def _():
        run_scr[0] = 0
        for cp in weight_copies(be_ref[0], 0):
            cp.start()

    @pl.when((first_ref[b] == 1) & (b < nu_ref[0]))
    def _():
        run = run_scr[0]
        slot = run & 1
        for cp in weight_copies(be_ref[b], slot):
            cp.wait()
        wg_scr[...] = wg_buf[slot].astype(BF16)
        wu_scr[...] = wu_buf[slot].astype(BF16)
        wd_scr[...] = wd_buf[slot].astype(BF16)

        @pl.when(nxt_ref[b] >= 0)
        def _():
            for cp in weight_copies(nxt_ref[b], 1 - slot):
                cp.start()

        run_scr[0] = run + 1

    @pl.when(b < nu_ref[0])
    def _():
        x = _unpack_bf16_pairs(_load_row_tiles(x_ref, ns)).astype(BF16)
        g = jnp.dot(x, wg_scr[...], preferred_element_type=F32)
        u = jnp.dot(x, wu_scr[...], preferred_element_type=F32)
        hid = (g * jax.nn.sigmoid(g) * u).astype(BF16)
        _store_row_tiles(y_ref, _pack_bf16_pairs(jnp.dot(hid, wd_scr[...], preferred_element_type=F32)))


def _experts(xb, w_gate, w_up, w_down, block_expert, first, nxt, n_used, bm, ns):
    d, de = w_gate.shape[1:]
    nb = xb.shape[0] // (bm * ns)

    def row_map(b, be, fi, nx, nu):
        return (jnp.minimum(b, nu[0] - 1), 0)

    return pl.pallas_call(
        functools.partial(_experts_kernel, ns=ns),
        out_shape=jax.ShapeDtypeStruct(xb.shape, jnp.uint32),
        grid_spec=pltpu.PrefetchScalarGridSpec(
            num_scalar_prefetch=4,
            grid=(nb,),
            in_specs=[
                pl.BlockSpec((bm * ns, LANES), row_map),
                pl.BlockSpec(memory_space=pl.ANY),
                pl.BlockSpec(memory_space=pl.ANY),
                pl.BlockSpec(memory_space=pl.ANY),
            ],
            out_specs=pl.BlockSpec((bm * ns, LANES), row_map),
            scratch_shapes=[
                pltpu.VMEM((2, d, de), F32),
                pltpu.VMEM((2, d, de), F32),
                pltpu.VMEM((2, de, d), F32),
                pltpu.VMEM((d, de), BF16),
                pltpu.VMEM((d, de), BF16),
                pltpu.VMEM((de, d), BF16),
                pltpu.SemaphoreType.DMA((2,)),
                pltpu.SMEM((1,), jnp.int32),
            ],
        ),
        input_output_aliases={4: 0},
        compiler_params=_params(("arbitrary",)),
        name="moe_experts",
    )(block_expert, first, nxt, n_used, xb, w_gate, w_up, w_down)


def _ple_kernel(d1_ref, d2_ref, x_ref, mg_ref, yb_ref, g_ref, p_ref, wg_ref, wp_ref, gf_ref, o_ref,
                ybuf, sem, x3_scr, h_scr, p_scr, *, tm, tn, ns):
    m = pl.program_id(0)
    j = pl.program_id(1)

    def gather(blk, slot):
        base = blk * tm

        def row_copy(r, dest, k):
            dst = ybuf.at[slot, k, pl.ds(pl.multiple_of(r * ns, ns), ns)]
            return pltpu.make_async_copy(yb_ref.at[pl.ds(pl.multiple_of(dest * ns, ns), ns)], dst, sem.at[slot])

        def start(r, carry):
            row_copy(r, d1_ref[base + r], 0).start()
            row_copy(r, d2_ref[base + r], 1).start()
            return carry

        lax.fori_loop(0, tm, start, 0, unroll=DMA_UNROLL)

    @pl.when(j == 0)
    def _():
        slot = m & 1

        @pl.when(m == 0)
        def _():
            gather(0, 0)

        @pl.when(m + 1 < pl.num_programs(0))
        def _():
            gather(m + 1, 1 - slot)

        for k in range(2):
            pltpu.make_async_copy(yb_ref.at[pl.ds(0, tm * ns)], ybuf.at[slot, k], sem.at[slot]).wait()
        mg = mg_ref[...]
        y1 = _unpack_bf16_pairs(_load_row_tiles(ybuf.at[slot, 0], ns))
        y2 = _unpack_bf16_pairs(_load_row_tiles(ybuf.at[slot, 1], ns))
        x3 = x_ref[...] + mg[:, 0:1] * y1 + mg[:, 1:2] * y2
        x3_scr[...] = x3
        h_scr[...] = _rmsnorm_f32(x3, g_ref[...]).astype(BF16)
        p_scr[...] = p_ref[...].astype(BF16)

    col = pl.multiple_of(j * tn, tn)
    gate = jnp.dot(h_scr[...], wg_ref[...], preferred_element_type=F32)
    ple = jnp.dot(p_scr[...], wp_ref[...], preferred_element_type=F32)
    o_ref[:, pl.ds(col, tn)] = x3_scr[:, pl.ds(col, tn)] + jax.nn.sigmoid(gate) * ple

    @pl.when(j == pl.num_programs(1) - 1)
    def _():
        o_ref[...] = _rmsnorm_f32(o_ref[...], gf_ref[...])


def _combine_ple_final(x2, moe_gates, yb, dest1, dest2, g_ple, p2d, w_gate, w_proj, g_final, tm, tn, ns):
    t, d = x2.shape
    dp = p2d.shape[1]
    return pl.pallas_call(
        functools.partial(_ple_kernel, tm=tm, tn=tn, ns=ns),
        out_shape=jax.ShapeDtypeStruct((t, d), F32),
        grid_spec=pltpu.PrefetchScalarGridSpec(
            num_scalar_prefetch=2,
            grid=(t // tm, d // tn),
            in_specs=[
                pl.BlockSpec((tm, d), lambda m, j, d1, d2: (m, 0)),
                pl.BlockSpec((tm, LANES), lambda m, j, d1, d2: (m, 0)),
                pl.BlockSpec(memory_space=pl.ANY),
                pl.BlockSpec((1, d), lambda m, j, d1, d2: (0, 0)),
                pl.BlockSpec((tm, dp), lambda m, j, d1, d2: (m, 0)),
                pl.BlockSpec((d, tn), lambda m, j, d1, d2: (0, j)),
                pl.BlockSpec((dp, tn), lambda m, j, d1, d2: (0, j)),
                pl.BlockSpec((1, d), lambda m, j, d1, d2: (0, 0)),
            ],
            out_specs=pl.BlockSpec((tm, d), lambda m, j, d1, d2: (m, 0)),
            scratch_shapes=[
                pltpu.VMEM((2, 2, tm * ns, LANES), yb.dtype),
                pltpu.SemaphoreType.DMA((2,)),
                pltpu.VMEM((tm, d), F32),
                pltpu.VMEM((tm, d), BF16),
                pltpu.VMEM((tm, dp), BF16),
            ],
        ),
        compiler_params=_params(("arbitrary", "arbitrary")),
        name="combine_ple_final",
    )(dest1, dest2, x2, moe_gates, yb, g_ple, p2d, w_gate, w_proj, g_final)


def _tile(n, pref):
    return pref if n % pref == 0 else n


def _layer(x2d, p2d, batch, seq, w_in, b_forget, w_branch_fox, w_branch_sb, w_mix_out, g_mix, g_ffn,
           w_group, w_expert, w_gate, w_up, w_down, g_ple, w_ple_proj, w_ple_gate, g_final):
    t, d = x2d.shape
    n_heads = b_forget.shape[0]
    w_att = n_heads * HEAD_DIM
    n_exp = w_expert.shape[1]

    qs = LOG2E * HEAD_DIM ** -0.5
    w_t, wf_t = _wprep(w_in.T, w_att, n_heads, qs, 512)
    b_row = jnp.pad(b_forget, (0, LANES - n_heads)).reshape(1, LANES)
    col_qa, col_ka, col_qb, col_kb = 0, w_att, 2 * w_att, 3 * w_att
    col_ga, col_gb = 4 * w_att, 4 * w_att + d
    row_va, row_vb = 0, w_att

    tm = _tile(t, 1024)
    tn = _tile(d, 512)
    proj, vt, f_tok = _inproj(x2d, g_mix.reshape(1, d), w_t, wf_t, 2 * w_att, tm, 512)
    c = _forget_cumsum(f_tok, b_row, batch, n_heads)

    tq = _tile(seq, 512)
    hp = 4
    o_a = _fox_attention(proj, vt, c, batch, seq, n_heads, col_qa, col_ka, row_va, tq, hp)
    o_b = _sb_attention(proj, vt, batch, seq, n_heads, col_qb, col_kb, row_vb, tq, _tile(seq, 256), hp)

    mixed = _merge(o_a, o_b, w_branch_fox.astype(BF16), w_branch_sb.astype(BF16), proj, col_ga, col_gb, tm, tn)
    x2 = _mixout(mixed, w_mix_out.astype(BF16), x2d, tm, tn)

    w_r = jnp.concatenate([w_group, w_expert], axis=1)
    w_r = jnp.pad(w_r, ((0, 0), (0, LANES - w_r.shape[1])))
    w_r_hi = w_r.astype(BF16)
    w_r_lo = (w_r - w_r_hi.astype(F32)).astype(BF16)
    tr = _tile(t, 512)
    h2, ids_rows, gates, counts = _router(x2, g_ffn.reshape(1, d), w_r_hi, w_r_lo, tr, N_GROUPS, EXPERTS_PER_GROUP)

    bm = 256
    n_assign = 2 * t
    nb = n_assign // bm + n_exp
    dest, block_expert, first, nxt, n_used = _plan(counts, ids_rows, n_exp, bm, nb)
    dest1, dest2 = dest[0], dest[1]

    td = _tile(t, 1024)
    ns = d // 2 // LANES
    xb = _dispatch(h2, dest1, dest2, first, n_used, nb, bm, td, ns)
    yb = _experts(xb, w_gate, w_up, w_down, block_expert, first, nxt, n_used, bm, ns)

    tp = _tile(t, 512)
    return _combine_ple_final(x2, gates, yb, dest1, dest2, g_ple.reshape(1, d), p2d, w_ple_gate.astype(BF16),
                              w_ple_proj.astype(BF16), g_final.reshape(1, d), tp, tn, ns)


def kernel(x, p, w_in, b_forget, w_branch_fox, w_branch_sb, w_mix_out, g_mix, g_ffn, w_group, w_expert,
           w_gate, w_up, w_down, g_ple, w_ple_proj, w_ple_gate, g_final):
    b, s, d = x.shape
    depth = w_in.shape[0]
    assert depth == 1, "the final norm is fused into the single layer"
    x2d = x.reshape(b * s, d)
    out = _layer(x2d, p[0].reshape(b * s, -1), b, s, w_in[0], b_forget[0], w_branch_fox[0], w_branch_sb[0],
                 w_mix_out[0], g_mix[0], g_ffn[0], w_group[0], w_expert[0], w_gate[0], w_up[0], w_down[0],
                 g_ple[0], w_ple_proj[0], w_ple_gate[0], g_final)
    return out.reshape(b, s, d)
```

```python
import functools

import jax
import jax.numpy as jnp
from jax import lax
from jax.experimental import pallas as pl
from jax.experimental.pallas import tpu as pltpu

F32 = jnp.float32
BF16 = jnp.bfloat16

HEAD_DIM = 128
N_GROUPS = 4
EXPERTS_PER_GROUP = 8
EPS = 1e-6
LANES = 128
SUBLANES = 8
VMEM_LIMIT = 56 * 1024 * 1024

LOG2E = 1.4426950408889634
DMA_UNROLL = 8
R_DONE = 160.0

NT_DIMS = (((1,), (1,)), ((), ()))


def _params(sem):
    return pltpu.CompilerParams(dimension_semantics=sem, vmem_limit_bytes=VMEM_LIMIT)


def _rmsnorm_f32(x, g):
    ms = jnp.mean(x * x, axis=-1, keepdims=True)
    return x * lax.rsqrt(ms + EPS) * g


def _pack_bf16_pairs(x):
    n = x.shape[1] // 2
    bits = lax.bitcast_convert_type(x.astype(BF16).astype(F32), jnp.uint32)
    return (bits[:, :n] >> 16) | (bits[:, n:] & jnp.uint32(0xFFFF0000))


def _unpack_bf16_pairs(w):
    lo = lax.bitcast_convert_type(w << 16, F32)
    hi = lax.bitcast_convert_type(w & jnp.uint32(0xFFFF0000), F32)
    return jnp.concatenate([lo, hi], axis=1)


def _store_row_tiles(ref, words):
    n = words.shape[0]
    ns = words.shape[1] // LANES
    for c in range(ns):
        ref[pl.ds(c, n, stride=ns), :] = words[:, c * LANES:(c + 1) * LANES]


def _load_row_tiles(ref, ns):
    n = ref.shape[0] // ns
    return jnp.concatenate([ref[pl.ds(c, n, stride=ns), :] for c in range(ns)], axis=1)


def _wprep_kernel(off_ref, scale_ref, w_ref, f_ref, wt_ref, wf_ref, *, n_heads):
    i = pl.program_id(0)
    wt_ref[...] = (w_ref[...] * scale_ref[i]).astype(BF16)

    @pl.when(i == 0)
    def _():
        row = lax.broadcasted_iota(jnp.int32, wf_ref.shape, 0)
        f_rows = jnp.concatenate([f_ref[...]] * (wf_ref.shape[0] // f_ref.shape[0]), axis=0)
        wf_ref[...] = jnp.where(row < n_heads, f_rows, 0.0).astype(BF16)


def _wprep(w_t, w_att, n_heads, qs, rb):
    n_in, d = w_t.shape
    off_b = 3 * w_att + n_heads
    segments = [(2 * w_att, w_att, 1.0), (off_b + 2 * w_att, w_att, 1.0),
                (0, w_att, qs), (w_att, w_att, 1.0),
                (off_b, w_att, qs), (off_b + w_att, w_att, 1.0),
                (off_b + 3 * w_att, n_in - off_b - 3 * w_att, 1.0)]
    offs, scales = [], []
    for start, length, scale in segments:
        assert length % rb == 0 and start % SUBLANES == 0
        for k in range(length // rb):
            offs.append((start + k * rb) // SUBLANES)
            scales.append(scale)
    n_out = len(offs) * rb
    f_blk = SUBLANES
    assert n_heads <= f_blk and (3 * w_att) % f_blk == 0
    return pl.pallas_call(
        functools.partial(_wprep_kernel, n_heads=n_heads),
        out_shape=(jax.ShapeDtypeStruct((n_out, d), BF16), jax.ShapeDtypeStruct((LANES, d), BF16)),
        grid_spec=pltpu.PrefetchScalarGridSpec(
            num_scalar_prefetch=2,
            grid=(len(offs),),
            in_specs=[
                pl.BlockSpec((pl.Element(rb), pl.Element(d)), lambda i, off, sc: (off[i] * SUBLANES, 0)),
                pl.BlockSpec((pl.Element(f_blk), pl.Element(d)), lambda i, off, sc: (3 * w_att, 0)),
            ],
            out_specs=(
                pl.BlockSpec((rb, d), lambda i, off, sc: (i, 0)),
                pl.BlockSpec((LANES, d), lambda i, off, sc: (0, 0)),
            ),
        ),
        compiler_params=_params(("arbitrary",)),
        name="inproj_weights",
    )(jnp.asarray(offs, jnp.int32), jnp.asarray(scales, F32), w_t, w_t)


def _inproj_kernel(x_ref, g_ref, w_ref, wf_ref, o_ref, vt_ref, f_ref, h_scr, *, nvt):
    j = pl.program_id(1)

    @pl.when(j == 0)
    def _():
        hb = _rmsnorm_f32(x_ref[...], g_ref[...]).astype(BF16)
        h_scr[...] = hb
        f_ref[...] = lax.dot_general(hb, wf_ref[...], NT_DIMS, preferred_element_type=F32)

    @pl.when(j < nvt)
    def _():
        vt_ref[...] = lax.dot_general(w_ref[...], h_scr[...], NT_DIMS,
                                      preferred_element_type=F32).astype(vt_ref.dtype)

    @pl.when(j >= nvt)
    def _():
        o_ref[...] = lax.dot_general(h_scr[...], w_ref[...], NT_DIMS,
                                     preferred_element_type=F32).astype(o_ref.dtype)


def _inproj(x2d, g, w_t, wf_t, n_v, tm, tn):
    t, d = x2d.shape
    n = w_t.shape[0] - n_v
    nvt = n_v // tn
    return pl.pallas_call(
        functools.partial(_inproj_kernel, nvt=nvt),
        out_shape=(jax.ShapeDtypeStruct((t, n), BF16), jax.ShapeDtypeStruct((n_v, t), BF16),
                   jax.ShapeDtypeStruct((t, LANES), F32)),
        grid=(t // tm, nvt + n // tn),
        in_specs=[
            pl.BlockSpec((tm, d), lambda m, j: (m, 0)),
            pl.BlockSpec((1, d), lambda m, j: (0, 0)),
            pl.BlockSpec((tn, d), lambda m, j: (j, 0)),
            pl.BlockSpec((LANES, d), lambda m, j: (0, 0)),
        ],
        out_specs=(
            pl.BlockSpec((tm, tn), lambda m, j: (m, jnp.maximum(j - nvt, 0))),
            pl.BlockSpec((tn, tm), lambda m, j: (jnp.minimum(j, nvt - 1), m)),
            pl.BlockSpec((tm, LANES), lambda m, j: (m, 0)),
        ),
        scratch_shapes=[pltpu.VMEM((tm, d), BF16)],
        compiler_params=_params(("parallel", "arbitrary")),
        name="inproj",
    )(x2d, g, w_t, wf_t)


def _split3(x):
    p1 = x.astype(BF16)
    r1 = x - p1.astype(F32)
    p2 = r1.astype(BF16)
    p3 = (r1 - p2.astype(F32)).astype(BF16)
    return p1, p2, p3


def _cumsum_kernel(f_ref, b_ref, lt_ref, c_ref, carry_scr, *, chunk, n_heads):
    @pl.when(pl.program_id(1) == 0)
    def _():
        carry_scr[...] = jnp.zeros(carry_scr.shape, F32)

    rows = f_ref.shape[0]
    lt = lt_ref[...]
    carry = carry_scr[...]
    for i in range(rows // chunk):
        lf = jax.nn.log_sigmoid(f_ref[i * chunk:(i + 1) * chunk, :] + b_ref[...])
        p1, p2, p3 = _split3(lf)
        cs = (jnp.dot(lt, p1, preferred_element_type=F32)
              + jnp.dot(lt, p2, preferred_element_type=F32)
              + jnp.dot(lt, p3, preferred_element_type=F32)) + carry
        for h in range(n_heads):
            c_ref[h, i * chunk:(i + 1) * chunk, :] = jnp.broadcast_to(cs[:, h:h + 1] * LOG2E, (chunk, LANES))
        carry = cs[chunk - 1:chunk, :]
    carry_scr[...] = carry


def _forget_cumsum(f_tok, b_row, batch, n_heads):
    t = f_tok.shape[0]
    s = t // batch
    rows = min(1024, s)
    chunk = min(256, s)
    lt = jnp.tril(jnp.ones((chunk, chunk), BF16))
    nr = s // rows
    return pl.pallas_call(
        functools.partial(_cumsum_kernel, chunk=chunk, n_heads=n_heads),
        out_shape=jax.ShapeDtypeStruct((n_heads, t, LANES), F32),
        grid=(batch, nr),
        in_specs=[
            pl.BlockSpec((rows, LANES), lambda b, r: (b * nr + r, 0)),
            pl.BlockSpec((1, LANES), lambda b, r: (0, 0)),
            pl.BlockSpec((chunk, chunk), lambda b, r: (0, 0)),
        ],
        out_specs=pl.BlockSpec((n_heads, rows, LANES), lambda b, r: (0, b * nr + r, 0)),
        scratch_shapes=[pltpu.VMEM((1, LANES), F32)],
        compiler_params=_params(("parallel", "arbitrary")),
        name="forget_cumsum",
    )(f_tok, b_row, lt)


def _fox_kernel(q_ref, k_ref, vt_ref, c_ref, o_ref, m_scr, l_scr, acc_scr, *, tq, tk, hp):
    i = pl.program_id(2)
    m_scr[...] = jnp.full(m_scr.shape, -jnp.inf, F32)
    l_scr[...] = jnp.zeros(l_scr.shape, F32)
    acc_scr[...] = jnp.zeros(acc_scr.shape, F32)

    def tile(j, masked):
        start = pl.multiple_of(j * tk, tk)
        if masked:
            key = j * tk + lax.broadcasted_iota(jnp.int32, (tk, tq), 0)
            qry = i * tq + lax.broadcasted_iota(jnp.int32, (tk, tq), 1)
            keep = key <= qry
        def scores(hh):
            lanes = slice(hh * HEAD_DIM, (hh + 1) * HEAD_DIM)
            kj = k_ref[pl.ds(start, tk), lanes]
            return lax.dot_general(kj, q_ref[:, lanes], NT_DIMS, preferred_element_type=F32)

        s_all = [scores(hh) for hh in range(hp)]
        for hh in range(hp):
            s = s_all[hh]
            cj = c_ref[hh, pl.ds(start, tk), :]
            s = s - jnp.concatenate([cj] * (tq // LANES), axis=1)
            if masked:
                s = jnp.where(keep, s, -jnp.inf)
            m_prev = m_scr[hh]
            m_new = jnp.maximum(m_prev, jnp.max(s, axis=0, keepdims=True))
            alpha = jnp.exp2(m_prev - m_new)
            p = jnp.exp2(s - m_new)
            l_scr[hh] = alpha * l_scr[hh] + jnp.sum(p, axis=0, keepdims=True)
            vtj = vt_ref[hh * HEAD_DIM:(hh + 1) * HEAD_DIM, pl.ds(start, tk)]
            acc_scr[hh] = alpha * acc_scr[hh] + jnp.dot(vtj, p.astype(BF16), preferred_element_type=F32)
            m_scr[hh] = m_new

    def body(j, carry):
        tile(j, False)
        return carry

    lax.fori_loop(0, i, body, 0)
    tile(i, True)
    for hh in range(hp):
        o = acc_scr[hh] / l_scr[hh]
        o_ref[:, hh * HEAD_DIM:(hh + 1) * HEAD_DIM] = o.T.astype(o_ref.dtype)


def _fox_attention(proj, vt, c, batch, seq, n_heads, col_q, col_k, row_v, tq, hp):
    t = proj.shape[0]
    nq = seq // tq
    wb = hp * HEAD_DIM
    cq, ck, rv = col_q // wb, col_k // wb, row_v // wb
    return pl.pallas_call(
        functools.partial(_fox_kernel, tq=tq, tk=tq, hp=hp),
        out_shape=jax.ShapeDtypeStruct((t, n_heads * HEAD_DIM), BF16),
        grid=(batch, n_heads // hp, nq),
        in_specs=[
            pl.BlockSpec((tq, wb), lambda b, g, i: (b * nq + i, cq + g)),
            pl.BlockSpec((seq, wb), lambda b, g, i: (b, ck + g)),
            pl.BlockSpec((wb, seq), lambda b, g, i: (rv + g, b)),
            pl.BlockSpec((hp, seq, LANES), lambda b, g, i: (g, b, 0)),
        ],
        out_specs=pl.BlockSpec((tq, wb), lambda b, g, i: (b * nq + i, g)),
        scratch_shapes=[
            pltpu.VMEM((hp, 1, tq), F32),
            pltpu.VMEM((hp, 1, tq), F32),
            pltpu.VMEM((hp, HEAD_DIM, tq), F32),
        ],
        compiler_params=_params(("parallel", "parallel", "arbitrary")),
        name="fox_attention",
    )(proj, proj, vt, c)


def _sb_kernel(q_ref, k_ref, vt_ref, tri_ref, o_ref, r_scr, acc_scr, *, tq, tk, hp):
    i = pl.program_id(2)
    r_scr[...] = jnp.zeros(r_scr.shape, F32)
    acc_scr[...] = jnp.zeros(acc_scr.shape, F32)

    sign_bit = jnp.uint32(0x80000000)

    def tile(j, masked, qlo=0):
        start = pl.multiple_of(j * tk, tk)
        nq = tq - qlo
        if masked:
            key = j * tk + lax.broadcasted_iota(jnp.int32, (tk, nq), 0)
            qry = i * tq + qlo + lax.broadcasted_iota(jnp.int32, (tk, nq), 1)
            mask = key < qry

        def scores(hh):
            lanes = slice(hh * HEAD_DIM, (hh + 1) * HEAD_DIM)
            kj = k_ref[pl.ds(start, tk), lanes]
            return lax.dot_general(kj, q_ref[qlo:tq, lanes], NT_DIMS, preferred_element_type=F32)

        u_all = [scores(hh) for hh in range(hp)]
        for hh in range(hp):
            u = u_all[hh]
            minus_abs = lax.bitcast_convert_type(lax.bitcast_convert_type(u, jnp.uint32) | sign_bit, F32)
            sp = jnp.maximum(u, 0.0) + jnp.log2(1.0 + jnp.exp2(minus_abs))
            if masked:
                sp = jnp.where(mask, sp, 0.0)
            w = jnp.dot(tri_ref[...], sp.astype(BF16), preferred_element_type=F32)
            a = jnp.exp2(u - ((sp + w) + r_scr[hh, :, qlo:tq]))
            if masked:
                a = jnp.where(mask, a, 0.0)
            vtj = vt_ref[hh * HEAD_DIM:(hh + 1) * HEAD_DIM, pl.ds(start, tk)]
            acc_scr[hh, :, qlo:tq] += jnp.dot(vtj, a.astype(BF16), preferred_element_type=F32)
            r_scr[hh, :, qlo:tq] += jnp.sum(sp, axis=0, keepdims=True)

    ratio = max(tq // tk, 1)
    n_full = (i * tq) // tk
    for d in reversed(range(ratio)):
        tile(n_full + d, True, qlo=d * tk if tq > tk else 0)

    def live():
        return (jnp.min(r_scr[...]) <= R_DONE).astype(jnp.int32)

    def cond(carry):
        it, alive = carry
        return (it < n_full) & (alive > 0)

    def body(carry):
        it, _ = carry
        tile(n_full - 1 - it, False)
        return it + 1, live()

    lax.while_loop(cond, body, (jnp.int32(0), live()))
    for hh in range(hp):
        o_ref[:, hh * HEAD_DIM:(hh + 1) * HEAD_DIM] = acc_scr[hh].T.astype(o_ref.dtype)


def _sb_attention(proj, vt, batch, seq, n_heads, col_q, col_k, row_v, tq, tk, hp):
    t = proj.shape[0]
    nq = seq // tq
    wb = hp * HEAD_DIM
    cq, ck, rv = col_q // wb, col_k // wb, row_v // wb
    tri = jnp.triu(jnp.ones((tk, tk), BF16), 1)
    return pl.pallas_call(
        functools.partial(_sb_kernel, tq=tq, tk=tk, hp=hp),
        out_shape=jax.ShapeDtypeStruct((t, n_heads * HEAD_DIM), BF16),
        grid=(batch, n_heads // hp, nq),
        in_specs=[
            pl.BlockSpec((tq, wb), lambda b, g, i: (b * nq + i, cq + g)),
            pl.BlockSpec((seq, wb), lambda b, g, i: (b, ck + g)),
            pl.BlockSpec((wb, seq), lambda b, g, i: (rv + g, b)),
            pl.BlockSpec((tk, tk), lambda b, g, i: (0, 0)),
        ],
        out_specs=pl.BlockSpec((tq, wb), lambda b, g, i: (b * nq + i, g)),
        scratch_shapes=[
            pltpu.VMEM((hp, 1, tq), F32),
            pltpu.VMEM((hp, HEAD_DIM, tq), F32),
        ],
        compiler_params=_params(("parallel", "parallel", "arbitrary")),
        name="sb_attention",
    )(proj, proj, vt, tri)


def _merge_kernel(oa_ref, ob_ref, wa_ref, wb_ref, ga_ref, gb_ref, o_ref):
    ya = jnp.dot(oa_ref[...], wa_ref[...], preferred_element_type=F32)
    yb = jnp.dot(ob_ref[...], wb_ref[...], preferred_element_type=F32)
    ga = jax.nn.sigmoid(ga_ref[...].astype(F32))
    gb = jax.nn.sigmoid(gb_ref[...].astype(F32))
    o_ref[...] = (ga * ya + gb * yb).astype(o_ref.dtype)


def _merge(o_a, o_b, wa, wb, proj, col_ga, col_gb, tm, tn):
    t, ka = o_a.shape
    kb = o_b.shape[1]
    d = wa.shape[1]
    ca, cb = col_ga // tn, col_gb // tn
    return pl.pallas_call(
        _merge_kernel,
        out_shape=jax.ShapeDtypeStruct((t, d), BF16),
        grid=(t // tm, d // tn),
        in_specs=[
            pl.BlockSpec((tm, ka), lambda m, j: (m, 0)),
            pl.BlockSpec((tm, kb), lambda m, j: (m, 0)),
            pl.BlockSpec((ka, tn), lambda m, j: (0, j)),
            pl.BlockSpec((kb, tn), lambda m, j: (0, j)),
            pl.BlockSpec((tm, tn), lambda m, j: (m, ca + j)),
            pl.BlockSpec((tm, tn), lambda m, j: (m, cb + j)),
        ],
        out_specs=pl.BlockSpec((tm, tn), lambda m, j: (m, j)),
        compiler_params=_params(("parallel", "arbitrary")),
        name="branch_merge",
    )(o_a, o_b, wa, wb, proj, proj)


def _mixout_kernel(a_ref, w_ref, x_ref, o_ref):
    o_ref[...] = x_ref[...] + jnp.dot(a_ref[...], w_ref[...], preferred_element_type=F32)


def _mixout(mixed, w, x2d, tm, tn):
    t, k = mixed.shape
    d = w.shape[1]
    return pl.pallas_call(
        _mixout_kernel,
        out_shape=jax.ShapeDtypeStruct((t, d), F32),
        grid=(t // tm, d // tn),
        in_specs=[
            pl.BlockSpec((tm, k), lambda m, j: (m, 0)),
            pl.BlockSpec((k, tn), lambda m, j: (0, j)),
            pl.BlockSpec((tm, tn), lambda m, j: (m, j)),
        ],
        out_specs=pl.BlockSpec((tm, tn), lambda m, j: (m, j)),
        compiler_params=_params(("parallel", "arbitrary")),
        name="mix_out",
    )(mixed, w, x2d)


def _router_kernel(x_ref, g_ref, wh_ref, wl_ref, lt_ref, sel_ref, h_ref, ids_ref, gate_ref, cnt_ref, carry_scr,
                   *, n_groups, per_group):
    @pl.when(pl.program_id(0) == 0)
    def _():
        carry_scr[...] = jnp.zeros(carry_scr.shape, F32)

    h = _rmsnorm_f32(x_ref[...], g_ref[...])
    hh = h.astype(BF16)
    _store_row_tiles(h_ref, _pack_bf16_pairs(h))
    hl = (h - hh.astype(F32)).astype(BF16)
    logits = (jnp.dot(hh, wh_ref[...], preferred_element_type=F32)
              + jnp.dot(hl, wh_ref[...], preferred_element_type=F32)
              + jnp.dot(hh, wl_ref[...], preferred_element_type=F32))
    tm = logits.shape[0]
    n_exp = n_groups * per_group
    lane = lax.broadcasted_iota(jnp.int32, (tm, LANES), 1)
    lane_f = lane.astype(F32)
    neg_inf = -jnp.inf

    def first_argmax(vals, valid):
        vmax = jnp.max(jnp.where(valid, vals, neg_inf), axis=-1, keepdims=True)
        idx = jnp.min(jnp.where(valid & (vals == vmax), lane_f, float(LANES)), axis=-1, keepdims=True)
        return vmax, idx.astype(jnp.int32)

    is_group = lane < n_groups
    gmax, gsel = first_argmax(logits, is_group)
    gsum = jnp.sum(jnp.where(is_group, jnp.exp(logits - gmax), 0.0), axis=-1, keepdims=True)
    p_group = 1.0 / gsum
    lo = n_groups + gsel * per_group
    in_group = (lane >= lo) & (lane < lo + per_group)
    v1, i1 = first_argmax(logits, in_group)
    v2, i2 = first_argmax(logits, in_group & (lane != i1))
    e2w = jnp.exp(v2 - v1)
    gate1 = (1.0 / (1.0 + e2w)) * p_group
    gate2 = (e2w / (1.0 + e2w)) * p_group
    e1 = i1 - n_groups
    e2 = i2 - n_groups

    onehot = ((lane == e1) | (lane == e2)).astype(BF16)
    before = jnp.dot(lt_ref[...], onehot, preferred_element_type=F32) + carry_scr[...]
    rank1 = jnp.sum(jnp.where(lane == e1, before, 0.0), axis=-1, keepdims=True).astype(jnp.int32)
    rank2 = jnp.sum(jnp.where(lane == e2, before, 0.0), axis=-1, keepdims=True).astype(jnp.int32)
    carry_new = carry_scr[...] + jnp.sum(onehot.astype(F32), axis=0, keepdims=True)
    carry_scr[...] = carry_new

    cols = jnp.where(lane == 0, e1, jnp.where(lane == 1, e2, jnp.where(lane == 2, rank1 >> 7,
           jnp.where(lane == 3, rank1 & 127, jnp.where(lane == 4, rank2 >> 7,
           jnp.where(lane == 5, rank2 & 127, 0))))))
    ids_ref[...] = lax.dot_general(sel_ref[...], cols.astype(F32).astype(BF16), NT_DIMS,
                                   preferred_element_type=F32)
    gate_ref[...] = jnp.where(lane == 0, gate1, jnp.where(lane == 1, gate2, 0.0))
    cnt_ref[...] = jnp.broadcast_to(carry_new, cnt_ref.shape).astype(jnp.int32)
    del n_exp


def _router(x2, g, w_hi, w_lo, tm, n_groups, per_group):
    t, d = x2.shape
    lt = jnp.tril(jnp.ones((tm, tm), BF16), -1)
    sel = jnp.eye(8, LANES, dtype=BF16)
    return pl.pallas_call(
        functools.partial(_router_kernel, n_groups=n_groups, per_group=per_group),
        out_shape=(
            jax.ShapeDtypeStruct((t * (d // 2 // LANES), LANES), jnp.uint32),
            jax.ShapeDtypeStruct((8, t), F32),
            jax.ShapeDtypeStruct((t, LANES), F32),
            jax.ShapeDtypeStruct((8, LANES), jnp.int32),
        ),
        grid=(t // tm,),
        in_specs=[
            pl.BlockSpec((tm, d), lambda m: (m, 0)),
            pl.BlockSpec((1, d), lambda m: (0, 0)),
            pl.BlockSpec((d, LANES), lambda m: (0, 0)),
            pl.BlockSpec((d, LANES), lambda m: (0, 0)),
            pl.BlockSpec((tm, tm), lambda m: (0, 0)),
            pl.BlockSpec((8, LANES), lambda m: (0, 0)),
        ],
        out_specs=(
            pl.BlockSpec((tm * (d // 2 // LANES), LANES), lambda m: (m, 0)),
            pl.BlockSpec((8, tm), lambda m: (0, m)),
            pl.BlockSpec((tm, LANES), lambda m: (m, 0)),
            pl.BlockSpec((8, LANES), lambda m: (0, 0)),
        ),
        scratch_shapes=[pltpu.VMEM((1, LANES), F32)],
        compiler_params=_params(("arbitrary",)),
        name="router",
    )(x2, g, w_hi, w_lo, lt, sel)


def _plan_kernel(cnt_ref, ids_ref, dest_ref, be_ref, first_ref, nxt_ref, nu_ref, ps_scr, *, n_exp, bm, nb):
    shift = bm.bit_length() - 1

    def clear(k, c):
        first_ref[k] = 0
        nxt_ref[k] = -1
        return c

    lax.fori_loop(0, nb, clear, 0)

    def per_expert(e, carry):
        blk_start, last, prev_start = carry
        n = lax.shift_right_logical(cnt_ref[0, e] + (bm - 1), shift)
        ps_scr[e] = blk_start * bm

        def fill(k, c):
            be_ref[blk_start + k] = e
            return c

        lax.fori_loop(0, n, fill, 0)

        @pl.when(n > 0)
        def _():
            first_ref[blk_start] = 1

            @pl.when(prev_start >= 0)
            def _():
                nxt_ref[prev_start] = e

        used = n > 0
        return blk_start + n, jnp.where(used, e, last), jnp.where(used, blk_start, prev_start)

    n_used, last, _ = lax.fori_loop(0, n_exp, per_expert, (jnp.int32(0), jnp.int32(0), jnp.int32(-1)))
    nu_ref[0] = n_used

    def tail(k, c):
        be_ref[k] = last
        return c

    lax.fori_loop(n_used, nb, tail, 0)

    e1 = ids_ref[0:1, :]
    e2 = ids_ref[1:2, :]
    p1 = jnp.zeros_like(e1)
    p2 = jnp.zeros_like(e2)
    for e in range(n_exp):
        ps = ps_scr[e].astype(F32)
        p1 = jnp.where(e1 == e, ps, p1)
        p2 = jnp.where(e2 == e, ps, p2)
    dest_ref[...] = jnp.zeros(dest_ref.shape, jnp.int32)
    dest_ref[0:1, :] = (p1 + ids_ref[2:3, :] * 128.0 + ids_ref[3:4, :]).astype(jnp.int32)
    dest_ref[1:2, :] = (p2 + ids_ref[4:5, :] * 128.0 + ids_ref[5:6, :]).astype(jnp.int32)


def _plan(counts, ids_rows, n_exp, bm, nb):
    t = ids_rows.shape[1]
    return pl.pallas_call(
        functools.partial(_plan_kernel, n_exp=n_exp, bm=bm, nb=nb),
        out_shape=(
            jax.ShapeDtypeStruct((8, t), jnp.int32),
            jax.ShapeDtypeStruct((nb,), jnp.int32),
            jax.ShapeDtypeStruct((nb,), jnp.int32),
            jax.ShapeDtypeStruct((nb,), jnp.int32),
            jax.ShapeDtypeStruct((1,), jnp.int32),
        ),
        in_specs=[
            pl.BlockSpec(memory_space=pltpu.SMEM),
            pl.BlockSpec(memory_space=pltpu.VMEM),
        ],
        out_specs=(
            pl.BlockSpec(memory_space=pltpu.VMEM),
            pl.BlockSpec(memory_space=pltpu.SMEM),
            pl.BlockSpec(memory_space=pltpu.SMEM),
            pl.BlockSpec(memory_space=pltpu.SMEM),
            pl.BlockSpec(memory_space=pltpu.SMEM),
        ),
        scratch_shapes=[pltpu.SMEM((n_exp,), jnp.int32)],
        name="moe_plan",
    )(counts, ids_rows)


def _dispatch_kernel(d1_ref, d2_ref, first_ref, nu_ref, h_ref, xb_ref, zbuf, sem, zsem, *, tm, ns, bm, nb):
    base = pl.program_id(0) * tm

    @pl.when(pl.program_id(0) == 0)
    def _():
        zbuf[...] = jnp.zeros(zbuf.shape, zbuf.dtype)
        n_used = nu_ref[0]

        def has_padding(b):
            nxt_first = first_ref[jnp.minimum(b + 1, nb - 1)]
            return (b >= n_used - 1) | (nxt_first == 1)

        def zero_copy(b):
            return pltpu.make_async_copy(zbuf, xb_ref.at[pl.ds(pl.multiple_of(b * (bm * ns), bm * ns), bm * ns)], zsem)

        def zstart(b, carry):
            @pl.when(has_padding(b))
            def _():
                zero_copy(b).start()
            return carry

        def zwait(b, carry):
            @pl.when(has_padding(b))
            def _():
                zero_copy(b).wait()
            return carry

        lax.fori_loop(0, nb, zstart, 0)
        lax.fori_loop(0, nb, zwait, 0)

    def row_copy(r, dest):
        src = h_ref.at[pl.ds(pl.multiple_of(r * ns, ns), ns)]
        return pltpu.make_async_copy(src, xb_ref.at[pl.ds(pl.multiple_of(dest * ns, ns), ns)], sem)

    def start(r, carry):
        row_copy(r, d1_ref[base + r]).start()
        row_copy(r, d2_ref[base + r]).start()
        return carry

    lax.fori_loop(0, tm, start, 0, unroll=DMA_UNROLL)
    for _ in range(2):
        pltpu.make_async_copy(h_ref, xb_ref.at[pl.ds(0, tm * ns)], sem).wait()


def _dispatch(h2p, dest1, dest2, first, n_used, nb, bm, tm, ns):
    t = h2p.shape[0] // ns
    return pl.pallas_call(
        functools.partial(_dispatch_kernel, tm=tm, ns=ns, bm=bm, nb=nb),
        out_shape=jax.ShapeDtypeStruct((nb * bm * ns, LANES), h2p.dtype),
        grid_spec=pltpu.PrefetchScalarGridSpec(
            num_scalar_prefetch=4,
            grid=(t // tm,),
            in_specs=[pl.BlockSpec((tm * ns, LANES), lambda m, d1, d2, fi, nu: (m, 0))],
            out_specs=pl.BlockSpec(memory_space=pl.ANY),
            scratch_shapes=[pltpu.VMEM((bm * ns, LANES), h2p.dtype), pltpu.SemaphoreType.DMA,
                            pltpu.SemaphoreType.DMA],
        ),
        compiler_params=_params(("arbitrary",)),
        name="moe_dispatch",
    )(dest1, dest2, first, n_used, h2p)


def _experts_kernel(be_ref, first_ref, nxt_ref, nu_ref, x_ref, wg_hbm, wu_hbm, wd_hbm, y_ref,
                    wg_buf, wu_buf, wd_buf, wg_scr, wu_scr, wd_scr, sem, run_scr, *, ns):
    b = pl.program_id(0)

    def weight_copies(e, slot):
        return (pltpu.make_async_copy(wg_hbm.at[e], wg_buf.at[slot], sem.at[slot]),
                pltpu.make_async_copy(wu_hbm.at[e], wu_buf.at[slot], sem.at[slot]),
                pltpu.make_async_copy(wd_hbm.at[e], wd_buf.at[slot], sem.at[slot]))

    @pl.when(b == 0)
    def _():
        run_scr[0] = 0
        for cp in weight_copies(be_ref[0], 0):
            cp.start()

    @pl.when((first_ref[b] == 1) & (b < nu_ref[0]))
    def _():
        run = run_scr[0]
        slot = run & 1
        for cp in weight_copies(be_ref[b], slot):
            cp.wait()
        wg_scr[...] = wg_buf[slot].astype(BF16)
        wu_scr[...] = wu_buf[slot].astype(BF16)
        wd_scr[...] = wd_buf[slot].astype(BF16)

        @pl.when(nxt_ref[b] >= 0)
        def _():
            for cp in weight_copies(nxt_ref[b], 1 - slot):
                cp.start()

        run_scr[0] = run + 1

    @pl.when(b < nu_ref[0])
    def _():
        x = _unpack_bf16_pairs(_load_row_tiles(x_ref, ns)).astype(BF16)
        g = jnp.dot(x, wg_scr[...], preferred_element_type=F32)
        u = jnp.dot(x, wu_scr[...], preferred_element_type=F32)
        hid = (g * jax.nn.sigmoid(g) * u).astype(BF16)
        _store_row_tiles(y_ref, _pack_bf16_pairs(jnp.dot(hid, wd_scr[...], preferred_element_type=F32)))


def _experts(xb, w_gate, w_up, w_down, block_expert, first, nxt, n_used, bm, ns):
    d, de = w_gate.shape[1:]
    nb = xb.shape[0] // (bm * ns)

    def row_map(b, be, fi, nx, nu):
        return (jnp.minimum(b, nu[0] - 1), 0)

    return pl.pallas_call(
        functools.partial(_experts_kernel, ns=ns),
        out_shape=jax.ShapeDtypeStruct(xb.shape, jnp.uint32),
        grid_spec=pltpu.PrefetchScalarGridSpec(
            num_scalar_prefetch=4,
            grid=(nb,),
            in_specs=[
                pl.BlockSpec((bm * ns, LANES), row_map),
                pl.BlockSpec(memory_space=pl.ANY),
                pl.BlockSpec(memory_space=pl.ANY),
                pl.BlockSpec(memory_space=pl.ANY),
            ],
            out_specs=pl.BlockSpec((bm * ns, LANES), row_map),
            scratch_shapes=[
                pltpu.VMEM((2, d, de), F32),
                pltpu.VMEM((2, d, de), F32),
                pltpu.VMEM((2, de, d), F32),
                pltpu.VMEM((d, de), BF16),
                pltpu.VMEM((d, de), BF16),
                pltpu.VMEM((de, d), BF16),
                pltpu.SemaphoreType.DMA((2,)),
                pltpu.SMEM((1,), jnp.int32),
            ],
        ),
        input_output_aliases={4: 0},
        compiler_params=_params(("arbitrary",)),
        name="moe_experts",
    )(block_expert, first, nxt, n_used, xb, w_gate, w_up, w_down)


def _ple_kernel(d1_ref, d2_ref, x_ref, mg_ref, yb_ref, g_ref, p_ref, wg_ref, wp_ref, gf_ref, o_ref,
                ybuf, sem, x3_scr, h_scr, p_scr, *, tm, tn, ns):
    m = pl.program_id(0)
    j = pl.program_id(1)

    def gather(blk, slot):
        base = blk * tm

        def row_copy(r, dest, k):
            dst = ybuf.at[slot, k, pl.ds(pl.multiple_of(r * ns, ns), ns)]
            return pltpu.make_async_copy(yb_ref.at[pl.ds(pl.multiple_of(dest * ns, ns), ns)], dst, sem.at[slot])

        def start(r, carry):
            row_copy(r, d1_ref[base + r], 0).start()
            row_copy(r, d2_ref[base + r], 1).start()
            return carry

        lax.fori_loop(0, tm, start, 0, unroll=DMA_UNROLL)

    @pl.when(j == 0)
    def _():
        slot = m & 1

        @pl.when(m == 0)
        def _():
            gather(0, 0)

        @pl.when(m + 1 < pl.num_programs(0))
        def _():
            gather(m + 1, 1 - slot)

        for k in range(2):
            pltpu.make_async_copy(yb_ref.at[pl.ds(0, tm * ns)], ybuf.at[slot, k], sem.at[slot]).wait()
        mg = mg_ref[...]
        y1 = _unpack_bf16_pairs(_load_row_tiles(ybuf.at[slot, 0], ns))
        y2 = _unpack_bf16_pairs(_load_row_tiles(ybuf.at[slot, 1], ns))
        x3 = x_ref[...] + mg[:, 0:1] * y1 + mg[:, 1:2] * y2
        x3_scr[...] = x3
        h_scr[...] = _rmsnorm_f32(x3, g_ref[...]).astype(BF16)
        p_scr[...] = p_ref[...].astype(BF16)

    col = pl.multiple_of(j * tn, tn)
    gate = jnp.dot(h_scr[...], wg_ref[...], preferred_element_type=F32)
    ple = jnp.dot(p_scr[...], wp_ref[...], preferred_element_type=F32)
    o_ref[:, pl.ds(col, tn)] = x3_scr[:, pl.ds(col, tn)] + jax.nn.sigmoid(gate) * ple

    @pl.when(j == pl.num_programs(1) - 1)
    def _():
        o_ref[...] = _rmsnorm_f32(o_ref[...], gf_ref[...])


def _combine_ple_final(x2, moe_gates, yb, dest1, dest2, g_ple, p2d, w_gate, w_proj, g_final, tm, tn, ns):
    t, d = x2.shape
    dp = p2d.shape[1]
    return pl.pallas_call(
        functools.partial(_ple_kernel, tm=tm, tn=tn, ns=ns),
        out_shape=jax.ShapeDtypeStruct((t, d), F32),
        grid_spec=pltpu.PrefetchScalarGridSpec(
            num_scalar_prefetch=2,
            grid=(t // tm, d // tn),
            in_specs=[
                pl.BlockSpec((tm, d), lambda m, j, d1, d2: (m, 0)),
                pl.BlockSpec((tm, LANES), lambda m, j, d1, d2: (m, 0)),
                pl.BlockSpec(memory_space=pl.ANY),
                pl.BlockSpec((1, d), lambda m, j, d1, d2: (0, 0)),
                pl.BlockSpec((tm, dp), lambda m, j, d1, d2: (m, 0)),
                pl.BlockSpec((d, tn), lambda m, j, d1, d2: (0, j)),
                pl.BlockSpec((dp, tn), lambda m, j, d1, d2: (0, j)),
                pl.BlockSpec((1, d), lambda m, j, d1, d2: (0, 0)),
            ],
            out_specs=pl.BlockSpec((tm, d), lambda m, j, d1, d2: (m, 0)),
            scratch_shapes=[
                pltpu.VMEM((2, 2, tm * ns, LANES), yb.dtype),
                pltpu.SemaphoreType.DMA((2,)),
                pltpu.VMEM((tm, d), F32),
                pltpu.VMEM((tm, d), BF16),
                pltpu.VMEM((tm, dp), BF16),
            ],
        ),
        compiler_params=_params(("arbitrary", "arbitrary")),
        name="combine_ple_final",
    )(dest1, dest2, x2, moe_gates, yb, g_ple, p2d, w_gate, w_proj, g_final)


def _tile(n, pref):
    return pref if n % pref == 0 else n


def _layer(x2d, p2d, batch, seq, w_in, b_forget, w_branch_fox, w_branch_sb, w_mix_out, g_mix, g_ffn,
           w_group, w_expert, w_gate, w_up, w_down, g_ple, w_ple_proj, w_ple_gate, g_final):
    t, d = x2d.shape
    n_heads = b_forget.shape[0]
    w_att = n_heads * HEAD_DIM
    n_exp = w_expert.shape[1]

    qs = LOG2E * HEAD_DIM ** -0.5
    w_t, wf_t = _wprep(w_in.T, w_att, n_heads, qs, 512)
    b_row = jnp.pad(b_forget, (0, LANES - n_heads)).reshape(1, LANES)
    col_qa, col_ka, col_qb, col_kb = 0, w_att, 2 * w_att, 3 * w_att
    col_ga, col_gb = 4 * w_att, 4 * w_att + d
    row_va, row_vb = 0, w_att

    tm = _tile(t, 1024)
    tn = _tile(d, 1024)
    proj, vt, f_tok = _inproj(x2d, g_mix.reshape(1, d), w_t, wf_t, 2 * w_att, tm, _tile(d // 2, 1024))
    c = _forget_cumsum(f_tok, b_row, batch, n_heads)

    tq = _tile(seq, 512)
    hp = 4
    o_a = _fox_attention(proj, vt, c, batch, seq, n_heads, col_qa, col_ka, row_va, tq, hp)
    o_b = _sb_attention(proj, vt, batch, seq, n_heads, col_qb, col_kb, row_vb, tq, _tile(seq, 256), hp)

    mixed = _merge(o_a, o_b, w_branch_fox.astype(BF16), w_branch_sb.astype(BF16), proj, col_ga, col_gb, tm, tn)
    x2 = _mixout(mixed, w_mix_out.astype(BF16), x2d, tm, tn)

    w_r = jnp.concatenate([w_group, w_expert], axis=1)
    w_r = jnp.pad(w_r, ((0, 0), (0, LANES - w_r.shape[1])))
    w_r_hi = w_r.astype(BF16)
    w_r_lo = (w_r - w_r_hi.astype(F32)).astype(BF16)
    tr = _tile(t, 512)
    h2, ids_rows, gates, counts = _router(x2, g_ffn.reshape(1, d), w_r_hi, w_r_lo, tr, N_GROUPS, EXPERTS_PER_GROUP)

    bm = 256
    n_assign = 2 * t
    nb = n_assign // bm + n_exp
    dest, block_expert, first, nxt, n_used = _plan(counts, ids_rows, n_exp, bm, nb)
    dest1, dest2 = dest[0], dest[1]

    td = _tile(t, 1024)
    ns = d // 2 // LANES
    xb = _dispatch(h2, dest1, dest2, first, n_used, nb, bm, td, ns)
    yb = _experts(xb, w_gate, w_up, w_down, block_expert, first, nxt, n_used, bm, ns)

    tp = _tile(t, 512)
    return _combine_ple_final(x2, gates, yb, dest1, dest2, g_ple.reshape(1, d), p2d, w_ple_gate.astype(BF16),
                              w_ple_proj.astype(BF16), g_final.reshape(1, d), tp, tn, ns)


def kernel(x, p, w_in, b_forget, w_branch_fox, w_branch_sb, w_mix_out, g_mix, g_ffn, w_group, w_expert,
           w_gate, w_up, w_down, g_ple, w_ple_proj, w_ple_gate, g_final):
    b, s, d = x.shape
    depth = w_in.shape[0]
    assert depth == 1, "the final norm is fused into the single layer"
    x2d = x.reshape(b * s, d)
    out = _layer(x2d, p[0].reshape(b * s, -1), b, s, w_in[0], b_forget[0], w_branch_fox[0], w_branch_sb[0],
                 w_mix_out[0], g_mix[0], g_ffn[0], w_group[0], w_expert[0], w_gate[0], w_up[0], w_down[0],
                 g_ple[0], w_ple_proj[0], w_ple_gate[0], g_final)
    return out.reshape(b, s, d)
```

```python
import functools

import jax
import jax.numpy as jnp
from jax import lax
from jax.experimental import pallas as pl
from jax.experimental.pallas import tpu as pltpu

F32 = jnp.float32
BF16 = jnp.bfloat16

HEAD_DIM = 128
N_GROUPS = 4
EXPERTS_PER_GROUP = 8
EPS = 1e-6
LANES = 128
SUBLANES = 8
VMEM_LIMIT = 56 * 1024 * 1024

LOG2E = 1.4426950408889634
DMA_UNROLL = 8
R_DONE = 160.0

NT_DIMS = (((1,), (1,)), ((), ()))


def _params(sem):
    return pltpu.CompilerParams(dimension_semantics=sem, vmem_limit_bytes=VMEM_LIMIT)


def _rmsnorm_f32(x, g):
    ms = jnp.mean(x * x, axis=-1, keepdims=True)
    return x * lax.rsqrt(ms + EPS) * g


def _pack_bf16_pairs(x):
    n = x.shape[1] // 2
    bits = lax.bitcast_convert_type(x.astype(BF16).astype(F32), jnp.uint32)
    return (bits[:, :n] >> 16) | (bits[:, n:] & jnp.uint32(0xFFFF0000))


def _unpack_bf16_pairs(w):
    lo = lax.bitcast_convert_type(w << 16, F32)
    hi = lax.bitcast_convert_type(w & jnp.uint32(0xFFFF0000), F32)
    return jnp.concatenate([lo, hi], axis=1)


def _store_row_tiles(ref, words):
    n = words.shape[0]
    ns = words.shape[1] // LANES
    for c in range(ns):
        ref[pl.ds(c, n, stride=ns), :] = words[:, c * LANES:(c + 1) * LANES]


def _load_row_tiles(ref, ns):
    n = ref.shape[0] // ns
    return jnp.concatenate([ref[pl.ds(c, n, stride=ns), :] for c in range(ns)], axis=1)


def _wprep_kernel(off_ref, scale_ref, w_ref, f_ref, wt_ref, wf_ref, *, n_heads):
    i = pl.program_id(0)
    wt_ref[...] = (w_ref[...] * scale_ref[i]).astype(BF16)

    @pl.when(i == 0)
    def _():
        row = lax.broadcasted_iota(jnp.int32, wf_ref.shape, 0)
        f_rows = jnp.concatenate([f_ref[...]] * (wf_ref.shape[0] // f_ref.shape[0]), axis=0)
        wf_ref[...] = jnp.where(row < n_heads, f_rows, 0.0).astype(BF16)


def _wprep(w_t, w_att, n_heads, qs, rb):
    n_in, d = w_t.shape
    off_b = 3 * w_att + n_heads
    segments = [(2 * w_att, w_att, 1.0), (off_b + 2 * w_att, w_att, 1.0),
                (0, w_att, qs), (w_att, w_att, 1.0),
                (off_b, w_att, qs), (off_b + w_att, w_att, 1.0),
                (off_b + 3 * w_att, n_in - off_b - 3 * w_att, 1.0)]
    offs, scales = [], []
    for start, length, scale in segments:
        assert length % rb == 0 and start % SUBLANES == 0
        for k in range(length // rb):
            offs.append((start + k * rb) // SUBLANES)
            scales.append(scale)
    n_out = len(offs) * rb
    f_blk = SUBLANES
    assert n_heads <= f_blk and (3 * w_att) % f_blk == 0
    return pl.pallas_call(
        functools.partial(_wprep_kernel, n_heads=n_heads),
        out_shape=(jax.ShapeDtypeStruct((n_out, d), BF16), jax.ShapeDtypeStruct((LANES, d), BF16)),
        grid_spec=pltpu.PrefetchScalarGridSpec(
            num_scalar_prefetch=2,
            grid=(len(offs),),
            in_specs=[
                pl.BlockSpec((pl.Element(rb), pl.Element(d)), lambda i, off, sc: (off[i] * SUBLANES, 0)),
                pl.BlockSpec((pl.Element(f_blk), pl.Element(d)), lambda i, off, sc: (3 * w_att, 0)),
            ],
            out_specs=(
                pl.BlockSpec((rb, d), lambda i, off, sc: (i, 0)),
                pl.BlockSpec((LANES, d), lambda i, off, sc: (0, 0)),
            ),
        ),
        compiler_params=_params(("arbitrary",)),
        name="inproj_weights",
    )(jnp.asarray(offs, jnp.int32), jnp.asarray(scales, F32), w_t, w_t)


def _inproj_kernel(x_ref, g_ref, w_ref, wf_ref, o_ref, vt_ref, f_ref, h_scr, *, nvt):
    j = pl.program_id(1)

    @pl.when(j == 0)
    def _():
        hb = _rmsnorm_f32(x_ref[...], g_ref[...]).astype(BF16)
        h_scr[...] = hb
        f_ref[...] = lax.dot_general(hb, wf_ref[...], NT_DIMS, preferred_element_type=F32)

    @pl.when(j < nvt)
    def _():
        vt_ref[...] = lax.dot_general(w_ref[...], h_scr[...], NT_DIMS,
                                      preferred_element_type=F32).astype(vt_ref.dtype)

    @pl.when(j >= nvt)
    def _():
        o_ref[...] = lax.dot_general(h_scr[...], w_ref[...], NT_DIMS,
                                     preferred_element_type=F32).astype(o_ref.dtype)


def _inproj(x2d, g, w_t, wf_t, n_v, tm, tn):
    t, d = x2d.shape
    n = w_t.shape[0] - n_v
    nvt = n_v // tn
    return pl.pallas_call(
        functools.partial(_inproj_kernel, nvt=nvt),
        out_shape=(jax.ShapeDtypeStruct((t, n), BF16), jax.ShapeDtypeStruct((n_v, t), BF16),
                   jax.ShapeDtypeStruct((t, LANES), F32)),
        grid=(t // tm, nvt + n // tn),
        in_specs=[
            pl.BlockSpec((tm, d), lambda m, j: (m, 0)),
            pl.BlockSpec((1, d), lambda m, j: (0, 0)),
            pl.BlockSpec((tn, d), lambda m, j: (j, 0)),
            pl.BlockSpec((LANES, d), lambda m, j: (0, 0)),
        ],
        out_specs=(
            pl.BlockSpec((tm, tn), lambda m, j: (m, jnp.maximum(j - nvt, 0))),
            pl.BlockSpec((tn, tm), lambda m, j: (jnp.minimum(j, nvt - 1), m)),
            pl.BlockSpec((tm, LANES), lambda m, j: (m, 0)),
        ),
        scratch_shapes=[pltpu.VMEM((tm, d), BF16)],
        compiler_params=_params(("parallel", "arbitrary")),
        name="inproj",
    )(x2d, g, w_t, wf_t)


def _split3(x):
    p1 = x.astype(BF16)
    r1 = x - p1.astype(F32)
    p2 = r1.astype(BF16)
    p3 = (r1 - p2.astype(F32)).astype(BF16)
    return p1, p2, p3


def _cumsum_kernel(f_ref, b_ref, lt_ref, c_ref, carry_scr, *, chunk, n_heads):
    @pl.when(pl.program_id(1) == 0)
    def _():
        carry_scr[...] = jnp.zeros(carry_scr.shape, F32)

    rows = f_ref.shape[0]
    lt = lt_ref[...]
    carry = carry_scr[...]
    for i in range(rows // chunk):
        lf = jax.nn.log_sigmoid(f_ref[i * chunk:(i + 1) * chunk, :] + b_ref[...])
        p1, p2, p3 = _split3(lf)
        cs = (jnp.dot(lt, p1, preferred_element_type=F32)
              + jnp.dot(lt, p2, preferred_element_type=F32)
              + jnp.dot(lt, p3, preferred_element_type=F32)) + carry
        for h in range(n_heads):
            c_ref[h, i * chunk:(i + 1) * chunk, :] = jnp.broadcast_to(cs[:, h:h + 1] * LOG2E, (chunk, LANES))
        carry = cs[chunk - 1:chunk, :]
    carry_scr[...] = carry


def _forget_cumsum(f_tok, b_row, batch, n_heads):
    t = f_tok.shape[0]
    s = t // batch
    rows = min(1024, s)
    chunk = min(256, s)
    lt = jnp.tril(jnp.ones((chunk, chunk), BF16))
    nr = s // rows
    return pl.pallas_call(
        functools.partial(_cumsum_kernel, chunk=chunk, n_heads=n_heads),
        out_shape=jax.ShapeDtypeStruct((n_heads, t, LANES), F32),
        grid=(batch, nr),
        in_specs=[
            pl.BlockSpec((rows, LANES), lambda b, r: (b * nr + r, 0)),
            pl.BlockSpec((1, LANES), lambda b, r: (0, 0)),
            pl.BlockSpec((chunk, chunk), lambda b, r: (0, 0)),
        ],
        out_specs=pl.BlockSpec((n_heads, rows, LANES), lambda b, r: (0, b * nr + r, 0)),
        scratch_shapes=[pltpu.VMEM((1, LANES), F32)],
        compiler_params=_params(("parallel", "arbitrary")),
        name="forget_cumsum",
    )(f_tok, b_row, lt)


def _fox_kernel(q_ref, k_ref, vt_ref, c_ref, o_ref, m_scr, l_scr, acc_scr, *, tq, tk, hp):
    i = pl.program_id(2)
    m_scr[...] = jnp.full(m_scr.shape, -jnp.inf, F32)
    l_scr[...] = jnp.zeros(l_scr.shape, F32)
    acc_scr[...] = jnp.zeros(acc_scr.shape, F32)

    def tile(j, masked):
        start = pl.multiple_of(j * tk, tk)
        if masked:
            key = j * tk + lax.broadcasted_iota(jnp.int32, (tk, tq), 0)
            qry = i * tq + lax.broadcasted_iota(jnp.int32, (tk, tq), 1)
            keep = key <= qry

        def scores(hh):
            lanes = slice(hh * HEAD_DIM, (hh + 1) * HEAD_DIM)
            kj = k_ref[pl.ds(start, tk), lanes]
            return lax.dot_general(kj, q_ref[:, lanes], NT_DIMS, preferred_element_type=F32)

        s_all = [scores(hh) for hh in range(hp)]
        for hh in range(hp):
            s = s_all[hh]
            cj = c_ref[hh, pl.ds(start, tk), :]
            s = s - jnp.concatenate([cj] * (tq // LANES), axis=1)
            if masked:
                s = jnp.where(keep, s, -jnp.inf)
            m_prev = m_scr[hh]
            m_new = jnp.maximum(m_prev, jnp.max(s, axis=0, keepdims=True))
            alpha = jnp.exp2(m_prev - m_new)
            p = jnp.exp2(s - m_new)
            l_scr[hh] = alpha * l_scr[hh] + jnp.sum(p, axis=0, keepdims=True)
            vtj = vt_ref[hh * HEAD_DIM:(hh + 1) * HEAD_DIM, pl.ds(start, tk)]
            acc_scr[hh] = alpha * acc_scr[hh] + jnp.dot(vtj, p.astype(BF16), preferred_element_type=F32)
            m_scr[hh] = m_new

    def body(j, carry):
        tile(j, False)
        return carry

    lax.fori_loop(0, i, body, 0)
    tile(i, True)
    for hh in range(hp):
        o = acc_scr[hh] / l_scr[hh]
        o_ref[:, hh * HEAD_DIM:(hh + 1) * HEAD_DIM] = o.T.astype(o_ref.dtype)


def _fox_attention(proj, vt, c, batch, seq, n_heads, col_q, col_k, row_v, tq, hp):
    t = proj.shape[0]
    nq = seq // tq
    wb = hp * HEAD_DIM
    cq, ck, rv = col_q // wb, col_k // wb, row_v // wb
    return pl.pallas_call(
        functools.partial(_fox_kernel, tq=tq, tk=tq, hp=hp),
        out_shape=jax.ShapeDtypeStruct((t, n_heads * HEAD_DIM), BF16),
        grid=(batch, n_heads // hp, nq),
        in_specs=[
            pl.BlockSpec((tq, wb), lambda b, g, i: (b * nq + i, cq + g)),
            pl.BlockSpec((seq, wb), lambda b, g, i: (b, ck + g)),
            pl.BlockSpec((wb, seq), lambda b, g, i: (rv + g, b)),
            pl.BlockSpec((hp, seq, LANES), lambda b, g, i: (g, b, 0)),
        ],
        out_specs=pl.BlockSpec((tq, wb), lambda b, g, i: (b * nq + i, g)),
        scratch_shapes=[
            pltpu.VMEM((hp, 1, tq), F32),
            pltpu.VMEM((hp, 1, tq), F32),
            pltpu.VMEM((hp, HEAD_DIM, tq), F32),
        ],
        compiler_params=_params(("parallel", "parallel", "arbitrary")),
        name="fox_attention",
    )(proj, proj, vt, c)


def _sb_kernel(q_ref, k_ref, vt_ref, tri_ref, o_ref, r_scr, acc_scr, *, tq, tk, hp):
    i = pl.program_id(2)
    r_scr[...] = jnp.zeros(r_scr.shape, F32)
    acc_scr[...] = jnp.zeros(acc_scr.shape, F32)

    sign_bit = jnp.uint32(0x80000000)

    def tile(j, masked, qlo=0):
        start = pl.multiple_of(j * tk, tk)
        nq = tq - qlo
        if masked:
            key = j * tk + lax.broadcasted_iota(jnp.int32, (tk, nq), 0)
            qry = i * tq + qlo + lax.broadcasted_iota(jnp.int32, (tk, nq), 1)
            mask = key < qry

        def scores(hh):
            lanes = slice(hh * HEAD_DIM, (hh + 1) * HEAD_DIM)
            kj = k_ref[pl.ds(start, tk), lanes]
            return lax.dot_general(kj, q_ref[qlo:tq, lanes], NT_DIMS, preferred_element_type=F32)

        u_all = [scores(hh) for hh in range(hp)]
        for hh in range(hp):
            u = u_all[hh]
            minus_abs = lax.bitcast_convert_type(lax.bitcast_convert_type(u, jnp.uint32) | sign_bit, F32)
            sp = jnp.maximum(u, 0.0) + jnp.log2(1.0 + jnp.exp2(minus_abs))
            if masked:
                sp = jnp.where(mask, sp, 0.0)
            w = jnp.dot(tri_ref[...], sp.astype(BF16), preferred_element_type=F32)
            a = jnp.exp2(u - ((sp + w) + r_scr[hh, :, qlo:tq]))
            if masked:
                a = jnp.where(mask, a, 0.0)
            vtj = vt_ref[hh * HEAD_DIM:(hh + 1) * HEAD_DIM, pl.ds(start, tk)]
            acc_scr[hh, :, qlo:tq] += jnp.dot(vtj, a.astype(BF16), preferred_element_type=F32)
            r_scr[hh, :, qlo:tq] += jnp.sum(sp, axis=0, keepdims=True)

    ratio = max(tq // tk, 1)
    n_full = (i * tq) // tk
    for d in reversed(range(ratio)):
        tile(n_full + d, True, qlo=d * tk if tq > tk else 0)

    def live():
        return (jnp.min(r_scr[...]) <= R_DONE).astype(jnp.int32)

    def cond(carry):
        it, alive = carry
        return (it < n_full) & (alive > 0)

    def body(carry):
        it, _ = carry
        tile(n_full - 1 - it, False)
        return it + 1, live()

    lax.while_loop(cond, body, (jnp.int32(0), live()))
    for hh in range(hp):
        o_ref[:, hh * HEAD_DIM:(hh + 1) * HEAD_DIM] = acc_scr[hh].T.astype(o_ref.dtype)


def _sb_attention(proj, vt, batch, seq, n_heads, col_q, col_k, row_v, tq, tk, hp):
    t = proj.shape[0]
    nq = seq // tq
    wb = hp * HEAD_DIM
    cq, ck, rv = col_q // wb, col_k // wb, row_v // wb
    tri = jnp.triu(jnp.ones((tk, tk), BF16), 1)
    return pl.pallas_call(
        functools.partial(_sb_kernel, tq=tq, tk=tk, hp=hp),
        out_shape=jax.ShapeDtypeStruct((t, n_heads * HEAD_DIM), BF16),
        grid=(batch, n_heads // hp, nq),
        in_specs=[
            pl.BlockSpec((tq, wb), lambda b, g, i: (b * nq + i, cq + g)),
            pl.BlockSpec((seq, wb), lambda b, g, i: (b, ck + g)),
            pl.BlockSpec((wb, seq), lambda b, g, i: (rv + g, b)),
            pl.BlockSpec((tk, tk), lambda b, g, i: (0, 0)),
        ],
        out_specs=pl.BlockSpec((tq, wb), lambda b, g, i: (b * nq + i, g)),
        scratch_shapes=[
            pltpu.VMEM((hp, 1, tq), F32),
            pltpu.VMEM((hp, HEAD_DIM, tq), F32),
        ],
        compiler_params=_params(("parallel", "parallel", "arbitrary")),
        name="sb_attention",
    )(proj, proj, vt, tri)


def _merge_kernel(oa_ref, ob_ref, wa_ref, wb_ref, ga_ref, gb_ref, o_ref):
    ya = jnp.dot(oa_ref[...], wa_ref[...], preferred_element_type=F32)
    yb = jnp.dot(ob_ref[...], wb_ref[...], preferred_element_type=F32)
    ga = jax.nn.sigmoid(ga_ref[...].astype(F32))
    gb = jax.nn.sigmoid(gb_ref[...].astype(F32))
    o_ref[...] = (ga * ya + gb * yb).astype(o_ref.dtype)


def _merge(o_a, o_b, wa, wb, proj, col_ga, col_gb, tm, tn):
    t, ka = o_a.shape
    kb = o_b.shape[1]
    d = wa.shape[1]
    ca, cb = col_ga // tn, col_gb // tn
    return pl.pallas_call(
        _merge_kernel,
        out_shape=jax.ShapeDtypeStruct((t, d), BF16),
        grid=(t // tm, d // tn),
        in_specs=[
            pl.BlockSpec((tm, ka), lambda m, j: (m, 0)),
            pl.BlockSpec((tm, kb), lambda m, j: (m, 0)),
            pl.BlockSpec((ka, tn), lambda m, j: (0, j)),
            pl.BlockSpec((kb, tn), lambda m, j: (0, j)),
            pl.BlockSpec((tm, tn), lambda m, j: (m, ca + j)),
            pl.BlockSpec((tm, tn), lambda m, j: (m, cb + j)),
        ],
        out_specs=pl.BlockSpec((tm, tn), lambda m, j: (m, j)),
        compiler_params=_params(("parallel", "arbitrary")),
        name="branch_merge",
    )(o_a, o_b, wa, wb, proj, proj)


def _mixout_kernel(a_ref, w_ref, x_ref, o_ref):
    o_ref[...] = x_ref[...] + jnp.dot(a_ref[...], w_ref[...], preferred_element_type=F32)


def _mixout(mixed, w, x2d, tm, tn):
    t, k = mixed.shape
    d = w.shape[1]
    return pl.pallas_call(
        _mixout_kernel,
        out_shape=jax.ShapeDtypeStruct((t, d), F32),
        grid=(t // tm, d // tn),
        in_specs=[
            pl.BlockSpec((tm, k), lambda m, j: (m, 0)),
            pl.BlockSpec((k, tn), lambda m, j: (0, j)),
            pl.BlockSpec((tm, tn), lambda m, j: (m, j)),
        ],
        out_specs=pl.BlockSpec((tm, tn), lambda m, j: (m, j)),
        compiler_params=_params(("parallel", "arbitrary")),
        name="mix_out",
    )(mixed, w, x2d)


def _router_kernel(x_ref, g_ref, wh_ref, wl_ref, lt_ref, sel_ref, h_ref, ids_ref, gate_ref, cnt_ref, carry_scr,
                   *, n_groups, per_group):
    @pl.when(pl.program_id(0) == 0)
    def _():
        carry_scr[...] = jnp.zeros(carry_scr.shape, F32)

    h = _rmsnorm_f32(x_ref[...], g_ref[...])
    hh = h.astype(BF16)
    _store_row_tiles(h_ref, _pack_bf16_pairs(h))
    hl = (h - hh.astype(F32)).astype(BF16)
    logits = (jnp.dot(hh, wh_ref[...], preferred_element_type=F32)
              + jnp.dot(hl, wh_ref[...], preferred_element_type=F32)
              + jnp.dot(hh, wl_ref[...], preferred_element_type=F32))
    tm = logits.shape[0]
    n_exp = n_groups * per_group
    lane = lax.broadcasted_iota(jnp.int32, (tm, LANES), 1)
    lane_f = lane.astype(F32)
    neg_inf = -jnp.inf

    def first_argmax(vals, valid):
        vmax = jnp.max(jnp.where(valid, vals, neg_inf), axis=-1, keepdims=True)
        idx = jnp.min(jnp.where(valid & (vals == vmax), lane_f, float(LANES)), axis=-1, keepdims=True)
        return vmax, idx.astype(jnp.int32)

    is_group = lane < n_groups
    gmax, gsel = first_argmax(logits, is_group)
    gsum = jnp.sum(jnp.where(is_group, jnp.exp(logits - gmax), 0.0), axis=-1, keepdims=True)
    p_group = 1.0 / gsum
    lo = n_groups + gsel * per_group
    in_group = (lane >= lo) & (lane < lo + per_group)
    v1, i1 = first_argmax(logits, in_group)
    v2, i2 = first_argmax(logits, in_group & (lane != i1))
    e2w = jnp.exp(v2 - v1)
    gate1 = (1.0 / (1.0 + e2w)) * p_group
    gate2 = (e2w / (1.0 + e2w)) * p_group
    e1 = i1 - n_groups
    e2 = i2 - n_groups

    onehot = ((lane == e1) | (lane == e2)).astype(BF16)
    before = jnp.dot(lt_ref[...], onehot, preferred_element_type=F32) + carry_scr[...]
    rank1 = jnp.sum(jnp.where(lane == e1, before, 0.0), axis=-1, keepdims=True).astype(jnp.int32)
    rank2 = jnp.sum(jnp.where(lane == e2, before, 0.0), axis=-1, keepdims=True).astype(jnp.int32)
    carry_new = carry_scr[...] + jnp.sum(onehot.astype(F32), axis=0, keepdims=True)
    carry_scr[...] = carry_new

    cols = jnp.where(lane == 0, e1, jnp.where(lane == 1, e2, jnp.where(lane == 2, rank1 >> 7,
           jnp.where(lane == 3, rank1 & 127, jnp.where(lane == 4, rank2 >> 7,
           jnp.where(lane == 5, rank2 & 127, 0))))))
    ids_ref[...] = lax.dot_general(sel_ref[...], cols.astype(F32).astype(BF16), NT_DIMS,
                                   preferred_element_type=F32)
    gate_ref[...] = jnp.where(lane == 0, gate1, jnp.where(lane == 1, gate2, 0.0))
    cnt_ref[...] = jnp.broadcast_to(carry_new, cnt_ref.shape).astype(jnp.int32)
    del n_exp


def _router(x2, g, w_hi, w_lo, tm, n_groups, per_group):
    t, d = x2.shape
    lt = jnp.tril(jnp.ones((tm, tm), BF16), -1)
    sel = jnp.eye(8, LANES, dtype=BF16)
    return pl.pallas_call(
        functools.partial(_router_kernel, n_groups=n_groups, per_group=per_group),
        out_shape=(
            jax.ShapeDtypeStruct((t * (d // 2 // LANES), LANES), jnp.uint32),
            jax.ShapeDtypeStruct((8, t), F32),
            jax.ShapeDtypeStruct((t, LANES), F32),
            jax.ShapeDtypeStruct((8, LANES), jnp.int32),
        ),
        grid=(t // tm,),
        in_specs=[
            pl.BlockSpec((tm, d), lambda m: (m, 0)),
            pl.BlockSpec((1, d), lambda m: (0, 0)),
            pl.BlockSpec((d, LANES), lambda m: (0, 0)),
            pl.BlockSpec((d, LANES), lambda m: (0, 0)),
            pl.BlockSpec((tm, tm), lambda m: (0, 0)),
            pl.BlockSpec((8, LANES), lambda m: (0, 0)),
        ],
        out_specs=(
            pl.BlockSpec((tm * (d // 2 // LANES), LANES), lambda m: (m, 0)),
            pl.BlockSpec((8, tm), lambda m: (0, m)),
            pl.BlockSpec((tm, LANES), lambda m: (m, 0)),
            pl.BlockSpec((8, LANES), lambda m: (0, 0)),
        ),
        scratch_shapes=[pltpu.VMEM((1, LANES), F32)],
        compiler_params=_params(("arbitrary",)),
        name="router",
    )(x2, g, w_hi, w_lo, lt, sel)


def _plan_kernel(cnt_ref, ids_ref, dest_ref, be_ref, first_ref, nxt_ref, nu_ref, ps_scr, *, n_exp, bm, nb):
    shift = bm.bit_length() - 1

    def clear(k, c):
        first_ref[k] = 0
        nxt_ref[k] = -1
        return c

    lax.fori_loop(0, nb, clear, 0)

    def per_expert(e, carry):
        blk_start, last, prev_start = carry
        n = lax.shift_right_logical(cnt_ref[0, e] + (bm - 1), shift)
        ps_scr[e] = blk_start * bm

        def fill(k, c):
            be_ref[blk_start + k] = e
            return c

        lax.fori_loop(0, n, fill, 0)

        @pl.when(n > 0)
        def _():
            first_ref[blk_start] = 1

            @pl.when(prev_start >= 0)
            def _():
                nxt_ref[prev_start] = e

        used = n > 0
        return blk_start + n, jnp.where(used, e, last), jnp.where(used, blk_start, prev_start)

    n_used, last, _ = lax.fori_loop(0, n_exp, per_expert, (jnp.int32(0), jnp.int32(0), jnp.int32(-1)))
    nu_ref[0] = n_used

    def tail(k, c):
        be_ref[k] = last
        return c

    lax.fori_loop(n_used, nb, tail, 0)

    e1 = ids_ref[0:1, :]
    e2 = ids_ref[1:2, :]
    p1 = jnp.zeros_like(e1)
    p2 = jnp.zeros_like(e2)
    for e in range(n_exp):
        ps = ps_scr[e].astype(F32)
        p1 = jnp.where(e1 == e, ps, p1)
        p2 = jnp.where(e2 == e, ps, p2)
    dest_ref[...] = jnp.zeros(dest_ref.shape, jnp.int32)
    dest_ref[0:1, :] = (p1 + ids_ref[2:3, :] * 128.0 + ids_ref[3:4, :]).astype(jnp.int32)
    dest_ref[1:2, :] = (p2 + ids_ref[4:5, :] * 128.0 + ids_ref[5:6, :]).astype(jnp.int32)


def _plan(counts, ids_rows, n_exp, bm, nb):
    t = ids_rows.shape[1]
    return pl.pallas_call(
        functools.partial(_plan_kernel, n_exp=n_exp, bm=bm, nb=nb),
        out_shape=(
            jax.ShapeDtypeStruct((8, t), jnp.int32),
            jax.ShapeDtypeStruct((nb,), jnp.int32),
            jax.ShapeDtypeStruct((nb,), jnp.int32),
            jax.ShapeDtypeStruct((nb,), jnp.int32),
            jax.ShapeDtypeStruct((1,), jnp.int32),
        ),
        in_specs=[
            pl.BlockSpec(memory_space=pltpu.SMEM),
            pl.BlockSpec(memory_space=pltpu.VMEM),
        ],
        out_specs=(
            pl.BlockSpec(memory_space=pltpu.VMEM),
            pl.BlockSpec(memory_space=pltpu.SMEM),
            pl.BlockSpec(memory_space=pltpu.SMEM),
            pl.BlockSpec(memory_space=pltpu.SMEM),
            pl.BlockSpec(memory_space=pltpu.SMEM),
        ),
        scratch_shapes=[pltpu.SMEM((n_exp,), jnp.int32)],
        name="moe_plan",
    )(counts, ids_rows)


def _dispatch_kernel(d1_ref, d2_ref, first_ref, nu_ref, h_ref, xb_ref, zbuf, sem, zsem, *, tm, ns, bm, nb):
    base = pl.program_id(0) * tm

    @pl.when(pl.program_id(0) == 0)
    def _():
        zbuf[...] = jnp.zeros(zbuf.shape, zbuf.dtype)
        n_used = nu_ref[0]

        def has_padding(b):
            nxt_first = first_ref[jnp.minimum(b + 1, nb - 1)]
            return (b >= n_used - 1) | (nxt_first == 1)

        def zero_copy(b):
            return pltpu.make_async_copy(zbuf, xb_ref.at[pl.ds(pl.multiple_of(b * (bm * ns), bm * ns), bm * ns)], zsem)

        def zstart(b, carry):
            @pl.when(has_padding(b))
            def _():
                zero_copy(b).start()
            return carry

        def zwait(b, carry):
            @pl.when(has_padding(b))
            def _():
                zero_copy(b).wait()
            return carry

        lax.fori_loop(0, nb, zstart, 0)
        lax.fori_loop(0, nb, zwait, 0)

    def row_copy(r, dest):
        src = h_ref.at[pl.ds(pl.multiple_of(r * ns, ns), ns)]
        return pltpu.make_async_copy(src, xb_ref.at[pl.ds(pl.multiple_of(dest * ns, ns), ns)], sem)

    def start(r, carry):
        row_copy(r, d1_ref[base + r]).start()
        row_copy(r, d2_ref[base + r]).start()
        return carry

    lax.fori_loop(0, tm, start, 0, unroll=DMA_UNROLL)
    for _ in range(2):
        pltpu.make_async_copy(h_ref, xb_ref.at[pl.ds(0, tm * ns)], sem).wait()


def _dispatch(h2p, dest1, dest2, first, n_used, nb, bm, tm, ns):
    t = h2p.shape[0] // ns
    return pl.pallas_call(
        functools.partial(_dispatch_kernel, tm=tm, ns=ns, bm=bm, nb=nb),
        out_shape=jax.ShapeDtypeStruct((nb * bm * ns, LANES), h2p.dtype),
        grid_spec=pltpu.PrefetchScalarGridSpec(
            num_scalar_prefetch=4,
            grid=(t // tm,),
            in_specs=[pl.BlockSpec((tm * ns, LANES), lambda m, d1, d2, fi, nu: (m, 0))],
            out_specs=pl.BlockSpec(memory_space=pl.ANY),
            scratch_shapes=[pltpu.VMEM((bm * ns, LANES), h2p.dtype), pltpu.SemaphoreType.DMA,
                            pltpu.SemaphoreType.DMA],
        ),
        compiler_params=_params(("arbitrary",)),
        name="moe_dispatch",
    )(dest1, dest2, first, n_used, h2p)


def _experts_kernel(be_ref, first_ref, nxt_ref, nu_ref, x_ref, wg_hbm, wu_hbm, wd_hbm, y_ref,
                    wg_buf, wu_buf, wd_buf, sem, run_scr, *, ns):
    b = pl.program_id(0)

    def weight_copies(e, slot):
        return (pltpu.make_async_copy(wg_hbm.at[e], wg_buf.at[slot], sem.at[slot]),
                pltpu.make_async_copy(wu_hbm.at[e], wu_buf.at[slot], sem.at[slot]),
                pltpu.make_async_copy(wd_hbm.at[e], wd_buf.at[slot], sem.at[slot]))

    @pl.when(b == 0)
    def _():
        run_scr[0] = 0
        for cp in weight_copies(be_ref[0], 0):
            cp.start()

    @pl.when((first_ref[b] == 1) & (b < nu_ref[0]))
    def _():
        run = run_scr[0]
        slot = run & 1
        for cp in weight_copies(be_ref[b], slot):
            cp.wait()

        @pl.when(nxt_ref[b] >= 0)
        def _():
            for cp in weight_copies(nxt_ref[b], 1 - slot):
                cp.start()

        run_scr[0] = run + 1

    @pl.when(b < nu_ref[0])
    def _():
        slot = (run_scr[0] - 1) & 1
        x = _unpack_bf16_pairs(_load_row_tiles(x_ref, ns)).astype(BF16)
        g = jnp.dot(x, wg_buf[slot].astype(BF16), preferred_element_type=F32)
        u = jnp.dot(x, wu_buf[slot].astype(BF16), preferred_element_type=F32)
        hid = (g * jax.nn.sigmoid(g) * u).astype(BF16)
        y = jnp.dot(hid, wd_buf[slot].astype(BF16), preferred_element_type=F32)
        _store_row_tiles(y_ref, _pack_bf16_pairs(y))


def _experts(xb, w_gate, w_up, w_down, block_expert, first, nxt, n_used, bm, ns):
    d, de = w_gate.shape[1:]
    nb = xb.shape[0] // (bm * ns)

    def row_map(b, be, fi, nx, nu):
        return (jnp.minimum(b, nu[0] - 1), 0)

    return pl.pallas_call(
        functools.partial(_experts_kernel, ns=ns),
        out_shape=jax.ShapeDtypeStruct(xb.shape, jnp.uint32),
        grid_spec=pltpu.PrefetchScalarGridSpec(
            num_scalar_prefetch=4,
            grid=(nb,),
            in_specs=[
                pl.BlockSpec((bm * ns, LANES), row_map),
                pl.BlockSpec(memory_space=pl.ANY),
                pl.BlockSpec(memory_space=pl.ANY),
                pl.BlockSpec(memory_space=pl.ANY),
            ],
            out_specs=pl.BlockSpec((bm * ns, LANES), row_map),
            scratch_shapes=[
                pltpu.VMEM((2, d, de), F32),
                pltpu.VMEM((2, d, de), F32),
                pltpu.VMEM((2, de, d), F32),
                pltpu.SemaphoreType.DMA((2,)),
                pltpu.SMEM((1,), jnp.int32),
            ],
        ),
        input_output_aliases={4: 0},
        compiler_params=_params(("arbitrary",)),
        name="moe_experts",
    )(block_expert, first, nxt, n_used, xb, w_gate, w_up, w_down)


def _ple_kernel(d1_ref, d2_ref, x_ref, mg_ref, yb_ref, g_ref, p_ref, wg_ref, wp_ref, gf_ref, o_ref,
                ybuf, sem, x3_scr, h_scr, p_scr, *, tm, tn, ns, nj):
    m = pl.program_id(0)
    j = pl.program_id(1)
    n_m = pl.num_programs(0)
    n_j = pl.num_programs(1)
    slot = m & 1

    def row_copies(blk, r, s):
        tok = blk * tm + r
        src1 = yb_ref.at[pl.ds(pl.multiple_of(d1_ref[tok] * ns, ns), ns)]
        src2 = yb_ref.at[pl.ds(pl.multiple_of(d2_ref[tok] * ns, ns), ns)]
        rows = pl.ds(pl.multiple_of(r * ns, ns), ns)
        return (pltpu.make_async_copy(src1, ybuf.at[s, 0, rows], sem.at[s]),
                pltpu.make_async_copy(src2, ybuf.at[s, 1, rows], sem.at[s]))

    def drain(s):
        for k in range(2):
            pltpu.make_async_copy(yb_ref.at[pl.ds(0, tm * ns)], ybuf.at[s, k], sem.at[s]).wait()

    @pl.when(j == 0)
    def _():
        @pl.when(m == 0)
        def _():
            def start(r, carry):
                for cp in row_copies(0, r, 0):
                    cp.start()
                return carry

            lax.fori_loop(0, tm, start, 0, unroll=DMA_UNROLL)

        drain(slot)
        mg = mg_ref[...]
        y1 = _unpack_bf16_pairs(_load_row_tiles(ybuf.at[slot, 0], ns))
        y2 = _unpack_bf16_pairs(_load_row_tiles(ybuf.at[slot, 1], ns))
        x3 = x_ref[...] + mg[:, 0:1] * y1 + mg[:, 1:2] * y2
        x3_scr[...] = x3
        h_scr[...] = _rmsnorm_f32(x3, g_ref[...]).astype(BF16)
        p_scr[...] = p_ref[...].astype(BF16)

    nxt = jnp.minimum(m + 1, n_m - 1)
    rows_per_step = tm // nj
    for r in range(rows_per_step):
        for cp in row_copies(nxt, j * rows_per_step + r, 1 - slot):
            cp.start()

    col = pl.multiple_of(j * tn, tn)
    gate = jnp.dot(h_scr[...], wg_ref[...], preferred_element_type=F32)
    ple = jnp.dot(p_scr[...], wp_ref[...], preferred_element_type=F32)
    o_ref[:, pl.ds(col, tn)] = x3_scr[:, pl.ds(col, tn)] + jax.nn.sigmoid(gate) * ple

    @pl.when(j == n_j - 1)
    def _():
        o_ref[...] = _rmsnorm_f32(o_ref[...], gf_ref[...])

        @pl.when(m == n_m - 1)
        def _():
            drain(1 - slot)


def _combine_ple_final(x2, moe_gates, yb, dest1, dest2, g_ple, p2d, w_gate, w_proj, g_final, tm, tn, ns):
    t, d = x2.shape
    dp = p2d.shape[1]
    return pl.pallas_call(
        functools.partial(_ple_kernel, tm=tm, tn=tn, ns=ns, nj=d // tn),
        out_shape=jax.ShapeDtypeStruct((t, d), F32),
        grid_spec=pltpu.PrefetchScalarGridSpec(
            num_scalar_prefetch=2,
            grid=(t // tm, d // tn),
            in_specs=[
                pl.BlockSpec((tm, d), lambda m, j, d1, d2: (m, 0)),
                pl.BlockSpec((tm, LANES), lambda m, j, d1, d2: (m, 0)),
                pl.BlockSpec(memory_space=pl.ANY),
                pl.BlockSpec((1, d), lambda m, j, d1, d2: (0, 0)),
                pl.BlockSpec((tm, dp), lambda m, j, d1, d2: (m, 0)),
                pl.BlockSpec((d, tn), lambda m, j, d1, d2: (0, j)),
                pl.BlockSpec((dp, tn), lambda m, j, d1, d2: (0, j)),
                pl.BlockSpec((1, d), lambda m, j, d1, d2: (0, 0)),
            ],
            out_specs=pl.BlockSpec((tm, d), lambda m, j, d1, d2: (m, 0)),
            scratch_shapes=[
                pltpu.VMEM((2, 2, tm * ns, LANES), yb.dtype),
                pltpu.SemaphoreType.DMA((2,)),
                pltpu.VMEM((tm, d), F32),
                pltpu.VMEM((tm, d), BF16),
                pltpu.VMEM((tm, dp), BF16),
            ],
        ),
        compiler_params=_params(("arbitrary", "arbitrary")),
        name="combine_ple_final",
    )(dest1, dest2, x2, moe_gates, yb, g_ple, p2d, w_gate, w_proj, g_final)


def _tile(n, pref):
    return pref if n % pref == 0 else n


def _layer(x2d, p2d, batch, seq, w_in, b_forget, w_branch_fox, w_branch_sb, w_mix_out, g_mix, g_ffn,
           w_group, w_expert, w_gate, w_up, w_down, g_ple, w_ple_proj, w_ple_gate, g_final):
    t, d = x2d.shape
    n_heads = b_forget.shape[0]
    w_att = n_heads * HEAD_DIM
    n_exp = w_expert.shape[1]

    qs = LOG2E * HEAD_DIM ** -0.5
    w_t, wf_t = _wprep(w_in.T, w_att, n_heads, qs, 512)
    b_row = jnp.pad(b_forget, (0, LANES - n_heads)).reshape(1, LANES)
    col_qa, col_ka, col_qb, col_kb = 0, w_att, 2 * w_att, 3 * w_att
    col_ga, col_gb = 4 * w_att, 4 * w_att + d
    row_va, row_vb = 0, w_att

    tm = _tile(t, 1024)
    tn = _tile(d, 1024)
    proj, vt, f_tok = _inproj(x2d, g_mix.reshape(1, d), w_t, wf_t, 2 * w_att, tm, _tile(d // 2, 1024))
    c = _forget_cumsum(f_tok, b_row, batch, n_heads)

    tq = _tile(seq, 512)
    hp = 4
    o_a = _fox_attention(proj, vt, c, batch, seq, n_heads, col_qa, col_ka, row_va, tq, hp)
    o_b = _sb_attention(proj, vt, batch, seq, n_heads, col_qb, col_kb, row_vb, tq, _tile(seq, 256), hp)

    mixed = _merge(o_a, o_b, w_branch_fox.astype(BF16), w_branch_sb.astype(BF16), proj, col_ga, col_gb, tm, tn)
    x2 = _mixout(mixed, w_mix_out.astype(BF16), x2d, tm, tn)

    w_r = jnp.concatenate([w_group, w_expert], axis=1)
    w_r = jnp.pad(w_r, ((0, 0), (0, LANES - w_r.shape[1])))
    w_r_hi = w_r.astype(BF16)
    w_r_lo = (w_r - w_r_hi.astype(F32)).astype(BF16)
    tr = _tile(t, 512)
    h2, ids_rows, gates, counts = _router(x2, g_ffn.reshape(1, d), w_r_hi, w_r_lo, tr, N_GROUPS, EXPERTS_PER_GROUP)

    bm = 256
    n_assign = 2 * t
    nb = n_assign // bm + n_exp
    dest, block_expert, first, nxt, n_used = _plan(counts, ids_rows, n_exp, bm, nb)
    dest1, dest2 = dest[0], dest[1]

    td = _tile(t, 1024)
    ns = d // 2 // LANES
    xb = _dispatch(h2, dest1, dest2, first, n_used, nb, bm, td, ns)
    yb = _experts(xb, w_gate, w_up, w_down, block_expert, first, nxt, n_used, bm, ns)

    tp = _tile(t, 512)
    return _combine_ple_final(x2, gates, yb, dest1, dest2, g_ple.reshape(1, d), p2d, w_ple_gate.astype(BF16),
                              w_ple_proj.astype(BF16), g_final.reshape(1, d), tp, tn, ns)


def kernel(x, p, w_in, b_forget, w_branch_fox, w_branch_sb, w_mix_out, g_mix, g_ffn, w_group, w_expert,
           w_gate, w_up, w_down, g_ple, w_ple_proj, w_ple_gate, g_final):
    b, s, d = x.shape
    depth = w_in.shape[0]
    assert depth == 1, "the final norm is fused into the single layer"
    x2d = x.reshape(b * s, d)
    out = _layer(x2d, p[0].reshape(b * s, -1), b, s, w_in[0], b_forget[0], w_branch_fox[0], w_branch_sb[0],
                 w_mix_out[0], g_mix[0], g_ffn[0], w_group[0], w_expert[0], w_gate[0], w_up[0], w_down[0],
                 g_ple[0], w_ple_proj[0], w_ple_gate[0], g_final)
    return out.reshape(b, s, d)
```

```python
import functools

import jax
import jax.numpy as jnp
from jax import lax
from jax.experimental import pallas as pl
from jax.experimental.pallas import tpu as pltpu

F32 = jnp.float32
BF16 = jnp.bfloat16

HEAD_DIM = 128
N_GROUPS = 4
EXPERTS_PER_GROUP = 8
EPS = 1e-6
LANES = 128
SUBLANES = 8
VMEM_LIMIT = 56 * 1024 * 1024

LOG2E = 1.4426950408889634
DMA_UNROLL = 8
R_DONE = 160.0

NT_DIMS = (((1,), (1,)), ((), ()))


def _params(sem):
    return pltpu.CompilerParams(dimension_semantics=sem, vmem_limit_bytes=VMEM_LIMIT)


def _rmsnorm_f32(x, g):
    ms = jnp.mean(x * x, axis=-1, keepdims=True)
    return x * lax.rsqrt(ms + EPS) * g


def _pack_bf16_pairs(x):
    n = x.shape[1] // 2
    bits = lax.bitcast_convert_type(x.astype(BF16).astype(F32), jnp.uint32)
    return (bits[:, :n] >> 16) | (bits[:, n:] & jnp.uint32(0xFFFF0000))


def _unpack_bf16_pairs(w):
    lo = lax.bitcast_convert_type(w << 16, F32)
    hi = lax.bitcast_convert_type(w & jnp.uint32(0xFFFF0000), F32)
    return jnp.concatenate([lo, hi], axis=1)


def _store_row_tiles(ref, words):
    n = words.shape[0]
    ns = words.shape[1] // LANES
    for c in range(ns):
        ref[pl.ds(c, n, stride=ns), :] = words[:, c * LANES:(c + 1) * LANES]


def _load_row_tiles(ref, ns):
    n = ref.shape[0] // ns
    return jnp.concatenate([ref[pl.ds(c, n, stride=ns), :] for c in range(ns)], axis=1)


def _wprep_kernel(off_ref, scale_ref, w_ref, f_ref, wt_ref, wf_ref, *, n_heads):
    i = pl.program_id(0)
    wt_ref[...] = (w_ref[...] * scale_ref[i]).astype(BF16)

    @pl.when(i == 0)
    def _():
        row = lax.broadcasted_iota(jnp.int32, wf_ref.shape, 0)
        f_rows = jnp.concatenate([f_ref[...]] * (wf_ref.shape[0] // f_ref.shape[0]), axis=0)
        wf_ref[...] = jnp.where(row < n_heads, f_rows, 0.0).astype(BF16)


def _wprep(w_t, w_att, n_heads, qs, rb):
    n_in, d = w_t.shape
    off_b = 3 * w_att + n_heads
    segments = [(2 * w_att, w_att, 1.0), (off_b + 2 * w_att, w_att, 1.0),
                (0, w_att, qs), (w_att, w_att, 1.0),
                (off_b, w_att, qs), (off_b + w_att, w_att, 1.0),
                (off_b + 3 * w_att, n_in - off_b - 3 * w_att, 1.0)]
    offs, scales = [], []
    for start, length, scale in segments:
        assert length % rb == 0 and start % SUBLANES == 0
        for k in range(length // rb):
            offs.append((start + k * rb) // SUBLANES)
            scales.append(scale)
    n_out = len(offs) * rb
    f_blk = SUBLANES
    assert n_heads <= f_blk and (3 * w_att) % f_blk == 0
    return pl.pallas_call(
        functools.partial(_wprep_kernel, n_heads=n_heads),
        out_shape=(jax.ShapeDtypeStruct((n_out, d), BF16), jax.ShapeDtypeStruct((LANES, d), BF16)),
        grid_spec=pltpu.PrefetchScalarGridSpec(
            num_scalar_prefetch=2,
            grid=(len(offs),),
            in_specs=[
                pl.BlockSpec((pl.Element(rb), pl.Element(d)), lambda i, off, sc: (off[i] * SUBLANES, 0)),
                pl.BlockSpec((pl.Element(f_blk), pl.Element(d)), lambda i, off, sc: (3 * w_att, 0)),
            ],
            out_specs=(
                pl.BlockSpec((rb, d), lambda i, off, sc: (i, 0)),
                pl.BlockSpec((LANES, d), lambda i, off, sc: (0, 0)),
            ),
        ),
        compiler_params=_params(("arbitrary",)),
        name="inproj_weights",
    )(jnp.asarray(offs, jnp.int32), jnp.asarray(scales, F32), w_t, w_t)


def _inproj_kernel(x_ref, g_ref, w_ref, wf_ref, o_ref, vt_ref, f_ref, h_scr, *, nvt):
    j = pl.program_id(1)

    @pl.when(j == 0)
    def _():
        hb = _rmsnorm_f32(x_ref[...], g_ref[...]).astype(BF16)
        h_scr[...] = hb
        f_ref[...] = lax.dot_general(hb, wf_ref[...], NT_DIMS, preferred_element_type=F32)

    @pl.when(j < nvt)
    def _():
        vt_ref[...] = lax.dot_general(w_ref[...], h_scr[...], NT_DIMS,
                                      preferred_element_type=F32).astype(vt_ref.dtype)

    @pl.when(j >= nvt)
    def _():
        o_ref[...] = lax.dot_general(h_scr[...], w_ref[...], NT_DIMS,
                                     preferred_element_type=F32).astype(o_ref.dtype)


def _inproj(x2d, g, w_t, wf_t, n_v, tm, tn):
    t, d = x2d.shape
    n = w_t.shape[0] - n_v
    nvt = n_v // tn
    return pl.pallas_call(
        functools.partial(_inproj_kernel, nvt=nvt),
        out_shape=(jax.ShapeDtypeStruct((t, n), BF16), jax.ShapeDtypeStruct((n_v, t), BF16),
                   jax.ShapeDtypeStruct((t, LANES), F32)),
        grid=(t // tm, nvt + n // tn),
        in_specs=[
            pl.BlockSpec((tm, d), lambda m, j: (m, 0)),
            pl.BlockSpec((1, d), lambda m, j: (0, 0)),
            pl.BlockSpec((tn, d), lambda m, j: (j, 0)),
            pl.BlockSpec((LANES, d), lambda m, j: (0, 0)),
        ],
        out_specs=(
            pl.BlockSpec((tm, tn), lambda m, j: (m, jnp.maximum(j - nvt, 0))),
            pl.BlockSpec((tn, tm), lambda m, j: (jnp.minimum(j, nvt - 1), m)),
            pl.BlockSpec((tm, LANES), lambda m, j: (m, 0)),
        ),
        scratch_shapes=[pltpu.VMEM((tm, d), BF16)],
        compiler_params=_params(("parallel", "arbitrary")),
        name="inproj",
    )(x2d, g, w_t, wf_t)


def _split3(x):
    p1 = x.astype(BF16)
    r1 = x - p1.astype(F32)
    p2 = r1.astype(BF16)
    p3 = (r1 - p2.astype(F32)).astype(BF16)
    return p1, p2, p3


def _cumsum_kernel(f_ref, b_ref, lt_ref, c_ref, carry_scr, *, chunk, n_heads):
    @pl.when(pl.program_id(1) == 0)
    def _():
        carry_scr[...] = jnp.zeros(carry_scr.shape, F32)

    rows = f_ref.shape[0]
    lt = lt_ref[...]
    carry = carry_scr[...]
    for i in range(rows // chunk):
        lf = jax.nn.log_sigmoid(f_ref[i * chunk:(i + 1) * chunk, :] + b_ref[...])
        p1, p2, p3 = _split3(lf)
        cs = (jnp.dot(lt, p1, preferred_element_type=F32)
              + jnp.dot(lt, p2, preferred_element_type=F32)
              + jnp.dot(lt, p3, preferred_element_type=F32)) + carry
        for h in range(n_heads):
            c_ref[h, i * chunk:(i + 1) * chunk, :] = jnp.broadcast_to(cs[:, h:h + 1] * LOG2E, (chunk, LANES))
        carry = cs[chunk - 1:chunk, :]
    carry_scr[...] = carry


def _forget_cumsum(f_tok, b_row, batch, n_heads):
    t = f_tok.shape[0]
    s = t // batch
    rows = min(1024, s)
    chunk = min(256, s)
    lt = jnp.tril(jnp.ones((chunk, chunk), BF16))
    nr = s // rows
    return pl.pallas_call(
        functools.partial(_cumsum_kernel, chunk=chunk, n_heads=n_heads),
        out_shape=jax.ShapeDtypeStruct((n_heads, t, LANES), F32),
        grid=(batch, nr),
        in_specs=[
            pl.BlockSpec((rows, LANES), lambda b, r: (b * nr + r, 0)),
            pl.BlockSpec((1, LANES), lambda b, r: (0, 0)),
            pl.BlockSpec((chunk, chunk), lambda b, r: (0, 0)),
        ],
        out_specs=pl.BlockSpec((n_heads, rows, LANES), lambda b, r: (0, b * nr + r, 0)),
        scratch_shapes=[pltpu.VMEM((1, LANES), F32)],
        compiler_params=_params(("parallel", "arbitrary")),
        name="forget_cumsum",
    )(f_tok, b_row, lt)


def _fox_kernel(q_ref, k_ref, vt_ref, c_ref, o_ref, m_scr, l_scr, acc_scr, *, tq, tk, hp):
    i = pl.program_id(2)
    m_scr[...] = jnp.full(m_scr.shape, -jnp.inf, F32)
    l_scr[...] = jnp.zeros(l_scr.shape, F32)
    acc_scr[...] = jnp.zeros(acc_scr.shape, F32)

    def tile(j, masked):
        start = pl.multiple_of(j * tk, tk)
        if masked:
            key = j * tk + lax.broadcasted_iota(jnp.int32, (tk, tq), 0)
            qry = i * tq + lax.broadcasted_iota(jnp.int32, (tk, tq), 1)
            keep = key <= qry

        def scores(hh):
            lanes = slice(hh * HEAD_DIM, (hh + 1) * HEAD_DIM)
            kj = k_ref[pl.ds(start, tk), lanes]
            return lax.dot_general(kj, q_ref[:, lanes], NT_DIMS, preferred_element_type=F32)

        s_all = [scores(hh) for hh in range(hp)]
        for hh in range(hp):
            s = s_all[hh]
            cj = c_ref[hh, pl.ds(start, tk), :]
            s = s - jnp.concatenate([cj] * (tq // LANES), axis=1)
            if masked:
                s = jnp.where(keep, s, -jnp.inf)
            m_prev = m_scr[hh]
            m_new = jnp.maximum(m_prev, jnp.max(s, axis=0, keepdims=True))
            alpha = jnp.exp2(m_prev - m_new)
            p = jnp.exp2(s - m_new)
            l_scr[hh] = alpha * l_scr[hh] + jnp.sum(p, axis=0, keepdims=True)
            vtj = vt_ref[hh * HEAD_DIM:(hh + 1) * HEAD_DIM, pl.ds(start, tk)]
            acc_scr[hh] = alpha * acc_scr[hh] + jnp.dot(vtj, p.astype(BF16), preferred_element_type=F32)
            m_scr[hh] = m_new

    def body(j, carry):
        tile(j, False)
        return carry

    lax.fori_loop(0, i, body, 0)
    tile(i, True)
    for hh in range(hp):
        o = acc_scr[hh] / l_scr[hh]
        o_ref[:, hh * HEAD_DIM:(hh + 1) * HEAD_DIM] = o.T.astype(o_ref.dtype)


def _fox_attention(proj, vt, c, batch, seq, n_heads, col_q, col_k, row_v, tq, hp):
    t = proj.shape[0]
    nq = seq // tq
    wb = hp * HEAD_DIM
    cq, ck, rv = col_q // wb, col_k // wb, row_v // wb
    return pl.pallas_call(
        functools.partial(_fox_kernel, tq=tq, tk=tq, hp=hp),
        out_shape=jax.ShapeDtypeStruct((t, n_heads * HEAD_DIM), BF16),
        grid=(batch, n_heads // hp, nq),
        in_specs=[
            pl.BlockSpec((tq, wb), lambda b, g, i: (b * nq + i, cq + g)),
            pl.BlockSpec((seq, wb), lambda b, g, i: (b, ck + g)),
            pl.BlockSpec((wb, seq), lambda b, g, i: (rv + g, b)),
            pl.BlockSpec((hp, seq, LANES), lambda b, g, i: (g, b, 0)),
        ],
        out_specs=pl.BlockSpec((tq, wb), lambda b, g, i: (b * nq + i, g)),
        scratch_shapes=[
            pltpu.VMEM((hp, 1, tq), F32),
            pltpu.VMEM((hp, 1, tq), F32),
            pltpu.VMEM((hp, HEAD_DIM, tq), F32),
        ],
        compiler_params=_params(("parallel", "parallel", "arbitrary")),
        name="fox_attention",
    )(proj, proj, vt, c)


def _sb_kernel(q_ref, k_ref, vt_ref, tri_ref, o_ref, r_scr, acc_scr, *, tq, tk, hp):
    i = pl.program_id(2)
    r_scr[...] = jnp.zeros(r_scr.shape, F32)
    acc_scr[...] = jnp.zeros(acc_scr.shape, F32)

    sign_bit = jnp.uint32(0x80000000)

    def tile(j, masked, qlo=0):
        start = pl.multiple_of(j * tk, tk)
        nq = tq - qlo
        if masked:
            key = j * tk + lax.broadcasted_iota(jnp.int32, (tk, nq), 0)
            qry = i * tq + qlo + lax.broadcasted_iota(jnp.int32, (tk, nq), 1)
            mask = key < qry

        def scores(hh):
            lanes = slice(hh * HEAD_DIM, (hh + 1) * HEAD_DIM)
            kj = k_ref[pl.ds(start, tk), lanes]
            return lax.dot_general(kj, q_ref[qlo:tq, lanes], NT_DIMS, preferred_element_type=F32)

        u_all = [scores(hh) for hh in range(hp)]
        for hh in range(hp):
            u = u_all[hh]
            minus_abs = lax.bitcast_convert_type(lax.bitcast_convert_type(u, jnp.uint32) | sign_bit, F32)
            sp = jnp.maximum(u, 0.0) + jnp.log2(1.0 + jnp.exp2(minus_abs))
            if masked:
                sp = jnp.where(mask, sp, 0.0)
            w = jnp.dot(tri_ref[...], sp.astype(BF16), preferred_element_type=F32)
            a = jnp.exp2(u - ((sp + w) + r_scr[hh, :, qlo:tq]))
            if masked:
                a = jnp.where(mask, a, 0.0)
            vtj = vt_ref[hh * HEAD_DIM:(hh + 1) * HEAD_DIM, pl.ds(start, tk)]
            acc_scr[hh, :, qlo:tq] += jnp.dot(vtj, a.astype(BF16), preferred_element_type=F32)
            r_scr[hh, :, qlo:tq] += jnp.sum(sp, axis=0, keepdims=True)

    ratio = max(tq // tk, 1)
    n_full = (i * tq) // tk
    for d in reversed(range(ratio)):
        tile(n_full + d, True, qlo=d * tk if tq > tk else 0)

    def live():
        return (jnp.min(r_scr[...]) <= R_DONE).astype(jnp.int32)

    def cond(carry):
        it, alive = carry
        return (it < n_full) & (alive > 0)

    def body(carry):
        it, _ = carry
        tile(n_full - 1 - it, False)
        return it + 1, live()

    lax.while_loop(cond, body, (jnp.int32(0), live()))
    for hh in range(hp):
        o_ref[:, hh * HEAD_DIM:(hh + 1) * HEAD_DIM] = acc_scr[hh].T.astype(o_ref.dtype)


def _sb_attention(proj, vt, batch, seq, n_heads, col_q, col_k, row_v, tq, tk, hp):
    t = proj.shape[0]
    nq = seq // tq
    wb = hp * HEAD_DIM
    cq, ck, rv = col_q // wb, col_k // wb, row_v // wb
    tri = jnp.triu(jnp.ones((tk, tk), BF16), 1)
    return pl.pallas_call(
        functools.partial(_sb_kernel, tq=tq, tk=tk, hp=hp),
        out_shape=jax.ShapeDtypeStruct((t, n_heads * HEAD_DIM), BF16),
        grid=(batch, n_heads // hp, nq),
        in_specs=[
            pl.BlockSpec((tq, wb), lambda b, g, i: (b * nq + i, cq + g)),
            pl.BlockSpec((seq, wb), lambda b, g, i: (b, ck + g)),
            pl.BlockSpec((wb, seq), lambda b, g, i: (rv + g, b)),
            pl.BlockSpec((tk, tk), lambda b, g, i: (0, 0)),
        ],
        out_specs=pl.BlockSpec((tq, wb), lambda b, g, i: (b * nq + i, g)),
        scratch_shapes=[
            pltpu.VMEM((hp, 1, tq), F32),
            pltpu.VMEM((hp, HEAD_DIM, tq), F32),
        ],
        compiler_params=_params(("parallel", "parallel", "arbitrary")),
        name="sb_attention",
    )(proj, proj, vt, tri)


_RESIDENT = dict(pipeline_mode=pl.Buffered(1))


def _merge_kernel(oa_ref, ob_ref, wa_ref, wb_ref, ga_ref, gb_ref, o_ref, *, tn):
    col = pl.multiple_of(pl.program_id(1) * tn, tn)
    ya = jnp.dot(oa_ref[...], wa_ref[:, pl.ds(col, tn)].astype(BF16), preferred_element_type=F32)
    yb = jnp.dot(ob_ref[...], wb_ref[:, pl.ds(col, tn)].astype(BF16), preferred_element_type=F32)
    ga = jax.nn.sigmoid(ga_ref[...].astype(F32))
    gb = jax.nn.sigmoid(gb_ref[...].astype(F32))
    o_ref[...] = (ga * ya + gb * yb).astype(o_ref.dtype)


def _merge(o_a, o_b, wa, wb, proj, col_ga, col_gb, tm, tn):
    t, ka = o_a.shape
    kb = o_b.shape[1]
    d = wa.shape[1]
    ca, cb = col_ga // tn, col_gb // tn
    return pl.pallas_call(
        functools.partial(_merge_kernel, tn=tn),
        out_shape=jax.ShapeDtypeStruct((t, d), BF16),
        grid=(t // tm, d // tn),
        in_specs=[
            pl.BlockSpec((tm, ka), lambda m, j: (m, 0)),
            pl.BlockSpec((tm, kb), lambda m, j: (m, 0)),
            pl.BlockSpec((ka, d), lambda m, j: (0, 0), **_RESIDENT),
            pl.BlockSpec((kb, d), lambda m, j: (0, 0), **_RESIDENT),
            pl.BlockSpec((tm, tn), lambda m, j: (m, ca + j)),
            pl.BlockSpec((tm, tn), lambda m, j: (m, cb + j)),
        ],
        out_specs=pl.BlockSpec((tm, tn), lambda m, j: (m, j)),
        compiler_params=_params(("parallel", "arbitrary")),
        name="branch_merge",
    )(o_a, o_b, wa, wb, proj, proj)


def _mixout_kernel(a_ref, w_ref, x_ref, o_ref, *, tn):
    col = pl.multiple_of(pl.program_id(1) * tn, tn)
    w = w_ref[:, pl.ds(col, tn)].astype(BF16)
    o_ref[...] = x_ref[...] + jnp.dot(a_ref[...], w, preferred_element_type=F32)


def _mixout(mixed, w, x2d, tm, tn):
    t, k = mixed.shape
    d = w.shape[1]
    return pl.pallas_call(
        functools.partial(_mixout_kernel, tn=tn),
        out_shape=jax.ShapeDtypeStruct((t, d), F32),
        grid=(t // tm, d // tn),
        in_specs=[
            pl.BlockSpec((tm, k), lambda m, j: (m, 0)),
            pl.BlockSpec((k, d), lambda m, j: (0, 0), **_RESIDENT),
            pl.BlockSpec((tm, tn), lambda m, j: (m, j)),
        ],
        out_specs=pl.BlockSpec((tm, tn), lambda m, j: (m, j)),
        compiler_params=_params(("parallel", "arbitrary")),
        name="mix_out",
    )(mixed, w, x2d)


def _router_kernel(x_ref, g_ref, wc_ref, lt_ref, sel_ref, h_ref, ids_ref, gate_ref, cnt_ref, carry_scr,
                   *, n_groups, per_group):
    @pl.when(pl.program_id(0) == 0)
    def _():
        carry_scr[...] = jnp.zeros(carry_scr.shape, F32)

    h = _rmsnorm_f32(x_ref[...], g_ref[...])
    hh = h.astype(BF16)
    _store_row_tiles(h_ref, _pack_bf16_pairs(h))
    hl = (h - hh.astype(F32)).astype(BF16)
    both = jnp.dot(hh, wc_ref[...], preferred_element_type=F32)
    logits = (both[:, :LANES] + jnp.dot(hl, wc_ref[:, :LANES], preferred_element_type=F32)) + both[:, LANES:]
    tm = logits.shape[0]
    n_exp = n_groups * per_group
    lane = lax.broadcasted_iota(jnp.int32, (tm, LANES), 1)
    lane_f = lane.astype(F32)
    neg_inf = -jnp.inf

    def first_argmax(vals, valid):
        vmax = jnp.max(jnp.where(valid, vals, neg_inf), axis=-1, keepdims=True)
        idx = jnp.min(jnp.where(valid & (vals == vmax), lane_f, float(LANES)), axis=-1, keepdims=True)
        return vmax, idx.astype(jnp.int32)

    is_group = lane < n_groups
    gmax, gsel = first_argmax(logits, is_group)
    gsum = jnp.sum(jnp.where(is_group, jnp.exp(logits - gmax), 0.0), axis=-1, keepdims=True)
    p_group = 1.0 / gsum
    lo = n_groups + gsel * per_group
    in_group = (lane >= lo) & (lane < lo + per_group)
    v1, i1 = first_argmax(logits, in_group)
    v2, i2 = first_argmax(logits, in_group & (lane != i1))
    e2w = jnp.exp(v2 - v1)
    gate1 = (1.0 / (1.0 + e2w)) * p_group
    gate2 = (e2w / (1.0 + e2w)) * p_group
    e1 = i1 - n_groups
    e2 = i2 - n_groups

    onehot = ((lane == e1) | (lane == e2)).astype(BF16)
    before = jnp.dot(lt_ref[...], onehot, preferred_element_type=F32) + carry_scr[...]
    rank1 = jnp.sum(jnp.where(lane == e1, before, 0.0), axis=-1, keepdims=True).astype(jnp.int32)
    rank2 = jnp.sum(jnp.where(lane == e2, before, 0.0), axis=-1, keepdims=True).astype(jnp.int32)
    carry_new = carry_scr[...] + jnp.sum(onehot.astype(F32), axis=0, keepdims=True)
    carry_scr[...] = carry_new

    cols = jnp.where(lane == 0, e1, jnp.where(lane == 1, e2, jnp.where(lane == 2, rank1 >> 7,
           jnp.where(lane == 3, rank1 & 127, jnp.where(lane == 4, rank2 >> 7,
           jnp.where(lane == 5, rank2 & 127, 0))))))
    ids_ref[...] = lax.dot_general(sel_ref[...], cols.astype(F32).astype(BF16), NT_DIMS,
                                   preferred_element_type=F32)
    gate_ref[...] = jnp.where(lane == 0, gate1, jnp.where(lane == 1, gate2, 0.0))
    cnt_ref[...] = jnp.broadcast_to(carry_new, cnt_ref.shape).astype(jnp.int32)
    del n_exp


def _router(x2, g, w_cat, tm, n_groups, per_group):
    t, d = x2.shape
    lt = jnp.tril(jnp.ones((tm, tm), BF16), -1)
    sel = jnp.eye(8, LANES, dtype=BF16)
    return pl.pallas_call(
        functools.partial(_router_kernel, n_groups=n_groups, per_group=per_group),
        out_shape=(
            jax.ShapeDtypeStruct((t * (d // 2 // LANES), LANES), jnp.uint32),
            jax.ShapeDtypeStruct((8, t), F32),
            jax.ShapeDtypeStruct((t, LANES), F32),
            jax.ShapeDtypeStruct((8, LANES), jnp.int32),
        ),
        grid=(t // tm,),
        in_specs=[
            pl.BlockSpec((tm, d), lambda m: (m, 0)),
            pl.BlockSpec((1, d), lambda m: (0, 0)),
            pl.BlockSpec((d, 2 * LANES), lambda m: (0, 0)),
            pl.BlockSpec((tm, tm), lambda m: (0, 0)),
            pl.BlockSpec((8, LANES), lambda m: (0, 0)),
        ],
        out_specs=(
            pl.BlockSpec((tm * (d // 2 // LANES), LANES), lambda m: (m, 0)),
            pl.BlockSpec((8, tm), lambda m: (0, m)),
            pl.BlockSpec((tm, LANES), lambda m: (m, 0)),
            pl.BlockSpec((8, LANES), lambda m: (0, 0)),
        ),
        scratch_shapes=[pltpu.VMEM((1, LANES), F32)],
        compiler_params=_params(("arbitrary",)),
        name="router",
    )(x2, g, w_cat, lt, sel)


def _plan_kernel(cnt_ref, ids_ref, dest_ref, be_ref, first_ref, nxt_ref, nu_ref, ps_scr, *, n_exp, bm, nb):
    shift = bm.bit_length() - 1

    def clear(k, c):
        first_ref[k] = 0
        nxt_ref[k] = -1
        return c

    lax.fori_loop(0, nb, clear, 0)

    def per_expert(e, carry):
        blk_start, last, prev_start = carry
        n = lax.shift_right_logical(cnt_ref[0, e] + (bm - 1), shift)
        ps_scr[e] = blk_start * bm

        def fill(k, c):
            be_ref[blk_start + k] = e
            return c

        lax.fori_loop(0, n, fill, 0)

        @pl.when(n > 0)
        def _():
            first_ref[blk_start] = 1

            @pl.when(prev_start >= 0)
            def _():
                nxt_ref[prev_start] = e

        used = n > 0
        return blk_start + n, jnp.where(used, e, last), jnp.where(used, blk_start, prev_start)

    n_used, last, _ = lax.fori_loop(0, n_exp, per_expert, (jnp.int32(0), jnp.int32(0), jnp.int32(-1)))
    nu_ref[0] = n_used

    def tail(k, c):
        be_ref[k] = last
        return c

    lax.fori_loop(n_used, nb, tail, 0)

    e1 = ids_ref[0:1, :]
    e2 = ids_ref[1:2, :]
    p1 = jnp.zeros_like(e1)
    p2 = jnp.zeros_like(e2)
    for e in range(n_exp):
        ps = ps_scr[e].astype(F32)
        p1 = jnp.where(e1 == e, ps, p1)
        p2 = jnp.where(e2 == e, ps, p2)
    dest_ref[...] = jnp.zeros(dest_ref.shape, jnp.int32)
    dest_ref[0:1, :] = (p1 + ids_ref[2:3, :] * 128.0 + ids_ref[3:4, :]).astype(jnp.int32)
    dest_ref[1:2, :] = (p2 + ids_ref[4:5, :] * 128.0 + ids_ref[5:6, :]).astype(jnp.int32)


def _plan(counts, ids_rows, n_exp, bm, nb):
    t = ids_rows.shape[1]
    return pl.pallas_call(
        functools.partial(_plan_kernel, n_exp=n_exp, bm=bm, nb=nb),
        out_shape=(
            jax.ShapeDtypeStruct((8, t), jnp.int32),
            jax.ShapeDtypeStruct((nb,), jnp.int32),
            jax.ShapeDtypeStruct((nb,), jnp.int32),
            jax.ShapeDtypeStruct((nb,), jnp.int32),
            jax.ShapeDtypeStruct((1,), jnp.int32),
        ),
        in_specs=[
            pl.BlockSpec(memory_space=pltpu.SMEM),
            pl.BlockSpec(memory_space=pltpu.VMEM),
        ],
        out_specs=(
            pl.BlockSpec(memory_space=pltpu.VMEM),
            pl.BlockSpec(memory_space=pltpu.SMEM),
            pl.BlockSpec(memory_space=pltpu.SMEM),
            pl.BlockSpec(memory_space=pltpu.SMEM),
            pl.BlockSpec(memory_space=pltpu.SMEM),
        ),
        scratch_shapes=[pltpu.SMEM((n_exp,), jnp.int32)],
        name="moe_plan",
    )(counts, ids_rows)


def _dispatch_kernel(d1_ref, d2_ref, first_ref, nu_ref, h_ref, xb_ref, zbuf, sem, zsem, *, tm, ns, bm, nb):
    base = pl.program_id(0) * tm

    @pl.when(pl.program_id(0) == 0)
    def _():
        zbuf[...] = jnp.zeros(zbuf.shape, zbuf.dtype)
        n_used = nu_ref[0]

        def has_padding(b):
            nxt_first = first_ref[jnp.minimum(b + 1, nb - 1)]
            return (b >= n_used - 1) | (nxt_first == 1)

        def zero_copy(b):
            return pltpu.make_async_copy(zbuf, xb_ref.at[pl.ds(pl.multiple_of(b * (bm * ns), bm * ns), bm * ns)], zsem)

        def zstart(b, carry):
            @pl.when(has_padding(b))
            def _():
                zero_copy(b).start()
            return carry

        def zwait(b, carry):
            @pl.when(has_padding(b))
            def _():
                zero_copy(b).wait()
            return carry

        lax.fori_loop(0, nb, zstart, 0)
        lax.fori_loop(0, nb, zwait, 0)

    def row_copy(r, dest):
        src = h_ref.at[pl.ds(pl.multiple_of(r * ns, ns), ns)]
        return pltpu.make_async_copy(src, xb_ref.at[pl.ds(pl.multiple_of(dest * ns, ns), ns)], sem)

    def start(r, carry):
        row_copy(r, d1_ref[base + r]).start()
        row_copy(r, d2_ref[base + r]).start()
        return carry

    lax.fori_loop(0, tm, start, 0, unroll=DMA_UNROLL)
    for _ in range(2):
        pltpu.make_async_copy(h_ref, xb_ref.at[pl.ds(0, tm * ns)], sem).wait()


def _dispatch(h2p, dest1, dest2, first, n_used, nb, bm, tm, ns):
    t = h2p.shape[0] // ns
    return pl.pallas_call(
        functools.partial(_dispatch_kernel, tm=tm, ns=ns, bm=bm, nb=nb),
        out_shape=jax.ShapeDtypeStruct((nb * bm * ns, LANES), h2p.dtype),
        grid_spec=pltpu.PrefetchScalarGridSpec(
            num_scalar_prefetch=4,
            grid=(t // tm,),
            in_specs=[pl.BlockSpec((tm * ns, LANES), lambda m, d1, d2, fi, nu: (m, 0))],
            out_specs=pl.BlockSpec(memory_space=pl.ANY),
            scratch_shapes=[pltpu.VMEM((bm * ns, LANES), h2p.dtype), pltpu.SemaphoreType.DMA,
                            pltpu.SemaphoreType.DMA],
        ),
        compiler_params=_params(("arbitrary",)),
        name="moe_dispatch",
    )(dest1, dest2, first, n_used, h2p)


def _experts_kernel(be_ref, first_ref, nxt_ref, nu_ref, x_ref, wg_hbm, wu_hbm, wd_hbm, y_ref,
                    wg_buf, wu_buf, wd_buf, sem, run_scr, *, ns):
    b = pl.program_id(0)

    def weight_copies(e, slot):
        return (pltpu.make_async_copy(wg_hbm.at[e], wg_buf.at[slot], sem.at[slot]),
                pltpu.make_async_copy(wu_hbm.at[e], wu_buf.at[slot], sem.at[slot]),
                pltpu.make_async_copy(wd_hbm.at[e], wd_buf.at[slot], sem.at[slot]))

    @pl.when(b == 0)
    def _():
        run_scr[0] = 0
        for cp in weight_copies(be_ref[0], 0):
            cp.start()

    @pl.when((first_ref[b] == 1) & (b < nu_ref[0]))
    def _():
        run = run_scr[0]
        slot = run & 1
        for cp in weight_copies(be_ref[b], slot):
            cp.wait()

        @pl.when(nxt_ref[b] >= 0)
        def _():
            for cp in weight_copies(nxt_ref[b], 1 - slot):
                cp.start()

        run_scr[0] = run + 1

    @pl.when(b < nu_ref[0])
    def _():
        slot = (run_scr[0] - 1) & 1
        x = _unpack_bf16_pairs(_load_row_tiles(x_ref, ns)).astype(BF16)
        g = jnp.dot(x, wg_buf[slot].astype(BF16), preferred_element_type=F32)
        u = jnp.dot(x, wu_buf[slot].astype(BF16), preferred_element_type=F32)
        hid = (g * jax.nn.sigmoid(g) * u).astype(BF16)
        y = jnp.dot(hid, wd_buf[slot].astype(BF16), preferred_element_type=F32)
        _store_row_tiles(y_ref, _pack_bf16_pairs(y))


def _experts(xb, w_gate, w_up, w_down, block_expert, first, nxt, n_used, bm, ns):
    d, de = w_gate.shape[1:]
    nb = xb.shape[0] // (bm * ns)

    def row_map(b, be, fi, nx, nu):
        return (jnp.minimum(b, nu[0] - 1), 0)

    return pl.pallas_call(
        functools.partial(_experts_kernel, ns=ns),
        out_shape=jax.ShapeDtypeStruct(xb.shape, jnp.uint32),
        grid_spec=pltpu.PrefetchScalarGridSpec(
            num_scalar_prefetch=4,
            grid=(nb,),
            in_specs=[
                pl.BlockSpec((bm * ns, LANES), row_map),
                pl.BlockSpec(memory_space=pl.ANY),
                pl.BlockSpec(memory_space=pl.ANY),
                pl.BlockSpec(memory_space=pl.ANY),
            ],
            out_specs=pl.BlockSpec((bm * ns, LANES), row_map),
            scratch_shapes=[
                pltpu.VMEM((2, d, de), F32),
                pltpu.VMEM((2, d, de), F32),
                pltpu.VMEM((2, de, d), F32),
                pltpu.SemaphoreType.DMA((2,)),
                pltpu.SMEM((1,), jnp.int32),
            ],
        ),
        input_output_aliases={4: 0},
        compiler_params=_params(("arbitrary",)),
        name="moe_experts",
    )(block_expert, first, nxt, n_used, xb, w_gate, w_up, w_down)


def _ple_kernel(d1_ref, d2_ref, x_ref, mg_ref, yb_ref, g_ref, p_ref, wg_ref, wp_ref, gf_ref, o_ref,
                ybuf, sem, x3_scr, h_scr, p_scr, *, tm, tn, ns, nj):
    m = pl.program_id(0)
    j = pl.program_id(1)
    n_m = pl.num_programs(0)
    n_j = pl.num_programs(1)
    slot = m & 1

    def row_copies(blk, r, s):
        tok = blk * tm + r
        src1 = yb_ref.at[pl.ds(pl.multiple_of(d1_ref[tok] * ns, ns), ns)]
        src2 = yb_ref.at[pl.ds(pl.multiple_of(d2_ref[tok] * ns, ns), ns)]
        rows = pl.ds(pl.multiple_of(r * ns, ns), ns)
        return (pltpu.make_async_copy(src1, ybuf.at[s, 0, rows], sem.at[s]),
                pltpu.make_async_copy(src2, ybuf.at[s, 1, rows], sem.at[s]))

    def drain(s):
        for k in range(2):
            pltpu.make_async_copy(yb_ref.at[pl.ds(0, tm * ns)], ybuf.at[s, k], sem.at[s]).wait()

    @pl.when(j == 0)
    def _():
        @pl.when(m == 0)
        def _():
            def start(r, carry):
                for cp in row_copies(0, r, 0):
                    cp.start()
                return carry

            lax.fori_loop(0, tm, start, 0, unroll=DMA_UNROLL)

        drain(slot)
        mg = mg_ref[...]
        y1 = _unpack_bf16_pairs(_load_row_tiles(ybuf.at[slot, 0], ns))
        y2 = _unpack_bf16_pairs(_load_row_tiles(ybuf.at[slot, 1], ns))
        x3 = x_ref[...] + mg[:, 0:1] * y1 + mg[:, 1:2] * y2
        x3_scr[...] = x3
        h_scr[...] = _rmsnorm_f32(x3, g_ref[...]).astype(BF16)
        p_scr[...] = p_ref[...].astype(BF16)

    nxt = jnp.minimum(m + 1, n_m - 1)
    rows_per_step = tm // nj
    for r in range(rows_per_step):
        for cp in row_copies(nxt, j * rows_per_step + r, 1 - slot):
            cp.start()

    col = pl.multiple_of(j * tn, tn)
    gate = jnp.dot(h_scr[...], wg_ref[:, pl.ds(col, tn)].astype(BF16), preferred_element_type=F32)
    ple = jnp.dot(p_scr[...], wp_ref[:, pl.ds(col, tn)].astype(BF16), preferred_element_type=F32)
    o_ref[:, pl.ds(col, tn)] = x3_scr[:, pl.ds(col, tn)] + jax.nn.sigmoid(gate) * ple

    @pl.when(j == n_j - 1)
    def _():
        o_ref[...] = _rmsnorm_f32(o_ref[...], gf_ref[...])

        @pl.when(m == n_m - 1)
        def _():
            drain(1 - slot)


def _combine_ple_final(x2, moe_gates, yb, dest1, dest2, g_ple, p2d, w_gate, w_proj, g_final, tm, tn, ns):
    t, d = x2.shape
    dp = p2d.shape[1]
    return pl.pallas_call(
        functools.partial(_ple_kernel, tm=tm, tn=tn, ns=ns, nj=d // tn),
        out_shape=jax.ShapeDtypeStruct((t, d), F32),
        grid_spec=pltpu.PrefetchScalarGridSpec(
            num_scalar_prefetch=2,
            grid=(t // tm, d // tn),
            in_specs=[
                pl.BlockSpec((tm, d), lambda m, j, d1, d2: (m, 0)),
                pl.BlockSpec((tm, LANES), lambda m, j, d1, d2: (m, 0)),
                pl.BlockSpec(memory_space=pl.ANY),
                pl.BlockSpec((1, d), lambda m, j, d1, d2: (0, 0)),
                pl.BlockSpec((tm, dp), lambda m, j, d1, d2: (m, 0)),
                pl.BlockSpec((d, d), lambda m, j, d1, d2: (0, 0), **_RESIDENT),
                pl.BlockSpec((dp, d), lambda m, j, d1, d2: (0, 0), **_RESIDENT),
                pl.BlockSpec((1, d), lambda m, j, d1, d2: (0, 0)),
            ],
            out_specs=pl.BlockSpec((tm, d), lambda m, j, d1, d2: (m, 0)),
            scratch_shapes=[
                pltpu.VMEM((2, 2, tm * ns, LANES), yb.dtype),
                pltpu.SemaphoreType.DMA((2,)),
                pltpu.VMEM((tm, d), F32),
                pltpu.VMEM((tm, d), BF16),
                pltpu.VMEM((tm, dp), BF16),
            ],
        ),
        compiler_params=_params(("arbitrary", "arbitrary")),
        name="combine_ple_final",
    )(dest1, dest2, x2, moe_gates, yb, g_ple, p2d, w_gate, w_proj, g_final)


def _tile(n, pref):
    return pref if n % pref == 0 else n


def _layer(x2d, p2d, batch, seq, w_in, b_forget, w_branch_fox, w_branch_sb, w_mix_out, g_mix, g_ffn,
           w_group, w_expert, w_gate, w_up, w_down, g_ple, w_ple_proj, w_ple_gate, g_final):
    t, d = x2d.shape
    n_heads = b_forget.shape[0]
    w_att = n_heads * HEAD_DIM
    n_exp = w_expert.shape[1]

    qs = LOG2E * HEAD_DIM ** -0.5
    w_t, wf_t = _wprep(w_in.T, w_att, n_heads, qs, 512)
    b_row = jnp.pad(b_forget, (0, LANES - n_heads)).reshape(1, LANES)
    col_qa, col_ka, col_qb, col_kb = 0, w_att, 2 * w_att, 3 * w_att
    col_ga, col_gb = 4 * w_att, 4 * w_att + d
    row_va, row_vb = 0, w_att

    tm = _tile(t, 1024)
    tn = _tile(d, 1024)
    proj, vt, f_tok = _inproj(x2d, g_mix.reshape(1, d), w_t, wf_t, 2 * w_att, tm, _tile(d // 2, 1024))
    c = _forget_cumsum(f_tok, b_row, batch, n_heads)

    tq = _tile(seq, 512)
    hp = 4
    o_a = _fox_attention(proj, vt, c, batch, seq, n_heads, col_qa, col_ka, row_va, tq, hp)
    o_b = _sb_attention(proj, vt, batch, seq, n_heads, col_qb, col_kb, row_vb, tq, _tile(seq, 256), hp)

    mixed = _merge(o_a, o_b, w_branch_fox, w_branch_sb, proj, col_ga, col_gb, tm, tn)
    x2 = _mixout(mixed, w_mix_out, x2d, tm, tn)

    w_r = jnp.concatenate([w_group, w_expert], axis=1)
    w_r = jnp.pad(w_r, ((0, 0), (0, LANES - w_r.shape[1])))
    w_r_hi = w_r.astype(BF16)
    w_r_lo = (w_r - w_r_hi.astype(F32)).astype(BF16)
    tr = _tile(t, 512)
    w_r_cat = jnp.concatenate([w_r_hi, w_r_lo], axis=1)
    h2, ids_rows, gates, counts = _router(x2, g_ffn.reshape(1, d), w_r_cat, tr, N_GROUPS, EXPERTS_PER_GROUP)

    bm = 256
    n_assign = 2 * t
    nb = n_assign // bm + n_exp
    dest, block_expert, first, nxt, n_used = _plan(counts, ids_rows, n_exp, bm, nb)
    dest1, dest2 = dest[0], dest[1]

    td = _tile(t, 1024)
    ns = d // 2 // LANES
    xb = _dispatch(h2, dest1, dest2, first, n_used, nb, bm, td, ns)
    yb = _experts(xb, w_gate, w_up, w_down, block_expert, first, nxt, n_used, bm, ns)

    tp = _tile(t, 512)
    return _combine_ple_final(x2, gates, yb, dest1, dest2, g_ple.reshape(1, d), p2d, w_ple_gate, w_ple_proj,
                              g_final.reshape(1, d), tp, tn, ns)


def kernel(x, p, w_in, b_forget, w_branch_fox, w_branch_sb, w_mix_out, g_mix, g_ffn, w_group, w_expert,
           w_gate, w_up, w_down, g_ple, w_ple_proj, w_ple_gate, g_final):
    b, s, d = x.shape
    depth = w_in.shape[0]
    assert depth == 1, "the final norm is fused into the single layer"
    x2d = x.reshape(b * s, d)
    out = _layer(x2d, p[0].reshape(b * s, -1), b, s, w_in[0], b_forget[0], w_branch_fox[0], w_branch_sb[0],
                 w_mix_out[0], g_mix[0], g_ffn[0], w_group[0], w_expert[0], w_gate[0], w_up[0], w_down[0],
                 g_ple[0], w_ple_proj[0], w_ple_gate[0], g_final)
    return out.reshape(b, s, d)
```

```python
import functools

import jax
import jax.numpy as jnp
from jax import lax
from jax.experimental import pallas as pl
from jax.experimental.pallas import tpu as pltpu

F32 = jnp.float32
BF16 = jnp.bfloat16

HEAD_DIM = 128
N_GROUPS = 4
EXPERTS_PER_GROUP = 8
EPS = 1e-6
LANES = 128
SUBLANES = 8
VMEM_LIMIT = 56 * 1024 * 1024

LOG2E = 1.4426950408889634
DMA_UNROLL = 8
R_DONE = 160.0

NT_DIMS = (((1,), (1,)), ((), ()))


def _params(sem):
    return pltpu.CompilerParams(dimension_semantics=sem, vmem_limit_bytes=VMEM_LIMIT)


def _rmsnorm_f32(x, g):
    ms = jnp.mean(x * x, axis=-1, keepdims=True)
    return x * lax.rsqrt(ms + EPS) * g


def _pack_bf16_pairs(x):
    n = x.shape[1] // 2
    bits = lax.bitcast_convert_type(x.astype(BF16).astype(F32), jnp.uint32)
    return (bits[:, :n] >> 16) | (bits[:, n:] & jnp.uint32(0xFFFF0000))


def _unpack_bf16_pairs(w):
    lo = lax.bitcast_convert_type(w << 16, F32)
    hi = lax.bitcast_convert_type(w & jnp.uint32(0xFFFF0000), F32)
    return jnp.concatenate([lo, hi], axis=1)


def _store_row_tiles(ref, words):
    n = words.shape[0]
    ns = words.shape[1] // LANES
    for c in range(ns):
        ref[pl.ds(c, n, stride=ns), :] = words[:, c * LANES:(c + 1) * LANES]


def _load_row_tiles(ref, ns):
    n = ref.shape[0] // ns
    return jnp.concatenate([ref[pl.ds(c, n, stride=ns), :] for c in range(ns)], axis=1)


def _wprep_kernel(off_ref, scale_ref, w_ref, f_ref, wt_ref, wf_ref, *, n_heads):
    i = pl.program_id(0)
    wt_ref[...] = (w_ref[...] * scale_ref[i]).astype(BF16)

    @pl.when(i == 0)
    def _():
        row = lax.broadcasted_iota(jnp.int32, wf_ref.shape, 0)
        f_rows = jnp.concatenate([f_ref[...]] * (wf_ref.shape[0] // f_ref.shape[0]), axis=0)
        wf_ref[...] = jnp.where(row < n_heads, f_rows, 0.0).astype(BF16)


def _wprep(w_t, w_att, n_heads, qs, rb):
    n_in, d = w_t.shape
    off_b = 3 * w_att + n_heads
    segments = [(2 * w_att, w_att, 1.0), (off_b + 2 * w_att, w_att, 1.0),
                (0, w_att, qs), (w_att, w_att, 1.0),
                (off_b, w_att, qs), (off_b + w_att, w_att, 1.0),
                (off_b + 3 * w_att, n_in - off_b - 3 * w_att, 1.0)]
    offs, scales = [], []
    for start, length, scale in segments:
        assert length % rb == 0 and start % SUBLANES == 0
        for k in range(length // rb):
            offs.append((start + k * rb) // SUBLANES)
            scales.append(scale)
    n_out = len(offs) * rb
    f_blk = SUBLANES
    assert n_heads <= f_blk and (3 * w_att) % f_blk == 0
    return pl.pallas_call(
        functools.partial(_wprep_kernel, n_heads=n_heads),
        out_shape=(jax.ShapeDtypeStruct((n_out, d), BF16), jax.ShapeDtypeStruct((LANES, d), BF16)),
        grid_spec=pltpu.PrefetchScalarGridSpec(
            num_scalar_prefetch=2,
            grid=(len(offs),),
            in_specs=[
                pl.BlockSpec((pl.Element(rb), pl.Element(d)), lambda i, off, sc: (off[i] * SUBLANES, 0)),
                pl.BlockSpec((pl.Element(f_blk), pl.Element(d)), lambda i, off, sc: (3 * w_att, 0)),
            ],
            out_specs=(
                pl.BlockSpec((rb, d), lambda i, off, sc: (i, 0)),
                pl.BlockSpec((LANES, d), lambda i, off, sc: (0, 0)),
            ),
        ),
        compiler_params=_params(("arbitrary",)),
        name="inproj_weights",
    )(jnp.asarray(offs, jnp.int32), jnp.asarray(scales, F32), w_t, w_t)


def _inproj_kernel(x_ref, g_ref, w_ref, wf_ref, o_ref, vt_ref, f_ref, h_scr, *, nvt):
    j = pl.program_id(1)

    @pl.when(j == 0)
    def _():
        hb = _rmsnorm_f32(x_ref[...], g_ref[...]).astype(BF16)
        h_scr[...] = hb
        f_ref[...] = lax.dot_general(hb, wf_ref[...], NT_DIMS, preferred_element_type=F32)

    @pl.when(j < nvt)
    def _():
        vt_ref[...] = lax.dot_general(w_ref[...], h_scr[...], NT_DIMS,
                                      preferred_element_type=F32).astype(vt_ref.dtype)

    @pl.when(j >= nvt)
    def _():
        o_ref[...] = lax.dot_general(h_scr[...], w_ref[...], NT_DIMS,
                                     preferred_element_type=F32).astype(o_ref.dtype)


def _inproj(x2d, g, w_t, wf_t, n_v, tm, tn):
    t, d = x2d.shape
    n = w_t.shape[0] - n_v
    nvt = n_v // tn
    return pl.pallas_call(
        functools.partial(_inproj_kernel, nvt=nvt),
        out_shape=(jax.ShapeDtypeStruct((t, n), BF16), jax.ShapeDtypeStruct((n_v, t), BF16),
                   jax.ShapeDtypeStruct((t, LANES), F32)),
        grid=(t // tm, nvt + n // tn),
        in_specs=[
            pl.BlockSpec((tm, d), lambda m, j: (m, 0)),
            pl.BlockSpec((1, d), lambda m, j: (0, 0)),
            pl.BlockSpec((tn, d), lambda m, j: (j, 0)),
            pl.BlockSpec((LANES, d), lambda m, j: (0, 0)),
        ],
        out_specs=(
            pl.BlockSpec((tm, tn), lambda m, j: (m, jnp.maximum(j - nvt, 0))),
            pl.BlockSpec((tn, tm), lambda m, j: (jnp.minimum(j, nvt - 1), m)),
            pl.BlockSpec((tm, LANES), lambda m, j: (m, 0)),
        ),
        scratch_shapes=[pltpu.VMEM((tm, d), BF16)],
        compiler_params=_params(("parallel", "arbitrary")),
        name="inproj",
    )(x2d, g, w_t, wf_t)


def _split3(x):
    p1 = x.astype(BF16)
    r1 = x - p1.astype(F32)
    p2 = r1.astype(BF16)
    p3 = (r1 - p2.astype(F32)).astype(BF16)
    return p1, p2, p3


def _cumsum_kernel(f_ref, b_ref, lt_ref, c_ref, carry_scr, *, chunk, n_heads):
    @pl.when(pl.program_id(1) == 0)
    def _():
        carry_scr[...] = jnp.zeros(carry_scr.shape, F32)

    rows = f_ref.shape[0]
    lt = lt_ref[...]
    carry = carry_scr[...]
    for i in range(rows // chunk):
        lf = jax.nn.log_sigmoid(f_ref[i * chunk:(i + 1) * chunk, :] + b_ref[...])
        p1, p2, p3 = _split3(lf)
        cs = (jnp.dot(lt, p1, preferred_element_type=F32)
              + jnp.dot(lt, p2, preferred_element_type=F32)
              + jnp.dot(lt, p3, preferred_element_type=F32)) + carry
        for h in range(n_heads):
            c_ref[h, i * chunk:(i + 1) * chunk, :] = jnp.broadcast_to(cs[:, h:h + 1] * LOG2E, (chunk, LANES))
        carry = cs[chunk - 1:chunk, :]
    carry_scr[...] = carry


def _forget_cumsum(f_tok, b_row, batch, n_heads):
    t = f_tok.shape[0]
    s = t // batch
    rows = min(1024, s)
    chunk = min(256, s)
    lt = jnp.tril(jnp.ones((chunk, chunk), BF16))
    nr = s // rows
    return pl.pallas_call(
        functools.partial(_cumsum_kernel, chunk=chunk, n_heads=n_heads),
        out_shape=jax.ShapeDtypeStruct((n_heads, t, LANES), F32),
        grid=(batch, nr),
        in_specs=[
            pl.BlockSpec((rows, LANES), lambda b, r: (b * nr + r, 0)),
            pl.BlockSpec((1, LANES), lambda b, r: (0, 0)),
            pl.BlockSpec((chunk, chunk), lambda b, r: (0, 0)),
        ],
        out_specs=pl.BlockSpec((n_heads, rows, LANES), lambda b, r: (0, b * nr + r, 0)),
        scratch_shapes=[pltpu.VMEM((1, LANES), F32)],
        compiler_params=_params(("parallel", "arbitrary")),
        name="forget_cumsum",
    )(f_tok, b_row, lt)


def _fox_kernel(q_ref, k_ref, vt_ref, c_ref, o_ref, m_scr, l_scr, acc_scr, *, tq, tk, hp):
    i = pl.program_id(2)
    m_scr[...] = jnp.full(m_scr.shape, -jnp.inf, F32)
    l_scr[...] = jnp.zeros(l_scr.shape, F32)
    acc_scr[...] = jnp.zeros(acc_scr.shape, F32)

    def tile(j, masked, koff=0, nk=tk, qlo=0):
        start = pl.multiple_of(j * tk + koff, nk)
        nq = tq - qlo
        if masked:
            key = j * tk + koff + lax.broadcasted_iota(jnp.int32, (nk, nq), 0)
            qry = i * tq + qlo + lax.broadcasted_iota(jnp.int32, (nk, nq), 1)
            keep = key <= qry

        def scores(hh):
            lanes = slice(hh * HEAD_DIM, (hh + 1) * HEAD_DIM)
            kj = k_ref[pl.ds(start, nk), lanes]
            return lax.dot_general(kj, q_ref[qlo:tq, lanes], NT_DIMS, preferred_element_type=F32)

        s_all = [scores(hh) for hh in range(hp)]
        for hh in range(hp):
            s = s_all[hh]
            cj = c_ref[hh, pl.ds(start, nk), :]
            s = s - jnp.concatenate([cj] * (nq // LANES), axis=1)
            if masked:
                s = jnp.where(keep, s, -jnp.inf)
            m_prev = m_scr[hh, :, qlo:tq]
            m_new = jnp.maximum(m_prev, jnp.max(s, axis=0, keepdims=True))
            alpha = jnp.exp2(m_prev - m_new)
            p = jnp.exp2(s - m_new)
            l_scr[hh, :, qlo:tq] = alpha * l_scr[hh, :, qlo:tq] + jnp.sum(p, axis=0, keepdims=True)
            vtj = vt_ref[hh * HEAD_DIM:(hh + 1) * HEAD_DIM, pl.ds(start, nk)]
            acc_scr[hh, :, qlo:tq] = (alpha * acc_scr[hh, :, qlo:tq]
                                      + jnp.dot(vtj, p.astype(BF16), preferred_element_type=F32))
            m_scr[hh, :, qlo:tq] = m_new

    def body(j, carry):
        tile(j, False)
        return carry

    lax.fori_loop(0, i, body, 0)
    half = tk // 2
    tile(i, True, koff=0, nk=half, qlo=0)
    tile(i, True, koff=half, nk=half, qlo=tq - half)
    for hh in range(hp):
        o = acc_scr[hh] / l_scr[hh]
        o_ref[:, hh * HEAD_DIM:(hh + 1) * HEAD_DIM] = o.T.astype(o_ref.dtype)


def _fox_attention(proj, vt, c, batch, seq, n_heads, col_q, col_k, row_v, tq, hp):
    t = proj.shape[0]
    nq = seq // tq
    wb = hp * HEAD_DIM
    cq, ck, rv = col_q // wb, col_k // wb, row_v // wb
    return pl.pallas_call(
        functools.partial(_fox_kernel, tq=tq, tk=tq, hp=hp),
        out_shape=jax.ShapeDtypeStruct((t, n_heads * HEAD_DIM), BF16),
        grid=(batch, n_heads // hp, nq),
        in_specs=[
            pl.BlockSpec((tq, wb), lambda b, g, i: (b * nq + i, cq + g)),
            pl.BlockSpec((seq, wb), lambda b, g, i: (b, ck + g)),
            pl.BlockSpec((wb, seq), lambda b, g, i: (rv + g, b)),
            pl.BlockSpec((hp, seq, LANES), lambda b, g, i: (g, b, 0)),
        ],
        out_specs=pl.BlockSpec((tq, wb), lambda b, g, i: (b * nq + i, g)),
        scratch_shapes=[
            pltpu.VMEM((hp, 1, tq), F32),
            pltpu.VMEM((hp, 1, tq), F32),
            pltpu.VMEM((hp, HEAD_DIM, tq), F32),
        ],
        compiler_params=_params(("parallel", "parallel", "arbitrary")),
        name="fox_attention",
    )(proj, proj, vt, c)


def _sb_kernel(q_ref, k_ref, vt_ref, tri_ref, o_ref, r_scr, acc_scr, *, tq, tk, hp):
    i = pl.program_id(2)
    r_scr[...] = jnp.zeros(r_scr.shape, F32)
    acc_scr[...] = jnp.zeros(acc_scr.shape, F32)

    sign_bit = jnp.uint32(0x80000000)

    def tile(j, masked, qlo=0, qhi=tq):
        start = pl.multiple_of(j * tk, tk)
        nq = qhi - qlo
        if masked:
            key = j * tk + lax.broadcasted_iota(jnp.int32, (tk, nq), 0)
            qry = i * tq + qlo + lax.broadcasted_iota(jnp.int32, (tk, nq), 1)
            mask = key < qry

        def scores(hh):
            lanes = slice(hh * HEAD_DIM, (hh + 1) * HEAD_DIM)
            kj = k_ref[pl.ds(start, tk), lanes]
            return lax.dot_general(kj, q_ref[qlo:qhi, lanes], NT_DIMS, preferred_element_type=F32)

        u_all = [scores(hh) for hh in range(hp)]
        for hh in range(hp):
            u = u_all[hh]
            minus_abs = lax.bitcast_convert_type(lax.bitcast_convert_type(u, jnp.uint32) | sign_bit, F32)
            sp = jnp.maximum(u, 0.0) + jnp.log2(1.0 + jnp.exp2(minus_abs))
            if masked:
                sp = jnp.where(mask, sp, 0.0)
            w = jnp.dot(tri_ref[...], sp.astype(BF16), preferred_element_type=F32)
            a = jnp.exp2(u - ((sp + w) + r_scr[hh, :, qlo:qhi]))
            if masked:
                a = jnp.where(mask, a, 0.0)
            vtj = vt_ref[hh * HEAD_DIM:(hh + 1) * HEAD_DIM, pl.ds(start, tk)]
            acc_scr[hh, :, qlo:qhi] += jnp.dot(vtj, a.astype(BF16), preferred_element_type=F32)
            r_scr[hh, :, qlo:qhi] += jnp.sum(sp, axis=0, keepdims=True)

    ratio = max(tq // tk, 1)
    n_full = (i * tq) // tk
    for d in reversed(range(ratio)):
        tile(n_full + d, True, qlo=d * tk if tq > tk else 0)

    def live():
        return (jnp.min(r_scr[...]) <= R_DONE).astype(jnp.int32)

    def cond(carry):
        it, alive = carry
        return (it < n_full) & (alive > 0)

    def body(carry):
        it, _ = carry
        tile(n_full - 1 - it, False)
        return it + 1, live()

    lax.while_loop(cond, body, (jnp.int32(0), live()))
    for hh in range(hp):
        o_ref[:, hh * HEAD_DIM:(hh + 1) * HEAD_DIM] = acc_scr[hh].T.astype(o_ref.dtype)


def _sb_attention(proj, vt, batch, seq, n_heads, col_q, col_k, row_v, tq, tk, hp):
    t = proj.shape[0]
    nq = seq // tq
    wb = hp * HEAD_DIM
    cq, ck, rv = col_q // wb, col_k // wb, row_v // wb
    tri = jnp.triu(jnp.ones((tk, tk), BF16), 1)
    return pl.pallas_call(
        functools.partial(_sb_kernel, tq=tq, tk=tk, hp=hp),
        out_shape=jax.ShapeDtypeStruct((t, n_heads * HEAD_DIM), BF16),
        grid=(batch, n_heads // hp, nq),
        in_specs=[
            pl.BlockSpec((tq, wb), lambda b, g, i: (b * nq + i, cq + g)),
            pl.BlockSpec((seq, wb), lambda b, g, i: (b, ck + g)),
            pl.BlockSpec((wb, seq), lambda b, g, i: (rv + g, b)),
            pl.BlockSpec((tk, tk), lambda b, g, i: (0, 0)),
        ],
        out_specs=pl.BlockSpec((tq, wb), lambda b, g, i: (b * nq + i, g)),
        scratch_shapes=[
            pltpu.VMEM((hp, 1, tq), F32),
            pltpu.VMEM((hp, HEAD_DIM, tq), F32),
        ],
        compiler_params=_params(("parallel", "parallel", "arbitrary")),
        name="sb_attention",
    )(proj, proj, vt, tri)


_RESIDENT = dict(pipeline_mode=pl.Buffered(1))


def _merge_kernel(oa_ref, ob_ref, wa_ref, wb_ref, ga_ref, gb_ref, o_ref, *, tn):
    col = pl.multiple_of(pl.program_id(1) * tn, tn)
    ya = jnp.dot(oa_ref[...], wa_ref[:, pl.ds(col, tn)].astype(BF16), preferred_element_type=F32)
    yb = jnp.dot(ob_ref[...], wb_ref[:, pl.ds(col, tn)].astype(BF16), preferred_element_type=F32)
    ga = jax.nn.sigmoid(ga_ref[...].astype(F32))
    gb = jax.nn.sigmoid(gb_ref[...].astype(F32))
    o_ref[...] = (ga * ya + gb * yb).astype(o_ref.dtype)


def _merge(o_a, o_b, wa, wb, proj, col_ga, col_gb, tm, tn):
    t, ka = o_a.shape
    kb = o_b.shape[1]
    d = wa.shape[1]
    ca, cb = col_ga // tn, col_gb // tn
    return pl.pallas_call(
        functools.partial(_merge_kernel, tn=tn),
        out_shape=jax.ShapeDtypeStruct((t, d), BF16),
        grid=(t // tm, d // tn),
        in_specs=[
            pl.BlockSpec((tm, ka), lambda m, j: (m, 0)),
            pl.BlockSpec((tm, kb), lambda m, j: (m, 0)),
            pl.BlockSpec((ka, d), lambda m, j: (0, 0), **_RESIDENT),
            pl.BlockSpec((kb, d), lambda m, j: (0, 0), **_RESIDENT),
            pl.BlockSpec((tm, tn), lambda m, j: (m, ca + j)),
            pl.BlockSpec((tm, tn), lambda m, j: (m, cb + j)),
        ],
        out_specs=pl.BlockSpec((tm, tn), lambda m, j: (m, j)),
        compiler_params=_params(("parallel", "arbitrary")),
        name="branch_merge",
    )(o_a, o_b, wa, wb, proj, proj)


def _mixout_kernel(a_ref, w_ref, x_ref, o_ref, *, tn):
    col = pl.multiple_of(pl.program_id(1) * tn, tn)
    w = w_ref[:, pl.ds(col, tn)].astype(BF16)
    o_ref[...] = x_ref[...] + jnp.dot(a_ref[...], w, preferred_element_type=F32)


def _mixout(mixed, w, x2d, tm, tn):
    t, k = mixed.shape
    d = w.shape[1]
    return pl.pallas_call(
        functools.partial(_mixout_kernel, tn=tn),
        out_shape=jax.ShapeDtypeStruct((t, d), F32),
        grid=(t // tm, d // tn),
        in_specs=[
            pl.BlockSpec((tm, k), lambda m, j: (m, 0)),
            pl.BlockSpec((k, d), lambda m, j: (0, 0), **_RESIDENT),
            pl.BlockSpec((tm, tn), lambda m, j: (m, j)),
        ],
        out_specs=pl.BlockSpec((tm, tn), lambda m, j: (m, j)),
        compiler_params=_params(("parallel", "arbitrary")),
        name="mix_out",
    )(mixed, w, x2d)


def _router_kernel(x_ref, g_ref, wc_ref, lt_ref, sel_ref, h_ref, ids_ref, gate_ref, cnt_ref, carry_scr,
                   *, n_groups, per_group):
    @pl.when(pl.program_id(0) == 0)
    def _():
        carry_scr[...] = jnp.zeros(carry_scr.shape, F32)

    h = _rmsnorm_f32(x_ref[...], g_ref[...])
    hh = h.astype(BF16)
    _store_row_tiles(h_ref, _pack_bf16_pairs(h))
    hl = (h - hh.astype(F32)).astype(BF16)
    both = jnp.dot(hh, wc_ref[...], preferred_element_type=F32)
    logits = (both[:, :LANES] + jnp.dot(hl, wc_ref[:, :LANES], preferred_element_type=F32)) + both[:, LANES:]
    tm = logits.shape[0]
    n_exp = n_groups * per_group
    lane = lax.broadcasted_iota(jnp.int32, (tm, LANES), 1)
    lane_f = lane.astype(F32)
    neg_inf = -jnp.inf

    def first_argmax(vals, valid):
        vmax = jnp.max(jnp.where(valid, vals, neg_inf), axis=-1, keepdims=True)
        idx = jnp.min(jnp.where(valid & (vals == vmax), lane_f, float(LANES)), axis=-1, keepdims=True)
        return vmax, idx.astype(jnp.int32)

    is_group = lane < n_groups
    gmax, gsel = first_argmax(logits, is_group)
    gsum = jnp.sum(jnp.where(is_group, jnp.exp(logits - gmax), 0.0), axis=-1, keepdims=True)
    p_group = 1.0 / gsum
    lo = n_groups + gsel * per_group
    in_group = (lane >= lo) & (lane < lo + per_group)
    v1, i1 = first_argmax(logits, in_group)
    v2, i2 = first_argmax(logits, in_group & (lane != i1))
    e2w = jnp.exp(v2 - v1)
    gate1 = (1.0 / (1.0 + e2w)) * p_group
    gate2 = (e2w / (1.0 + e2w)) * p_group
    e1 = i1 - n_groups
    e2 = i2 - n_groups

    onehot = ((lane == e1) | (lane == e2)).astype(BF16)
    before = jnp.dot(lt_ref[...], onehot, preferred_element_type=F32) + carry_scr[...]
    rank1 = jnp.sum(jnp.where(lane == e1, before, 0.0), axis=-1, keepdims=True).astype(jnp.int32)
    rank2 = jnp.sum(jnp.where(lane == e2, before, 0.0), axis=-1, keepdims=True).astype(jnp.int32)
    carry_new = carry_scr[...] + jnp.sum(onehot.astype(F32), axis=0, keepdims=True)
    carry_scr[...] = carry_new

    cols = jnp.where(lane == 0, e1, jnp.where(lane == 1, e2, jnp.where(lane == 2, rank1 >> 7,
           jnp.where(lane == 3, rank1 & 127, jnp.where(lane == 4, rank2 >> 7,
           jnp.where(lane == 5, rank2 & 127, 0))))))
    ids_ref[...] = lax.dot_general(sel_ref[...], cols.astype(F32).astype(BF16), NT_DIMS,
                                   preferred_element_type=F32)
    gate_ref[...] = jnp.where(lane == 0, gate1, jnp.where(lane == 1, gate2, 0.0))
    cnt_ref[...] = jnp.broadcast_to(carry_new, cnt_ref.shape).astype(jnp.int32)
    del n_exp


def _router(x2, g, w_cat, tm, n_groups, per_group):
    t, d = x2.shape
    lt = jnp.tril(jnp.ones((tm, tm), BF16), -1)
    sel = jnp.eye(8, LANES, dtype=BF16)
    return pl.pallas_call(
        functools.partial(_router_kernel, n_groups=n_groups, per_group=per_group),
        out_shape=(
            jax.ShapeDtypeStruct((t * (d // 2 // LANES), LANES), jnp.uint32),
            jax.ShapeDtypeStruct((8, t), F32),
            jax.ShapeDtypeStruct((t, LANES), F32),
            jax.ShapeDtypeStruct((8, LANES), jnp.int32),
        ),
        grid=(t // tm,),
        in_specs=[
            pl.BlockSpec((tm, d), lambda m: (m, 0)),
            pl.BlockSpec((1, d), lambda m: (0, 0)),
            pl.BlockSpec((d, 2 * LANES), lambda m: (0, 0)),
            pl.BlockSpec((tm, tm), lambda m: (0, 0)),
            pl.BlockSpec((8, LANES), lambda m: (0, 0)),
        ],
        out_specs=(
            pl.BlockSpec((tm * (d // 2 // LANES), LANES), lambda m: (m, 0)),
            pl.BlockSpec((8, tm), lambda m: (0, m)),
            pl.BlockSpec((tm, LANES), lambda m: (m, 0)),
            pl.BlockSpec((8, LANES), lambda m: (0, 0)),
        ),
        scratch_shapes=[pltpu.VMEM((1, LANES), F32)],
        compiler_params=_params(("arbitrary",)),
        name="router",
    )(x2, g, w_cat, lt, sel)


def _plan_kernel(cnt_ref, ids_ref, dest_ref, be_ref, first_ref, nxt_ref, nu_ref, ps_scr, *, n_exp, bm, nb):
    shift = bm.bit_length() - 1

    def clear(k, c):
        first_ref[k] = 0
        nxt_ref[k] = -1
        return c

    lax.fori_loop(0, nb, clear, 0)

    def per_expert(e, carry):
        blk_start, last, prev_start = carry
        n = lax.shift_right_logical(cnt_ref[0, e] + (bm - 1), shift)
        ps_scr[e] = blk_start * bm

        def fill(k, c):
            be_ref[blk_start + k] = e
            return c

        lax.fori_loop(0, n, fill, 0)

        @pl.when(n > 0)
        def _():
            first_ref[blk_start] = 1

            @pl.when(prev_start >= 0)
            def _():
                nxt_ref[prev_start] = e

        used = n > 0
        return blk_start + n, jnp.where(used, e, last), jnp.where(used, blk_start, prev_start)

    n_used, last, _ = lax.fori_loop(0, n_exp, per_expert, (jnp.int32(0), jnp.int32(0), jnp.int32(-1)))
    nu_ref[0] = n_used

    def tail(k, c):
        be_ref[k] = last
        return c

    lax.fori_loop(n_used, nb, tail, 0)

    e1 = ids_ref[0:1, :]
    e2 = ids_ref[1:2, :]
    p1 = jnp.zeros_like(e1)
    p2 = jnp.zeros_like(e2)
    for e in range(n_exp):
        ps = ps_scr[e].astype(F32)
        p1 = jnp.where(e1 == e, ps, p1)
        p2 = jnp.where(e2 == e, ps, p2)
    dest_ref[...] = jnp.zeros(dest_ref.shape, jnp.int32)
    dest_ref[0:1, :] = (p1 + ids_ref[2:3, :] * 128.0 + ids_ref[3:4, :]).astype(jnp.int32)
    dest_ref[1:2, :] = (p2 + ids_ref[4:5, :] * 128.0 + ids_ref[5:6, :]).astype(jnp.int32)


def _plan(counts, ids_rows, n_exp, bm, nb):
    t = ids_rows.shape[1]
    return pl.pallas_call(
        functools.partial(_plan_kernel, n_exp=n_exp, bm=bm, nb=nb),
        out_shape=(
            jax.ShapeDtypeStruct((8, t), jnp.int32),
            jax.ShapeDtypeStruct((nb,), jnp.int32),
            jax.ShapeDtypeStruct((nb,), jnp.int32),
            jax.ShapeDtypeStruct((nb,), jnp.int32),
            jax.ShapeDtypeStruct((1,), jnp.int32),
        ),
        in_specs=[
            pl.BlockSpec(memory_space=pltpu.SMEM),
            pl.BlockSpec(memory_space=pltpu.VMEM),
        ],
        out_specs=(
            pl.BlockSpec(memory_space=pltpu.VMEM),
            pl.BlockSpec(memory_space=pltpu.SMEM),
            pl.BlockSpec(memory_space=pltpu.SMEM),
            pl.BlockSpec(memory_space=pltpu.SMEM),
            pl.BlockSpec(memory_space=pltpu.SMEM),
        ),
        scratch_shapes=[pltpu.SMEM((n_exp,), jnp.int32)],
        name="moe_plan",
    )(counts, ids_rows)


def _dispatch_kernel(d1_ref, d2_ref, first_ref, nu_ref, h_ref, xb_ref, zbuf, sem, zsem, *, tm, ns, bm, nb):
    base = pl.program_id(0) * tm

    @pl.when(pl.program_id(0) == 0)
    def _():
        zbuf[...] = jnp.zeros(zbuf.shape, zbuf.dtype)
        n_used = nu_ref[0]

        def has_padding(b):
            nxt_first = first_ref[jnp.minimum(b + 1, nb - 1)]
            return (b >= n_used - 1) | (nxt_first == 1)

        def zero_copy(b):
            return pltpu.make_async_copy(zbuf, xb_ref.at[pl.ds(pl.multiple_of(b * (bm * ns), bm * ns), bm * ns)], zsem)

        def zstart(b, carry):
            @pl.when(has_padding(b))
            def _():
                zero_copy(b).start()
            return carry

        def zwait(b, carry):
            @pl.when(has_padding(b))
            def _():
                zero_copy(b).wait()
            return carry

        lax.fori_loop(0, nb, zstart, 0)
        lax.fori_loop(0, nb, zwait, 0)

    def row_copy(r, dest):
        src = h_ref.at[pl.ds(pl.multiple_of(r * ns, ns), ns)]
        return pltpu.make_async_copy(src, xb_ref.at[pl.ds(pl.multiple_of(dest * ns, ns), ns)], sem)

    def start(r, carry):
        row_copy(r, d1_ref[base + r]).start()
        row_copy(r, d2_ref[base + r]).start()
        return carry

    lax.fori_loop(0, tm, start, 0, unroll=DMA_UNROLL)
    for _ in range(2):
        pltpu.make_async_copy(h_ref, xb_ref.at[pl.ds(0, tm * ns)], sem).wait()


def _dispatch(h2p, dest1, dest2, first, n_used, nb, bm, tm, ns):
    t = h2p.shape[0] // ns
    return pl.pallas_call(
        functools.partial(_dispatch_kernel, tm=tm, ns=ns, bm=bm, nb=nb),
        out_shape=jax.ShapeDtypeStruct((nb * bm * ns, LANES), h2p.dtype),
        grid_spec=pltpu.PrefetchScalarGridSpec(
            num_scalar_prefetch=4,
            grid=(t // tm,),
            in_specs=[pl.BlockSpec((tm * ns, LANES), lambda m, d1, d2, fi, nu: (m, 0))],
            out_specs=pl.BlockSpec(memory_space=pl.ANY),
            scratch_shapes=[pltpu.VMEM((bm * ns, LANES), h2p.dtype), pltpu.SemaphoreType.DMA,
                            pltpu.SemaphoreType.DMA],
        ),
        compiler_params=_params(("arbitrary",)),
        name="moe_dispatch",
    )(dest1, dest2, first, n_used, h2p)


def _experts_kernel(be_ref, first_ref, nxt_ref, nu_ref, x_ref, wg_hbm, wu_hbm, wd_hbm, y_ref,
                    wg_buf, wu_buf, wd_buf, sem, run_scr, *, ns):
    b = pl.program_id(0)

    def weight_copies(e, slot):
        return (pltpu.make_async_copy(wg_hbm.at[e], wg_buf.at[slot], sem.at[slot]),
                pltpu.make_async_copy(wu_hbm.at[e], wu_buf.at[slot], sem.at[slot]),
                pltpu.make_async_copy(wd_hbm.at[e], wd_buf.at[slot], sem.at[slot]))

    @pl.when(b == 0)
    def _():
        run_scr[0] = 0
        for cp in weight_copies(be_ref[0], 0):
            cp.start()

    @pl.when((first_ref[b] == 1) & (b < nu_ref[0]))
    def _():
        run = run_scr[0]
        slot = run & 1
        for cp in weight_copies(be_ref[b], slot):
            cp.wait()

        @pl.when(nxt_ref[b] >= 0)
        def _():
            for cp in weight_copies(nxt_ref[b], 1 - slot):
                cp.start()

        run_scr[0] = run + 1

    @pl.when(b < nu_ref[0])
    def _():
        slot = (run_scr[0] - 1) & 1
        x = _unpack_bf16_pairs(_load_row_tiles(x_ref, ns)).astype(BF16)
        g = jnp.dot(x, wg_buf[slot].astype(BF16), preferred_element_type=F32)
        u = jnp.dot(x, wu_buf[slot].astype(BF16), preferred_element_type=F32)
        hid = (g * jax.nn.sigmoid(g) * u).astype(BF16)
        y = jnp.dot(hid, wd_buf[slot].astype(BF16), preferred_element_type=F32)
        _store_row_tiles(y_ref, _pack_bf16_pairs(y))


def _experts(xb, w_gate, w_up, w_down, block_expert, first, nxt, n_used, bm, ns):
    d, de = w_gate.shape[1:]
    nb = xb.shape[0] // (bm * ns)

    def row_map(b, be, fi, nx, nu):
        return (jnp.minimum(b, nu[0] - 1), 0)

    return pl.pallas_call(
        functools.partial(_experts_kernel, ns=ns),
        out_shape=jax.ShapeDtypeStruct(xb.shape, jnp.uint32),
        grid_spec=pltpu.PrefetchScalarGridSpec(
            num_scalar_prefetch=4,
            grid=(nb,),
            in_specs=[
                pl.BlockSpec((bm * ns, LANES), row_map),
                pl.BlockSpec(memory_space=pl.ANY),
                pl.BlockSpec(memory_space=pl.ANY),
                pl.BlockSpec(memory_space=pl.ANY),
            ],
            out_specs=pl.BlockSpec((bm * ns, LANES), row_map),
            scratch_shapes=[
                pltpu.VMEM((2, d, de), F32),
                pltpu.VMEM((2, d, de), F32),
                pltpu.VMEM((2, de, d), F32),
                pltpu.SemaphoreType.DMA((2,)),
                pltpu.SMEM((1,), jnp.int32),
            ],
        ),
        input_output_aliases={4: 0},
        compiler_params=_params(("arbitrary",)),
        name="moe_experts",
    )(block_expert, first, nxt, n_used, xb, w_gate, w_up, w_down)


def _ple_kernel(d1_ref, d2_ref, x_ref, mg_ref, yb_ref, g_ref, p_ref, wg_ref, wp_ref, gf_ref, o_ref,
                ybuf, sem, x3_scr, h_scr, p_scr, *, tm, tn, ns, nj):
    m = pl.program_id(0)
    j = pl.program_id(1)
    n_m = pl.num_programs(0)
    n_j = pl.num_programs(1)
    slot = m & 1

    def row_copies(blk, r, s):
        tok = blk * tm + r
        src1 = yb_ref.at[pl.ds(pl.multiple_of(d1_ref[tok] * ns, ns), ns)]
        src2 = yb_ref.at[pl.ds(pl.multiple_of(d2_ref[tok] * ns, ns), ns)]
        rows = pl.ds(pl.multiple_of(r * ns, ns), ns)
        return (pltpu.make_async_copy(src1, ybuf.at[s, 0, rows], sem.at[s]),
                pltpu.make_async_copy(src2, ybuf.at[s, 1, rows], sem.at[s]))

    def drain(s):
        for k in range(2):
            pltpu.make_async_copy(yb_ref.at[pl.ds(0, tm * ns)], ybuf.at[s, k], sem.at[s]).wait()

    @pl.when(j == 0)
    def _():
        @pl.when(m == 0)
        def _():
            def start(r, carry):
                for cp in row_copies(0, r, 0):
                    cp.start()
                return carry

            lax.fori_loop(0, tm, start, 0, unroll=DMA_UNROLL)

        drain(slot)
        mg = mg_ref[...]
        y1 = _unpack_bf16_pairs(_load_row_tiles(ybuf.at[slot, 0], ns))
        y2 = _unpack_bf16_pairs(_load_row_tiles(ybuf.at[slot, 1], ns))
        x3 = x_ref[...] + mg[:, 0:1] * y1 + mg[:, 1:2] * y2
        x3_scr[...] = x3
        h_scr[...] = _rmsnorm_f32(x3, g_ref[...]).astype(BF16)
        p_scr[...] = p_ref[...].astype(BF16)

    nxt = jnp.minimum(m + 1, n_m - 1)
    rows_per_step = tm // nj
    for r in range(rows_per_step):
        for cp in row_copies(nxt, j * rows_per_step + r, 1 - slot):
            cp.start()

    col = pl.multiple_of(j * tn, tn)
    gate = jnp.dot(h_scr[...], wg_ref[:, pl.ds(col, tn)].astype(BF16), preferred_element_type=F32)
    ple = jnp.dot(p_scr[...], wp_ref[:, pl.ds(col, tn)].astype(BF16), preferred_element_type=F32)
    o_ref[:, pl.ds(col, tn)] = x3_scr[:, pl.ds(col, tn)] + jax.nn.sigmoid(gate) * ple

    @pl.when(j == n_j - 1)
    def _():
        o_ref[...] = _rmsnorm_f32(o_ref[...], gf_ref[...])

        @pl.when(m == n_m - 1)
        def _():
            drain(1 - slot)


def _combine_ple_final(x2, moe_gates, yb, dest1, dest2, g_ple, p2d, w_gate, w_proj, g_final, tm, tn, ns):
    t, d = x2.shape
    dp = p2d.shape[1]
    return pl.pallas_call(
        functools.partial(_ple_kernel, tm=tm, tn=tn, ns=ns, nj=d // tn),
        out_shape=jax.ShapeDtypeStruct((t, d), F32),
        grid_spec=pltpu.PrefetchScalarGridSpec(
            num_scalar_prefetch=2,
            grid=(t // tm, d // tn),
            in_specs=[
                pl.BlockSpec((tm, d), lambda m, j, d1, d2: (m, 0)),
                pl.BlockSpec((tm, LANES), lambda m, j, d1, d2: (m, 0)),
                pl.BlockSpec(memory_space=pl.ANY),
                pl.BlockSpec((1, d), lambda m, j, d1, d2: (0, 0)),
                pl.BlockSpec((tm, dp), lambda m, j, d1, d2: (m, 0)),
                pl.BlockSpec((d, d), lambda m, j, d1, d2: (0, 0), **_RESIDENT),
                pl.BlockSpec((dp, d), lambda m, j, d1, d2: (0, 0), **_RESIDENT),
                pl.BlockSpec((1, d), lambda m, j, d1, d2: (0, 0)),
            ],
            out_specs=pl.BlockSpec((tm, d), lambda m, j, d1, d2: (m, 0)),
            scratch_shapes=[
                pltpu.VMEM((2, 2, tm * ns, LANES), yb.dtype),
                pltpu.SemaphoreType.DMA((2,)),
                pltpu.VMEM((tm, d), F32),
                pltpu.VMEM((tm, d), BF16),
                pltpu.VMEM((tm, dp), BF16),
            ],
        ),
        compiler_params=_params(("arbitrary", "arbitrary")),
        name="combine_ple_final",
    )(dest1, dest2, x2, moe_gates, yb, g_ple, p2d, w_gate, w_proj, g_final)


def _tile(n, pref):
    return pref if n % pref == 0 else n


def _layer(x2d, p2d, batch, seq, w_in, b_forget, w_branch_fox, w_branch_sb, w_mix_out, g_mix, g_ffn,
           w_group, w_expert, w_gate, w_up, w_down, g_ple, w_ple_proj, w_ple_gate, g_final):
    t, d = x2d.shape
    n_heads = b_forget.shape[0]
    w_att = n_heads * HEAD_DIM
    n_exp = w_expert.shape[1]

    qs = LOG2E * HEAD_DIM ** -0.5
    w_t, wf_t = _wprep(w_in.T, w_att, n_heads, qs, 512)
    b_row = jnp.pad(b_forget, (0, LANES - n_heads)).reshape(1, LANES)
    col_qa, col_ka, col_qb, col_kb = 0, w_att, 2 * w_att, 3 * w_att
    col_ga, col_gb = 4 * w_att, 4 * w_att + d
    row_va, row_vb = 0, w_att

    tm = _tile(t, 1024)
    tn = _tile(d, 1024)
    proj, vt, f_tok = _inproj(x2d, g_mix.reshape(1, d), w_t, wf_t, 2 * w_att, tm, _tile(d // 2, 1024))
    c = _forget_cumsum(f_tok, b_row, batch, n_heads)

    tq = _tile(seq, 512)
    hp = 4
    o_a = _fox_attention(proj, vt, c, batch, seq, n_heads, col_qa, col_ka, row_va, tq, hp)
    o_b = _sb_attention(proj, vt, batch, seq, n_heads, col_qb, col_kb, row_vb, tq, _tile(seq, 256), hp)

    mixed = _merge(o_a, o_b, w_branch_fox, w_branch_sb, proj, col_ga, col_gb, tm, tn)
    x2 = _mixout(mixed, w_mix_out, x2d, tm, tn)

    w_r = jnp.concatenate([w_group, w_expert], axis=1)
    w_r = jnp.pad(w_r, ((0, 0), (0, LANES - w_r.shape[1])))
    w_r_hi = w_r.astype(BF16)
    w_r_lo = (w_r - w_r_hi.astype(F32)).astype(BF16)
    tr = _tile(t, 512)
    w_r_cat = jnp.concatenate([w_r_hi, w_r_lo], axis=1)
    h2, ids_rows, gates, counts = _router(x2, g_ffn.reshape(1, d), w_r_cat, tr, N_GROUPS, EXPERTS_PER_GROUP)

    bm = 256
    n_assign = 2 * t
    nb = n_assign // bm + n_exp
    dest, block_expert, first, nxt, n_used = _plan(counts, ids_rows, n_exp, bm, nb)
    dest1, dest2 = dest[0], dest[1]

    td = _tile(t, 2048)
    ns = d // 2 // LANES
    xb = _dispatch(h2, dest1, dest2, first, n_used, nb, bm, td, ns)
    yb = _experts(xb, w_gate, w_up, w_down, block_expert, first, nxt, n_used, bm, ns)

    tp = _tile(t, 512)
    return _combine_ple_final(x2, gates, yb, dest1, dest2, g_ple.reshape(1, d), p2d, w_ple_gate, w_ple_proj,
                              g_final.reshape(1, d), tp, tn, ns)


def kernel(x, p, w_in, b_forget, w_branch_fox, w_branch_sb, w_mix_out, g_mix, g_ffn, w_group, w_expert,
           w_gate, w_up, w_down, g_ple, w_ple_proj, w_ple_gate, g_final):
    b, s, d = x.shape
    depth = w_in.shape[0]
    assert depth == 1, "the final norm is fused into the single layer"
    x2d = x.reshape(b * s, d)
    out = _layer(x2d, p[0].reshape(b * s, -1), b, s, w_in[0], b_forget[0], w_branch_fox[0], w_branch_sb[0],
                 w_mix_out[0], g_mix[0], g_ffn[0], w_group[0], w_expert[0], w_gate[0], w_up[0], w_down[0],
                 g_ple[0], w_ple_proj[0], w_ple_gate[0], g_final)
    return out.reshape(b, s, d)
```

```python
import functools

import jax
import jax.numpy as jnp
from jax import lax
from jax.experimental import pallas as pl
from jax.experimental.pallas import tpu as pltpu

F32 = jnp.float32
BF16 = jnp.bfloat16

HEAD_DIM = 128
N_GROUPS = 4
EXPERTS_PER_GROUP = 8
EPS = 1e-6
LANES = 128
SUBLANES = 8
VMEM_LIMIT = 56 * 1024 * 1024

LOG2E = 1.4426950408889634
DMA_UNROLL = 8
R_DONE = 160.0

NT_DIMS = (((1,), (1,)), ((), ()))


def _params(sem):
    return pltpu.CompilerParams(dimension_semantics=sem, vmem_limit_bytes=VMEM_LIMIT)


def _rmsnorm_f32(x, g):
    ms = jnp.mean(x * x, axis=-1, keepdims=True)
    return x * lax.rsqrt(ms + EPS) * g


def _pack_bf16_pairs(x):
    n = x.shape[1] // 2
    bits = lax.bitcast_convert_type(x.astype(BF16).astype(F32), jnp.uint32)
    return (bits[:, :n] >> 16) | (bits[:, n:] & jnp.uint32(0xFFFF0000))


def _unpack_bf16_pairs(w):
    lo = lax.bitcast_convert_type(w << 16, F32)
    hi = lax.bitcast_convert_type(w & jnp.uint32(0xFFFF0000), F32)
    return jnp.concatenate([lo, hi], axis=1)


def _store_row_tiles(ref, words):
    n = words.shape[0]
    ns = words.shape[1] // LANES
    for c in range(ns):
        ref[pl.ds(c, n, stride=ns), :] = words[:, c * LANES:(c + 1) * LANES]


def _load_row_tiles(ref, ns):
    n = ref.shape[0] // ns
    return jnp.concatenate([ref[pl.ds(c, n, stride=ns), :] for c in range(ns)], axis=1)


def _wprep_kernel(off_ref, scale_ref, w_ref, f_ref, wt_ref, wf_ref, *, n_heads):
    i = pl.program_id(0)
    wt_ref[...] = (w_ref[...] * scale_ref[i]).astype(BF16)

    @pl.when(i == 0)
    def _():
        row = lax.broadcasted_iota(jnp.int32, wf_ref.shape, 0)
        f_rows = jnp.concatenate([f_ref[...]] * (wf_ref.shape[0] // f_ref.shape[0]), axis=0)
        wf_ref[...] = jnp.where(row < n_heads, f_rows, 0.0).astype(BF16)


def _wprep(w_t, w_att, n_heads, qs, rb):
    n_in, d = w_t.shape
    off_b = 3 * w_att + n_heads
    segments = [(2 * w_att, w_att, 1.0), (off_b + 2 * w_att, w_att, 1.0),
                (0, w_att, qs), (w_att, w_att, 1.0),
                (off_b, w_att, qs), (off_b + w_att, w_att, 1.0),
                (off_b + 3 * w_att, n_in - off_b - 3 * w_att, 1.0)]
    offs, scales = [], []
    for start, length, scale in segments:
        assert length % rb == 0 and start % SUBLANES == 0
        for k in range(length // rb):
            offs.append((start + k * rb) // SUBLANES)
            scales.append(scale)
    n_out = len(offs) * rb
    f_blk = SUBLANES
    assert n_heads <= f_blk and (3 * w_att) % f_blk == 0
    return pl.pallas_call(
        functools.partial(_wprep_kernel, n_heads=n_heads),
        out_shape=(jax.ShapeDtypeStruct((n_out, d), BF16), jax.ShapeDtypeStruct((LANES, d), BF16)),
        grid_spec=pltpu.PrefetchScalarGridSpec(
            num_scalar_prefetch=2,
            grid=(len(offs),),
            in_specs=[
                pl.BlockSpec((pl.Element(rb), pl.Element(d)), lambda i, off, sc: (off[i] * SUBLANES, 0)),
                pl.BlockSpec((pl.Element(f_blk), pl.Element(d)), lambda i, off, sc: (3 * w_att, 0)),
            ],
            out_specs=(
                pl.BlockSpec((rb, d), lambda i, off, sc: (i, 0)),
                pl.BlockSpec((LANES, d), lambda i, off, sc: (0, 0)),
            ),
        ),
        compiler_params=_params(("arbitrary",)),
        name="inproj_weights",
    )(jnp.asarray(offs, jnp.int32), jnp.asarray(scales, F32), w_t, w_t)


def _inproj_kernel(x_ref, g_ref, w_ref, wf_ref, o_ref, vt_ref, f_ref, h_scr, *, nvt):
    j = pl.program_id(1)

    @pl.when(j == 0)
    def _():
        hb = _rmsnorm_f32(x_ref[...], g_ref[...]).astype(BF16)
        h_scr[...] = hb
        f_ref[...] = lax.dot_general(hb, wf_ref[...], NT_DIMS, preferred_element_type=F32)

    @pl.when(j < nvt)
    def _():
        vt_ref[...] = lax.dot_general(w_ref[...], h_scr[...], NT_DIMS,
                                      preferred_element_type=F32).astype(vt_ref.dtype)

    @pl.when(j >= nvt)
    def _():
        o_ref[...] = lax.dot_general(h_scr[...], w_ref[...], NT_DIMS,
                                     preferred_element_type=F32).astype(o_ref.dtype)


def _inproj(x2d, g, w_t, wf_t, n_v, tm, tn):
    t, d = x2d.shape
    n = w_t.shape[0] - n_v
    nvt = n_v // tn
    return pl.pallas_call(
        functools.partial(_inproj_kernel, nvt=nvt),
        out_shape=(jax.ShapeDtypeStruct((t, n), BF16), jax.ShapeDtypeStruct((n_v, t), BF16),
                   jax.ShapeDtypeStruct((t, LANES), F32)),
        grid=(t // tm, nvt + n // tn),
        in_specs=[
            pl.BlockSpec((tm, d), lambda m, j: (m, 0)),
            pl.BlockSpec((1, d), lambda m, j: (0, 0)),
            pl.BlockSpec((tn, d), lambda m, j: (j, 0)),
            pl.BlockSpec((LANES, d), lambda m, j: (0, 0)),
        ],
        out_specs=(
            pl.BlockSpec((tm, tn), lambda m, j: (m, jnp.maximum(j - nvt, 0))),
            pl.BlockSpec((tn, tm), lambda m, j: (jnp.minimum(j, nvt - 1), m)),
            pl.BlockSpec((tm, LANES), lambda m, j: (m, 0)),
        ),
        scratch_shapes=[pltpu.VMEM((tm, d), BF16)],
        compiler_params=_params(("parallel", "arbitrary")),
        name="inproj",
    )(x2d, g, w_t, wf_t)


def _split3(x):
    p1 = x.astype(BF16)
    r1 = x - p1.astype(F32)
    p2 = r1.astype(BF16)
    p3 = (r1 - p2.astype(F32)).astype(BF16)
    return p1, p2, p3


def _cumsum_kernel(f_ref, b_ref, lt_ref, c_ref, carry_scr, *, chunk, n_heads):
    @pl.when(pl.program_id(1) == 0)
    def _():
        carry_scr[...] = jnp.zeros(carry_scr.shape, F32)

    rows = f_ref.shape[0]
    lt = lt_ref[...]
    carry = carry_scr[...]
    for i in range(rows // chunk):
        lf = jax.nn.log_sigmoid(f_ref[i * chunk:(i + 1) * chunk, :] + b_ref[...])
        p1, p2, p3 = _split3(lf)
        cs = (jnp.dot(lt, p1, preferred_element_type=F32)
              + jnp.dot(lt, p2, preferred_element_type=F32)
              + jnp.dot(lt, p3, preferred_element_type=F32)) + carry
        for h in range(n_heads):
            c_ref[h, i * chunk:(i + 1) * chunk, :] = jnp.broadcast_to(cs[:, h:h + 1] * LOG2E, (chunk, LANES))
        carry = cs[chunk - 1:chunk, :]
    carry_scr[...] = carry


def _forget_cumsum(f_tok, b_row, batch, n_heads):
    t = f_tok.shape[0]
    s = t // batch
    rows = min(1024, s)
    chunk = min(256, s)
    lt = jnp.tril(jnp.ones((chunk, chunk), BF16))
    nr = s // rows
    return pl.pallas_call(
        functools.partial(_cumsum_kernel, chunk=chunk, n_heads=n_heads),
        out_shape=jax.ShapeDtypeStruct((n_heads, t, LANES), F32),
        grid=(batch, nr),
        in_specs=[
            pl.BlockSpec((rows, LANES), lambda b, r: (b * nr + r, 0)),
            pl.BlockSpec((1, LANES), lambda b, r: (0, 0)),
            pl.BlockSpec((chunk, chunk), lambda b, r: (0, 0)),
        ],
        out_specs=pl.BlockSpec((n_heads, rows, LANES), lambda b, r: (0, b * nr + r, 0)),
        scratch_shapes=[pltpu.VMEM((1, LANES), F32)],
        compiler_params=_params(("parallel", "arbitrary")),
        name="forget_cumsum",
    )(f_tok, b_row, lt)


def _fox_kernel(q_ref, k_ref, vt_ref, c_ref, o_ref, m_scr, l_scr, acc_scr, *, tq, tk, hp):
    i = pl.program_id(2)
    m_scr[...] = jnp.full(m_scr.shape, -jnp.inf, F32)
    l_scr[...] = jnp.zeros(l_scr.shape, F32)
    acc_scr[...] = jnp.zeros(acc_scr.shape, F32)

    def tile(j, masked, koff=0, nk=tk, qlo=0):
        start = pl.multiple_of(j * tk + koff, nk)
        nq = tq - qlo
        if masked:
            key = j * tk + koff + lax.broadcasted_iota(jnp.int32, (nk, nq), 0)
            qry = i * tq + qlo + lax.broadcasted_iota(jnp.int32, (nk, nq), 1)
            keep = key <= qry

        def scores(hh):
            lanes = slice(hh * HEAD_DIM, (hh + 1) * HEAD_DIM)
            kj = k_ref[pl.ds(start, nk), lanes]
            return lax.dot_general(kj, q_ref[qlo:tq, lanes], NT_DIMS, preferred_element_type=F32)

        s_all = [scores(hh) for hh in range(hp)]
        for hh in range(hp):
            s = s_all[hh]
            cj = c_ref[hh, pl.ds(start, nk), :]
            s = s - jnp.concatenate([cj] * (nq // LANES), axis=1)
            if masked:
                s = jnp.where(keep, s, -jnp.inf)
            m_prev = m_scr[hh, :, qlo:tq]
            m_new = jnp.maximum(m_prev, jnp.max(s, axis=0, keepdims=True))
            alpha = jnp.exp2(m_prev - m_new)
            p = jnp.exp2(s - m_new)
            l_scr[hh, :, qlo:tq] = alpha * l_scr[hh, :, qlo:tq] + jnp.sum(p, axis=0, keepdims=True)
            vtj = vt_ref[hh * HEAD_DIM:(hh + 1) * HEAD_DIM, pl.ds(start, nk)]
            acc_scr[hh, :, qlo:tq] = (alpha * acc_scr[hh, :, qlo:tq]
                                      + jnp.dot(vtj, p.astype(BF16), preferred_element_type=F32))
            m_scr[hh, :, qlo:tq] = m_new

    def body(j, carry):
        tile(j, False)
        return carry

    lax.fori_loop(0, i, body, 0)
    half = tk // 2
    tile(i, True, koff=0, nk=half, qlo=0)
    tile(i, True, koff=half, nk=half, qlo=tq - half)
    for hh in range(hp):
        o = acc_scr[hh] / l_scr[hh]
        o_ref[:, hh * HEAD_DIM:(hh + 1) * HEAD_DIM] = o.T.astype(o_ref.dtype)


def _fox_attention(proj, vt, c, batch, seq, n_heads, col_q, col_k, row_v, tq, hp):
    t = proj.shape[0]
    nq = seq // tq
    wb = hp * HEAD_DIM
    cq, ck, rv = col_q // wb, col_k // wb, row_v // wb
    return pl.pallas_call(
        functools.partial(_fox_kernel, tq=tq, tk=tq, hp=hp),
        out_shape=jax.ShapeDtypeStruct((t, n_heads * HEAD_DIM), BF16),
        grid=(batch, n_heads // hp, nq),
        in_specs=[
            pl.BlockSpec((tq, wb), lambda b, g, i: (b * nq + i, cq + g)),
            pl.BlockSpec((seq, wb), lambda b, g, i: (b, ck + g)),
            pl.BlockSpec((wb, seq), lambda b, g, i: (rv + g, b)),
            pl.BlockSpec((hp, seq, LANES), lambda b, g, i: (g, b, 0)),
        ],
        out_specs=pl.BlockSpec((tq, wb), lambda b, g, i: (b * nq + i, g)),
        scratch_shapes=[
            pltpu.VMEM((hp, 1, tq), F32),
            pltpu.VMEM((hp, 1, tq), F32),
            pltpu.VMEM((hp, HEAD_DIM, tq), F32),
        ],
        compiler_params=_params(("parallel", "parallel", "arbitrary")),
        name="fox_attention",
    )(proj, proj, vt, c)


def _sb_kernel(q_ref, k_ref, vt_ref, tri_ref, o_ref, r_scr, acc_scr, *, tq, tk, hp):
    i = pl.program_id(2)
    r_scr[...] = jnp.zeros(r_scr.shape, F32)
    acc_scr[...] = jnp.zeros(acc_scr.shape, F32)

    sign_bit = jnp.uint32(0x80000000)

    def tile(j, masked, qlo=0, qhi=tq):
        start = pl.multiple_of(j * tk, tk)
        nq = qhi - qlo
        if masked:
            key = j * tk + lax.broadcasted_iota(jnp.int32, (tk, nq), 0)
            qry = i * tq + qlo + lax.broadcasted_iota(jnp.int32, (tk, nq), 1)
            mask = key < qry

        def scores(hh):
            lanes = slice(hh * HEAD_DIM, (hh + 1) * HEAD_DIM)
            kj = k_ref[pl.ds(start, tk), lanes]
            return lax.dot_general(kj, q_ref[qlo:qhi, lanes], NT_DIMS, preferred_element_type=F32)

        u_all = [scores(hh) for hh in range(hp)]
        for hh in range(hp):
            u = u_all[hh]
            minus_abs = lax.bitcast_convert_type(lax.bitcast_convert_type(u, jnp.uint32) | sign_bit, F32)
            sp = jnp.maximum(u, 0.0) + jnp.log2(1.0 + jnp.exp2(minus_abs))
            if masked:
                sp = jnp.where(mask, sp, 0.0)
            w = jnp.dot(tri_ref[...], sp.astype(BF16), preferred_element_type=F32)
            a = jnp.exp2(u - ((sp + w) + r_scr[hh, :, qlo:qhi]))
            if masked:
                a = jnp.where(mask, a, 0.0)
            vtj = vt_ref[hh * HEAD_DIM:(hh + 1) * HEAD_DIM, pl.ds(start, tk)]
            acc_scr[hh, :, qlo:qhi] += jnp.dot(vtj, a.astype(BF16), preferred_element_type=F32)
            r_scr[hh, :, qlo:qhi] += jnp.sum(sp, axis=0, keepdims=True)

    ratio = max(tq // tk, 1)
    n_full = (i * tq) // tk
    for d in reversed(range(ratio)):
        tile(n_full + d, True, qlo=d * tk if tq > tk else 0)

    def live():
        return (jnp.min(r_scr[...]) <= R_DONE).astype(jnp.int32)

    def cond(carry):
        it, alive = carry
        return (it < n_full) & (alive > 0)

    def body(carry):
        it, _ = carry
        tile(n_full - 1 - it, False)
        return it + 1, live()

    lax.while_loop(cond, body, (jnp.int32(0), live()))
    for hh in range(hp):
        o_ref[:, hh * HEAD_DIM:(hh + 1) * HEAD_DIM] = acc_scr[hh].T.astype(o_ref.dtype)


def _sb_attention(proj, vt, batch, seq, n_heads, col_q, col_k, row_v, tq, tk, hp):
    t = proj.shape[0]
    nq = seq // tq
    wb = hp * HEAD_DIM
    cq, ck, rv = col_q // wb, col_k // wb, row_v // wb
    tri = jnp.triu(jnp.ones((tk, tk), BF16), 1)
    return pl.pallas_call(
        functools.partial(_sb_kernel, tq=tq, tk=tk, hp=hp),
        out_shape=jax.ShapeDtypeStruct((t, n_heads * HEAD_DIM), BF16),
        grid=(batch, n_heads // hp, nq),
        in_specs=[
            pl.BlockSpec((tq, wb), lambda b, g, i: (b * nq + i, cq + g)),
            pl.BlockSpec((seq, wb), lambda b, g, i: (b, ck + g)),
            pl.BlockSpec((wb, seq), lambda b, g, i: (rv + g, b)),
            pl.BlockSpec((tk, tk), lambda b, g, i: (0, 0)),
        ],
        out_specs=pl.BlockSpec((tq, wb), lambda b, g, i: (b * nq + i, g)),
        scratch_shapes=[
            pltpu.VMEM((hp, 1, tq), F32),
            pltpu.VMEM((hp, HEAD_DIM, tq), F32),
        ],
        compiler_params=_params(("parallel", "parallel", "arbitrary")),
        name="sb_attention",
    )(proj, proj, vt, tri)


_RESIDENT = dict(pipeline_mode=pl.Buffered(1))


def _merge_kernel(oa_ref, ob_ref, wa_ref, wb_ref, ga_ref, gb_ref, o_ref, *, tn):
    col = pl.multiple_of(pl.program_id(1) * tn, tn)
    ya = jnp.dot(oa_ref[...], wa_ref[:, pl.ds(col, tn)].astype(BF16), preferred_element_type=F32)
    yb = jnp.dot(ob_ref[...], wb_ref[:, pl.ds(col, tn)].astype(BF16), preferred_element_type=F32)
    ga = jax.nn.sigmoid(ga_ref[...].astype(F32))
    gb = jax.nn.sigmoid(gb_ref[...].astype(F32))
    o_ref[...] = (ga * ya + gb * yb).astype(o_ref.dtype)


def _merge(o_a, o_b, wa, wb, proj, col_ga, col_gb, tm, tn):
    t, ka = o_a.shape
    kb = o_b.shape[1]
    d = wa.shape[1]
    ca, cb = col_ga // tn, col_gb // tn
    return pl.pallas_call(
        functools.partial(_merge_kernel, tn=tn),
        out_shape=jax.ShapeDtypeStruct((t, d), BF16),
        grid=(t // tm, d // tn),
        in_specs=[
            pl.BlockSpec((tm, ka), lambda m, j: (m, 0)),
            pl.BlockSpec((tm, kb), lambda m, j: (m, 0)),
            pl.BlockSpec((ka, d), lambda m, j: (0, 0), **_RESIDENT),
            pl.BlockSpec((kb, d), lambda m, j: (0, 0), **_RESIDENT),
            pl.BlockSpec((tm, tn), lambda m, j: (m, ca + j)),
            pl.BlockSpec((tm, tn), lambda m, j: (m, cb + j)),
        ],
        out_specs=pl.BlockSpec((tm, tn), lambda m, j: (m, j)),
        compiler_params=_params(("parallel", "arbitrary")),
        name="branch_merge",
    )(o_a, o_b, wa, wb, proj, proj)


def _mixout_kernel(a_ref, w_ref, x_ref, o_ref, *, tn):
    col = pl.multiple_of(pl.program_id(1) * tn, tn)
    w = w_ref[:, pl.ds(col, tn)].astype(BF16)
    o_ref[...] = x_ref[...] + jnp.dot(a_ref[...], w, preferred_element_type=F32)


def _mixout(mixed, w, x2d, tm, tn):
    t, k = mixed.shape
    d = w.shape[1]
    return pl.pallas_call(
        functools.partial(_mixout_kernel, tn=tn),
        out_shape=jax.ShapeDtypeStruct((t, d), F32),
        grid=(t // tm, d // tn),
        in_specs=[
            pl.BlockSpec((tm, k), lambda m, j: (m, 0)),
            pl.BlockSpec((k, d), lambda m, j: (0, 0), **_RESIDENT),
            pl.BlockSpec((tm, tn), lambda m, j: (m, j)),
        ],
        out_specs=pl.BlockSpec((tm, tn), lambda m, j: (m, j)),
        compiler_params=_params(("parallel", "arbitrary")),
        name="mix_out",
    )(mixed, w, x2d)


def _router_kernel(x_ref, g_ref, wc_ref, lt_ref, sel_ref, h_ref, ids_ref, gate_ref, cnt_ref, carry_scr,
                   *, n_groups, per_group):
    @pl.when(pl.program_id(0) == 0)
    def _():
        carry_scr[...] = jnp.zeros(carry_scr.shape, F32)

    h = _rmsnorm_f32(x_ref[...], g_ref[...])
    hh = h.astype(BF16)
    _store_row_tiles(h_ref, _pack_bf16_pairs(h))
    hl = (h - hh.astype(F32)).astype(BF16)
    both = jnp.dot(hh, wc_ref[...], preferred_element_type=F32)
    logits = (both[:, :LANES] + jnp.dot(hl, wc_ref[:, :LANES], preferred_element_type=F32)) + both[:, LANES:]
    tm = logits.shape[0]
    n_exp = n_groups * per_group
    lane = lax.broadcasted_iota(jnp.int32, (tm, LANES), 1)
    lane_f = lane.astype(F32)
    neg_inf = -jnp.inf

    def first_argmax(vals, valid):
        vmax = jnp.max(jnp.where(valid, vals, neg_inf), axis=-1, keepdims=True)
        idx = jnp.min(jnp.where(valid & (vals == vmax), lane_f, float(LANES)), axis=-1, keepdims=True)
        return vmax, idx.astype(jnp.int32)

    is_group = lane < n_groups
    gmax, gsel = first_argmax(logits, is_group)
    gsum = jnp.sum(jnp.where(is_group, jnp.exp(logits - gmax), 0.0), axis=-1, keepdims=True)
    p_group = 1.0 / gsum
    lo = n_groups + gsel * per_group
    in_group = (lane >= lo) & (lane < lo + per_group)
    v1, i1 = first_argmax(logits, in_group)
    v2, i2 = first_argmax(logits, in_group & (lane != i1))
    e2w = jnp.exp(v2 - v1)
    gate1 = (1.0 / (1.0 + e2w)) * p_group
    gate2 = (e2w / (1.0 + e2w)) * p_group
    e1 = i1 - n_groups
    e2 = i2 - n_groups

    onehot = ((lane == e1) | (lane == e2)).astype(BF16)
    before = jnp.dot(lt_ref[...], onehot, preferred_element_type=F32) + carry_scr[...]
    rank1 = jnp.sum(jnp.where(lane == e1, before, 0.0), axis=-1, keepdims=True).astype(jnp.int32)
    rank2 = jnp.sum(jnp.where(lane == e2, before, 0.0), axis=-1, keepdims=True).astype(jnp.int32)
    carry_new = carry_scr[...] + jnp.sum(onehot.astype(F32), axis=0, keepdims=True)
    carry_scr[...] = carry_new

    cols = jnp.where(lane == 0, e1, jnp.where(lane == 1, e2, jnp.where(lane == 2, rank1 >> 7,
           jnp.where(lane == 3, rank1 & 127, jnp.where(lane == 4, rank2 >> 7,
           jnp.where(lane == 5, rank2 & 127, 0))))))
    ids_ref[...] = lax.dot_general(sel_ref[...], cols.astype(F32).astype(BF16), NT_DIMS,
                                   preferred_element_type=F32)
    gate_ref[...] = jnp.where(lane == 0, gate1, jnp.where(lane == 1, gate2, 0.0))
    cnt_ref[...] = jnp.broadcast_to(carry_new, cnt_ref.shape).astype(jnp.int32)
    del n_exp


def _router(x2, g, w_cat, tm, n_groups, per_group):
    t, d = x2.shape
    lt = jnp.tril(jnp.ones((tm, tm), BF16), -1)
    sel = jnp.eye(8, LANES, dtype=BF16)
    return pl.pallas_call(
        functools.partial(_router_kernel, n_groups=n_groups, per_group=per_group),
        out_shape=(
            jax.ShapeDtypeStruct((t * (d // 2 // LANES), LANES), jnp.uint32),
            jax.ShapeDtypeStruct((8, t), F32),
            jax.ShapeDtypeStruct((t, LANES), F32),
            jax.ShapeDtypeStruct((8, LANES), jnp.int32),
        ),
        grid=(t // tm,),
        in_specs=[
            pl.BlockSpec((tm, d), lambda m: (m, 0)),
            pl.BlockSpec((1, d), lambda m: (0, 0)),
            pl.BlockSpec((d, 2 * LANES), lambda m: (0, 0)),
            pl.BlockSpec((tm, tm), lambda m: (0, 0)),
            pl.BlockSpec((8, LANES), lambda m: (0, 0)),
        ],
        out_specs=(
            pl.BlockSpec((tm * (d // 2 // LANES), LANES), lambda m: (m, 0)),
            pl.BlockSpec((8, tm), lambda m: (0, m)),
            pl.BlockSpec((tm, LANES), lambda m: (m, 0)),
            pl.BlockSpec((8, LANES), lambda m: (0, 0)),
        ),
        scratch_shapes=[pltpu.VMEM((1, LANES), F32)],
        compiler_params=_params(("arbitrary",)),
        name="router",
    )(x2, g, w_cat, lt, sel)


def _plan_kernel(cnt_ref, ids_ref, dest_ref, be_ref, first_ref, nxt_ref, plo_ref, pn_ref, nu_ref, ps_scr,
                 *, n_exp, bm, nb):
    shift = bm.bit_length() - 1

    def clear(k, c):
        first_ref[k] = 0
        nxt_ref[k] = -1
        return c

    lax.fori_loop(0, nb, clear, 0)

    def per_expert(e, carry):
        blk_start, last, prev_start = carry
        cnt = cnt_ref[0, e]
        n = lax.shift_right_logical(cnt + (bm - 1), shift)
        ps_scr[e] = blk_start * bm
        plo_ref[e] = blk_start * bm + cnt
        pn_ref[e] = n * bm - cnt

        def fill(k, c):
            be_ref[blk_start + k] = e
            return c

        lax.fori_loop(0, n, fill, 0)

        @pl.when(n > 0)
        def _():
            first_ref[blk_start] = 1

            @pl.when(prev_start >= 0)
            def _():
                nxt_ref[prev_start] = e

        used = n > 0
        return blk_start + n, jnp.where(used, e, last), jnp.where(used, blk_start, prev_start)

    n_used, last, _ = lax.fori_loop(0, n_exp, per_expert, (jnp.int32(0), jnp.int32(0), jnp.int32(-1)))
    nu_ref[0] = n_used

    def tail(k, c):
        be_ref[k] = last
        return c

    lax.fori_loop(n_used, nb, tail, 0)

    e1 = ids_ref[0:1, :]
    e2 = ids_ref[1:2, :]
    p1 = jnp.zeros_like(e1)
    p2 = jnp.zeros_like(e2)
    for e in range(n_exp):
        ps = ps_scr[e].astype(F32)
        p1 = jnp.where(e1 == e, ps, p1)
        p2 = jnp.where(e2 == e, ps, p2)
    dest_ref[...] = jnp.zeros(dest_ref.shape, jnp.int32)
    dest_ref[0:1, :] = (p1 + ids_ref[2:3, :] * 128.0 + ids_ref[3:4, :]).astype(jnp.int32)
    dest_ref[1:2, :] = (p2 + ids_ref[4:5, :] * 128.0 + ids_ref[5:6, :]).astype(jnp.int32)


def _plan(counts, ids_rows, n_exp, bm, nb):
    t = ids_rows.shape[1]
    return pl.pallas_call(
        functools.partial(_plan_kernel, n_exp=n_exp, bm=bm, nb=nb),
        out_shape=(
            jax.ShapeDtypeStruct((8, t), jnp.int32),
            jax.ShapeDtypeStruct((nb,), jnp.int32),
            jax.ShapeDtypeStruct((nb,), jnp.int32),
            jax.ShapeDtypeStruct((nb,), jnp.int32),
            jax.ShapeDtypeStruct((n_exp,), jnp.int32),
            jax.ShapeDtypeStruct((n_exp,), jnp.int32),
            jax.ShapeDtypeStruct((1,), jnp.int32),
        ),
        in_specs=[
            pl.BlockSpec(memory_space=pltpu.SMEM),
            pl.BlockSpec(memory_space=pltpu.VMEM),
        ],
        out_specs=(
            pl.BlockSpec(memory_space=pltpu.VMEM),
            pl.BlockSpec(memory_space=pltpu.SMEM),
            pl.BlockSpec(memory_space=pltpu.SMEM),
            pl.BlockSpec(memory_space=pltpu.SMEM),
            pl.BlockSpec(memory_space=pltpu.SMEM),
            pl.BlockSpec(memory_space=pltpu.SMEM),
            pl.BlockSpec(memory_space=pltpu.SMEM),
        ),
        scratch_shapes=[pltpu.SMEM((n_exp,), jnp.int32)],
        name="moe_plan",
    )(counts, ids_rows)


def _dispatch_kernel(d1_ref, d2_ref, plo_ref, pn_ref, nu_ref, h_ref, xb_ref, zbuf, sem, zsem,
                     *, tm, ns, bm, nb, n_exp):
    base = pl.program_id(0) * tm

    def zero_copies(fn):
        def per_expert(e, carry):
            pos = plo_ref[e]
            rem = pn_ref[e]
            chunk = bm // 2
            while chunk >= 1:
                hit = (rem & chunk) != 0

                @pl.when(hit)
                def _(pos=pos, chunk=chunk):
                    dst = xb_ref.at[pl.ds(pl.multiple_of(pos * ns, ns), chunk * ns)]
                    fn(pltpu.make_async_copy(zbuf.at[pl.ds(0, chunk * ns)], dst, zsem))

                pos = pos + jnp.where(hit, chunk, 0)
                chunk //= 2
            return carry

        lax.fori_loop(0, n_exp, per_expert, 0)

        def per_tail_block(b, carry):
            dst = xb_ref.at[pl.ds(pl.multiple_of(b * (bm * ns), bm * ns), bm * ns)]
            fn(pltpu.make_async_copy(zbuf, dst, zsem))
            return carry

        lax.fori_loop(nu_ref[0], nb, per_tail_block, 0)

    @pl.when(pl.program_id(0) == 0)
    def _():
        zbuf[...] = jnp.zeros(zbuf.shape, zbuf.dtype)
        zero_copies(lambda cp: cp.start())

    def row_copy(r, dest):
        src = h_ref.at[pl.ds(pl.multiple_of(r * ns, ns), ns)]
        return pltpu.make_async_copy(src, xb_ref.at[pl.ds(pl.multiple_of(dest * ns, ns), ns)], sem)

    def start(r, carry):
        row_copy(r, d1_ref[base + r]).start()
        row_copy(r, d2_ref[base + r]).start()
        return carry

    lax.fori_loop(0, tm, start, 0, unroll=DMA_UNROLL)
    for _ in range(2):
        pltpu.make_async_copy(h_ref, xb_ref.at[pl.ds(0, tm * ns)], sem).wait()

    @pl.when(pl.program_id(0) == pl.num_programs(0) - 1)
    def _():
        zero_copies(lambda cp: cp.wait())


def _dispatch(h2p, dest1, dest2, pad_lo, pad_n, n_used, nb, bm, tm, ns):
    t = h2p.shape[0] // ns
    return pl.pallas_call(
        functools.partial(_dispatch_kernel, tm=tm, ns=ns, bm=bm, nb=nb, n_exp=pad_lo.shape[0]),
        out_shape=jax.ShapeDtypeStruct((nb * bm * ns, LANES), h2p.dtype),
        grid_spec=pltpu.PrefetchScalarGridSpec(
            num_scalar_prefetch=5,
            grid=(t // tm,),
            in_specs=[pl.BlockSpec((tm * ns, LANES), lambda m, d1, d2, plo, pn, nu: (m, 0))],
            out_specs=pl.BlockSpec(memory_space=pl.ANY),
            scratch_shapes=[pltpu.VMEM((bm * ns, LANES), h2p.dtype), pltpu.SemaphoreType.DMA,
                            pltpu.SemaphoreType.DMA],
        ),
        compiler_params=_params(("arbitrary",)),
        name="moe_dispatch",
    )(dest1, dest2, pad_lo, pad_n, n_used, h2p)


def _experts_kernel(be_ref, first_ref, nxt_ref, nu_ref, x_ref, wg_hbm, wu_hbm, wd_hbm, y_ref,
                    wg_buf, wu_buf, wd_buf, sem, run_scr, *, ns):
    b = pl.program_id(0)

    def weight_copies(e, slot):
        return (pltpu.make_async_copy(wg_hbm.at[e], wg_buf.at[slot], sem.at[slot]),
                pltpu.make_async_copy(wu_hbm.at[e], wu_buf.at[slot], sem.at[slot]),
                pltpu.make_async_copy(wd_hbm.at[e], wd_buf.at[slot], sem.at[slot]))

    @pl.when(b == 0)
    def _():
        run_scr[0] = 0
        for cp in weight_copies(be_ref[0], 0):
            cp.start()

    @pl.when((first_ref[b] == 1) & (b < nu_ref[0]))
    def _():
        run = run_scr[0]
        slot = run & 1
        for cp in weight_copies(be_ref[b], slot):
            cp.wait()

        @pl.when(nxt_ref[b] >= 0)
        def _():
            for cp in weight_copies(nxt_ref[b], 1 - slot):
                cp.start()

        run_scr[0] = run + 1

    @pl.when(b < nu_ref[0])
    def _():
        slot = (run_scr[0] - 1) & 1
        x = _unpack_bf16_pairs(_load_row_tiles(x_ref, ns)).astype(BF16)
        g = jnp.dot(x, wg_buf[slot].astype(BF16), preferred_element_type=F32)
        u = jnp.dot(x, wu_buf[slot].astype(BF16), preferred_element_type=F32)
        hid = (g * jax.nn.sigmoid(g) * u).astype(BF16)
        y = jnp.dot(hid, wd_buf[slot].astype(BF16), preferred_element_type=F32)
        _store_row_tiles(y_ref, _pack_bf16_pairs(y))


def _experts(xb, w_gate, w_up, w_down, block_expert, first, nxt, n_used, bm, ns):
    d, de = w_gate.shape[1:]
    nb = xb.shape[0] // (bm * ns)

    def row_map(b, be, fi, nx, nu):
        return (jnp.minimum(b, nu[0] - 1), 0)

    return pl.pallas_call(
        functools.partial(_experts_kernel, ns=ns),
        out_shape=jax.ShapeDtypeStruct(xb.shape, jnp.uint32),
        grid_spec=pltpu.PrefetchScalarGridSpec(
            num_scalar_prefetch=4,
            grid=(nb,),
            in_specs=[
                pl.BlockSpec((bm * ns, LANES), row_map),
                pl.BlockSpec(memory_space=pl.ANY),
                pl.BlockSpec(memory_space=pl.ANY),
                pl.BlockSpec(memory_space=pl.ANY),
            ],
            out_specs=pl.BlockSpec((bm * ns, LANES), row_map),
            scratch_shapes=[
                pltpu.VMEM((2, d, de), F32),
                pltpu.VMEM((2, d, de), F32),
                pltpu.VMEM((2, de, d), F32),
                pltpu.SemaphoreType.DMA((2,)),
                pltpu.SMEM((1,), jnp.int32),
            ],
        ),
        input_output_aliases={4: 0},
        compiler_params=_params(("arbitrary",)),
        name="moe_experts",
    )(block_expert, first, nxt, n_used, xb, w_gate, w_up, w_down)


def _ple_kernel(d1_ref, d2_ref, x_ref, mg_ref, yb_ref, g_ref, p_ref, wg_ref, wp_ref, gf_ref, o_ref,
                ybuf, sem, x3_scr, h_scr, p_scr, *, tm, tn, ns, nj):
    m = pl.program_id(0)
    j = pl.program_id(1)
    n_m = pl.num_programs(0)
    n_j = pl.num_programs(1)
    slot = m & 1

    def row_copies(blk, r, s):
        tok = blk * tm + r
        src1 = yb_ref.at[pl.ds(pl.multiple_of(d1_ref[tok] * ns, ns), ns)]
        src2 = yb_ref.at[pl.ds(pl.multiple_of(d2_ref[tok] * ns, ns), ns)]
        rows = pl.ds(pl.multiple_of(r * ns, ns), ns)
        return (pltpu.make_async_copy(src1, ybuf.at[s, 0, rows], sem.at[s]),
                pltpu.make_async_copy(src2, ybuf.at[s, 1, rows], sem.at[s]))

    def drain(s):
        for k in range(2):
            pltpu.make_async_copy(yb_ref.at[pl.ds(0, tm * ns)], ybuf.at[s, k], sem.at[s]).wait()

    @pl.when(j == 0)
    def _():
        @pl.when(m == 0)
        def _():
            def start(r, carry):
                for cp in row_copies(0, r, 0):
                    cp.start()
                return carry

            lax.fori_loop(0, tm, start, 0, unroll=DMA_UNROLL)

        drain(slot)
        mg = mg_ref[...]
        y1 = _unpack_bf16_pairs(_load_row_tiles(ybuf.at[slot, 0], ns))
        y2 = _unpack_bf16_pairs(_load_row_tiles(ybuf.at[slot, 1], ns))
        x3 = x_ref[...] + mg[:, 0:1] * y1 + mg[:, 1:2] * y2
        x3_scr[...] = x3
        h_scr[...] = _rmsnorm_f32(x3, g_ref[...]).astype(BF16)
        p_scr[...] = p_ref[...].astype(BF16)

    nxt = jnp.minimum(m + 1, n_m - 1)
    rows_per_step = tm // nj
    for r in range(rows_per_step):
        for cp in row_copies(nxt, j * rows_per_step + r, 1 - slot):
            cp.start()

    col = pl.multiple_of(j * tn, tn)
    gate = jnp.dot(h_scr[...], wg_ref[:, pl.ds(col, tn)].astype(BF16), preferred_element_type=F32)
    ple = jnp.dot(p_scr[...], wp_ref[:, pl.ds(col, tn)].astype(BF16), preferred_element_type=F32)
    o_ref[:, pl.ds(col, tn)] = x3_scr[:, pl.ds(col, tn)] + jax.nn.sigmoid(gate) * ple

    @pl.when(j == n_j - 1)
    def _():
        o_ref[...] = _rmsnorm_f32(o_ref[...], gf_ref[...])

        @pl.when(m == n_m - 1)
        def _():
            drain(1 - slot)


def _combine_ple_final(x2, moe_gates, yb, dest1, dest2, g_ple, p2d, w_gate, w_proj, g_final, tm, tn, ns):
    t, d = x2.shape
    dp = p2d.shape[1]
    return pl.pallas_call(
        functools.partial(_ple_kernel, tm=tm, tn=tn, ns=ns, nj=d // tn),
        out_shape=jax.ShapeDtypeStruct((t, d), F32),
        grid_spec=pltpu.PrefetchScalarGridSpec(
            num_scalar_prefetch=2,
            grid=(t // tm, d // tn),
            in_specs=[
                pl.BlockSpec((tm, d), lambda m, j, d1, d2: (m, 0)),
                pl.BlockSpec((tm, LANES), lambda m, j, d1, d2: (m, 0)),
                pl.BlockSpec(memory_space=pl.ANY),
                pl.BlockSpec((1, d), lambda m, j, d1, d2: (0, 0)),
                pl.BlockSpec((tm, dp), lambda m, j, d1, d2: (m, 0)),
                pl.BlockSpec((d, d), lambda m, j, d1, d2: (0, 0), **_RESIDENT),
                pl.BlockSpec((dp, d), lambda m, j, d1, d2: (0, 0), **_RESIDENT),
                pl.BlockSpec((1, d), lambda m, j, d1, d2: (0, 0)),
            ],
            out_specs=pl.BlockSpec((tm, d), lambda m, j, d1, d2: (m, 0)),
            scratch_shapes=[
                pltpu.VMEM((2, 2, tm * ns, LANES), yb.dtype),
                pltpu.SemaphoreType.DMA((2,)),
                pltpu.VMEM((tm, d), F32),
                pltpu.VMEM((tm, d), BF16),
                pltpu.VMEM((tm, dp), BF16),
            ],
        ),
        compiler_params=_params(("arbitrary", "arbitrary")),
        name="combine_ple_final",
    )(dest1, dest2, x2, moe_gates, yb, g_ple, p2d, w_gate, w_proj, g_final)


def _tile(n, pref):
    return pref if n % pref == 0 else n


def _layer(x2d, p2d, batch, seq, w_in, b_forget, w_branch_fox, w_branch_sb, w_mix_out, g_mix, g_ffn,
           w_group, w_expert, w_gate, w_up, w_down, g_ple, w_ple_proj, w_ple_gate, g_final):
    t, d = x2d.shape
    n_heads = b_forget.shape[0]
    w_att = n_heads * HEAD_DIM
    n_exp = w_expert.shape[1]

    qs = LOG2E * HEAD_DIM ** -0.5
    w_t, wf_t = _wprep(w_in.T, w_att, n_heads, qs, 512)
    b_row = jnp.pad(b_forget, (0, LANES - n_heads)).reshape(1, LANES)
    col_qa, col_ka, col_qb, col_kb = 0, w_att, 2 * w_att, 3 * w_att
    col_ga, col_gb = 4 * w_att, 4 * w_att + d
    row_va, row_vb = 0, w_att

    tm = _tile(t, 1024)
    tn = _tile(d, 1024)
    proj, vt, f_tok = _inproj(x2d, g_mix.reshape(1, d), w_t, wf_t, 2 * w_att, tm, _tile(d // 2, 1024))
    c = _forget_cumsum(f_tok, b_row, batch, n_heads)

    tq = _tile(seq, 512)
    hp = 4
    o_a = _fox_attention(proj, vt, c, batch, seq, n_heads, col_qa, col_ka, row_va, tq, hp)
    o_b = _sb_attention(proj, vt, batch, seq, n_heads, col_qb, col_kb, row_vb, tq, _tile(seq, 256), hp)

    mixed = _merge(o_a, o_b, w_branch_fox, w_branch_sb, proj, col_ga, col_gb, tm, tn)
    x2 = _mixout(mixed, w_mix_out, x2d, tm, tn)

    w_r = jnp.concatenate([w_group, w_expert], axis=1)
    w_r = jnp.pad(w_r, ((0, 0), (0, LANES - w_r.shape[1])))
    w_r_hi = w_r.astype(BF16)
    w_r_lo = (w_r - w_r_hi.astype(F32)).astype(BF16)
    tr = _tile(t, 512)
    w_r_cat = jnp.concatenate([w_r_hi, w_r_lo], axis=1)
    h2, ids_rows, gates, counts = _router(x2, g_ffn.reshape(1, d), w_r_cat, tr, N_GROUPS, EXPERTS_PER_GROUP)

    bm = 256
    n_assign = 2 * t
    nb = n_assign // bm + n_exp
    dest, block_expert, first, nxt, pad_lo, pad_n, n_used = _plan(counts, ids_rows, n_exp, bm, nb)
    dest1, dest2 = dest[0], dest[1]

    td = _tile(t, 2048)
    ns = d // 2 // LANES
    xb = _dispatch(h2, dest1, dest2, pad_lo, pad_n, n_used, nb, bm, td, ns)
    yb = _experts(xb, w_gate, w_up, w_down, block_expert, first, nxt, n_used, bm, ns)

    tp = _tile(t, 512)
    return _combine_ple_final(x2, gates, yb, dest1, dest2, g_ple.reshape(1, d), p2d, w_ple_gate, w_ple_proj,
                              g_final.reshape(1, d), tp, tn, ns)


def kernel(x, p, w_in, b_forget, w_branch_fox, w_branch_sb, w_mix_out, g_mix, g_ffn, w_group, w_expert,
           w_gate, w_up, w_down, g_ple, w_ple_proj, w_ple_gate, g_final):
    b, s, d = x.shape
    depth = w_in.shape[0]
    assert depth == 1, "the final norm is fused into the single layer"
    x2d = x.reshape(b * s, d)
    out = _layer(x2d, p[0].reshape(b * s, -1), b, s, w_in[0], b_forget[0], w_branch_fox[0], w_branch_sb[0],
                 w_mix_out[0], g_mix[0], g_ffn[0], w_group[0], w_expert[0], w_gate[0], w_up[0], w_down[0],
                 g_ple[0], w_ple_proj[0], w_ple_gate[0], g_final)
    return out.reshape(b, s, d)
```

```python
import functools

import jax
import jax.numpy as jnp
from jax import lax
from jax.experimental import pallas as pl
from jax.experimental.pallas import tpu as pltpu

F32 = jnp.float32
BF16 = jnp.bfloat16

HEAD_DIM = 128
N_GROUPS = 4
EXPERTS_PER_GROUP = 8
EPS = 1e-6
LANES = 128
SUBLANES = 8
VMEM_LIMIT = 56 * 1024 * 1024

LOG2E = 1.4426950408889634
DMA_UNROLL = 8
R_DONE = 160.0

NT_DIMS = (((1,), (1,)), ((), ()))


def _params(sem):
    return pltpu.CompilerParams(dimension_semantics=sem, vmem_limit_bytes=VMEM_LIMIT)


def _rmsnorm_f32(x, g):
    ms = jnp.mean(x * x, axis=-1, keepdims=True)
    return x * lax.rsqrt(ms + EPS) * g


def _pack_bf16_pairs(x):
    n = x.shape[1] // 2
    bits = lax.bitcast_convert_type(x.astype(BF16).astype(F32), jnp.uint32)
    return (bits[:, :n] >> 16) | (bits[:, n:] & jnp.uint32(0xFFFF0000))


def _unpack_bf16_pairs(w):
    lo = lax.bitcast_convert_type(w << 16, F32)
    hi = lax.bitcast_convert_type(w & jnp.uint32(0xFFFF0000), F32)
    return jnp.concatenate([lo, hi], axis=1)


def _store_row_tiles(ref, words):
    n = words.shape[0]
    ns = words.shape[1] // LANES
    for c in range(ns):
        ref[pl.ds(c, n, stride=ns), :] = words[:, c * LANES:(c + 1) * LANES]


def _load_row_tiles(ref, ns):
    n = ref.shape[0] // ns
    return jnp.concatenate([ref[pl.ds(c, n, stride=ns), :] for c in range(ns)], axis=1)


def _wprep_kernel(off_ref, scale_ref, w_ref, f_ref, wt_ref, wf_ref, *, n_heads):
    i = pl.program_id(0)
    wt_ref[...] = (w_ref[...] * scale_ref[i]).astype(BF16)

    @pl.when(i == 0)
    def _():
        row = lax.broadcasted_iota(jnp.int32, wf_ref.shape, 0)
        f_rows = jnp.concatenate([f_ref[...]] * (wf_ref.shape[0] // f_ref.shape[0]), axis=0)
        wf_ref[...] = jnp.where(row < n_heads, f_rows, 0.0).astype(BF16)


def _wprep(w_t, w_att, n_heads, qs, rb):
    n_in, d = w_t.shape
    off_b = 3 * w_att + n_heads
    segments = [(2 * w_att, w_att, 1.0), (off_b + 2 * w_att, w_att, 1.0),
                (0, w_att, qs), (w_att, w_att, 1.0),
                (off_b, w_att, qs), (off_b + w_att, w_att, 1.0),
                (off_b + 3 * w_att, n_in - off_b - 3 * w_att, 1.0)]
    offs, scales = [], []
    for start, length, scale in segments:
        assert length % rb == 0 and start % SUBLANES == 0
        for k in range(length // rb):
            offs.append((start + k * rb) // SUBLANES)
            scales.append(scale)
    n_out = len(offs) * rb
    f_blk = SUBLANES
    assert n_heads <= f_blk and (3 * w_att) % f_blk == 0
    return pl.pallas_call(
        functools.partial(_wprep_kernel, n_heads=n_heads),
        out_shape=(jax.ShapeDtypeStruct((n_out, d), BF16), jax.ShapeDtypeStruct((LANES, d), BF16)),
        grid_spec=pltpu.PrefetchScalarGridSpec(
            num_scalar_prefetch=2,
            grid=(len(offs),),
            in_specs=[
                pl.BlockSpec((pl.Element(rb), pl.Element(d)), lambda i, off, sc: (off[i] * SUBLANES, 0)),
                pl.BlockSpec((pl.Element(f_blk), pl.Element(d)), lambda i, off, sc: (3 * w_att, 0)),
            ],
            out_specs=(
                pl.BlockSpec((rb, d), lambda i, off, sc: (i, 0)),
                pl.BlockSpec((LANES, d), lambda i, off, sc: (0, 0)),
            ),
        ),
        compiler_params=_params(("arbitrary",)),
        name="inproj_weights",
    )(jnp.asarray(offs, jnp.int32), jnp.asarray(scales, F32), w_t, w_t)


def _inproj_kernel(x_ref, g_ref, w_ref, wf_ref, o_ref, vt_ref, f_ref, h_scr, *, nvt):
    j = pl.program_id(1)

    @pl.when(j == 0)
    def _():
        hb = _rmsnorm_f32(x_ref[...], g_ref[...]).astype(BF16)
        h_scr[...] = hb
        f_ref[...] = lax.dot_general(hb, wf_ref[...], NT_DIMS, preferred_element_type=F32)

    @pl.when(j < nvt)
    def _():
        vt_ref[...] = lax.dot_general(w_ref[...], h_scr[...], NT_DIMS,
                                      preferred_element_type=F32).astype(vt_ref.dtype)

    @pl.when(j >= nvt)
    def _():
        o_ref[...] = lax.dot_general(h_scr[...], w_ref[...], NT_DIMS,
                                     preferred_element_type=F32).astype(o_ref.dtype)


def _inproj(x2d, g, w_t, wf_t, n_v, tm, tn):
    t, d = x2d.shape
    n = w_t.shape[0] - n_v
    nvt = n_v // tn
    return pl.pallas_call(
        functools.partial(_inproj_kernel, nvt=nvt),
        out_shape=(jax.ShapeDtypeStruct((t, n), BF16), jax.ShapeDtypeStruct((n_v, t), BF16),
                   jax.ShapeDtypeStruct((t, LANES), F32)),
        grid=(t // tm, nvt + n // tn),
        in_specs=[
            pl.BlockSpec((tm, d), lambda m, j: (m, 0)),
            pl.BlockSpec((1, d), lambda m, j: (0, 0)),
            pl.BlockSpec((tn, d), lambda m, j: (j, 0)),
            pl.BlockSpec((LANES, d), lambda m, j: (0, 0)),
        ],
        out_specs=(
            pl.BlockSpec((tm, tn), lambda m, j: (m, jnp.maximum(j - nvt, 0))),
            pl.BlockSpec((tn, tm), lambda m, j: (jnp.minimum(j, nvt - 1), m)),
            pl.BlockSpec((tm, LANES), lambda m, j: (m, 0)),
        ),
        scratch_shapes=[pltpu.VMEM((tm, d), BF16)],
        compiler_params=_params(("parallel", "arbitrary")),
        name="inproj",
    )(x2d, g, w_t, wf_t)


def _split3(x):
    p1 = x.astype(BF16)
    r1 = x - p1.astype(F32)
    p2 = r1.astype(BF16)
    p3 = (r1 - p2.astype(F32)).astype(BF16)
    return p1, p2, p3


def _cumsum_kernel(f_ref, b_ref, lt_ref, c_ref, carry_scr, *, chunk, n_heads):
    @pl.when(pl.program_id(1) == 0)
    def _():
        carry_scr[...] = jnp.zeros(carry_scr.shape, F32)

    rows = f_ref.shape[0]
    lt = lt_ref[...]
    carry = carry_scr[...]
    for i in range(rows // chunk):
        lf = jax.nn.log_sigmoid(f_ref[i * chunk:(i + 1) * chunk, :] + b_ref[...])
        p1, p2, p3 = _split3(lf)
        cs = (jnp.dot(lt, p1, preferred_element_type=F32)
              + jnp.dot(lt, p2, preferred_element_type=F32)
              + jnp.dot(lt, p3, preferred_element_type=F32)) + carry
        for h in range(n_heads):
            c_ref[h, i * chunk:(i + 1) * chunk, :] = jnp.broadcast_to(cs[:, h:h + 1] * LOG2E, (chunk, LANES))
        carry = cs[chunk - 1:chunk, :]
    carry_scr[...] = carry


def _forget_cumsum(f_tok, b_row, batch, n_heads):
    t = f_tok.shape[0]
    s = t // batch
    rows = min(1024, s)
    chunk = min(256, s)
    lt = jnp.tril(jnp.ones((chunk, chunk), BF16))
    nr = s // rows
    return pl.pallas_call(
        functools.partial(_cumsum_kernel, chunk=chunk, n_heads=n_heads),
        out_shape=jax.ShapeDtypeStruct((n_heads, t, LANES), F32),
        grid=(batch, nr),
        in_specs=[
            pl.BlockSpec((rows, LANES), lambda b, r: (b * nr + r, 0)),
            pl.BlockSpec((1, LANES), lambda b, r: (0, 0)),
            pl.BlockSpec((chunk, chunk), lambda b, r: (0, 0)),
        ],
        out_specs=pl.BlockSpec((n_heads, rows, LANES), lambda b, r: (0, b * nr + r, 0)),
        scratch_shapes=[pltpu.VMEM((1, LANES), F32)],
        compiler_params=_params(("parallel", "arbitrary")),
        name="forget_cumsum",
    )(f_tok, b_row, lt)


def _fox_kernel(q_ref, k_ref, vt_ref, c_ref, o_ref, m_scr, l_scr, acc_scr, *, tq, tk, hp):
    i = pl.program_id(2)
    m_scr[...] = jnp.full(m_scr.shape, -jnp.inf, F32)
    l_scr[...] = jnp.zeros(l_scr.shape, F32)
    acc_scr[...] = jnp.zeros(acc_scr.shape, F32)

    def tile(j, masked, koff=0, nk=tk, qlo=0):
        start = pl.multiple_of(j * tk + koff, nk)
        nq = tq - qlo
        if masked:
            key = j * tk + koff + lax.broadcasted_iota(jnp.int32, (nk, nq), 0)
            qry = i * tq + qlo + lax.broadcasted_iota(jnp.int32, (nk, nq), 1)
            keep = key <= qry

        def scores(hh):
            lanes = slice(hh * HEAD_DIM, (hh + 1) * HEAD_DIM)
            kj = k_ref[pl.ds(start, nk), lanes]
            return lax.dot_general(kj, q_ref[qlo:tq, lanes], NT_DIMS, preferred_element_type=F32)

        s_all = [scores(hh) for hh in range(hp)]
        for hh in range(hp):
            s = s_all[hh]
            cj = c_ref[hh, pl.ds(start, nk), :]
            s = s - jnp.concatenate([cj] * (nq // LANES), axis=1)
            if masked:
                s = jnp.where(keep, s, -jnp.inf)
            m_prev = m_scr[hh, :, qlo:tq]
            m_new = jnp.maximum(m_prev, jnp.max(s, axis=0, keepdims=True))
            alpha = jnp.exp2(m_prev - m_new)
            p = jnp.exp2(s - m_new)
            l_scr[hh, :, qlo:tq] = alpha * l_scr[hh, :, qlo:tq] + jnp.sum(p, axis=0, keepdims=True)
            vtj = vt_ref[hh * HEAD_DIM:(hh + 1) * HEAD_DIM, pl.ds(start, nk)]
            acc_scr[hh, :, qlo:tq] = (alpha * acc_scr[hh, :, qlo:tq]
                                      + jnp.dot(vtj, p.astype(BF16), preferred_element_type=F32))
            m_scr[hh, :, qlo:tq] = m_new

    def body(j, carry):
        tile(j, False)
        return carry

    lax.fori_loop(0, i, body, 0)
    half = tk // 2
    tile(i, True, koff=0, nk=half, qlo=0)
    tile(i, True, koff=half, nk=half, qlo=tq - half)
    for hh in range(hp):
        o = acc_scr[hh] / l_scr[hh]
        o_ref[:, hh * HEAD_DIM:(hh + 1) * HEAD_DIM] = o.T.astype(o_ref.dtype)


def _fox_attention(proj, vt, c, batch, seq, n_heads, col_q, col_k, row_v, tq, hp):
    t = proj.shape[0]
    nq = seq // tq
    wb = hp * HEAD_DIM
    cq, ck, rv = col_q // wb, col_k // wb, row_v // wb
    return pl.pallas_call(
        functools.partial(_fox_kernel, tq=tq, tk=tq, hp=hp),
        out_shape=jax.ShapeDtypeStruct((t, n_heads * HEAD_DIM), BF16),
        grid=(batch, n_heads // hp, nq),
        in_specs=[
            pl.BlockSpec((tq, wb), lambda b, g, i: (b * nq + i, cq + g)),
            pl.BlockSpec((seq, wb), lambda b, g, i: (b, ck + g)),
            pl.BlockSpec((wb, seq), lambda b, g, i: (rv + g, b)),
            pl.BlockSpec((hp, seq, LANES), lambda b, g, i: (g, b, 0)),
        ],
        out_specs=pl.BlockSpec((tq, wb), lambda b, g, i: (b * nq + i, g)),
        scratch_shapes=[
            pltpu.VMEM((hp, 1, tq), F32),
            pltpu.VMEM((hp, 1, tq), F32),
            pltpu.VMEM((hp, HEAD_DIM, tq), F32),
        ],
        compiler_params=_params(("parallel", "parallel", "arbitrary")),
        name="fox_attention",
    )(proj, proj, vt, c)


def _sb_kernel(q_ref, k_ref, vt_ref, tri_ref, o_ref, r_scr, acc_scr, *, tq, tk, hp):
    i = pl.program_id(2)
    r_scr[...] = jnp.zeros(r_scr.shape, F32)
    acc_scr[...] = jnp.zeros(acc_scr.shape, F32)

    sign_bit = jnp.uint32(0x80000000)

    def tile(j, masked, qlo=0, qhi=tq):
        start = pl.multiple_of(j * tk, tk)
        nq = qhi - qlo
        if masked:
            key = j * tk + lax.broadcasted_iota(jnp.int32, (tk, nq), 0)
            qry = i * tq + qlo + lax.broadcasted_iota(jnp.int32, (tk, nq), 1)
            mask = key < qry

        def scores(hh):
            lanes = slice(hh * HEAD_DIM, (hh + 1) * HEAD_DIM)
            kj = k_ref[pl.ds(start, tk), lanes]
            return lax.dot_general(kj, q_ref[qlo:qhi, lanes], NT_DIMS, preferred_element_type=F32)

        u_all = [scores(hh) for hh in range(hp)]
        for hh in range(hp):
            u = u_all[hh]
            minus_abs = lax.bitcast_convert_type(lax.bitcast_convert_type(u, jnp.uint32) | sign_bit, F32)
            sp = jnp.maximum(u, 0.0) + jnp.log2(1.0 + jnp.exp2(minus_abs))
            if masked:
                sp = jnp.where(mask, sp, 0.0)
            w = jnp.dot(tri_ref[...], sp.astype(BF16), preferred_element_type=F32)
            a = jnp.exp2(u - ((sp + w) + r_scr[hh, :, qlo:qhi]))
            if masked:
                a = jnp.where(mask, a, 0.0)
            vtj = vt_ref[hh * HEAD_DIM:(hh + 1) * HEAD_DIM, pl.ds(start, tk)]
            acc_scr[hh, :, qlo:qhi] += jnp.dot(vtj, a.astype(BF16), preferred_element_type=F32)
            r_scr[hh, :, qlo:qhi] += jnp.sum(sp, axis=0, keepdims=True)

    ratio = max(tq // tk, 1)
    n_full = (i * tq) // tk
    for d in reversed(range(ratio)):
        tile(n_full + d, True, qlo=d * tk if tq > tk else 0)

    def live():
        return (jnp.min(r_scr[...]) <= R_DONE).astype(jnp.int32)

    def cond(carry):
        it, alive = carry
        return (it < n_full) & (alive > 0)

    def body(carry):
        it, _ = carry
        tile(n_full - 1 - it, False)
        return it + 1, live()

    lax.while_loop(cond, body, (jnp.int32(0), live()))
    for hh in range(hp):
        o_ref[:, hh * HEAD_DIM:(hh + 1) * HEAD_DIM] = acc_scr[hh].T.astype(o_ref.dtype)


def _sb_attention(proj, vt, batch, seq, n_heads, col_q, col_k, row_v, tq, tk, hp):
    t = proj.shape[0]
    nq = seq // tq
    wb = hp * HEAD_DIM
    cq, ck, rv = col_q // wb, col_k // wb, row_v // wb
    tri = jnp.triu(jnp.ones((tk, tk), BF16), 1)
    return pl.pallas_call(
        functools.partial(_sb_kernel, tq=tq, tk=tk, hp=hp),
        out_shape=jax.ShapeDtypeStruct((t, n_heads * HEAD_DIM), BF16),
        grid=(batch, n_heads // hp, nq),
        in_specs=[
            pl.BlockSpec((tq, wb), lambda b, g, i: (b * nq + i, cq + g)),
            pl.BlockSpec((seq, wb), lambda b, g, i: (b, ck + g)),
            pl.BlockSpec((wb, seq), lambda b, g, i: (rv + g, b)),
            pl.BlockSpec((tk, tk), lambda b, g, i: (0, 0)),
        ],
        out_specs=pl.BlockSpec((tq, wb), lambda b, g, i: (b * nq + i, g)),
        scratch_shapes=[
            pltpu.VMEM((hp, 1, tq), F32),
            pltpu.VMEM((hp, HEAD_DIM, tq), F32),
        ],
        compiler_params=_params(("parallel", "parallel", "arbitrary")),
        name="sb_attention",
    )(proj, proj, vt, tri)


_RESIDENT = dict(pipeline_mode=pl.Buffered(1))


def _merge_kernel(oa_ref, ob_ref, wa_ref, wb_ref, ga_ref, gb_ref, o_ref, *, tn):
    col = pl.multiple_of(pl.program_id(1) * tn, tn)
    ya = jnp.dot(oa_ref[...], wa_ref[:, pl.ds(col, tn)].astype(BF16), preferred_element_type=F32)
    yb = jnp.dot(ob_ref[...], wb_ref[:, pl.ds(col, tn)].astype(BF16), preferred_element_type=F32)
    ga = jax.nn.sigmoid(ga_ref[...].astype(F32))
    gb = jax.nn.sigmoid(gb_ref[...].astype(F32))
    o_ref[...] = (ga * ya + gb * yb).astype(o_ref.dtype)


def _merge(o_a, o_b, wa, wb, proj, col_ga, col_gb, tm, tn):
    t, ka = o_a.shape
    kb = o_b.shape[1]
    d = wa.shape[1]
    ca, cb = col_ga // tn, col_gb // tn
    return pl.pallas_call(
        functools.partial(_merge_kernel, tn=tn),
        out_shape=jax.ShapeDtypeStruct((t, d), BF16),
        grid=(t // tm, d // tn),
        in_specs=[
            pl.BlockSpec((tm, ka), lambda m, j: (m, 0)),
            pl.BlockSpec((tm, kb), lambda m, j: (m, 0)),
            pl.BlockSpec((ka, d), lambda m, j: (0, 0), **_RESIDENT),
            pl.BlockSpec((kb, d), lambda m, j: (0, 0), **_RESIDENT),
            pl.BlockSpec((tm, tn), lambda m, j: (m, ca + j)),
            pl.BlockSpec((tm, tn), lambda m, j: (m, cb + j)),
        ],
        out_specs=pl.BlockSpec((tm, tn), lambda m, j: (m, j)),
        compiler_params=_params(("parallel", "arbitrary")),
        name="branch_merge",
    )(o_a, o_b, wa, wb, proj, proj)


def _mixout_kernel(a_ref, w_ref, x_ref, o_ref, *, tn):
    col = pl.multiple_of(pl.program_id(1) * tn, tn)
    w = w_ref[:, pl.ds(col, tn)].astype(BF16)
    o_ref[...] = x_ref[...] + jnp.dot(a_ref[...], w, preferred_element_type=F32)


def _mixout(mixed, w, x2d, tm, tn):
    t, k = mixed.shape
    d = w.shape[1]
    return pl.pallas_call(
        functools.partial(_mixout_kernel, tn=tn),
        out_shape=jax.ShapeDtypeStruct((t, d), F32),
        grid=(t // tm, d // tn),
        in_specs=[
            pl.BlockSpec((tm, k), lambda m, j: (m, 0)),
            pl.BlockSpec((k, d), lambda m, j: (0, 0), **_RESIDENT),
            pl.BlockSpec((tm, tn), lambda m, j: (m, j)),
        ],
        out_specs=pl.BlockSpec((tm, tn), lambda m, j: (m, j)),
        compiler_params=_params(("parallel", "arbitrary")),
        name="mix_out",
    )(mixed, w, x2d)


def _router_kernel(x_ref, g_ref, wc_ref, lt_ref, sel_ref, h_ref, ids_ref, gate_ref, cnt_ref, carry_scr,
                   *, n_groups, per_group):
    @pl.when(pl.program_id(0) == 0)
    def _():
        carry_scr[...] = jnp.zeros(carry_scr.shape, F32)

    h = _rmsnorm_f32(x_ref[...], g_ref[...])
    hh = h.astype(BF16)
    _store_row_tiles(h_ref, _pack_bf16_pairs(h))
    hl = (h - hh.astype(F32)).astype(BF16)
    both = jnp.dot(hh, wc_ref[...], preferred_element_type=F32)
    logits = (both[:, :LANES] + jnp.dot(hl, wc_ref[:, :LANES], preferred_element_type=F32)) + both[:, LANES:]
    tm = logits.shape[0]
    n_exp = n_groups * per_group
    lane = lax.broadcasted_iota(jnp.int32, (tm, LANES), 1)
    lane_f = lane.astype(F32)
    neg_inf = -jnp.inf

    def first_argmax(vals, valid):
        vmax = jnp.max(jnp.where(valid, vals, neg_inf), axis=-1, keepdims=True)
        idx = jnp.min(jnp.where(valid & (vals == vmax), lane_f, float(LANES)), axis=-1, keepdims=True)
        return vmax, idx.astype(jnp.int32)

    is_group = lane < n_groups
    gmax, gsel = first_argmax(logits, is_group)
    gsum = jnp.sum(jnp.where(is_group, jnp.exp(logits - gmax), 0.0), axis=-1, keepdims=True)
    p_group = 1.0 / gsum
    lo = n_groups + gsel * per_group
    in_group = (lane >= lo) & (lane < lo + per_group)
    v1, i1 = first_argmax(logits, in_group)
    v2, i2 = first_argmax(logits, in_group & (lane != i1))
    e2w = jnp.exp(v2 - v1)
    gate1 = (1.0 / (1.0 + e2w)) * p_group
    gate2 = (e2w / (1.0 + e2w)) * p_group
    e1 = i1 - n_groups
    e2 = i2 - n_groups

    onehot = ((lane == e1) | (lane == e2)).astype(BF16)
    before = jnp.dot(lt_ref[...], onehot, preferred_element_type=F32) + carry_scr[...]
    rank1 = jnp.sum(jnp.where(lane == e1, before, 0.0), axis=-1, keepdims=True).astype(jnp.int32)
    rank2 = jnp.sum(jnp.where(lane == e2, before, 0.0), axis=-1, keepdims=True).astype(jnp.int32)
    carry_new = carry_scr[...] + jnp.sum(onehot.astype(F32), axis=0, keepdims=True)
    carry_scr[...] = carry_new

    cols = jnp.where(lane == 0, e1, jnp.where(lane == 1, e2, jnp.where(lane == 2, rank1 >> 7,
           jnp.where(lane == 3, rank1 & 127, jnp.where(lane == 4, rank2 >> 7,
           jnp.where(lane == 5, rank2 & 127, 0))))))
    ids_ref[...] = lax.dot_general(sel_ref[...], cols.astype(F32).astype(BF16), NT_DIMS,
                                   preferred_element_type=F32)
    gate_ref[...] = jnp.where(lane == 0, gate1, jnp.where(lane == 1, gate2, 0.0))
    cnt_ref[...] = jnp.broadcast_to(carry_new, cnt_ref.shape).astype(jnp.int32)
    del n_exp


def _router(x2, g, w_cat, tm, n_groups, per_group):
    t, d = x2.shape
    lt = jnp.tril(jnp.ones((tm, tm), BF16), -1)
    sel = jnp.eye(8, LANES, dtype=BF16)
    return pl.pallas_call(
        functools.partial(_router_kernel, n_groups=n_groups, per_group=per_group),
        out_shape=(
            jax.ShapeDtypeStruct((t * (d // 2 // LANES), LANES), jnp.uint32),
            jax.ShapeDtypeStruct((8, t), F32),
            jax.ShapeDtypeStruct((t, LANES), F32),
            jax.ShapeDtypeStruct((8, LANES), jnp.int32),
        ),
        grid=(t // tm,),
        in_specs=[
            pl.BlockSpec((tm, d), lambda m: (m, 0)),
            pl.BlockSpec((1, d), lambda m: (0, 0)),
            pl.BlockSpec((d, 2 * LANES), lambda m: (0, 0)),
            pl.BlockSpec((tm, tm), lambda m: (0, 0)),
            pl.BlockSpec((8, LANES), lambda m: (0, 0)),
        ],
        out_specs=(
            pl.BlockSpec((tm * (d // 2 // LANES), LANES), lambda m: (m, 0)),
            pl.BlockSpec((8, tm), lambda m: (0, m)),
            pl.BlockSpec((tm, LANES), lambda m: (m, 0)),
            pl.BlockSpec((8, LANES), lambda m: (0, 0)),
        ),
        scratch_shapes=[pltpu.VMEM((1, LANES), F32)],
        compiler_params=_params(("arbitrary",)),
        name="router",
    )(x2, g, w_cat, lt, sel)


def _plan_kernel(cnt_ref, ids_ref, dest_ref, be_ref, first_ref, nxt_ref, plo_ref, pn_ref, nu_ref, ps_scr,
                 *, n_exp, bm, nb):
    shift = bm.bit_length() - 1

    def clear(k, c):
        first_ref[k] = 0
        nxt_ref[k] = -1
        return c

    lax.fori_loop(0, nb, clear, 0)

    def per_expert(e, carry):
        blk_start, last, prev_start = carry
        cnt = cnt_ref[0, e]
        n = lax.shift_right_logical(cnt + (bm - 1), shift)
        ps_scr[e] = blk_start * bm
        plo_ref[e] = blk_start * bm + cnt
        pn_ref[e] = n * bm - cnt

        def fill(k, c):
            be_ref[blk_start + k] = e
            return c

        lax.fori_loop(0, n, fill, 0)

        @pl.when(n > 0)
        def _():
            first_ref[blk_start] = 1

            @pl.when(prev_start >= 0)
            def _():
                nxt_ref[prev_start] = e

        used = n > 0
        return blk_start + n, jnp.where(used, e, last), jnp.where(used, blk_start, prev_start)

    n_used, last, _ = lax.fori_loop(0, n_exp, per_expert, (jnp.int32(0), jnp.int32(0), jnp.int32(-1)))
    nu_ref[0] = n_used

    def tail(k, c):
        be_ref[k] = last
        return c

    lax.fori_loop(n_used, nb, tail, 0)

    e1 = ids_ref[0:1, :]
    e2 = ids_ref[1:2, :]
    p1 = jnp.zeros_like(e1)
    p2 = jnp.zeros_like(e2)
    for e in range(n_exp):
        ps = ps_scr[e].astype(F32)
        p1 = jnp.where(e1 == e, ps, p1)
        p2 = jnp.where(e2 == e, ps, p2)
    dest_ref[...] = jnp.zeros(dest_ref.shape, jnp.int32)
    dest_ref[0:1, :] = (p1 + ids_ref[2:3, :] * 128.0 + ids_ref[3:4, :]).astype(jnp.int32)
    dest_ref[1:2, :] = (p2 + ids_ref[4:5, :] * 128.0 + ids_ref[5:6, :]).astype(jnp.int32)


def _plan(counts, ids_rows, n_exp, bm, nb):
    t = ids_rows.shape[1]
    return pl.pallas_call(
        functools.partial(_plan_kernel, n_exp=n_exp, bm=bm, nb=nb),
        out_shape=(
            jax.ShapeDtypeStruct((8, t), jnp.int32),
            jax.ShapeDtypeStruct((nb,), jnp.int32),
            jax.ShapeDtypeStruct((nb,), jnp.int32),
            jax.ShapeDtypeStruct((nb,), jnp.int32),
            jax.ShapeDtypeStruct((n_exp,), jnp.int32),
            jax.ShapeDtypeStruct((n_exp,), jnp.int32),
            jax.ShapeDtypeStruct((1,), jnp.int32),
        ),
        in_specs=[
            pl.BlockSpec(memory_space=pltpu.SMEM),
            pl.BlockSpec(memory_space=pltpu.VMEM),
        ],
        out_specs=(
            pl.BlockSpec(memory_space=pltpu.VMEM),
            pl.BlockSpec(memory_space=pltpu.SMEM),
            pl.BlockSpec(memory_space=pltpu.SMEM),
            pl.BlockSpec(memory_space=pltpu.SMEM),
            pl.BlockSpec(memory_space=pltpu.SMEM),
            pl.BlockSpec(memory_space=pltpu.SMEM),
            pl.BlockSpec(memory_space=pltpu.SMEM),
        ),
        scratch_shapes=[pltpu.SMEM((n_exp,), jnp.int32)],
        name="moe_plan",
    )(counts, ids_rows)


def _dispatch_kernel(d1_ref, d2_ref, plo_ref, pn_ref, nu_ref, h_ref, xb_ref, zbuf, sem, zsem,
                     *, tm, ns, bm, nb, n_exp):
    base = pl.program_id(0) * tm

    def zero_copies(fn):
        def per_expert(e, carry):
            pos = plo_ref[e]
            rem = pn_ref[e]
            chunk = bm // 2
            while chunk >= 1:
                hit = (rem & chunk) != 0

                @pl.when(hit)
                def _(pos=pos, chunk=chunk):
                    dst = xb_ref.at[pl.ds(pl.multiple_of(pos * ns, ns), chunk * ns)]
                    fn(pltpu.make_async_copy(zbuf.at[pl.ds(0, chunk * ns)], dst, zsem))

                pos = pos + jnp.where(hit, chunk, 0)
                chunk //= 2
            return carry

        lax.fori_loop(0, n_exp, per_expert, 0)

        def per_tail_block(b, carry):
            dst = xb_ref.at[pl.ds(pl.multiple_of(b * (bm * ns), bm * ns), bm * ns)]
            fn(pltpu.make_async_copy(zbuf, dst, zsem))
            return carry

        lax.fori_loop(nu_ref[0], nb, per_tail_block, 0)

    @pl.when(pl.program_id(0) == 0)
    def _():
        zbuf[...] = jnp.zeros(zbuf.shape, zbuf.dtype)
        zero_copies(lambda cp: cp.start())

    def row_copy(r, dest):
        src = h_ref.at[pl.ds(pl.multiple_of(r * ns, ns), ns)]
        return pltpu.make_async_copy(src, xb_ref.at[pl.ds(pl.multiple_of(dest * ns, ns), ns)], sem)

    def start(r, carry):
        row_copy(r, d1_ref[base + r]).start()
        row_copy(r, d2_ref[base + r]).start()
        return carry

    lax.fori_loop(0, tm, start, 0, unroll=DMA_UNROLL)
    for _ in range(2):
        pltpu.make_async_copy(h_ref, xb_ref.at[pl.ds(0, tm * ns)], sem).wait()

    @pl.when(pl.program_id(0) == pl.num_programs(0) - 1)
    def _():
        zero_copies(lambda cp: cp.wait())


def _dispatch(h2p, dest1, dest2, pad_lo, pad_n, n_used, nb, bm, tm, ns):
    t = h2p.shape[0] // ns
    return pl.pallas_call(
        functools.partial(_dispatch_kernel, tm=tm, ns=ns, bm=bm, nb=nb, n_exp=pad_lo.shape[0]),
        out_shape=jax.ShapeDtypeStruct((nb * bm * ns, LANES), h2p.dtype),
        grid_spec=pltpu.PrefetchScalarGridSpec(
            num_scalar_prefetch=5,
            grid=(t // tm,),
            in_specs=[pl.BlockSpec((tm * ns, LANES), lambda m, d1, d2, plo, pn, nu: (m, 0))],
            out_specs=pl.BlockSpec(memory_space=pl.ANY),
            scratch_shapes=[pltpu.VMEM((bm * ns, LANES), h2p.dtype), pltpu.SemaphoreType.DMA,
                            pltpu.SemaphoreType.DMA],
        ),
        compiler_params=_params(("arbitrary",)),
        name="moe_dispatch",
    )(dest1, dest2, pad_lo, pad_n, n_used, h2p)


def _experts_kernel(be_ref, first_ref, nxt_ref, nu_ref, x_ref, wg_hbm, wu_hbm, wd_hbm, y_ref,
                    wg_buf, wu_buf, wd_buf, sem, run_scr, *, ns):
    b = pl.program_id(0)

    def weight_copies(e, slot):
        return (pltpu.make_async_copy(wg_hbm.at[e], wg_buf.at[slot], sem.at[slot]),
                pltpu.make_async_copy(wu_hbm.at[e], wu_buf.at[slot], sem.at[slot]),
                pltpu.make_async_copy(wd_hbm.at[e], wd_buf.at[slot], sem.at[slot]))

    @pl.when(b == 0)
    def _():
        run_scr[0] = 0
        for cp in weight_copies(be_ref[0], 0):
            cp.start()

    @pl.when((first_ref[b] == 1) & (b < nu_ref[0]))
    def _():
        run = run_scr[0]
        slot = run & 1
        for cp in weight_copies(be_ref[b], slot):
            cp.wait()

        @pl.when(nxt_ref[b] >= 0)
        def _():
            for cp in weight_copies(nxt_ref[b], 1 - slot):
                cp.start()

        run_scr[0] = run + 1

    @pl.when(b < nu_ref[0])
    def _():
        slot = (run_scr[0] - 1) & 1
        x = _unpack_bf16_pairs(_load_row_tiles(x_ref, ns)).astype(BF16)
        g = jnp.dot(x, wg_buf[slot].astype(BF16), preferred_element_type=F32)
        u = jnp.dot(x, wu_buf[slot].astype(BF16), preferred_element_type=F32)
        hid = (g * jax.nn.sigmoid(g) * u).astype(BF16)
        y = jnp.dot(hid, wd_buf[slot].astype(BF16), preferred_element_type=F32)
        _store_row_tiles(y_ref, _pack_bf16_pairs(y))


def _experts(xb, w_gate, w_up, w_down, block_expert, first, nxt, n_used, bm, ns):
    d, de = w_gate.shape[1:]
    nb = xb.shape[0] // (bm * ns)

    def row_map(b, be, fi, nx, nu):
        return (jnp.minimum(b, nu[0] - 1), 0)

    return pl.pallas_call(
        functools.partial(_experts_kernel, ns=ns),
        out_shape=jax.ShapeDtypeStruct(xb.shape, jnp.uint32),
        grid_spec=pltpu.PrefetchScalarGridSpec(
            num_scalar_prefetch=4,
            grid=(nb,),
            in_specs=[
                pl.BlockSpec((bm * ns, LANES), row_map),
                pl.BlockSpec(memory_space=pl.ANY),
                pl.BlockSpec(memory_space=pl.ANY),
                pl.BlockSpec(memory_space=pl.ANY),
            ],
            out_specs=pl.BlockSpec((bm * ns, LANES), row_map),
            scratch_shapes=[
                pltpu.VMEM((2, d, de), F32),
                pltpu.VMEM((2, d, de), F32),
                pltpu.VMEM((2, de, d), F32),
                pltpu.SemaphoreType.DMA((2,)),
                pltpu.SMEM((1,), jnp.int32),
            ],
        ),
        input_output_aliases={4: 0},
        compiler_params=_params(("arbitrary",)),
        name="moe_experts",
    )(block_expert, first, nxt, n_used, xb, w_gate, w_up, w_down)


def _ple_kernel(d1_ref, d2_ref, x_ref, mg_ref, yb_ref, g_ref, p_ref, wg_ref, wp_ref, gf_ref, o_ref,
                ybuf, sem, x3_scr, h_scr, p_scr, *, tm, tn, ns, nj):
    m = pl.program_id(0)
    j = pl.program_id(1)
    n_m = pl.num_programs(0)
    n_j = pl.num_programs(1)
    slot = m & 1

    def row_copies(blk, r, s):
        tok = blk * tm + r
        src1 = yb_ref.at[pl.ds(pl.multiple_of(d1_ref[tok] * ns, ns), ns)]
        src2 = yb_ref.at[pl.ds(pl.multiple_of(d2_ref[tok] * ns, ns), ns)]
        rows = pl.ds(pl.multiple_of(r * ns, ns), ns)
        return (pltpu.make_async_copy(src1, ybuf.at[s, 0, rows], sem.at[s]),
                pltpu.make_async_copy(src2, ybuf.at[s, 1, rows], sem.at[s]))

    def drain(s):
        for k in range(2):
            pltpu.make_async_copy(yb_ref.at[pl.ds(0, tm * ns)], ybuf.at[s, k], sem.at[s]).wait()

    @pl.when(j == 0)
    def _():
        @pl.when(m == 0)
        def _():
            def start(r, carry):
                for cp in row_copies(0, r, 0):
                    cp.start()
                return carry

            lax.fori_loop(0, tm, start, 0, unroll=DMA_UNROLL)

        drain(slot)
        mg = mg_ref[...]
        y1 = _unpack_bf16_pairs(_load_row_tiles(ybuf.at[slot, 0], ns))
        y2 = _unpack_bf16_pairs(_load_row_tiles(ybuf.at[slot, 1], ns))
        x3 = x_ref[...] + mg[:, 0:1] * y1 + mg[:, 1:2] * y2
        x3_scr[...] = x3
        h_scr[...] = _rmsnorm_f32(x3, g_ref[...]).astype(BF16)
        p_scr[...] = p_ref[...].astype(BF16)

    nxt = jnp.minimum(m + 1, n_m - 1)
    rows_per_step = tm // nj
    for r in range(rows_per_step):
        for cp in row_copies(nxt, j * rows_per_step + r, 1 - slot):
            cp.start()

    col = pl.multiple_of(j * tn, tn)
    gate = jnp.dot(h_scr[...], wg_ref[:, pl.ds(col, tn)].astype(BF16), preferred_element_type=F32)
    ple = jnp.dot(p_scr[...], wp_ref[:, pl.ds(col, tn)].astype(BF16), preferred_element_type=F32)
    o_ref[:, pl.ds(col, tn)] = x3_scr[:, pl.ds(col, tn)] + jax.nn.sigmoid(gate) * ple

    @pl.when(j == n_j - 1)
    def _():
        o_ref[...] = _rmsnorm_f32(o_ref[...], gf_ref[...])

        @pl.when(m == n_m - 1)
        def _():
            drain(1 - slot)


def _combine_ple_final(x2, moe_gates, yb, dest1, dest2, g_ple, p2d, w_gate, w_proj, g_final, tm, tn, ns):
    t, d = x2.shape
    dp = p2d.shape[1]
    return pl.pallas_call(
        functools.partial(_ple_kernel, tm=tm, tn=tn, ns=ns, nj=d // tn),
        out_shape=jax.ShapeDtypeStruct((t, d), F32),
        grid_spec=pltpu.PrefetchScalarGridSpec(
            num_scalar_prefetch=2,
            grid=(t // tm, d // tn),
            in_specs=[
                pl.BlockSpec((tm, d), lambda m, j, d1, d2: (m, 0)),
                pl.BlockSpec((tm, LANES), lambda m, j, d1, d2: (m, 0)),
                pl.BlockSpec(memory_space=pl.ANY),
                pl.BlockSpec((1, d), lambda m, j, d1, d2: (0, 0)),
                pl.BlockSpec((tm, dp), lambda m, j, d1, d2: (m, 0)),
                pl.BlockSpec((d, d), lambda m, j, d1, d2: (0, 0), **_RESIDENT),
                pl.BlockSpec((dp, d), lambda m, j, d1, d2: (0, 0), **_RESIDENT),
                pl.BlockSpec((1, d), lambda m, j, d1, d2: (0, 0)),
            ],
            out_specs=pl.BlockSpec((tm, d), lambda m, j, d1, d2: (m, 0)),
            scratch_shapes=[
                pltpu.VMEM((2, 2, tm * ns, LANES), yb.dtype),
                pltpu.SemaphoreType.DMA((2,)),
                pltpu.VMEM((tm, d), F32),
                pltpu.VMEM((tm, d), BF16),
                pltpu.VMEM((tm, dp), BF16),
            ],
        ),
        compiler_params=_params(("arbitrary", "arbitrary")),
        name="combine_ple_final",
    )(dest1, dest2, x2, moe_gates, yb, g_ple, p2d, w_gate, w_proj, g_final)


def _tile(n, pref):
    return pref if n % pref == 0 else n


def _layer(x2d, p2d, batch, seq, w_in, b_forget, w_branch_fox, w_branch_sb, w_mix_out, g_mix, g_ffn,
           w_group, w_expert, w_gate, w_up, w_down, g_ple, w_ple_proj, w_ple_gate, g_final):
    t, d = x2d.shape
    n_heads = b_forget.shape[0]
    w_att = n_heads * HEAD_DIM
    n_exp = w_expert.shape[1]

    qs = LOG2E * HEAD_DIM ** -0.5
    w_t, wf_t = _wprep(w_in.T, w_att, n_heads, qs, 512)
    b_row = jnp.pad(b_forget, (0, LANES - n_heads)).reshape(1, LANES)
    col_qa, col_ka, col_qb, col_kb = 0, w_att, 2 * w_att, 3 * w_att
    col_ga, col_gb = 4 * w_att, 4 * w_att + d
    row_va, row_vb = 0, w_att

    tm = _tile(t, 1024)
    tn = _tile(d, 1024)
    proj, vt, f_tok = _inproj(x2d, g_mix.reshape(1, d), w_t, wf_t, 2 * w_att, tm, _tile(d // 2, 1024))
    c = _forget_cumsum(f_tok, b_row, batch, n_heads)

    tq = _tile(seq, 512)
    hp = 4
    o_a = _fox_attention(proj, vt, c, batch, seq, n_heads, col_qa, col_ka, row_va, tq, hp)
    o_b = _sb_attention(proj, vt, batch, seq, n_heads, col_qb, col_kb, row_vb, tq, _tile(seq, 256), hp)

    mixed = _merge(o_a, o_b, w_branch_fox, w_branch_sb, proj, col_ga, col_gb, tm, tn)
    x2 = _mixout(mixed, w_mix_out, x2d, tm, tn)

    w_r = jnp.concatenate([w_group, w_expert], axis=1)
    w_r = jnp.pad(w_r, ((0, 0), (0, LANES - w_r.shape[1])))
    w_r_hi = w_r.astype(BF16)
    w_r_lo = (w_r - w_r_hi.astype(F32)).astype(BF16)
    tr = _tile(t, 512)
    w_r_cat = jnp.concatenate([w_r_hi, w_r_lo], axis=1)
    h2, ids_rows, gates, counts = _router(x2, g_ffn.reshape(1, d), w_r_cat, tr, N_GROUPS, EXPERTS_PER_GROUP)

    bm = 256
    n_assign = 2 * t
    nb = n_assign // bm + n_exp
    dest, block_expert, first, nxt, pad_lo, pad_n, n_used = _plan(counts, ids_rows, n_exp, bm, nb)
    dest1, dest2 = dest[0], dest[1]

    td = _tile(t, 2048)
    ns = d // 2 // LANES
    xb = _dispatch(h2, dest1, dest2, pad_lo, pad_n, n_used, nb, bm, td, ns)
    yb = _experts(xb, w_gate, w_up, w_down, block_expert, first, nxt, n_used, bm, ns)

    tp = _tile(t, 512)
    return _combine_ple_final(x2, gates, yb, dest1, dest2, g_ple.reshape(1, d), p2d, w_ple_gate, w_ple_proj,
                              g_final.reshape(1, d), tp, d, ns)


def kernel(x, p, w_in, b_forget, w_branch_fox, w_branch_sb, w_mix_out, g_mix, g_ffn, w_group, w_expert,
           w_gate, w_up, w_down, g_ple, w_ple_proj, w_ple_gate, g_final):
    b, s, d = x.shape
    depth = w_in.shape[0]
    assert depth == 1, "the final norm is fused into the single layer"
    x2d = x.reshape(b * s, d)
    out = _layer(x2d, p[0].reshape(b * s, -1), b, s, w_in[0], b_forget[0], w_branch_fox[0], w_branch_sb[0],
                 w_mix_out[0], g_mix[0], g_ffn[0], w_group[0], w_expert[0], w_gate[0], w_up[0], w_down[0],
                 g_ple[0], w_ple_proj[0], w_ple_gate[0], g_final)
    return out.reshape(b, s, d)
```

```python
import functools

import jax
import jax.numpy as jnp
from jax import lax
from jax.experimental import pallas as pl
from jax.experimental.pallas import tpu as pltpu

F32 = jnp.float32
BF16 = jnp.bfloat16

HEAD_DIM = 128
N_GROUPS = 4
EXPERTS_PER_GROUP = 8
EPS = 1e-6
LANES = 128
SUBLANES = 8
VMEM_LIMIT = 56 * 1024 * 1024

LOG2E = 1.4426950408889634
DMA_UNROLL = 8
RANK_BITS = 7
R_DONE = 160.0

NT_DIMS = (((1,), (1,)), ((), ()))


def _params(sem):
    return pltpu.CompilerParams(dimension_semantics=sem, vmem_limit_bytes=VMEM_LIMIT)


def _rmsnorm_f32(x, g):
    ms = jnp.mean(x * x, axis=-1, keepdims=True)
    return x * lax.rsqrt(ms + EPS) * g


def _pack_bf16_pairs(x):
    n = x.shape[1] // 2
    bits = lax.bitcast_convert_type(x.astype(BF16).astype(F32), jnp.uint32)
    return (bits[:, :n] >> 16) | (bits[:, n:] & jnp.uint32(0xFFFF0000))


def _unpack_bf16_pairs(w):
    lo = lax.bitcast_convert_type(w << 16, F32)
    hi = lax.bitcast_convert_type(w & jnp.uint32(0xFFFF0000), F32)
    return jnp.concatenate([lo, hi], axis=1)


def _store_row_tiles(ref, words):
    n = words.shape[0]
    ns = words.shape[1] // LANES
    for c in range(ns):
        ref[pl.ds(c, n, stride=ns), :] = words[:, c * LANES:(c + 1) * LANES]


def _load_row_tiles(ref, ns):
    n = ref.shape[0] // ns
    return jnp.concatenate([ref[pl.ds(c, n, stride=ns), :] for c in range(ns)], axis=1)


def _wprep_kernel(off_ref, scale_ref, w_ref, f_ref, wt_ref, wf_ref, *, n_heads):
    i = pl.program_id(0)
    wt_ref[...] = (w_ref[...] * scale_ref[i]).astype(BF16)

    @pl.when(i == 0)
    def _():
        row = lax.broadcasted_iota(jnp.int32, wf_ref.shape, 0)
        f_rows = jnp.concatenate([f_ref[...]] * (wf_ref.shape[0] // f_ref.shape[0]), axis=0)
        wf_ref[...] = jnp.where(row < n_heads, f_rows, 0.0).astype(BF16)


def _wprep(w_t, w_att, n_heads, qs, rb):
    n_in, d = w_t.shape
    off_b = 3 * w_att + n_heads
    segments = [(2 * w_att, w_att, 1.0), (off_b + 2 * w_att, w_att, 1.0),
                (0, w_att, qs), (w_att, w_att, 1.0),
                (off_b, w_att, qs), (off_b + w_att, w_att, 1.0),
                (off_b + 3 * w_att, n_in - off_b - 3 * w_att, 1.0)]
    offs, scales = [], []
    for start, length, scale in segments:
        assert length % rb == 0 and start % SUBLANES == 0
        for k in range(length // rb):
            offs.append((start + k * rb) // SUBLANES)
            scales.append(scale)
    n_out = len(offs) * rb
    f_blk = SUBLANES
    assert n_heads <= f_blk and (3 * w_att) % f_blk == 0
    return pl.pallas_call(
        functools.partial(_wprep_kernel, n_heads=n_heads),
        out_shape=(jax.ShapeDtypeStruct((n_out, d), BF16), jax.ShapeDtypeStruct((LANES, d), BF16)),
        grid_spec=pltpu.PrefetchScalarGridSpec(
            num_scalar_prefetch=2,
            grid=(len(offs),),
            in_specs=[
                pl.BlockSpec((pl.Element(rb), pl.Element(d)), lambda i, off, sc: (off[i] * SUBLANES, 0)),
                pl.BlockSpec((pl.Element(f_blk), pl.Element(d)), lambda i, off, sc: (3 * w_att, 0)),
            ],
            out_specs=(
                pl.BlockSpec((rb, d), lambda i, off, sc: (i, 0)),
                pl.BlockSpec((LANES, d), lambda i, off, sc: (0, 0)),
            ),
        ),
        compiler_params=_params(("arbitrary",)),
        name="inproj_weights",
    )(jnp.asarray(offs, jnp.int32), jnp.asarray(scales, F32), w_t, w_t)


def _inproj_kernel(x_ref, g_ref, w_ref, wf_ref, o_ref, vt_ref, f_ref, h_scr, *, nvt):
    j = pl.program_id(1)

    @pl.when(j == 0)
    def _():
        hb = _rmsnorm_f32(x_ref[...], g_ref[...]).astype(BF16)
        h_scr[...] = hb
        f_ref[...] = lax.dot_general(hb, wf_ref[...], NT_DIMS, preferred_element_type=F32)

    @pl.when(j < nvt)
    def _():
        vt_ref[...] = lax.dot_general(w_ref[...], h_scr[...], NT_DIMS,
                                      preferred_element_type=F32).astype(vt_ref.dtype)

    @pl.when(j >= nvt)
    def _():
        o_ref[...] = lax.dot_general(h_scr[...], w_ref[...], NT_DIMS,
                                     preferred_element_type=F32).astype(o_ref.dtype)


def _inproj(x2d, g, w_t, wf_t, n_v, tm, tn):
    t, d = x2d.shape
    n = w_t.shape[0] - n_v
    nvt = n_v // tn
    return pl.pallas_call(
        functools.partial(_inproj_kernel, nvt=nvt),
        out_shape=(jax.ShapeDtypeStruct((t, n), BF16), jax.ShapeDtypeStruct((n_v, t), BF16),
                   jax.ShapeDtypeStruct((t, LANES), F32)),
        grid=(t // tm, nvt + n // tn),
        in_specs=[
            pl.BlockSpec((tm, d), lambda m, j: (m, 0)),
            pl.BlockSpec((1, d), lambda m, j: (0, 0)),
            pl.BlockSpec((tn, d), lambda m, j: (j, 0)),
            pl.BlockSpec((LANES, d), lambda m, j: (0, 0)),
        ],
        out_specs=(
            pl.BlockSpec((tm, tn), lambda m, j: (m, jnp.maximum(j - nvt, 0))),
            pl.BlockSpec((tn, tm), lambda m, j: (jnp.minimum(j, nvt - 1), m)),
            pl.BlockSpec((tm, LANES), lambda m, j: (m, 0)),
        ),
        scratch_shapes=[pltpu.VMEM((tm, d), BF16)],
        compiler_params=_params(("parallel", "arbitrary")),
        name="inproj",
    )(x2d, g, w_t, wf_t)


def _split3(x):
    p1 = x.astype(BF16)
    r1 = x - p1.astype(F32)
    p2 = r1.astype(BF16)
    p3 = (r1 - p2.astype(F32)).astype(BF16)
    return p1, p2, p3


def _cumsum_kernel(f_ref, b_ref, lt_ref, c_ref, carry_scr, *, chunk, n_heads):
    @pl.when(pl.program_id(1) == 0)
    def _():
        carry_scr[...] = jnp.zeros(carry_scr.shape, F32)

    rows = f_ref.shape[0]
    lt = lt_ref[...]
    carry = carry_scr[...]
    for i in range(rows // chunk):
        lf = jax.nn.log_sigmoid(f_ref[i * chunk:(i + 1) * chunk, :] + b_ref[...])
        p1, p2, p3 = _split3(lf)
        cs = (jnp.dot(lt, p1, preferred_element_type=F32)
              + jnp.dot(lt, p2, preferred_element_type=F32)
              + jnp.dot(lt, p3, preferred_element_type=F32)) + carry
        for h in range(n_heads):
            c_ref[h, i * chunk:(i + 1) * chunk, :] = jnp.broadcast_to(cs[:, h:h + 1] * LOG2E, (chunk, LANES))
        carry = cs[chunk - 1:chunk, :]
    carry_scr[...] = carry


def _forget_cumsum(f_tok, b_row, batch, n_heads):
    t = f_tok.shape[0]
    s = t // batch
    rows = min(1024, s)
    chunk = min(256, s)
    lt = jnp.tril(jnp.ones((chunk, chunk), BF16))
    nr = s // rows
    return pl.pallas_call(
        functools.partial(_cumsum_kernel, chunk=chunk, n_heads=n_heads),
        out_shape=jax.ShapeDtypeStruct((n_heads, t, LANES), F32),
        grid=(batch, nr),
        in_specs=[
            pl.BlockSpec((rows, LANES), lambda b, r: (b * nr + r, 0)),
            pl.BlockSpec((1, LANES), lambda b, r: (0, 0)),
            pl.BlockSpec((chunk, chunk), lambda b, r: (0, 0)),
        ],
        out_specs=pl.BlockSpec((n_heads, rows, LANES), lambda b, r: (0, b * nr + r, 0)),
        scratch_shapes=[pltpu.VMEM((1, LANES), F32)],
        compiler_params=_params(("parallel", "arbitrary")),
        name="forget_cumsum",
    )(f_tok, b_row, lt)


def _fox_kernel(q_ref, k_ref, vt_ref, c_ref, o_ref, m_scr, l_scr, acc_scr, *, tq, tk, hp):
    i = pl.program_id(2)
    m_scr[...] = jnp.full(m_scr.shape, -jnp.inf, F32)
    l_scr[...] = jnp.zeros(l_scr.shape, F32)
    acc_scr[...] = jnp.zeros(acc_scr.shape, F32)

    def tile(j, masked, koff=0, nk=tk, qlo=0):
        start = pl.multiple_of(j * tk + koff, nk)
        nq = tq - qlo
        if masked:
            key = j * tk + koff + lax.broadcasted_iota(jnp.int32, (nk, nq), 0)
            qry = i * tq + qlo + lax.broadcasted_iota(jnp.int32, (nk, nq), 1)
            keep = key <= qry

        def scores(hh):
            lanes = slice(hh * HEAD_DIM, (hh + 1) * HEAD_DIM)
            kj = k_ref[pl.ds(start, nk), lanes]
            return lax.dot_general(kj, q_ref[qlo:tq, lanes], NT_DIMS, preferred_element_type=F32)

        s_all = [scores(hh) for hh in range(hp)]
        for hh in range(hp):
            s = s_all[hh]
            cj = c_ref[hh, pl.ds(start, nk), :]
            s = s - jnp.concatenate([cj] * (nq // LANES), axis=1)
            if masked:
                s = jnp.where(keep, s, -jnp.inf)
            m_prev = m_scr[hh, :, qlo:tq]
            m_new = jnp.maximum(m_prev, jnp.max(s, axis=0, keepdims=True))
            alpha = jnp.exp2(m_prev - m_new)
            p = jnp.exp2(s - m_new)
            l_scr[hh, :, qlo:tq] = alpha * l_scr[hh, :, qlo:tq] + jnp.sum(p, axis=0, keepdims=True)
            vtj = vt_ref[hh * HEAD_DIM:(hh + 1) * HEAD_DIM, pl.ds(start, nk)]
            acc_scr[hh, :, qlo:tq] = (alpha * acc_scr[hh, :, qlo:tq]
                                      + jnp.dot(vtj, p.astype(BF16), preferred_element_type=F32))
            m_scr[hh, :, qlo:tq] = m_new

    def body(j, carry):
        tile(j, False)
        return carry

    lax.fori_loop(0, i, body, 0)
    half = tk // 2
    tile(i, True, koff=0, nk=half, qlo=0)
    tile(i, True, koff=half, nk=half, qlo=tq - half)
    for hh in range(hp):
        o = acc_scr[hh] / l_scr[hh]
        o_ref[:, hh * HEAD_DIM:(hh + 1) * HEAD_DIM] = o.T.astype(o_ref.dtype)


def _fox_attention(proj, vt, c, batch, seq, n_heads, col_q, col_k, row_v, tq, hp):
    t = proj.shape[0]
    nq = seq // tq
    wb = hp * HEAD_DIM
    cq, ck, rv = col_q // wb, col_k // wb, row_v // wb
    return pl.pallas_call(
        functools.partial(_fox_kernel, tq=tq, tk=tq, hp=hp),
        out_shape=jax.ShapeDtypeStruct((t, n_heads * HEAD_DIM), BF16),
        grid=(batch, n_heads // hp, nq),
        in_specs=[
            pl.BlockSpec((tq, wb), lambda b, g, i: (b * nq + i, cq + g)),
            pl.BlockSpec((seq, wb), lambda b, g, i: (b, ck + g)),
            pl.BlockSpec((wb, seq), lambda b, g, i: (rv + g, b)),
            pl.BlockSpec((hp, seq, LANES), lambda b, g, i: (g, b, 0)),
        ],
        out_specs=pl.BlockSpec((tq, wb), lambda b, g, i: (b * nq + i, g)),
        scratch_shapes=[
            pltpu.VMEM((hp, 1, tq), F32),
            pltpu.VMEM((hp, 1, tq), F32),
            pltpu.VMEM((hp, HEAD_DIM, tq), F32),
        ],
        compiler_params=_params(("parallel", "parallel", "arbitrary")),
        name="fox_attention",
    )(proj, proj, vt, c)


def _sb_kernel(q_ref, k_ref, vt_ref, tri_ref, o_ref, r_scr, acc_scr, *, tq, tk, hp):
    i = pl.program_id(2)
    r_scr[...] = jnp.zeros(r_scr.shape, F32)
    acc_scr[...] = jnp.zeros(acc_scr.shape, F32)

    sign_bit = jnp.uint32(0x80000000)

    def tile(j, masked, qlo=0):
        start = pl.multiple_of(j * tk, tk)
        nq = tq - qlo
        if masked:
            key = j * tk + lax.broadcasted_iota(jnp.int32, (tk, nq), 0)
            qry = i * tq + qlo + lax.broadcasted_iota(jnp.int32, (tk, nq), 1)
            mask = key < qry

        def scores(hh):
            lanes = slice(hh * HEAD_DIM, (hh + 1) * HEAD_DIM)
            kj = k_ref[pl.ds(start, tk), lanes]
            return lax.dot_general(kj, q_ref[qlo:tq, lanes], NT_DIMS, preferred_element_type=F32)

        u_all = [scores(hh) for hh in range(hp)]
        for hh in range(hp):
            u = u_all[hh]
            minus_abs = lax.bitcast_convert_type(lax.bitcast_convert_type(u, jnp.uint32) | sign_bit, F32)
            sp = jnp.maximum(u, 0.0) + jnp.log2(1.0 + jnp.exp2(minus_abs))
            if masked:
                sp = jnp.where(mask, sp, 0.0)
            w = jnp.dot(tri_ref[...], sp.astype(BF16), preferred_element_type=F32)
            a = jnp.exp2(u - ((sp + w) + r_scr[hh, :, qlo:tq]))
            if masked:
                a = jnp.where(mask, a, 0.0)
            vtj = vt_ref[hh * HEAD_DIM:(hh + 1) * HEAD_DIM, pl.ds(start, tk)]
            acc_scr[hh, :, qlo:tq] += jnp.dot(vtj, a.astype(BF16), preferred_element_type=F32)
            r_scr[hh, :, qlo:tq] += jnp.sum(sp, axis=0, keepdims=True)

    ratio = max(tq // tk, 1)
    n_full = (i * tq) // tk
    for d in reversed(range(ratio)):
        tile(n_full + d, True, qlo=d * tk if tq > tk else 0)

    def live():
        return (jnp.min(r_scr[...]) <= R_DONE).astype(jnp.int32)

    def cond(carry):
        it, alive = carry
        return (it < n_full) & (alive > 0)

    def body(carry):
        it, _ = carry
        tile(n_full - 1 - it, False)
        return it + 1, live()

    lax.while_loop(cond, body, (jnp.int32(0), live()))
    for hh in range(hp):
        o_ref[:, hh * HEAD_DIM:(hh + 1) * HEAD_DIM] = acc_scr[hh].T.astype(o_ref.dtype)


def _sb_attention(proj, vt, batch, seq, n_heads, col_q, col_k, row_v, tq, tk, hp):
    t = proj.shape[0]
    nq = seq // tq
    wb = hp * HEAD_DIM
    cq, ck, rv = col_q // wb, col_k // wb, row_v // wb
    tri = jnp.triu(jnp.ones((tk, tk), BF16), 1)
    return pl.pallas_call(
        functools.partial(_sb_kernel, tq=tq, tk=tk, hp=hp),
        out_shape=jax.ShapeDtypeStruct((t, n_heads * HEAD_DIM), BF16),
        grid=(batch, n_heads // hp, nq),
        in_specs=[
            pl.BlockSpec((tq, wb), lambda b, g, i: (b * nq + i, cq + g)),
            pl.BlockSpec((seq, wb), lambda b, g, i: (b, ck + g)),
            pl.BlockSpec((wb, seq), lambda b, g, i: (rv + g, b)),
            pl.BlockSpec((tk, tk), lambda b, g, i: (0, 0)),
        ],
        out_specs=pl.BlockSpec((tq, wb), lambda b, g, i: (b * nq + i, g)),
        scratch_shapes=[
            pltpu.VMEM((hp, 1, tq), F32),
            pltpu.VMEM((hp, HEAD_DIM, tq), F32),
        ],
        compiler_params=_params(("parallel", "parallel", "arbitrary")),
        name="sb_attention",
    )(proj, proj, vt, tri)


_RESIDENT = dict(pipeline_mode=pl.Buffered(1))


def _merge_kernel(oa_ref, ob_ref, wa_ref, wb_ref, ga_ref, gb_ref, o_ref, *, tn):
    col = pl.multiple_of(pl.program_id(1) * tn, tn)
    ya = jnp.dot(oa_ref[...], wa_ref[:, pl.ds(col, tn)].astype(BF16), preferred_element_type=F32)
    yb = jnp.dot(ob_ref[...], wb_ref[:, pl.ds(col, tn)].astype(BF16), preferred_element_type=F32)
    ga = jax.nn.sigmoid(ga_ref[...].astype(F32))
    gb = jax.nn.sigmoid(gb_ref[...].astype(F32))
    o_ref[...] = (ga * ya + gb * yb).astype(o_ref.dtype)


def _merge(o_a, o_b, wa, wb, proj, col_ga, col_gb, tm, tn):
    t, ka = o_a.shape
    kb = o_b.shape[1]
    d = wa.shape[1]
    ca, cb = col_ga // tn, col_gb // tn
    return pl.pallas_call(
        functools.partial(_merge_kernel, tn=tn),
        out_shape=jax.ShapeDtypeStruct((t, d), BF16),
        grid=(t // tm, d // tn),
        in_specs=[
            pl.BlockSpec((tm, ka), lambda m, j: (m, 0)),
            pl.BlockSpec((tm, kb), lambda m, j: (m, 0)),
            pl.BlockSpec((ka, d), lambda m, j: (0, 0), **_RESIDENT),
            pl.BlockSpec((kb, d), lambda m, j: (0, 0), **_RESIDENT),
            pl.BlockSpec((tm, tn), lambda m, j: (m, ca + j)),
            pl.BlockSpec((tm, tn), lambda m, j: (m, cb + j)),
        ],
        out_specs=pl.BlockSpec((tm, tn), lambda m, j: (m, j)),
        compiler_params=_params(("parallel", "arbitrary")),
        name="branch_merge",
    )(o_a, o_b, wa, wb, proj, proj)


def _mixout_kernel(a_ref, w_ref, x_ref, o_ref, *, tn):
    col = pl.multiple_of(pl.program_id(1) * tn, tn)
    w = w_ref[:, pl.ds(col, tn)].astype(BF16)
    o_ref[...] = x_ref[...] + jnp.dot(a_ref[...], w, preferred_element_type=F32)


def _mixout(mixed, w, x2d, tm, tn):
    t, k = mixed.shape
    d = w.shape[1]
    return pl.pallas_call(
        functools.partial(_mixout_kernel, tn=tn),
        out_shape=jax.ShapeDtypeStruct((t, d), F32),
        grid=(t // tm, d // tn),
        in_specs=[
            pl.BlockSpec((tm, k), lambda m, j: (m, 0)),
            pl.BlockSpec((k, d), lambda m, j: (0, 0), **_RESIDENT),
            pl.BlockSpec((tm, tn), lambda m, j: (m, j)),
        ],
        out_specs=pl.BlockSpec((tm, tn), lambda m, j: (m, j)),
        compiler_params=_params(("parallel", "arbitrary")),
        name="mix_out",
    )(mixed, w, x2d)


def _router_kernel(x_ref, g_ref, wc_ref, lt_ref, sel_ref, h_ref, ids_ref, gate_ref, cnt_ref, carry_scr,
                   *, n_groups, per_group):
    @pl.when(pl.program_id(0) == 0)
    def _():
        carry_scr[...] = jnp.zeros(carry_scr.shape, F32)

    h = _rmsnorm_f32(x_ref[...], g_ref[...])
    hh = h.astype(BF16)
    _store_row_tiles(h_ref, _pack_bf16_pairs(h))
    hl = (h - hh.astype(F32)).astype(BF16)
    both = jnp.dot(hh, wc_ref[...], preferred_element_type=F32)
    logits = (both[:, :LANES] + jnp.dot(hl, wc_ref[:, :LANES], preferred_element_type=F32)) + both[:, LANES:]
    tm = logits.shape[0]
    lane = lax.broadcasted_iota(jnp.int32, (tm, LANES), 1)
    lane_f = lane.astype(F32)
    neg_inf = -jnp.inf

    def first_argmax(vals, valid):
        vmax = jnp.max(jnp.where(valid, vals, neg_inf), axis=-1, keepdims=True)
        idx = jnp.min(jnp.where(valid & (vals == vmax), lane_f, float(LANES)), axis=-1, keepdims=True)
        return vmax, idx.astype(jnp.int32)

    is_group = lane < n_groups
    gmax, gsel = first_argmax(logits, is_group)
    gsum = jnp.sum(jnp.where(is_group, jnp.exp(logits - gmax), 0.0), axis=-1, keepdims=True)
    p_group = 1.0 / gsum
    lo = n_groups + gsel * per_group
    in_group = (lane >= lo) & (lane < lo + per_group)
    v1, i1 = first_argmax(logits, in_group)
    v2, i2 = first_argmax(logits, in_group & (lane != i1))
    e2w = jnp.exp(v2 - v1)
    gate1 = (1.0 / (1.0 + e2w)) * p_group
    gate2 = (e2w / (1.0 + e2w)) * p_group
    e1 = i1 - n_groups
    e2 = i2 - n_groups

    onehot = ((lane == e1) | (lane == e2)).astype(BF16)
    before = jnp.dot(lt_ref[...], onehot, preferred_element_type=F32) + carry_scr[...]
    rank1 = jnp.sum(jnp.where(lane == e1, before, 0.0), axis=-1, keepdims=True).astype(jnp.int32)
    rank2 = jnp.sum(jnp.where(lane == e2, before, 0.0), axis=-1, keepdims=True).astype(jnp.int32)
    carry_new = carry_scr[...] + jnp.sum(onehot.astype(F32), axis=0, keepdims=True)
    carry_scr[...] = carry_new

    low = (1 << RANK_BITS) - 1
    cols = jnp.where(lane == 0, e1, jnp.where(lane == 1, e2, jnp.where(lane == 2, rank1 >> RANK_BITS,
           jnp.where(lane == 3, rank1 & low, jnp.where(lane == 4, rank2 >> RANK_BITS,
           jnp.where(lane == 5, rank2 & low, 0))))))
    ids_ref[...] = lax.dot_general(sel_ref[...], cols.astype(F32).astype(BF16), NT_DIMS,
                                   preferred_element_type=F32)
    gate_ref[...] = jnp.where(lane == 0, gate1, jnp.where(lane == 1, gate2, 0.0))
    cnt_ref[...] = jnp.broadcast_to(carry_new, cnt_ref.shape).astype(jnp.int32)


def _router(x2, g, w_cat, tm, n_groups, per_group):
    t, d = x2.shape
    lt = jnp.tril(jnp.ones((tm, tm), BF16), -1)
    sel = jnp.eye(8, LANES, dtype=BF16)
    return pl.pallas_call(
        functools.partial(_router_kernel, n_groups=n_groups, per_group=per_group),
        out_shape=(
            jax.ShapeDtypeStruct((t * (d // 2 // LANES), LANES), jnp.uint32),
            jax.ShapeDtypeStruct((8, t), F32),
            jax.ShapeDtypeStruct((t, LANES), F32),
            jax.ShapeDtypeStruct((8, LANES), jnp.int32),
        ),
        grid=(t // tm,),
        in_specs=[
            pl.BlockSpec((tm, d), lambda m: (m, 0)),
            pl.BlockSpec((1, d), lambda m: (0, 0)),
            pl.BlockSpec((d, 2 * LANES), lambda m: (0, 0)),
            pl.BlockSpec((tm, tm), lambda m: (0, 0)),
            pl.BlockSpec((8, LANES), lambda m: (0, 0)),
        ],
        out_specs=(
            pl.BlockSpec((tm * (d // 2 // LANES), LANES), lambda m: (m, 0)),
            pl.BlockSpec((8, tm), lambda m: (0, m)),
            pl.BlockSpec((tm, LANES), lambda m: (m, 0)),
            pl.BlockSpec((8, LANES), lambda m: (0, 0)),
        ),
        scratch_shapes=[pltpu.VMEM((1, LANES), F32)],
        compiler_params=_params(("arbitrary",)),
        name="router",
    )(x2, g, w_cat, lt, sel)


def _plan_kernel(cnt_ref, ids_ref, dest_ref, be_ref, first_ref, nxt_ref, plo_ref, pn_ref, nu_ref, ps_scr,
                 *, n_exp, bm, nb):
    shift = bm.bit_length() - 1

    def clear(k, c):
        first_ref[k] = 0
        nxt_ref[k] = -1
        return c

    lax.fori_loop(0, nb, clear, 0)

    def per_expert(e, carry):
        blk_start, last, prev_start = carry
        cnt = cnt_ref[0, e]
        n = lax.shift_right_logical(cnt + (bm - 1), shift)
        ps_scr[e] = blk_start * bm
        plo_ref[e] = blk_start * bm + cnt
        pn_ref[e] = n * bm - cnt

        def fill(k, c):
            be_ref[blk_start + k] = e
            return c

        lax.fori_loop(0, n, fill, 0)

        @pl.when(n > 0)
        def _():
            first_ref[blk_start] = 1

            @pl.when(prev_start >= 0)
            def _():
                nxt_ref[prev_start] = e

        used = n > 0
        return blk_start + n, jnp.where(used, e, last), jnp.where(used, blk_start, prev_start)

    n_used, last, _ = lax.fori_loop(0, n_exp, per_expert, (jnp.int32(0), jnp.int32(0), jnp.int32(-1)))
    nu_ref[0] = n_used

    def tail(k, c):
        be_ref[k] = last
        return c

    lax.fori_loop(n_used, nb, tail, 0)

    e1 = ids_ref[0:1, :]
    e2 = ids_ref[1:2, :]
    p1 = jnp.zeros_like(e1)
    p2 = jnp.zeros_like(e2)
    for e in range(n_exp):
        ps = ps_scr[e].astype(F32)
        p1 = jnp.where(e1 == e, ps, p1)
        p2 = jnp.where(e2 == e, ps, p2)
    dest_ref[...] = jnp.zeros(dest_ref.shape, jnp.int32)
    hi = float(1 << RANK_BITS)
    dest_ref[0:1, :] = (p1 + ids_ref[2:3, :] * hi + ids_ref[3:4, :]).astype(jnp.int32)
    dest_ref[1:2, :] = (p2 + ids_ref[4:5, :] * hi + ids_ref[5:6, :]).astype(jnp.int32)


def _plan(counts, ids_rows, n_exp, bm, nb):
    t = ids_rows.shape[1]
    return pl.pallas_call(
        functools.partial(_plan_kernel, n_exp=n_exp, bm=bm, nb=nb),
        out_shape=(
            jax.ShapeDtypeStruct((8, t), jnp.int32),
            jax.ShapeDtypeStruct((nb,), jnp.int32),
            jax.ShapeDtypeStruct((nb,), jnp.int32),
            jax.ShapeDtypeStruct((nb,), jnp.int32),
            jax.ShapeDtypeStruct((n_exp,), jnp.int32),
            jax.ShapeDtypeStruct((n_exp,), jnp.int32),
            jax.ShapeDtypeStruct((1,), jnp.int32),
        ),
        in_specs=[
            pl.BlockSpec(memory_space=pltpu.SMEM),
            pl.BlockSpec(memory_space=pltpu.VMEM),
        ],
        out_specs=(
            pl.BlockSpec(memory_space=pltpu.VMEM),
            pl.BlockSpec(memory_space=pltpu.SMEM),
            pl.BlockSpec(memory_space=pltpu.SMEM),
            pl.BlockSpec(memory_space=pltpu.SMEM),
            pl.BlockSpec(memory_space=pltpu.SMEM),
            pl.BlockSpec(memory_space=pltpu.SMEM),
            pl.BlockSpec(memory_space=pltpu.SMEM),
        ),
        scratch_shapes=[pltpu.SMEM((n_exp,), jnp.int32)],
        name="moe_plan",
    )(counts, ids_rows)


def _dispatch_kernel(d1_ref, d2_ref, plo_ref, pn_ref, nu_ref, h_ref, xb_ref, zbuf, sem, zsem,
                     *, tm, ns, bm, nb, n_exp):
    base = pl.program_id(0) * tm

    def zero_copies(fn):
        def per_expert(e, carry):
            pos = plo_ref[e]
            rem = pn_ref[e]
            chunk = bm // 2
            while chunk >= 1:
                hit = (rem & chunk) != 0

                @pl.when(hit)
                def _(pos=pos, chunk=chunk):
                    dst = xb_ref.at[pl.ds(pl.multiple_of(pos * ns, ns), chunk * ns)]
                    fn(pltpu.make_async_copy(zbuf.at[pl.ds(0, chunk * ns)], dst, zsem))

                pos = pos + jnp.where(hit, chunk, 0)
                chunk //= 2
            return carry

        lax.fori_loop(0, n_exp, per_expert, 0)

        def per_tail_block(b, carry):
            dst = xb_ref.at[pl.ds(pl.multiple_of(b * (bm * ns), bm * ns), bm * ns)]
            fn(pltpu.make_async_copy(zbuf, dst, zsem))
            return carry

        lax.fori_loop(nu_ref[0], nb, per_tail_block, 0)

    @pl.when(pl.program_id(0) == 0)
    def _():
        zbuf[...] = jnp.zeros(zbuf.shape, zbuf.dtype)
        zero_copies(lambda cp: cp.start())

    def row_copy(r, dest):
        src = h_ref.at[pl.ds(pl.multiple_of(r * ns, ns), ns)]
        return pltpu.make_async_copy(src, xb_ref.at[pl.ds(pl.multiple_of(dest * ns, ns), ns)], sem)

    def start(r, carry):
        row_copy(r, d1_ref[base + r]).start()
        row_copy(r, d2_ref[base + r]).start()
        return carry

    lax.fori_loop(0, tm, start, 0, unroll=DMA_UNROLL)
    for _ in range(2):
        pltpu.make_async_copy(h_ref, xb_ref.at[pl.ds(0, tm * ns)], sem).wait()

    @pl.when(pl.program_id(0) == pl.num_programs(0) - 1)
    def _():
        zero_copies(lambda cp: cp.wait())


def _dispatch(h2p, dest1, dest2, pad_lo, pad_n, n_used, nb, bm, tm, ns):
    t = h2p.shape[0] // ns
    return pl.pallas_call(
        functools.partial(_dispatch_kernel, tm=tm, ns=ns, bm=bm, nb=nb, n_exp=pad_lo.shape[0]),
        out_shape=jax.ShapeDtypeStruct((nb * bm * ns, LANES), h2p.dtype),
        grid_spec=pltpu.PrefetchScalarGridSpec(
            num_scalar_prefetch=5,
            grid=(t // tm,),
            in_specs=[pl.BlockSpec((tm * ns, LANES), lambda m, d1, d2, plo, pn, nu: (m, 0))],
            out_specs=pl.BlockSpec(memory_space=pl.ANY),
            scratch_shapes=[pltpu.VMEM((bm * ns, LANES), h2p.dtype), pltpu.SemaphoreType.DMA,
                            pltpu.SemaphoreType.DMA],
        ),
        compiler_params=_params(("arbitrary",)),
        name="moe_dispatch",
    )(dest1, dest2, pad_lo, pad_n, n_used, h2p)


def _experts_kernel(be_ref, first_ref, nxt_ref, nu_ref, x_ref, wg_hbm, wu_hbm, wd_hbm, y_ref,
                    wg_buf, wu_buf, wd_buf, sem, run_scr, *, ns):
    b = pl.program_id(0)

    def weight_copies(e, slot):
        return (pltpu.make_async_copy(wg_hbm.at[e], wg_buf.at[slot], sem.at[slot]),
                pltpu.make_async_copy(wu_hbm.at[e], wu_buf.at[slot], sem.at[slot]),
                pltpu.make_async_copy(wd_hbm.at[e], wd_buf.at[slot], sem.at[slot]))

    @pl.when(b == 0)
    def _():
        run_scr[0] = 0
        for cp in weight_copies(be_ref[0], 0):
            cp.start()

    @pl.when((first_ref[b] == 1) & (b < nu_ref[0]))
    def _():
        run = run_scr[0]
        slot = run & 1
        for cp in weight_copies(be_ref[b], slot):
            cp.wait()

        @pl.when(nxt_ref[b] >= 0)
        def _():
            for cp in weight_copies(nxt_ref[b], 1 - slot):
                cp.start()

        run_scr[0] = run + 1

    @pl.when(b < nu_ref[0])
    def _():
        slot = (run_scr[0] - 1) & 1
        x = _unpack_bf16_pairs(_load_row_tiles(x_ref, ns)).astype(BF16)
        g = jnp.dot(x, wg_buf[slot].astype(BF16), preferred_element_type=F32)
        u = jnp.dot(x, wu_buf[slot].astype(BF16), preferred_element_type=F32)
        hid = (g * jax.nn.sigmoid(g) * u).astype(BF16)
        y = jnp.dot(hid, wd_buf[slot].astype(BF16), preferred_element_type=F32)
        _store_row_tiles(y_ref, _pack_bf16_pairs(y))


def _experts(xb, w_gate, w_up, w_down, block_expert, first, nxt, n_used, bm, ns):
    d, de = w_gate.shape[1:]
    nb = xb.shape[0] // (bm * ns)

    def row_map(b, be, fi, nx, nu):
        return (jnp.minimum(b, nu[0] - 1), 0)

    return pl.pallas_call(
        functools.partial(_experts_kernel, ns=ns),
        out_shape=jax.ShapeDtypeStruct(xb.shape, jnp.uint32),
        grid_spec=pltpu.PrefetchScalarGridSpec(
            num_scalar_prefetch=4,
            grid=(nb,),
            in_specs=[
                pl.BlockSpec((bm * ns, LANES), row_map),
                pl.BlockSpec(memory_space=pl.ANY),
                pl.BlockSpec(memory_space=pl.ANY),
                pl.BlockSpec(memory_space=pl.ANY),
            ],
            out_specs=pl.BlockSpec((bm * ns, LANES), row_map),
            scratch_shapes=[
                pltpu.VMEM((2, d, de), F32),
                pltpu.VMEM((2, d, de), F32),
                pltpu.VMEM((2, de, d), F32),
                pltpu.SemaphoreType.DMA((2,)),
                pltpu.SMEM((1,), jnp.int32),
            ],
        ),
        input_output_aliases={4: 0},
        compiler_params=_params(("arbitrary",)),
        name="moe_experts",
    )(block_expert, first, nxt, n_used, xb, w_gate, w_up, w_down)


def _ple_kernel(d1_ref, d2_ref, x_ref, mg_ref, yb_ref, g_ref, p_ref, wg_ref, wp_ref, gf_ref, o_ref,
                ybuf, sem, x3_scr, h_scr, p_scr, *, tm, tn, ns, nj):
    m = pl.program_id(0)
    j = pl.program_id(1)
    n_m = pl.num_programs(0)
    n_j = pl.num_programs(1)
    slot = m & 1

    def row_copies(blk, r, s):
        tok = blk * tm + r
        src1 = yb_ref.at[pl.ds(pl.multiple_of(d1_ref[tok] * ns, ns), ns)]
        src2 = yb_ref.at[pl.ds(pl.multiple_of(d2_ref[tok] * ns, ns), ns)]
        rows = pl.ds(pl.multiple_of(r * ns, ns), ns)
        return (pltpu.make_async_copy(src1, ybuf.at[s, 0, rows], sem.at[s]),
                pltpu.make_async_copy(src2, ybuf.at[s, 1, rows], sem.at[s]))

    def drain(s):
        for k in range(2):
            pltpu.make_async_copy(yb_ref.at[pl.ds(0, tm * ns)], ybuf.at[s, k], sem.at[s]).wait()

    @pl.when(j == 0)
    def _():
        @pl.when(m == 0)
        def _():
            def start(r, carry):
                for cp in row_copies(0, r, 0):
                    cp.start()
                return carry

            lax.fori_loop(0, tm, start, 0, unroll=DMA_UNROLL)

        drain(slot)
        mg = mg_ref[...]
        y1 = _unpack_bf16_pairs(_load_row_tiles(ybuf.at[slot, 0], ns))
        y2 = _unpack_bf16_pairs(_load_row_tiles(ybuf.at[slot, 1], ns))
        x3 = x_ref[...] + mg[:, 0:1] * y1 + mg[:, 1:2] * y2
        x3_scr[...] = x3
        h_scr[...] = _rmsnorm_f32(x3, g_ref[...]).astype(BF16)
        p_scr[...] = p_ref[...].astype(BF16)

    nxt = jnp.minimum(m + 1, n_m - 1)
    rows_per_step = tm // nj
    for r in range(rows_per_step):
        for cp in row_copies(nxt, j * rows_per_step + r, 1 - slot):
            cp.start()

    col = pl.multiple_of(j * tn, tn)
    gate = jnp.dot(h_scr[...], wg_ref[:, pl.ds(col, tn)].astype(BF16), preferred_element_type=F32)
    ple = jnp.dot(p_scr[...], wp_ref[:, pl.ds(col, tn)].astype(BF16), preferred_element_type=F32)
    o_ref[:, pl.ds(col, tn)] = x3_scr[:, pl.ds(col, tn)] + jax.nn.sigmoid(gate) * ple

    @pl.when(j == n_j - 1)
    def _():
        o_ref[...] = _rmsnorm_f32(o_ref[...], gf_ref[...])

        @pl.when(m == n_m - 1)
        def _():
            drain(1 - slot)


def _combine_ple_final(x2, moe_gates, yb, dest1, dest2, g_ple, p2d, w_gate, w_proj, g_final, tm, tn, ns):
    t, d = x2.shape
    dp = p2d.shape[1]
    return pl.pallas_call(
        functools.partial(_ple_kernel, tm=tm, tn=tn, ns=ns, nj=d // tn),
        out_shape=jax.ShapeDtypeStruct((t, d), F32),
        grid_spec=pltpu.PrefetchScalarGridSpec(
            num_scalar_prefetch=2,
            grid=(t // tm, d // tn),
            in_specs=[
                pl.BlockSpec((tm, d), lambda m, j, d1, d2: (m, 0)),
                pl.BlockSpec((tm, LANES), lambda m, j, d1, d2: (m, 0)),
                pl.BlockSpec(memory_space=pl.ANY),
                pl.BlockSpec((1, d), lambda m, j, d1, d2: (0, 0)),
                pl.BlockSpec((tm, dp), lambda m, j, d1, d2: (m, 0)),
                pl.BlockSpec((d, d), lambda m, j, d1, d2: (0, 0), **_RESIDENT),
                pl.BlockSpec((dp, d), lambda m, j, d1, d2: (0, 0), **_RESIDENT),
                pl.BlockSpec((1, d), lambda m, j, d1, d2: (0, 0)),
            ],
            out_specs=pl.BlockSpec((tm, d), lambda m, j, d1, d2: (m, 0)),
            scratch_shapes=[
                pltpu.VMEM((2, 2, tm * ns, LANES), yb.dtype),
                pltpu.SemaphoreType.DMA((2,)),
                pltpu.VMEM((tm, d), F32),
                pltpu.VMEM((tm, d), BF16),
                pltpu.VMEM((tm, dp), BF16),
            ],
        ),
        compiler_params=_params(("arbitrary", "arbitrary")),
        name="combine_ple_final",
    )(dest1, dest2, x2, moe_gates, yb, g_ple, p2d, w_gate, w_proj, g_final)


def _tile(n, pref):
    return pref if n % pref == 0 else n


def _layer(x2d, p2d, batch, seq, w_in, b_forget, w_branch_fox, w_branch_sb, w_mix_out, g_mix, g_ffn,
           w_group, w_expert, w_gate, w_up, w_down, g_ple, w_ple_proj, w_ple_gate, g_final):
    t, d = x2d.shape
    n_heads = b_forget.shape[0]
    w_att = n_heads * HEAD_DIM
    n_exp = w_expert.shape[1]

    qs = LOG2E * HEAD_DIM ** -0.5
    w_t, wf_t = _wprep(w_in.T, w_att, n_heads, qs, 512)
    b_row = jnp.pad(b_forget, (0, LANES - n_heads)).reshape(1, LANES)
    col_qa, col_ka, col_qb, col_kb = 0, w_att, 2 * w_att, 3 * w_att
    col_ga, col_gb = 4 * w_att, 4 * w_att + d
    row_va, row_vb = 0, w_att

    tm = _tile(t, 1024)
    tn = _tile(d, 1024)
    proj, vt, f_tok = _inproj(x2d, g_mix.reshape(1, d), w_t, wf_t, 2 * w_att, tm, _tile(d // 2, 1024))
    c = _forget_cumsum(f_tok, b_row, batch, n_heads)

    tq = _tile(seq, 512)
    hp = 4
    o_a = _fox_attention(proj, vt, c, batch, seq, n_heads, col_qa, col_ka, row_va, tq, hp)
    o_b = _sb_attention(proj, vt, batch, seq, n_heads, col_qb, col_kb, row_vb, tq, _tile(seq, 256), hp)

    mixed = _merge(o_a, o_b, w_branch_fox, w_branch_sb, proj, col_ga, col_gb, tm, tn)
    x2 = _mixout(mixed, w_mix_out, x2d, tm, tn)

    w_r = jnp.concatenate([w_group, w_expert], axis=1)
    w_r = jnp.pad(w_r, ((0, 0), (0, LANES - w_r.shape[1])))
    w_r_hi = w_r.astype(BF16)
    w_r_lo = (w_r - w_r_hi.astype(F32)).astype(BF16)
    tr = _tile(t, 512)
    w_r_cat = jnp.concatenate([w_r_hi, w_r_lo], axis=1)
    h2, ids_rows, gates, counts = _router(x2, g_ffn.reshape(1, d), w_r_cat, tr, N_GROUPS, EXPERTS_PER_GROUP)

    bm = 256
    n_assign = 2 * t
    assert n_assign <= 1 << (2 * RANK_BITS), "in-expert ranks must fit two RANK_BITS-bit halves"
    nb = n_assign // bm + n_exp
    dest, block_expert, first, nxt, pad_lo, pad_n, n_used = _plan(counts, ids_rows, n_exp, bm, nb)
    dest1, dest2 = dest[0], dest[1]

    td = _tile(t, 2048)
    ns = d // 2 // LANES
    xb = _dispatch(h2, dest1, dest2, pad_lo, pad_n, n_used, nb, bm, td, ns)
    yb = _experts(xb, w_gate, w_up, w_down, block_expert, first, nxt, n_used, bm, ns)

    tp = _tile(t, 512)
    return _combine_ple_final(x2, gates, yb, dest1, dest2, g_ple.reshape(1, d), p2d, w_ple_gate, w_ple_proj,
                              g_final.reshape(1, d), tp, d, ns)


def kernel(x, p, w_in, b_forget, w_branch_fox, w_branch_sb, w_mix_out, g_mix, g_ffn, w_group, w_expert,
           w_gate, w_up, w_down, g_ple, w_ple_proj, w_ple_gate, g_final):
    b, s, d = x.shape
    depth = w_in.shape[0]
    assert depth == 1, "the final norm is fused into the single layer"
    x2d = x.reshape(b * s, d)
    out = _layer(x2d, p[0].reshape(b * s, -1), b, s, w_in[0], b_forget[0], w_branch_fox[0], w_branch_sb[0],
                 w_mix_out[0], g_mix[0], g_ffn[0], w_group[0], w_expert[0], w_gate[0], w_up[0], w_down[0],
                 g_ple[0], w_ple_proj[0], w_ple_gate[0], g_final)
    return out.reshape(b, s, d)
```

```python
import functools

import jax
import jax.numpy as jnp
from jax import lax
from jax.experimental import pallas as pl
from jax.experimental.pallas import tpu as pltpu

F32 = jnp.float32
BF16 = jnp.bfloat16

HEAD_DIM = 128
N_GROUPS = 4
EXPERTS_PER_GROUP = 8
EPS = 1e-6
LANES = 128
SUBLANES = 8
VMEM_LIMIT = 56 * 1024 * 1024

LOG2E = 1.4426950408889634
DMA_UNROLL = 8
RANK_BITS = 7
R_DONE = 160.0

NT_DIMS = (((1,), (1,)), ((), ()))


def _params(sem):
    return pltpu.CompilerParams(dimension_semantics=sem, vmem_limit_bytes=VMEM_LIMIT)


def _rmsnorm_f32(x, g):
    ms = jnp.mean(x * x, axis=-1, keepdims=True)
    return x * lax.rsqrt(ms + EPS) * g


def _pack_bf16_pairs(x):
    n = x.shape[1] // 2
    bits = lax.bitcast_convert_type(x.astype(BF16).astype(F32), jnp.uint32)
    return (bits[:, :n] >> 16) | (bits[:, n:] & jnp.uint32(0xFFFF0000))


def _unpack_bf16_pairs(w):
    lo = lax.bitcast_convert_type(w << 16, F32)
    hi = lax.bitcast_convert_type(w & jnp.uint32(0xFFFF0000), F32)
    return jnp.concatenate([lo, hi], axis=1)


def _store_row_tiles(ref, words):
    n = words.shape[0]
    ns = words.shape[1] // LANES
    for c in range(ns):
        ref[pl.ds(c, n, stride=ns), :] = words[:, c * LANES:(c + 1) * LANES]


def _load_row_tiles(ref, ns):
    n = ref.shape[0] // ns
    return jnp.concatenate([ref[pl.ds(c, n, stride=ns), :] for c in range(ns)], axis=1)


def _wprep_kernel(off_ref, scale_ref, w_ref, f_ref, wt_ref, wf_ref, *, n_heads):
    i = pl.program_id(0)
    wt_ref[...] = (w_ref[...] * scale_ref[i]).astype(BF16)

    @pl.when(i == 0)
    def _():
        row = lax.broadcasted_iota(jnp.int32, wf_ref.shape, 0)
        f_rows = jnp.concatenate([f_ref[...]] * (wf_ref.shape[0] // f_ref.shape[0]), axis=0)
        wf_ref[...] = jnp.where(row < n_heads, f_rows, 0.0).astype(BF16)


def _wprep(w_t, w_att, n_heads, qs, rb):
    n_in, d = w_t.shape
    off_b = 3 * w_att + n_heads
    segments = [(2 * w_att, w_att, 1.0), (off_b + 2 * w_att, w_att, 1.0),
                (0, w_att, qs), (w_att, w_att, 1.0),
                (off_b, w_att, qs), (off_b + w_att, w_att, 1.0),
                (off_b + 3 * w_att, n_in - off_b - 3 * w_att, 1.0)]
    offs, scales = [], []
    for start, length, scale in segments:
        assert length % rb == 0 and start % SUBLANES == 0
        for k in range(length // rb):
            offs.append((start + k * rb) // SUBLANES)
            scales.append(scale)
    n_out = len(offs) * rb
    f_blk = SUBLANES
    assert n_heads <= f_blk and (3 * w_att) % f_blk == 0
    return pl.pallas_call(
        functools.partial(_wprep_kernel, n_heads=n_heads),
        out_shape=(jax.ShapeDtypeStruct((n_out, d), BF16), jax.ShapeDtypeStruct((LANES, d), BF16)),
        grid_spec=pltpu.PrefetchScalarGridSpec(
            num_scalar_prefetch=2,
            grid=(len(offs),),
            in_specs=[
                pl.BlockSpec((pl.Element(rb), pl.Element(d)), lambda i, off, sc: (off[i] * SUBLANES, 0)),
                pl.BlockSpec((pl.Element(f_blk), pl.Element(d)), lambda i, off, sc: (3 * w_att, 0)),
            ],
            out_specs=(
                pl.BlockSpec((rb, d), lambda i, off, sc: (i, 0)),
                pl.BlockSpec((LANES, d), lambda i, off, sc: (0, 0)),
            ),
        ),
        compiler_params=_params(("arbitrary",)),
        name="inproj_weights",
    )(jnp.asarray(offs, jnp.int32), jnp.asarray(scales, F32), w_t, w_t)


def _inproj_kernel(x_ref, g_ref, w_ref, wf_ref, o_ref, vt_ref, f_ref, h_scr, *, nvt):
    j = pl.program_id(1)

    @pl.when(j == 0)
    def _():
        hb = _rmsnorm_f32(x_ref[...], g_ref[...]).astype(BF16)
        h_scr[...] = hb
        f_ref[...] = lax.dot_general(hb, wf_ref[...], NT_DIMS, preferred_element_type=F32)

    @pl.when(j < nvt)
    def _():
        vt_ref[...] = lax.dot_general(w_ref[...], h_scr[...], NT_DIMS,
                                      preferred_element_type=F32).astype(vt_ref.dtype)

    @pl.when(j >= nvt)
    def _():
        o_ref[...] = lax.dot_general(h_scr[...], w_ref[...], NT_DIMS,
                                     preferred_element_type=F32).astype(o_ref.dtype)


def _inproj(x2d, g, w_t, wf_t, n_v, tm, tn):
    t, d = x2d.shape
    n = w_t.shape[0] - n_v
    nvt = n_v // tn
    return pl.pallas_call(
        functools.partial(_inproj_kernel, nvt=nvt),
        out_shape=(jax.ShapeDtypeStruct((t, n), BF16), jax.ShapeDtypeStruct((n_v, t), BF16),
                   jax.ShapeDtypeStruct((t, LANES), F32)),
        grid=(t // tm, nvt + n // tn),
        in_specs=[
            pl.BlockSpec((tm, d), lambda m, j: (m, 0)),
            pl.BlockSpec((1, d), lambda m, j: (0, 0)),
            pl.BlockSpec((tn, d), lambda m, j: (j, 0)),
            pl.BlockSpec((LANES, d), lambda m, j: (0, 0)),
        ],
        out_specs=(
            pl.BlockSpec((tm, tn), lambda m, j: (m, jnp.maximum(j - nvt, 0))),
            pl.BlockSpec((tn, tm), lambda m, j: (jnp.minimum(j, nvt - 1), m)),
            pl.BlockSpec((tm, LANES), lambda m, j: (m, 0)),
        ),
        scratch_shapes=[pltpu.VMEM((tm, d), BF16)],
        compiler_params=_params(("parallel", "arbitrary")),
        name="inproj",
    )(x2d, g, w_t, wf_t)


def _split3(x):
    p1 = x.astype(BF16)
    r1 = x - p1.astype(F32)
    p2 = r1.astype(BF16)
    p3 = (r1 - p2.astype(F32)).astype(BF16)
    return p1, p2, p3


def _cumsum_kernel(f_ref, b_ref, lt_ref, c_ref, carry_scr, *, chunk, n_heads):
    @pl.when(pl.program_id(1) == 0)
    def _():
        carry_scr[...] = jnp.zeros(carry_scr.shape, F32)

    rows = f_ref.shape[0]
    lt = lt_ref[...]
    carry = carry_scr[...]
    for i in range(rows // chunk):
        lf = jax.nn.log_sigmoid(f_ref[i * chunk:(i + 1) * chunk, :] + b_ref[...])
        p1, p2, p3 = _split3(lf)
        cs = (jnp.dot(lt, p1, preferred_element_type=F32)
              + jnp.dot(lt, p2, preferred_element_type=F32)
              + jnp.dot(lt, p3, preferred_element_type=F32)) + carry
        for h in range(n_heads):
            c_ref[h, i * chunk:(i + 1) * chunk, :] = jnp.broadcast_to(cs[:, h:h + 1] * LOG2E, (chunk, LANES))
        carry = cs[chunk - 1:chunk, :]
    carry_scr[...] = carry


def _forget_cumsum(f_tok, b_row, batch, n_heads):
    t = f_tok.shape[0]
    s = t // batch
    rows = min(1024, s)
    chunk = min(256, s)
    lt = jnp.tril(jnp.ones((chunk, chunk), BF16))
    nr = s // rows
    return pl.pallas_call(
        functools.partial(_cumsum_kernel, chunk=chunk, n_heads=n_heads),
        out_shape=jax.ShapeDtypeStruct((n_heads, t, LANES), F32),
        grid=(batch, nr),
        in_specs=[
            pl.BlockSpec((rows, LANES), lambda b, r: (b * nr + r, 0)),
            pl.BlockSpec((1, LANES), lambda b, r: (0, 0)),
            pl.BlockSpec((chunk, chunk), lambda b, r: (0, 0)),
        ],
        out_specs=pl.BlockSpec((n_heads, rows, LANES), lambda b, r: (0, b * nr + r, 0)),
        scratch_shapes=[pltpu.VMEM((1, LANES), F32)],
        compiler_params=_params(("parallel", "arbitrary")),
        name="forget_cumsum",
    )(f_tok, b_row, lt)


def _fox_kernel(q_ref, k_ref, vt_ref, c_ref, o_ref, m_scr, l_scr, acc_scr, *, tq, tk, hp):
    i = pl.program_id(2)
    m_scr[...] = jnp.full(m_scr.shape, -jnp.inf, F32)
    l_scr[...] = jnp.zeros(l_scr.shape, F32)
    acc_scr[...] = jnp.zeros(acc_scr.shape, F32)

    def tile(j, masked, koff=0, nk=tk, qlo=0):
        start = pl.multiple_of(j * tk + koff, nk)
        nq = tq - qlo
        if masked:
            key = j * tk + koff + lax.broadcasted_iota(jnp.int32, (nk, nq), 0)
            qry = i * tq + qlo + lax.broadcasted_iota(jnp.int32, (nk, nq), 1)
            keep = key <= qry

        def scores(hh):
            lanes = slice(hh * HEAD_DIM, (hh + 1) * HEAD_DIM)
            kj = k_ref[pl.ds(start, nk), lanes]
            return lax.dot_general(kj, q_ref[qlo:tq, lanes], NT_DIMS, preferred_element_type=F32)

        s_all = [scores(hh) for hh in range(hp)]
        for hh in range(hp):
            s = s_all[hh]
            cj = c_ref[hh, pl.ds(start, nk), :]
            s = s - jnp.concatenate([cj] * (nq // LANES), axis=1)
            if masked:
                s = jnp.where(keep, s, -jnp.inf)
            m_prev = m_scr[hh, :, qlo:tq]
            m_new = jnp.maximum(m_prev, jnp.max(s, axis=0, keepdims=True))
            alpha = jnp.exp2(m_prev - m_new)
            p = jnp.exp2(s - m_new)
            l_scr[hh, :, qlo:tq] = alpha * l_scr[hh, :, qlo:tq] + jnp.sum(p, axis=0, keepdims=True)
            vtj = vt_ref[hh * HEAD_DIM:(hh + 1) * HEAD_DIM, pl.ds(start, nk)]
            acc_scr[hh, :, qlo:tq] = (alpha * acc_scr[hh, :, qlo:tq]
                                      + jnp.dot(vtj, p.astype(BF16), preferred_element_type=F32))
            m_scr[hh, :, qlo:tq] = m_new

    def body(j, carry):
        tile(j, False)
        return carry

    lax.fori_loop(0, i, body, 0)
    half = tk // 2
    tile(i, True, koff=0, nk=half, qlo=0)
    tile(i, True, koff=half, nk=half, qlo=tq - half)
    for hh in range(hp):
        o = acc_scr[hh] / l_scr[hh]
        o_ref[:, hh * HEAD_DIM:(hh + 1) * HEAD_DIM] = o.T.astype(o_ref.dtype)


def _fox_attention(proj, vt, c, batch, seq, n_heads, col_q, col_k, row_v, tq, hp):
    t = proj.shape[0]
    nq = seq // tq
    wb = hp * HEAD_DIM
    cq, ck, rv = col_q // wb, col_k // wb, row_v // wb
    return pl.pallas_call(
        functools.partial(_fox_kernel, tq=tq, tk=tq, hp=hp),
        out_shape=jax.ShapeDtypeStruct((t, n_heads * HEAD_DIM), BF16),
        grid=(batch, n_heads // hp, nq),
        in_specs=[
            pl.BlockSpec((tq, wb), lambda b, g, i: (b * nq + i, cq + g)),
            pl.BlockSpec((seq, wb), lambda b, g, i: (b, ck + g)),
            pl.BlockSpec((wb, seq), lambda b, g, i: (rv + g, b)),
            pl.BlockSpec((hp, seq, LANES), lambda b, g, i: (g, b, 0)),
        ],
        out_specs=pl.BlockSpec((tq, wb), lambda b, g, i: (b * nq + i, g)),
        scratch_shapes=[
            pltpu.VMEM((hp, 1, tq), F32),
            pltpu.VMEM((hp, 1, tq), F32),
            pltpu.VMEM((hp, HEAD_DIM, tq), F32),
        ],
        compiler_params=_params(("parallel", "parallel", "arbitrary")),
        name="fox_attention",
    )(proj, proj, vt, c)


def _sb_kernel(q_ref, k_ref, vt_ref, tri_ref, o_ref, r_scr, acc_scr, *, tq, tk, hp):
    i = pl.program_id(2)
    r_scr[...] = jnp.zeros(r_scr.shape, F32)
    acc_scr[...] = jnp.zeros(acc_scr.shape, F32)

    sign_bit = jnp.uint32(0x80000000)

    def tile(j, masked, qlo=0, qhi=tq):
        start = pl.multiple_of(j * tk, tk)
        nq = qhi - qlo
        if masked:
            key = j * tk + lax.broadcasted_iota(jnp.int32, (tk, nq), 0)
            qry = i * tq + qlo + lax.broadcasted_iota(jnp.int32, (tk, nq), 1)
            mask = key < qry

        def scores(hh):
            lanes = slice(hh * HEAD_DIM, (hh + 1) * HEAD_DIM)
            kj = k_ref[pl.ds(start, tk), lanes]
            return lax.dot_general(kj, q_ref[qlo:qhi, lanes], NT_DIMS, preferred_element_type=F32)

        u_all = [scores(hh) for hh in range(hp)]
        for hh in range(hp):
            u = u_all[hh]
            minus_abs = lax.bitcast_convert_type(lax.bitcast_convert_type(u, jnp.uint32) | sign_bit, F32)
            sp = jnp.maximum(u, 0.0) + jnp.log2(1.0 + jnp.exp2(minus_abs))
            if masked:
                sp = jnp.where(mask, sp, 0.0)
            w = jnp.dot(tri_ref[...], sp.astype(BF16), preferred_element_type=F32)
            a = jnp.exp2(u - ((sp + w) + r_scr[hh, :, qlo:qhi]))
            if masked:
                a = jnp.where(mask, a, 0.0)
            vtj = vt_ref[hh * HEAD_DIM:(hh + 1) * HEAD_DIM, pl.ds(start, tk)]
            acc_scr[hh, :, qlo:qhi] += jnp.dot(vtj, a.astype(BF16), preferred_element_type=F32)
            r_scr[hh, :, qlo:qhi] += jnp.sum(sp, axis=0, keepdims=True)

    ratio = max(tq // tk, 1)
    n_full = (i * tq) // tk
    for d in reversed(range(ratio)):
        tile(n_full + d, True, qlo=d * tk if tq > tk else 0)

    def live():
        return (jnp.min(r_scr[...]) <= R_DONE).astype(jnp.int32)

    def cond(carry):
        it, alive = carry
        return (it < n_full) & (alive > 0)

    def body(carry):
        it, _ = carry
        late_live = jnp.min(r_scr[:, :, tq // 2:]) <= R_DONE

        @pl.when(late_live)
        def _():
            tile(n_full - 1 - it, False)

        @pl.when(jnp.logical_not(late_live))
        def _():
            tile(n_full - 1 - it, False, qhi=tq // 2)

        return it + 1, live()

    lax.while_loop(cond, body, (jnp.int32(0), live()))
    for hh in range(hp):
        o_ref[:, hh * HEAD_DIM:(hh + 1) * HEAD_DIM] = acc_scr[hh].T.astype(o_ref.dtype)


def _sb_attention(proj, vt, batch, seq, n_heads, col_q, col_k, row_v, tq, tk, hp):
    t = proj.shape[0]
    nq = seq // tq
    wb = hp * HEAD_DIM
    cq, ck, rv = col_q // wb, col_k // wb, row_v // wb
    tri = jnp.triu(jnp.ones((tk, tk), BF16), 1)
    return pl.pallas_call(
        functools.partial(_sb_kernel, tq=tq, tk=tk, hp=hp),
        out_shape=jax.ShapeDtypeStruct((t, n_heads * HEAD_DIM), BF16),
        grid=(batch, n_heads // hp, nq),
        in_specs=[
            pl.BlockSpec((tq, wb), lambda b, g, i: (b * nq + i, cq + g)),
            pl.BlockSpec((seq, wb), lambda b, g, i: (b, ck + g)),
            pl.BlockSpec((wb, seq), lambda b, g, i: (rv + g, b)),
            pl.BlockSpec((tk, tk), lambda b, g, i: (0, 0)),
        ],
        out_specs=pl.BlockSpec((tq, wb), lambda b, g, i: (b * nq + i, g)),
        scratch_shapes=[
            pltpu.VMEM((hp, 1, tq), F32),
            pltpu.VMEM((hp, HEAD_DIM, tq), F32),
        ],
        compiler_params=_params(("parallel", "parallel", "arbitrary")),
        name="sb_attention",
    )(proj, proj, vt, tri)


_RESIDENT = dict(pipeline_mode=pl.Buffered(1))


def _merge_kernel(oa_ref, ob_ref, wa_ref, wb_ref, ga_ref, gb_ref, o_ref, *, tn):
    col = pl.multiple_of(pl.program_id(1) * tn, tn)
    ya = jnp.dot(oa_ref[...], wa_ref[:, pl.ds(col, tn)].astype(BF16), preferred_element_type=F32)
    yb = jnp.dot(ob_ref[...], wb_ref[:, pl.ds(col, tn)].astype(BF16), preferred_element_type=F32)
    ga = jax.nn.sigmoid(ga_ref[...].astype(F32))
    gb = jax.nn.sigmoid(gb_ref[...].astype(F32))
    o_ref[...] = (ga * ya + gb * yb).astype(o_ref.dtype)


def _merge(o_a, o_b, wa, wb, proj, col_ga, col_gb, tm, tn):
    t, ka = o_a.shape
    kb = o_b.shape[1]
    d = wa.shape[1]
    ca, cb = col_ga // tn, col_gb // tn
    return pl.pallas_call(
        functools.partial(_merge_kernel, tn=tn),
        out_shape=jax.ShapeDtypeStruct((t, d), BF16),
        grid=(t // tm, d // tn),
        in_specs=[
            pl.BlockSpec((tm, ka), lambda m, j: (m, 0)),
            pl.BlockSpec((tm, kb), lambda m, j: (m, 0)),
            pl.BlockSpec((ka, d), lambda m, j: (0, 0), **_RESIDENT),
            pl.BlockSpec((kb, d), lambda m, j: (0, 0), **_RESIDENT),
            pl.BlockSpec((tm, tn), lambda m, j: (m, ca + j)),
            pl.BlockSpec((tm, tn), lambda m, j: (m, cb + j)),
        ],
        out_specs=pl.BlockSpec((tm, tn), lambda m, j: (m, j)),
        compiler_params=_params(("parallel", "arbitrary")),
        name="branch_merge",
    )(o_a, o_b, wa, wb, proj, proj)


def _mixout_kernel(a_ref, w_ref, x_ref, o_ref, *, tn):
    col = pl.multiple_of(pl.program_id(1) * tn, tn)
    w = w_ref[:, pl.ds(col, tn)].astype(BF16)
    o_ref[...] = x_ref[...] + jnp.dot(a_ref[...], w, preferred_element_type=F32)


def _mixout(mixed, w, x2d, tm, tn):
    t, k = mixed.shape
    d = w.shape[1]
    return pl.pallas_call(
        functools.partial(_mixout_kernel, tn=tn),
        out_shape=jax.ShapeDtypeStruct((t, d), F32),
        grid=(t // tm, d // tn),
        in_specs=[
            pl.BlockSpec((tm, k), lambda m, j: (m, 0)),
            pl.BlockSpec((k, d), lambda m, j: (0, 0), **_RESIDENT),
            pl.BlockSpec((tm, tn), lambda m, j: (m, j)),
        ],
        out_specs=pl.BlockSpec((tm, tn), lambda m, j: (m, j)),
        compiler_params=_params(("parallel", "arbitrary")),
        name="mix_out",
    )(mixed, w, x2d)


def _router_kernel(x_ref, g_ref, wc_ref, lt_ref, sel_ref, h_ref, ids_ref, gate_ref, cnt_ref, carry_scr,
                   *, n_groups, per_group):
    @pl.when(pl.program_id(0) == 0)
    def _():
        carry_scr[...] = jnp.zeros(carry_scr.shape, F32)

    h = _rmsnorm_f32(x_ref[...], g_ref[...])
    hh = h.astype(BF16)
    _store_row_tiles(h_ref, _pack_bf16_pairs(h))
    hl = (h - hh.astype(F32)).astype(BF16)
    both = jnp.dot(hh, wc_ref[...], preferred_element_type=F32)
    logits = (both[:, :LANES] + jnp.dot(hl, wc_ref[:, :LANES], preferred_element_type=F32)) + both[:, LANES:]
    tm = logits.shape[0]
    lane = lax.broadcasted_iota(jnp.int32, (tm, LANES), 1)
    lane_f = lane.astype(F32)
    neg_inf = -jnp.inf

    def first_argmax(vals, valid):
        vmax = jnp.max(jnp.where(valid, vals, neg_inf), axis=-1, keepdims=True)
        idx = jnp.min(jnp.where(valid & (vals == vmax), lane_f, float(LANES)), axis=-1, keepdims=True)
        return vmax, idx.astype(jnp.int32)

    is_group = lane < n_groups
    gmax, gsel = first_argmax(logits, is_group)
    gsum = jnp.sum(jnp.where(is_group, jnp.exp(logits - gmax), 0.0), axis=-1, keepdims=True)
    p_group = 1.0 / gsum
    lo = n_groups + gsel * per_group
    in_group = (lane >= lo) & (lane < lo + per_group)
    v1, i1 = first_argmax(logits, in_group)
    v2, i2 = first_argmax(logits, in_group & (lane != i1))
    e2w = jnp.exp(v2 - v1)
    gate1 = (1.0 / (1.0 + e2w)) * p_group
    gate2 = (e2w / (1.0 + e2w)) * p_group
    e1 = i1 - n_groups
    e2 = i2 - n_groups

    onehot = ((lane == e1) | (lane == e2)).astype(BF16)
    before = jnp.dot(lt_ref[...], onehot, preferred_element_type=F32) + carry_scr[...]
    rank1 = jnp.sum(jnp.where(lane == e1, before, 0.0), axis=-1, keepdims=True).astype(jnp.int32)
    rank2 = jnp.sum(jnp.where(lane == e2, before, 0.0), axis=-1, keepdims=True).astype(jnp.int32)
    carry_new = carry_scr[...] + jnp.sum(onehot.astype(F32), axis=0, keepdims=True)
    carry_scr[...] = carry_new

    low = (1 << RANK_BITS) - 1
    cols = jnp.where(lane == 0, e1, jnp.where(lane == 1, e2, jnp.where(lane == 2, rank1 >> RANK_BITS,
           jnp.where(lane == 3, rank1 & low, jnp.where(lane == 4, rank2 >> RANK_BITS,
           jnp.where(lane == 5, rank2 & low, 0))))))
    ids_ref[...] = lax.dot_general(sel_ref[...], cols.astype(F32).astype(BF16), NT_DIMS,
                                   preferred_element_type=F32)
    gate_ref[...] = jnp.where(lane == 0, gate1, jnp.where(lane == 1, gate2, 0.0))
    cnt_ref[...] = jnp.broadcast_to(carry_new, cnt_ref.shape).astype(jnp.int32)


def _router(x2, g, w_cat, tm, n_groups, per_group):
    t, d = x2.shape
    lt = jnp.tril(jnp.ones((tm, tm), BF16), -1)
    sel = jnp.eye(8, LANES, dtype=BF16)
    return pl.pallas_call(
        functools.partial(_router_kernel, n_groups=n_groups, per_group=per_group),
        out_shape=(
            jax.ShapeDtypeStruct((t * (d // 2 // LANES), LANES), jnp.uint32),
            jax.ShapeDtypeStruct((8, t), F32),
            jax.ShapeDtypeStruct((t, LANES), F32),
            jax.ShapeDtypeStruct((8, LANES), jnp.int32),
        ),
        grid=(t // tm,),
        in_specs=[
            pl.BlockSpec((tm, d), lambda m: (m, 0)),
            pl.BlockSpec((1, d), lambda m: (0, 0)),
            pl.BlockSpec((d, 2 * LANES), lambda m: (0, 0)),
            pl.BlockSpec((tm, tm), lambda m: (0, 0)),
            pl.BlockSpec((8, LANES), lambda m: (0, 0)),
        ],
        out_specs=(
            pl.BlockSpec((tm * (d // 2 // LANES), LANES), lambda m: (m, 0)),
            pl.BlockSpec((8, tm), lambda m: (0, m)),
            pl.BlockSpec((tm, LANES), lambda m: (m, 0)),
            pl.BlockSpec((8, LANES), lambda m: (0, 0)),
        ),
        scratch_shapes=[pltpu.VMEM((1, LANES), F32)],
        compiler_params=_params(("arbitrary",)),
        name="router",
    )(x2, g, w_cat, lt, sel)


def _plan_kernel(cnt_ref, ids_ref, dest_ref, be_ref, first_ref, nxt_ref, plo_ref, pn_ref, nu_ref, ps_scr,
                 *, n_exp, bm, nb):
    shift = bm.bit_length() - 1

    def clear(k, c):
        first_ref[k] = 0
        nxt_ref[k] = -1
        return c

    lax.fori_loop(0, nb, clear, 0)

    def per_expert(e, carry):
        blk_start, last, prev_start = carry
        cnt = cnt_ref[0, e]
        n = lax.shift_right_logical(cnt + (bm - 1), shift)
        ps_scr[e] = blk_start * bm
        plo_ref[e] = blk_start * bm + cnt
        pn_ref[e] = n * bm - cnt

        def fill(k, c):
            be_ref[blk_start + k] = e
            return c

        lax.fori_loop(0, n, fill, 0)

        @pl.when(n > 0)
        def _():
            first_ref[blk_start] = 1

            @pl.when(prev_start >= 0)
            def _():
                nxt_ref[prev_start] = e

        used = n > 0
        return blk_start + n, jnp.where(used, e, last), jnp.where(used, blk_start, prev_start)

    n_used, last, _ = lax.fori_loop(0, n_exp, per_expert, (jnp.int32(0), jnp.int32(0), jnp.int32(-1)))
    nu_ref[0] = n_used

    def tail(k, c):
        be_ref[k] = last
        return c

    lax.fori_loop(n_used, nb, tail, 0)

    e1 = ids_ref[0:1, :]
    e2 = ids_ref[1:2, :]
    p1 = jnp.zeros_like(e1)
    p2 = jnp.zeros_like(e2)
    for e in range(n_exp):
        ps = ps_scr[e].astype(F32)
        p1 = jnp.where(e1 == e, ps, p1)
        p2 = jnp.where(e2 == e, ps, p2)
    dest_ref[...] = jnp.zeros(dest_ref.shape, jnp.int32)
    hi = float(1 << RANK_BITS)
    dest_ref[0:1, :] = (p1 + ids_ref[2:3, :] * hi + ids_ref[3:4, :]).astype(jnp.int32)
    dest_ref[1:2, :] = (p2 + ids_ref[4:5, :] * hi + ids_ref[5:6, :]).astype(jnp.int32)


def _plan(counts, ids_rows, n_exp, bm, nb):
    t = ids_rows.shape[1]
    return pl.pallas_call(
        functools.partial(_plan_kernel, n_exp=n_exp, bm=bm, nb=nb),
        out_shape=(
            jax.ShapeDtypeStruct((8, t), jnp.int32),
            jax.ShapeDtypeStruct((nb,), jnp.int32),
            jax.ShapeDtypeStruct((nb,), jnp.int32),
            jax.ShapeDtypeStruct((nb,), jnp.int32),
            jax.ShapeDtypeStruct((n_exp,), jnp.int32),
            jax.ShapeDtypeStruct((n_exp,), jnp.int32),
            jax.ShapeDtypeStruct((1,), jnp.int32),
        ),
        in_specs=[
            pl.BlockSpec(memory_space=pltpu.SMEM),
            pl.BlockSpec(memory_space=pltpu.VMEM),
        ],
        out_specs=(
            pl.BlockSpec(memory_space=pltpu.VMEM),
            pl.BlockSpec(memory_space=pltpu.SMEM),
            pl.BlockSpec(memory_space=pltpu.SMEM),
            pl.BlockSpec(memory_space=pltpu.SMEM),
            pl.BlockSpec(memory_space=pltpu.SMEM),
            pl.BlockSpec(memory_space=pltpu.SMEM),
            pl.BlockSpec(memory_space=pltpu.SMEM),
        ),
        scratch_shapes=[pltpu.SMEM((n_exp,), jnp.int32)],
        name="moe_plan",
    )(counts, ids_rows)


def _dispatch_kernel(d1_ref, d2_ref, plo_ref, pn_ref, nu_ref, h_ref, xb_ref, zbuf, sem, zsem,
                     *, tm, ns, bm, nb, n_exp):
    base = pl.program_id(0) * tm

    def zero_copies(fn):
        def per_expert(e, carry):
            pos = plo_ref[e]
            rem = pn_ref[e]
            chunk = bm // 2
            while chunk >= 1:
                hit = (rem & chunk) != 0

                @pl.when(hit)
                def _(pos=pos, chunk=chunk):
                    dst = xb_ref.at[pl.ds(pl.multiple_of(pos * ns, ns), chunk * ns)]
                    fn(pltpu.make_async_copy(zbuf.at[pl.ds(0, chunk * ns)], dst, zsem))

                pos = pos + jnp.where(hit, chunk, 0)
                chunk //= 2
            return carry

        lax.fori_loop(0, n_exp, per_expert, 0)

        def per_tail_block(b, carry):
            dst = xb_ref.at[pl.ds(pl.multiple_of(b * (bm * ns), bm * ns), bm * ns)]
            fn(pltpu.make_async_copy(zbuf, dst, zsem))
            return carry

        lax.fori_loop(nu_ref[0], nb, per_tail_block, 0)

    @pl.when(pl.program_id(0) == 0)
    def _():
        zbuf[...] = jnp.zeros(zbuf.shape, zbuf.dtype)
        zero_copies(lambda cp: cp.start())

    def row_copy(r, dest):
        src = h_ref.at[pl.ds(pl.multiple_of(r * ns, ns), ns)]
        return pltpu.make_async_copy(src, xb_ref.at[pl.ds(pl.multiple_of(dest * ns, ns), ns)], sem)

    def start(r, carry):
        row_copy(r, d1_ref[base + r]).start()
        row_copy(r, d2_ref[base + r]).start()
        return carry

    lax.fori_loop(0, tm, start, 0, unroll=DMA_UNROLL)
    for _ in range(2):
        pltpu.make_async_copy(h_ref, xb_ref.at[pl.ds(0, tm * ns)], sem).wait()

    @pl.when(pl.program_id(0) == pl.num_programs(0) - 1)
    def _():
        zero_copies(lambda cp: cp.wait())


def _dispatch(h2p, dest1, dest2, pad_lo, pad_n, n_used, nb, bm, tm, ns):
    t = h2p.shape[0] // ns
    return pl.pallas_call(
        functools.partial(_dispatch_kernel, tm=tm, ns=ns, bm=bm, nb=nb, n_exp=pad_lo.shape[0]),
        out_shape=jax.ShapeDtypeStruct((nb * bm * ns, LANES), h2p.dtype),
        grid_spec=pltpu.PrefetchScalarGridSpec(
            num_scalar_prefetch=5,
            grid=(t // tm,),
            in_specs=[pl.BlockSpec((tm * ns, LANES), lambda m, d1, d2, plo, pn, nu: (m, 0))],
            out_specs=pl.BlockSpec(memory_space=pl.ANY),
            scratch_shapes=[pltpu.VMEM((bm * ns, LANES), h2p.dtype), pltpu.SemaphoreType.DMA,
                            pltpu.SemaphoreType.DMA],
        ),
        compiler_params=_params(("arbitrary",)),
        name="moe_dispatch",
    )(dest1, dest2, pad_lo, pad_n, n_used, h2p)


def _experts_kernel(be_ref, first_ref, nxt_ref, nu_ref, x_ref, wg_hbm, wu_hbm, wd_hbm, y_ref,
                    wg_buf, wu_buf, wd_buf, sem, run_scr, *, ns):
    b = pl.program_id(0)

    def weight_copies(e, slot):
        return (pltpu.make_async_copy(wg_hbm.at[e], wg_buf.at[slot], sem.at[slot]),
                pltpu.make_async_copy(wu_hbm.at[e], wu_buf.at[slot], sem.at[slot]),
                pltpu.make_async_copy(wd_hbm.at[e], wd_buf.at[slot], sem.at[slot]))

    @pl.when(b == 0)
    def _():
        run_scr[0] = 0
        for cp in weight_copies(be_ref[0], 0):
            cp.start()

    @pl.when((first_ref[b] == 1) & (b < nu_ref[0]))
    def _():
        run = run_scr[0]
        slot = run & 1
        for cp in weight_copies(be_ref[b], slot):
            cp.wait()

        @pl.when(nxt_ref[b] >= 0)
        def _():
            for cp in weight_copies(nxt_ref[b], 1 - slot):
                cp.start()

        run_scr[0] = run + 1

    @pl.when(b < nu_ref[0])
    def _():
        slot = (run_scr[0] - 1) & 1
        x = _unpack_bf16_pairs(_load_row_tiles(x_ref, ns)).astype(BF16)
        g = jnp.dot(x, wg_buf[slot].astype(BF16), preferred_element_type=F32)
        u = jnp.dot(x, wu_buf[slot].astype(BF16), preferred_element_type=F32)
        hid = (g * jax.nn.sigmoid(g) * u).astype(BF16)
        y = jnp.dot(hid, wd_buf[slot].astype(BF16), preferred_element_type=F32)
        _store_row_tiles(y_ref, _pack_bf16_pairs(y))


def _experts(xb, w_gate, w_up, w_down, block_expert, first, nxt, n_used, bm, ns):
    d, de = w_gate.shape[1:]
    nb = xb.shape[0] // (bm * ns)

    def row_map(b, be, fi, nx, nu):
        return (jnp.minimum(b, nu[0] - 1), 0)

    return pl.pallas_call(
        functools.partial(_experts_kernel, ns=ns),
        out_shape=jax.ShapeDtypeStruct(xb.shape, jnp.uint32),
        grid_spec=pltpu.PrefetchScalarGridSpec(
            num_scalar_prefetch=4,
            grid=(nb,),
            in_specs=[
                pl.BlockSpec((bm * ns, LANES), row_map),
                pl.BlockSpec(memory_space=pl.ANY),
                pl.BlockSpec(memory_space=pl.ANY),
                pl.BlockSpec(memory_space=pl.ANY),
            ],
            out_specs=pl.BlockSpec((bm * ns, LANES), row_map),
            scratch_shapes=[
                pltpu.VMEM((2, d, de), F32),
                pltpu.VMEM((2, d, de), F32),
                pltpu.VMEM((2, de, d), F32),
                pltpu.SemaphoreType.DMA((2,)),
                pltpu.SMEM((1,), jnp.int32),
            ],
        ),
        input_output_aliases={4: 0},
        compiler_params=_params(("arbitrary",)),
        name="moe_experts",
    )(block_expert, first, nxt, n_used, xb, w_gate, w_up, w_down)


def _ple_kernel(d1_ref, d2_ref, x_ref, mg_ref, yb_ref, g_ref, p_ref, wg_ref, wp_ref, gf_ref, o_ref,
                ybuf, sem, x3_scr, h_scr, p_scr, *, tm, tn, ns, nj):
    m = pl.program_id(0)
    j = pl.program_id(1)
    n_m = pl.num_programs(0)
    n_j = pl.num_programs(1)
    slot = m & 1

    def row_copies(blk, r, s):
        tok = blk * tm + r
        src1 = yb_ref.at[pl.ds(pl.multiple_of(d1_ref[tok] * ns, ns), ns)]
        src2 = yb_ref.at[pl.ds(pl.multiple_of(d2_ref[tok] * ns, ns), ns)]
        rows = pl.ds(pl.multiple_of(r * ns, ns), ns)
        return (pltpu.make_async_copy(src1, ybuf.at[s, 0, rows], sem.at[s]),
                pltpu.make_async_copy(src2, ybuf.at[s, 1, rows], sem.at[s]))

    def drain(s):
        for k in range(2):
            pltpu.make_async_copy(yb_ref.at[pl.ds(0, tm * ns)], ybuf.at[s, k], sem.at[s]).wait()

    @pl.when(j == 0)
    def _():
        @pl.when(m == 0)
        def _():
            def start(r, carry):
                for cp in row_copies(0, r, 0):
                    cp.start()
                return carry

            lax.fori_loop(0, tm, start, 0, unroll=DMA_UNROLL)

        drain(slot)
        mg = mg_ref[...]
        y1 = _unpack_bf16_pairs(_load_row_tiles(ybuf.at[slot, 0], ns))
        y2 = _unpack_bf16_pairs(_load_row_tiles(ybuf.at[slot, 1], ns))
        x3 = x_ref[...] + mg[:, 0:1] * y1 + mg[:, 1:2] * y2
        x3_scr[...] = x3
        h_scr[...] = _rmsnorm_f32(x3, g_ref[...]).astype(BF16)
        p_scr[...] = p_ref[...].astype(BF16)

    nxt = jnp.minimum(m + 1, n_m - 1)
    rows_per_step = tm // nj
    for r in range(rows_per_step):
        for cp in row_copies(nxt, j * rows_per_step + r, 1 - slot):
            cp.start()

    col = pl.multiple_of(j * tn, tn)
    gate = jnp.dot(h_scr[...], wg_ref[:, pl.ds(col, tn)].astype(BF16), preferred_element_type=F32)
    ple = jnp.dot(p_scr[...], wp_ref[:, pl.ds(col, tn)].astype(BF16), preferred_element_type=F32)
    o_ref[:, pl.ds(col, tn)] = x3_scr[:, pl.ds(col, tn)] + jax.nn.sigmoid(gate) * ple

    @pl.when(j == n_j - 1)
    def _():
        o_ref[...] = _rmsnorm_f32(o_ref[...], gf_ref[...])

        @pl.when(m == n_m - 1)
        def _():
            drain(1 - slot)


def _combine_ple_final(x2, moe_gates, yb, dest1, dest2, g_ple, p2d, w_gate, w_proj, g_final, tm, tn, ns):
    t, d = x2.shape
    dp = p2d.shape[1]
    return pl.pallas_call(
        functools.partial(_ple_kernel, tm=tm, tn=tn, ns=ns, nj=d // tn),
        out_shape=jax.ShapeDtypeStruct((t, d), F32),
        grid_spec=pltpu.PrefetchScalarGridSpec(
            num_scalar_prefetch=2,
            grid=(t // tm, d // tn),
            in_specs=[
                pl.BlockSpec((tm, d), lambda m, j, d1, d2: (m, 0)),
                pl.BlockSpec((tm, LANES), lambda m, j, d1, d2: (m, 0)),
                pl.BlockSpec(memory_space=pl.ANY),
                pl.BlockSpec((1, d), lambda m, j, d1, d2: (0, 0)),
                pl.BlockSpec((tm, dp), lambda m, j, d1, d2: (m, 0)),
                pl.BlockSpec((d, d), lambda m, j, d1, d2: (0, 0), **_RESIDENT),
                pl.BlockSpec((dp, d), lambda m, j, d1, d2: (0, 0), **_RESIDENT),
                pl.BlockSpec((1, d), lambda m, j, d1, d2: (0, 0)),
            ],
            out_specs=pl.BlockSpec((tm, d), lambda m, j, d1, d2: (m, 0)),
            scratch_shapes=[
                pltpu.VMEM((2, 2, tm * ns, LANES), yb.dtype),
                pltpu.SemaphoreType.DMA((2,)),
                pltpu.VMEM((tm, d), F32),
                pltpu.VMEM((tm, d), BF16),
                pltpu.VMEM((tm, dp), BF16),
            ],
        ),
        compiler_params=_params(("arbitrary", "arbitrary")),
        name="combine_ple_final",
    )(dest1, dest2, x2, moe_gates, yb, g_ple, p2d, w_gate, w_proj, g_final)


def _tile(n, pref):
    return pref if n % pref == 0 else n


def _layer(x2d, p2d, batch, seq, w_in, b_forget, w_branch_fox, w_branch_sb, w_mix_out, g_mix, g_ffn,
           w_group, w_expert, w_gate, w_up, w_down, g_ple, w_ple_proj, w_ple_gate, g_final):
    t, d = x2d.shape
    n_heads = b_forget.shape[0]
    w_att = n_heads * HEAD_DIM
    n_exp = w_expert.shape[1]

    qs = LOG2E * HEAD_DIM ** -0.5
    w_t, wf_t = _wprep(w_in.T, w_att, n_heads, qs, 512)
    b_row = jnp.pad(b_forget, (0, LANES - n_heads)).reshape(1, LANES)
    col_qa, col_ka, col_qb, col_kb = 0, w_att, 2 * w_att, 3 * w_att
    col_ga, col_gb = 4 * w_att, 4 * w_att + d
    row_va, row_vb = 0, w_att

    tm = _tile(t, 1024)
    tn = _tile(d, 1024)
    proj, vt, f_tok = _inproj(x2d, g_mix.reshape(1, d), w_t, wf_t, 2 * w_att, tm, _tile(d // 2, 1024))
    c = _forget_cumsum(f_tok, b_row, batch, n_heads)

    tq = _tile(seq, 512)
    hp = 4
    o_a = _fox_attention(proj, vt, c, batch, seq, n_heads, col_qa, col_ka, row_va, tq, hp)
    o_b = _sb_attention(proj, vt, batch, seq, n_heads, col_qb, col_kb, row_vb, tq, _tile(seq, 256), hp)

    mixed = _merge(o_a, o_b, w_branch_fox, w_branch_sb, proj, col_ga, col_gb, tm, tn)
    x2 = _mixout(mixed, w_mix_out, x2d, tm, tn)

    w_r = jnp.concatenate([w_group, w_expert], axis=1)
    w_r = jnp.pad(w_r, ((0, 0), (0, LANES - w_r.shape[1])))
    w_r_hi = w_r.astype(BF16)
    w_r_lo = (w_r - w_r_hi.astype(F32)).astype(BF16)
    tr = _tile(t, 512)
    w_r_cat = jnp.concatenate([w_r_hi, w_r_lo], axis=1)
    h2, ids_rows, gates, counts = _router(x2, g_ffn.reshape(1, d), w_r_cat, tr, N_GROUPS, EXPERTS_PER_GROUP)

    bm = 256
    n_assign = 2 * t
    assert n_assign <= 1 << (2 * RANK_BITS), "in-expert ranks must fit two RANK_BITS-bit halves"
    nb = n_assign // bm + n_exp
    dest, block_expert, first, nxt, pad_lo, pad_n, n_used = _plan(counts, ids_rows, n_exp, bm, nb)
    dest1, dest2 = dest[0], dest[1]

    td = _tile(t, 2048)
    ns = d // 2 // LANES
    xb = _dispatch(h2, dest1, dest2, pad_lo, pad_n, n_used, nb, bm, td, ns)
    yb = _experts(xb, w_gate, w_up, w_down, block_expert, first, nxt, n_used, bm, ns)

    tp = _tile(t, 512)
    return _combine_ple_final(x2, gates, yb, dest1, dest2, g_ple.reshape(1, d), p2d, w_ple_gate, w_ple_proj,
                              g_final.reshape(1, d), tp, d, ns)


def kernel(x, p, w_in, b_forget, w_branch_fox, w_branch_sb, w_mix_out, g_mix, g_ffn, w_group, w_expert,
           w_gate, w_up, w_down, g_ple, w_ple_proj, w_ple_gate, g_final):
    b, s, d = x.shape
    depth = w_in.shape[0]
    assert depth == 1, "the final norm is fused into the single layer"
    x2d = x.reshape(b * s, d)
    out = _layer(x2d, p[0].reshape(b * s, -1), b, s, w_in[0], b_forget[0], w_branch_fox[0], w_branch_sb[0],
                 w_mix_out[0], g_mix[0], g_ffn[0], w_group[0], w_expert[0], w_gate[0], w_up[0], w_down[0],
                 g_ple[0], w_ple_proj[0], w_ple_gate[0], g_final)
    return out.reshape(b, s, d)
```

```python
import functools

import jax
import jax.numpy as jnp
from jax import lax
from jax.experimental import pallas as pl
from jax.experimental.pallas import tpu as pltpu

F32 = jnp.float32
BF16 = jnp.bfloat16

HEAD_DIM = 128
N_GROUPS = 4
EXPERTS_PER_GROUP = 8
EPS = 1e-6
LANES = 128
SUBLANES = 8
VMEM_LIMIT = 56 * 1024 * 1024

LOG2E = 1.4426950408889634
DMA_UNROLL = 8
RANK_BITS = 7
R_DONE = 160.0
MASKED_SCORE = -1e4

NT_DIMS = (((1,), (1,)), ((), ()))


def _params(sem):
    return pltpu.CompilerParams(dimension_semantics=sem, vmem_limit_bytes=VMEM_LIMIT)


def _rmsnorm_f32(x, g):
    ms = jnp.mean(x * x, axis=-1, keepdims=True)
    return x * lax.rsqrt(ms + EPS) * g


def _pack_bf16_pairs(x):
    n = x.shape[1] // 2
    bits = lax.bitcast_convert_type(x.astype(BF16).astype(F32), jnp.uint32)
    return (bits[:, :n] >> 16) | (bits[:, n:] & jnp.uint32(0xFFFF0000))


def _unpack_bf16_pairs(w):
    lo = lax.bitcast_convert_type(w << 16, F32)
    hi = lax.bitcast_convert_type(w & jnp.uint32(0xFFFF0000), F32)
    return jnp.concatenate([lo, hi], axis=1)


def _store_row_tiles(ref, words):
    n = words.shape[0]
    ns = words.shape[1] // LANES
    for c in range(ns):
        ref[pl.ds(c, n, stride=ns), :] = words[:, c * LANES:(c + 1) * LANES]


def _load_row_tiles(ref, ns):
    n = ref.shape[0] // ns
    return jnp.concatenate([ref[pl.ds(c, n, stride=ns), :] for c in range(ns)], axis=1)


def _wprep_kernel(off_ref, scale_ref, w_ref, f_ref, wt_ref, wf_ref, *, n_heads):
    i = pl.program_id(0)
    wt_ref[...] = (w_ref[...] * scale_ref[i]).astype(BF16)

    @pl.when(i == 0)
    def _():
        row = lax.broadcasted_iota(jnp.int32, wf_ref.shape, 0)
        f_rows = jnp.concatenate([f_ref[...]] * (wf_ref.shape[0] // f_ref.shape[0]), axis=0)
        wf_ref[...] = jnp.where(row < n_heads, f_rows, 0.0).astype(BF16)


def _wprep(w_t, w_att, n_heads, qs, rb):
    n_in, d = w_t.shape
    off_b = 3 * w_att + n_heads
    segments = [(2 * w_att, w_att, 1.0), (off_b + 2 * w_att, w_att, 1.0),
                (0, w_att, qs), (w_att, w_att, 1.0),
                (off_b, w_att, qs), (off_b + w_att, w_att, 1.0),
                (off_b + 3 * w_att, n_in - off_b - 3 * w_att, 1.0)]
    offs, scales = [], []
    for start, length, scale in segments:
        assert length % rb == 0 and start % SUBLANES == 0
        for k in range(length // rb):
            offs.append((start + k * rb) // SUBLANES)
            scales.append(scale)
    n_out = len(offs) * rb
    f_blk = SUBLANES
    assert n_heads <= f_blk and (3 * w_att) % f_blk == 0
    return pl.pallas_call(
        functools.partial(_wprep_kernel, n_heads=n_heads),
        out_shape=(jax.ShapeDtypeStruct((n_out, d), BF16), jax.ShapeDtypeStruct((LANES, d), BF16)),
        grid_spec=pltpu.PrefetchScalarGridSpec(
            num_scalar_prefetch=2,
            grid=(len(offs),),
            in_specs=[
                pl.BlockSpec((pl.Element(rb), pl.Element(d)), lambda i, off, sc: (off[i] * SUBLANES, 0)),
                pl.BlockSpec((pl.Element(f_blk), pl.Element(d)), lambda i, off, sc: (3 * w_att, 0)),
            ],
            out_specs=(
                pl.BlockSpec((rb, d), lambda i, off, sc: (i, 0)),
                pl.BlockSpec((LANES, d), lambda i, off, sc: (0, 0)),
            ),
        ),
        compiler_params=_params(("arbitrary",)),
        name="inproj_weights",
    )(jnp.asarray(offs, jnp.int32), jnp.asarray(scales, F32), w_t, w_t)


def _inproj_kernel(x_ref, g_ref, w_ref, wf_ref, o_ref, vt_ref, f_ref, h_scr, *, nvt):
    j = pl.program_id(1)

    @pl.when(j == 0)
    def _():
        hb = _rmsnorm_f32(x_ref[...], g_ref[...]).astype(BF16)
        h_scr[...] = hb
        f_ref[...] = lax.dot_general(hb, wf_ref[...], NT_DIMS, preferred_element_type=F32)

    @pl.when(j < nvt)
    def _():
        vt_ref[...] = lax.dot_general(w_ref[...], h_scr[...], NT_DIMS,
                                      preferred_element_type=F32).astype(vt_ref.dtype)

    @pl.when(j >= nvt)
    def _():
        o_ref[...] = lax.dot_general(h_scr[...], w_ref[...], NT_DIMS,
                                     preferred_element_type=F32).astype(o_ref.dtype)


def _inproj(x2d, g, w_t, wf_t, n_v, tm, tn):
    t, d = x2d.shape
    n = w_t.shape[0] - n_v
    nvt = n_v // tn
    return pl.pallas_call(
        functools.partial(_inproj_kernel, nvt=nvt),
        out_shape=(jax.ShapeDtypeStruct((t, n), BF16), jax.ShapeDtypeStruct((n_v, t), BF16),
                   jax.ShapeDtypeStruct((t, LANES), F32)),
        grid=(t // tm, nvt + n // tn),
        in_specs=[
            pl.BlockSpec((tm, d), lambda m, j: (m, 0)),
            pl.BlockSpec((1, d), lambda m, j: (0, 0)),
            pl.BlockSpec((tn, d), lambda m, j: (j, 0)),
            pl.BlockSpec((LANES, d), lambda m, j: (0, 0)),
        ],
        out_specs=(
            pl.BlockSpec((tm, tn), lambda m, j: (m, jnp.maximum(j - nvt, 0))),
            pl.BlockSpec((tn, tm), lambda m, j: (jnp.minimum(j, nvt - 1), m)),
            pl.BlockSpec((tm, LANES), lambda m, j: (m, 0)),
        ),
        scratch_shapes=[pltpu.VMEM((tm, d), BF16)],
        compiler_params=_params(("parallel", "arbitrary")),
        name="inproj",
    )(x2d, g, w_t, wf_t)


def _split3(x):
    p1 = x.astype(BF16)
    r1 = x - p1.astype(F32)
    p2 = r1.astype(BF16)
    p3 = (r1 - p2.astype(F32)).astype(BF16)
    return p1, p2, p3


def _cumsum_kernel(f_ref, b_ref, lt_ref, c_ref, carry_scr, *, chunk, n_heads):
    @pl.when(pl.program_id(1) == 0)
    def _():
        carry_scr[...] = jnp.zeros(carry_scr.shape, F32)

    rows = f_ref.shape[0]
    lt = lt_ref[...]
    carry = carry_scr[...]
    for i in range(rows // chunk):
        lf = jax.nn.log_sigmoid(f_ref[i * chunk:(i + 1) * chunk, :] + b_ref[...])
        p1, p2, p3 = _split3(lf)
        cs = (jnp.dot(lt, p1, preferred_element_type=F32)
              + jnp.dot(lt, p2, preferred_element_type=F32)
              + jnp.dot(lt, p3, preferred_element_type=F32)) + carry
        for h in range(n_heads):
            c_ref[h, i * chunk:(i + 1) * chunk, :] = jnp.broadcast_to(cs[:, h:h + 1] * LOG2E, (chunk, LANES))
        carry = cs[chunk - 1:chunk, :]
    carry_scr[...] = carry


def _forget_cumsum(f_tok, b_row, batch, n_heads):
    t = f_tok.shape[0]
    s = t // batch
    rows = min(1024, s)
    chunk = min(256, s)
    lt = jnp.tril(jnp.ones((chunk, chunk), BF16))
    nr = s // rows
    return pl.pallas_call(
        functools.partial(_cumsum_kernel, chunk=chunk, n_heads=n_heads),
        out_shape=jax.ShapeDtypeStruct((n_heads, t, LANES), F32),
        grid=(batch, nr),
        in_specs=[
            pl.BlockSpec((rows, LANES), lambda b, r: (b * nr + r, 0)),
            pl.BlockSpec((1, LANES), lambda b, r: (0, 0)),
            pl.BlockSpec((chunk, chunk), lambda b, r: (0, 0)),
        ],
        out_specs=pl.BlockSpec((n_heads, rows, LANES), lambda b, r: (0, b * nr + r, 0)),
        scratch_shapes=[pltpu.VMEM((1, LANES), F32)],
        compiler_params=_params(("parallel", "arbitrary")),
        name="forget_cumsum",
    )(f_tok, b_row, lt)


def _fox_kernel(q_ref, k_ref, vt_ref, c_ref, o_ref, m_scr, l_scr, acc_scr, *, tq, tk, hp):
    i = pl.program_id(2)
    m_scr[...] = jnp.full(m_scr.shape, -jnp.inf, F32)
    l_scr[...] = jnp.zeros(l_scr.shape, F32)
    acc_scr[...] = jnp.zeros(acc_scr.shape, F32)

    def tile(j, masked, koff=0, nk=tk, qlo=0):
        start = pl.multiple_of(j * tk + koff, nk)
        nq = tq - qlo
        if masked:
            key = j * tk + koff + lax.broadcasted_iota(jnp.int32, (nk, nq), 0)
            qry = i * tq + qlo + lax.broadcasted_iota(jnp.int32, (nk, nq), 1)
            keep = key <= qry

        def scores(hh):
            lanes = slice(hh * HEAD_DIM, (hh + 1) * HEAD_DIM)
            kj = k_ref[pl.ds(start, nk), lanes]
            return lax.dot_general(kj, q_ref[qlo:tq, lanes], NT_DIMS, preferred_element_type=F32)

        s_all = [scores(hh) for hh in range(hp)]
        for hh in range(hp):
            s = s_all[hh]
            cj = c_ref[hh, pl.ds(start, nk), :]
            s = s - jnp.concatenate([cj] * (nq // LANES), axis=1)
            if masked:
                s = jnp.where(keep, s, -jnp.inf)
            m_prev = m_scr[hh, :, qlo:tq]
            m_new = jnp.maximum(m_prev, jnp.max(s, axis=0, keepdims=True))
            alpha = jnp.exp2(m_prev - m_new)
            p = jnp.exp2(s - m_new)
            l_scr[hh, :, qlo:tq] = alpha * l_scr[hh, :, qlo:tq] + jnp.sum(p, axis=0, keepdims=True)
            vtj = vt_ref[hh * HEAD_DIM:(hh + 1) * HEAD_DIM, pl.ds(start, nk)]
            acc_scr[hh, :, qlo:tq] = (alpha * acc_scr[hh, :, qlo:tq]
                                      + jnp.dot(vtj, p.astype(BF16), preferred_element_type=F32))
            m_scr[hh, :, qlo:tq] = m_new

    def body(j, carry):
        tile(j, False)
        return carry

    lax.fori_loop(0, i, body, 0)
    half = tk // 2
    tile(i, True, koff=0, nk=half, qlo=0)
    tile(i, True, koff=half, nk=half, qlo=tq - half)
    for hh in range(hp):
        o = acc_scr[hh] / l_scr[hh]
        o_ref[:, hh * HEAD_DIM:(hh + 1) * HEAD_DIM] = o.T.astype(o_ref.dtype)


def _fox_attention(proj, vt, c, batch, seq, n_heads, col_q, col_k, row_v, tq, hp):
    t = proj.shape[0]
    nq = seq // tq
    wb = hp * HEAD_DIM
    cq, ck, rv = col_q // wb, col_k // wb, row_v // wb
    return pl.pallas_call(
        functools.partial(_fox_kernel, tq=tq, tk=tq, hp=hp),
        out_shape=jax.ShapeDtypeStruct((t, n_heads * HEAD_DIM), BF16),
        grid=(batch, n_heads // hp, nq),
        in_specs=[
            pl.BlockSpec((tq, wb), lambda b, g, i: (b * nq + i, cq + g)),
            pl.BlockSpec((seq, wb), lambda b, g, i: (b, ck + g)),
            pl.BlockSpec((wb, seq), lambda b, g, i: (rv + g, b)),
            pl.BlockSpec((hp, seq, LANES), lambda b, g, i: (g, b, 0)),
        ],
        out_specs=pl.BlockSpec((tq, wb), lambda b, g, i: (b * nq + i, g)),
        scratch_shapes=[
            pltpu.VMEM((hp, 1, tq), F32),
            pltpu.VMEM((hp, 1, tq), F32),
            pltpu.VMEM((hp, HEAD_DIM, tq), F32),
        ],
        compiler_params=_params(("parallel", "parallel", "arbitrary")),
        name="fox_attention",
    )(proj, proj, vt, c)


def _sb_kernel(q_ref, k_ref, vt_ref, tri_ref, o_ref, r_scr, acc_scr, *, tq, tk, hp):
    i = pl.program_id(2)
    r_scr[...] = jnp.zeros(r_scr.shape, F32)
    acc_scr[...] = jnp.zeros(acc_scr.shape, F32)

    sign_bit = jnp.uint32(0x80000000)

    def tile(j, masked, qlo=0, qhi=tq):
        start = pl.multiple_of(j * tk, tk)
        nq = qhi - qlo
        if masked:
            key = j * tk + lax.broadcasted_iota(jnp.int32, (tk, nq), 0)
            qry = i * tq + qlo + lax.broadcasted_iota(jnp.int32, (tk, nq), 1)
            mask = key < qry

        def scores(hh):
            lanes = slice(hh * HEAD_DIM, (hh + 1) * HEAD_DIM)
            kj = k_ref[pl.ds(start, tk), lanes]
            return lax.dot_general(kj, q_ref[qlo:qhi, lanes], NT_DIMS, preferred_element_type=F32)

        u_all = [scores(hh) for hh in range(hp)]
        for hh in range(hp):
            u = u_all[hh]
            if masked:
                u = jnp.where(mask, u, MASKED_SCORE)
            minus_abs = lax.bitcast_convert_type(lax.bitcast_convert_type(u, jnp.uint32) | sign_bit, F32)
            sp = jnp.maximum(u, 0.0) + jnp.log2(1.0 + jnp.exp2(minus_abs))
            w = jnp.dot(tri_ref[...], sp.astype(BF16), preferred_element_type=F32)
            a = jnp.exp2(u - ((sp + w) + r_scr[hh, :, qlo:qhi]))
            vtj = vt_ref[hh * HEAD_DIM:(hh + 1) * HEAD_DIM, pl.ds(start, tk)]
            acc_scr[hh, :, qlo:qhi] += jnp.dot(vtj, a.astype(BF16), preferred_element_type=F32)
            r_scr[hh, :, qlo:qhi] += jnp.sum(sp, axis=0, keepdims=True)

    ratio = max(tq // tk, 1)
    n_full = (i * tq) // tk
    for d in reversed(range(ratio)):
        tile(n_full + d, True, qlo=d * tk if tq > tk else 0)

    def live():
        return (jnp.min(r_scr[...]) <= R_DONE).astype(jnp.int32)

    def cond(carry):
        it, alive = carry
        return (it < n_full) & (alive > 0)

    def body(carry):
        it, _ = carry
        late_live = jnp.min(r_scr[:, :, tq // 2:]) <= R_DONE

        @pl.when(late_live)
        def _():
            tile(n_full - 1 - it, False)

        @pl.when(jnp.logical_not(late_live))
        def _():
            tile(n_full - 1 - it, False, qhi=tq // 2)

        return it + 1, live()

    lax.while_loop(cond, body, (jnp.int32(0), live()))
    for hh in range(hp):
        o_ref[:, hh * HEAD_DIM:(hh + 1) * HEAD_DIM] = acc_scr[hh].T.astype(o_ref.dtype)


def _sb_attention(proj, vt, batch, seq, n_heads, col_q, col_k, row_v, tq, tk, hp):
    t = proj.shape[0]
    nq = seq // tq
    wb = hp * HEAD_DIM
    cq, ck, rv = col_q // wb, col_k // wb, row_v // wb
    tri = jnp.triu(jnp.ones((tk, tk), BF16), 1)
    return pl.pallas_call(
        functools.partial(_sb_kernel, tq=tq, tk=tk, hp=hp),
        out_shape=jax.ShapeDtypeStruct((t, n_heads * HEAD_DIM), BF16),
        grid=(batch, n_heads // hp, nq),
        in_specs=[
            pl.BlockSpec((tq, wb), lambda b, g, i: (b * nq + i, cq + g)),
            pl.BlockSpec((seq, wb), lambda b, g, i: (b, ck + g)),
            pl.BlockSpec((wb, seq), lambda b, g, i: (rv + g, b)),
            pl.BlockSpec((tk, tk), lambda b, g, i: (0, 0)),
        ],
        out_specs=pl.BlockSpec((tq, wb), lambda b, g, i: (b * nq + i, g)),
        scratch_shapes=[
            pltpu.VMEM((hp, 1, tq), F32),
            pltpu.VMEM((hp, HEAD_DIM, tq), F32),
        ],
        compiler_params=_params(("parallel", "parallel", "arbitrary")),
        name="sb_attention",
    )(proj, proj, vt, tri)


_RESIDENT = dict(pipeline_mode=pl.Buffered(1))


def _merge_kernel(oa_ref, ob_ref, wa_ref, wb_ref, ga_ref, gb_ref, o_ref, *, tn):
    col = pl.multiple_of(pl.program_id(1) * tn, tn)
    ya = jnp.dot(oa_ref[...], wa_ref[:, pl.ds(col, tn)].astype(BF16), preferred_element_type=F32)
    yb = jnp.dot(ob_ref[...], wb_ref[:, pl.ds(col, tn)].astype(BF16), preferred_element_type=F32)
    ga = jax.nn.sigmoid(ga_ref[...].astype(F32))
    gb = jax.nn.sigmoid(gb_ref[...].astype(F32))
    o_ref[...] = (ga * ya + gb * yb).astype(o_ref.dtype)


def _merge(o_a, o_b, wa, wb, proj, col_ga, col_gb, tm, tn):
    t, ka = o_a.shape
    kb = o_b.shape[1]
    d = wa.shape[1]
    ca, cb = col_ga // tn, col_gb // tn
    return pl.pallas_call(
        functools.partial(_merge_kernel, tn=tn),
        out_shape=jax.ShapeDtypeStruct((t, d), BF16),
        grid=(t // tm, d // tn),
        in_specs=[
            pl.BlockSpec((tm, ka), lambda m, j: (m, 0)),
            pl.BlockSpec((tm, kb), lambda m, j: (m, 0)),
            pl.BlockSpec((ka, d), lambda m, j: (0, 0), **_RESIDENT),
            pl.BlockSpec((kb, d), lambda m, j: (0, 0), **_RESIDENT),
            pl.BlockSpec((tm, tn), lambda m, j: (m, ca + j)),
            pl.BlockSpec((tm, tn), lambda m, j: (m, cb + j)),
        ],
        out_specs=pl.BlockSpec((tm, tn), lambda m, j: (m, j)),
        compiler_params=_params(("parallel", "arbitrary")),
        name="branch_merge",
    )(o_a, o_b, wa, wb, proj, proj)


def _mixout_kernel(a_ref, w_ref, x_ref, o_ref, *, tn):
    col = pl.multiple_of(pl.program_id(1) * tn, tn)
    w = w_ref[:, pl.ds(col, tn)].astype(BF16)
    o_ref[...] = x_ref[...] + jnp.dot(a_ref[...], w, preferred_element_type=F32)


def _mixout(mixed, w, x2d, tm, tn):
    t, k = mixed.shape
    d = w.shape[1]
    return pl.pallas_call(
        functools.partial(_mixout_kernel, tn=tn),
        out_shape=jax.ShapeDtypeStruct((t, d), F32),
        grid=(t // tm, d // tn),
        in_specs=[
            pl.BlockSpec((tm, k), lambda m, j: (m, 0)),
            pl.BlockSpec((k, d), lambda m, j: (0, 0), **_RESIDENT),
            pl.BlockSpec((tm, tn), lambda m, j: (m, j)),
        ],
        out_specs=pl.BlockSpec((tm, tn), lambda m, j: (m, j)),
        compiler_params=_params(("parallel", "arbitrary")),
        name="mix_out",
    )(mixed, w, x2d)


def _router_kernel(x_ref, g_ref, wc_ref, lt_ref, sel_ref, h_ref, ids_ref, gate_ref, cnt_ref, carry_scr,
                   *, n_groups, per_group):
    @pl.when(pl.program_id(0) == 0)
    def _():
        carry_scr[...] = jnp.zeros(carry_scr.shape, F32)

    h = _rmsnorm_f32(x_ref[...], g_ref[...])
    hh = h.astype(BF16)
    _store_row_tiles(h_ref, _pack_bf16_pairs(h))
    hl = (h - hh.astype(F32)).astype(BF16)
    both = jnp.dot(hh, wc_ref[...], preferred_element_type=F32)
    logits = (both[:, :LANES] + jnp.dot(hl, wc_ref[:, :LANES], preferred_element_type=F32)) + both[:, LANES:]
    tm = logits.shape[0]
    lane = lax.broadcasted_iota(jnp.int32, (tm, LANES), 1)
    lane_f = lane.astype(F32)
    neg_inf = -jnp.inf

    def first_argmax(vals, valid):
        vmax = jnp.max(jnp.where(valid, vals, neg_inf), axis=-1, keepdims=True)
        idx = jnp.min(jnp.where(valid & (vals == vmax), lane_f, float(LANES)), axis=-1, keepdims=True)
        return vmax, idx.astype(jnp.int32)

    is_group = lane < n_groups
    gmax, gsel = first_argmax(logits, is_group)
    gsum = jnp.sum(jnp.where(is_group, jnp.exp(logits - gmax), 0.0), axis=-1, keepdims=True)
    p_group = 1.0 / gsum
    lo = n_groups + gsel * per_group
    in_group = (lane >= lo) & (lane < lo + per_group)
    v1, i1 = first_argmax(logits, in_group)
    v2, i2 = first_argmax(logits, in_group & (lane != i1))
    e2w = jnp.exp(v2 - v1)
    gate1 = (1.0 / (1.0 + e2w)) * p_group
    gate2 = (e2w / (1.0 + e2w)) * p_group
    e1 = i1 - n_groups
    e2 = i2 - n_groups

    onehot = ((lane == e1) | (lane == e2)).astype(BF16)
    before = jnp.dot(lt_ref[...], onehot, preferred_element_type=F32) + carry_scr[...]
    rank1 = jnp.sum(jnp.where(lane == e1, before, 0.0), axis=-1, keepdims=True).astype(jnp.int32)
    rank2 = jnp.sum(jnp.where(lane == e2, before, 0.0), axis=-1, keepdims=True).astype(jnp.int32)
    carry_new = carry_scr[...] + jnp.sum(onehot.astype(F32), axis=0, keepdims=True)
    carry_scr[...] = carry_new

    low = (1 << RANK_BITS) - 1
    cols = jnp.where(lane == 0, e1, jnp.where(lane == 1, e2, jnp.where(lane == 2, rank1 >> RANK_BITS,
           jnp.where(lane == 3, rank1 & low, jnp.where(lane == 4, rank2 >> RANK_BITS,
           jnp.where(lane == 5, rank2 & low, 0))))))
    ids_ref[...] = lax.dot_general(sel_ref[...], cols.astype(F32).astype(BF16), NT_DIMS,
                                   preferred_element_type=F32)
    gate_ref[...] = jnp.where(lane == 0, gate1, jnp.where(lane == 1, gate2, 0.0))
    cnt_ref[...] = jnp.broadcast_to(carry_new, cnt_ref.shape).astype(jnp.int32)


def _router(x2, g, w_cat, tm, n_groups, per_group):
    t, d = x2.shape
    lt = jnp.tril(jnp.ones((tm, tm), BF16), -1)
    sel = jnp.eye(8, LANES, dtype=BF16)
    return pl.pallas_call(
        functools.partial(_router_kernel, n_groups=n_groups, per_group=per_group),
        out_shape=(
            jax.ShapeDtypeStruct((t * (d // 2 // LANES), LANES), jnp.uint32),
            jax.ShapeDtypeStruct((8, t), F32),
            jax.ShapeDtypeStruct((t, LANES), F32),
            jax.ShapeDtypeStruct((8, LANES), jnp.int32),
        ),
        grid=(t // tm,),
        in_specs=[
            pl.BlockSpec((tm, d), lambda m: (m, 0)),
            pl.BlockSpec((1, d), lambda m: (0, 0)),
            pl.BlockSpec((d, 2 * LANES), lambda m: (0, 0)),
            pl.BlockSpec((tm, tm), lambda m: (0, 0)),
            pl.BlockSpec((8, LANES), lambda m: (0, 0)),
        ],
        out_specs=(
            pl.BlockSpec((tm * (d // 2 // LANES), LANES), lambda m: (m, 0)),
            pl.BlockSpec((8, tm), lambda m: (0, m)),
            pl.BlockSpec((tm, LANES), lambda m: (m, 0)),
            pl.BlockSpec((8, LANES), lambda m: (0, 0)),
        ),
        scratch_shapes=[pltpu.VMEM((1, LANES), F32)],
        compiler_params=_params(("arbitrary",)),
        name="router",
    )(x2, g, w_cat, lt, sel)


def _plan_kernel(cnt_ref, ids_ref, dest_ref, be_ref, first_ref, nxt_ref, plo_ref, pn_ref, nu_ref, ps_scr,
                 *, n_exp, bm, nb):
    shift = bm.bit_length() - 1

    def clear(k, c):
        first_ref[k] = 0
        nxt_ref[k] = -1
        return c

    lax.fori_loop(0, nb, clear, 0)

    def per_expert(e, carry):
        blk_start, last, prev_start = carry
        cnt = cnt_ref[0, e]
        n = lax.shift_right_logical(cnt + (bm - 1), shift)
        ps_scr[e] = blk_start * bm
        plo_ref[e] = blk_start * bm + cnt
        pn_ref[e] = n * bm - cnt

        def fill(k, c):
            be_ref[blk_start + k] = e
            return c

        lax.fori_loop(0, n, fill, 0)

        @pl.when(n > 0)
        def _():
            first_ref[blk_start] = 1

            @pl.when(prev_start >= 0)
            def _():
                nxt_ref[prev_start] = e

        used = n > 0
        return blk_start + n, jnp.where(used, e, last), jnp.where(used, blk_start, prev_start)

    n_used, last, _ = lax.fori_loop(0, n_exp, per_expert, (jnp.int32(0), jnp.int32(0), jnp.int32(-1)))
    nu_ref[0] = n_used

    def tail(k, c):
        be_ref[k] = last
        return c

    lax.fori_loop(n_used, nb, tail, 0)

    e1 = ids_ref[0:1, :]
    e2 = ids_ref[1:2, :]
    p1 = jnp.zeros_like(e1)
    p2 = jnp.zeros_like(e2)
    for e in range(n_exp):
        ps = ps_scr[e].astype(F32)
        p1 = jnp.where(e1 == e, ps, p1)
        p2 = jnp.where(e2 == e, ps, p2)
    dest_ref[...] = jnp.zeros(dest_ref.shape, jnp.int32)
    hi = float(1 << RANK_BITS)
    dest_ref[0:1, :] = (p1 + ids_ref[2:3, :] * hi + ids_ref[3:4, :]).astype(jnp.int32)
    dest_ref[1:2, :] = (p2 + ids_ref[4:5, :] * hi + ids_ref[5:6, :]).astype(jnp.int32)


def _plan(counts, ids_rows, n_exp, bm, nb):
    t = ids_rows.shape[1]
    return pl.pallas_call(
        functools.partial(_plan_kernel, n_exp=n_exp, bm=bm, nb=nb),
        out_shape=(
            jax.ShapeDtypeStruct((8, t), jnp.int32),
            jax.ShapeDtypeStruct((nb,), jnp.int32),
            jax.ShapeDtypeStruct((nb,), jnp.int32),
            jax.ShapeDtypeStruct((nb,), jnp.int32),
            jax.ShapeDtypeStruct((n_exp,), jnp.int32),
            jax.ShapeDtypeStruct((n_exp,), jnp.int32),
            jax.ShapeDtypeStruct((1,), jnp.int32),
        ),
        in_specs=[
            pl.BlockSpec(memory_space=pltpu.SMEM),
            pl.BlockSpec(memory_space=pltpu.VMEM),
        ],
        out_specs=(
            pl.BlockSpec(memory_space=pltpu.VMEM),
            pl.BlockSpec(memory_space=pltpu.SMEM),
            pl.BlockSpec(memory_space=pltpu.SMEM),
            pl.BlockSpec(memory_space=pltpu.SMEM),
            pl.BlockSpec(memory_space=pltpu.SMEM),
            pl.BlockSpec(memory_space=pltpu.SMEM),
            pl.BlockSpec(memory_space=pltpu.SMEM),
        ),
        scratch_shapes=[pltpu.SMEM((n_exp,), jnp.int32)],
        name="moe_plan",
    )(counts, ids_rows)


def _dispatch_kernel(d1_ref, d2_ref, plo_ref, pn_ref, nu_ref, h_ref, xb_ref, zbuf, sem, zsem,
                     *, tm, ns, bm, nb, n_exp):
    base = pl.program_id(0) * tm

    def zero_copies(fn):
        def per_expert(e, carry):
            pos = plo_ref[e]
            rem = pn_ref[e]
            chunk = bm // 2
            while chunk >= 1:
                hit = (rem & chunk) != 0

                @pl.when(hit)
                def _(pos=pos, chunk=chunk):
                    dst = xb_ref.at[pl.ds(pl.multiple_of(pos * ns, ns), chunk * ns)]
                    fn(pltpu.make_async_copy(zbuf.at[pl.ds(0, chunk * ns)], dst, zsem))

                pos = pos + jnp.where(hit, chunk, 0)
                chunk //= 2
            return carry

        lax.fori_loop(0, n_exp, per_expert, 0)

        def per_tail_block(b, carry):
            dst = xb_ref.at[pl.ds(pl.multiple_of(b * (bm * ns), bm * ns), bm * ns)]
            fn(pltpu.make_async_copy(zbuf, dst, zsem))
            return carry

        lax.fori_loop(nu_ref[0], nb, per_tail_block, 0)

    @pl.when(pl.program_id(0) == 0)
    def _():
        zbuf[...] = jnp.zeros(zbuf.shape, zbuf.dtype)
        zero_copies(lambda cp: cp.start())

    def row_copy(r, dest):
        src = h_ref.at[pl.ds(pl.multiple_of(r * ns, ns), ns)]
        return pltpu.make_async_copy(src, xb_ref.at[pl.ds(pl.multiple_of(dest * ns, ns), ns)], sem)

    def start(r, carry):
        row_copy(r, d1_ref[base + r]).start()
        row_copy(r, d2_ref[base + r]).start()
        return carry

    lax.fori_loop(0, tm, start, 0, unroll=DMA_UNROLL)
    for _ in range(2):
        pltpu.make_async_copy(h_ref, xb_ref.at[pl.ds(0, tm * ns)], sem).wait()

    @pl.when(pl.program_id(0) == pl.num_programs(0) - 1)
    def _():
        zero_copies(lambda cp: cp.wait())


def _dispatch(h2p, dest1, dest2, pad_lo, pad_n, n_used, nb, bm, tm, ns):
    t = h2p.shape[0] // ns
    return pl.pallas_call(
        functools.partial(_dispatch_kernel, tm=tm, ns=ns, bm=bm, nb=nb, n_exp=pad_lo.shape[0]),
        out_shape=jax.ShapeDtypeStruct((nb * bm * ns, LANES), h2p.dtype),
        grid_spec=pltpu.PrefetchScalarGridSpec(
            num_scalar_prefetch=5,
            grid=(t // tm,),
            in_specs=[pl.BlockSpec((tm * ns, LANES), lambda m, d1, d2, plo, pn, nu: (m, 0))],
            out_specs=pl.BlockSpec(memory_space=pl.ANY),
            scratch_shapes=[pltpu.VMEM((bm * ns, LANES), h2p.dtype), pltpu.SemaphoreType.DMA,
                            pltpu.SemaphoreType.DMA],
        ),
        compiler_params=_params(("arbitrary",)),
        name="moe_dispatch",
    )(dest1, dest2, pad_lo, pad_n, n_used, h2p)


def _experts_kernel(be_ref, first_ref, nxt_ref, nu_ref, x_ref, wg_hbm, wu_hbm, wd_hbm, y_ref,
                    wg_buf, wu_buf, wd_buf, sem, run_scr, *, ns):
    b = pl.program_id(0)

    def weight_copies(e, slot):
        return (pltpu.make_async_copy(wg_hbm.at[e], wg_buf.at[slot], sem.at[slot]),
                pltpu.make_async_copy(wu_hbm.at[e], wu_buf.at[slot], sem.at[slot]),
                pltpu.make_async_copy(wd_hbm.at[e], wd_buf.at[slot], sem.at[slot]))

    @pl.when(b == 0)
    def _():
        run_scr[0] = 0
        for cp in weight_copies(be_ref[0], 0):
            cp.start()

    @pl.when((first_ref[b] == 1) & (b < nu_ref[0]))
    def _():
        run = run_scr[0]
        slot = run & 1
        for cp in weight_copies(be_ref[b], slot):
            cp.wait()

        @pl.when(nxt_ref[b] >= 0)
        def _():
            for cp in weight_copies(nxt_ref[b], 1 - slot):
                cp.start()

        run_scr[0] = run + 1

    @pl.when(b < nu_ref[0])
    def _():
        slot = (run_scr[0] - 1) & 1
        x = _unpack_bf16_pairs(_load_row_tiles(x_ref, ns)).astype(BF16)
        g = jnp.dot(x, wg_buf[slot].astype(BF16), preferred_element_type=F32)
        u = jnp.dot(x, wu_buf[slot].astype(BF16), preferred_element_type=F32)
        hid = (g * jax.nn.sigmoid(g) * u).astype(BF16)
        y = jnp.dot(hid, wd_buf[slot].astype(BF16), preferred_element_type=F32)
        _store_row_tiles(y_ref, _pack_bf16_pairs(y))


def _experts(xb, w_gate, w_up, w_down, block_expert, first, nxt, n_used, bm, ns):
    d, de = w_gate.shape[1:]
    nb = xb.shape[0] // (bm * ns)

    def row_map(b, be, fi, nx, nu):
        return (jnp.minimum(b, nu[0] - 1), 0)

    return pl.pallas_call(
        functools.partial(_experts_kernel, ns=ns),
        out_shape=jax.ShapeDtypeStruct(xb.shape, jnp.uint32),
        grid_spec=pltpu.PrefetchScalarGridSpec(
            num_scalar_prefetch=4,
            grid=(nb,),
            in_specs=[
                pl.BlockSpec((bm * ns, LANES), row_map),
                pl.BlockSpec(memory_space=pl.ANY),
                pl.BlockSpec(memory_space=pl.ANY),
                pl.BlockSpec(memory_space=pl.ANY),
            ],
            out_specs=pl.BlockSpec((bm * ns, LANES), row_map),
            scratch_shapes=[
                pltpu.VMEM((2, d, de), F32),
                pltpu.VMEM((2, d, de), F32),
                pltpu.VMEM((2, de, d), F32),
                pltpu.SemaphoreType.DMA((2,)),
                pltpu.SMEM((1,), jnp.int32),
            ],
        ),
        input_output_aliases={4: 0},
        compiler_params=_params(("arbitrary",)),
        name="moe_experts",
    )(block_expert, first, nxt, n_used, xb, w_gate, w_up, w_down)


def _ple_kernel(d1_ref, d2_ref, x_ref, mg_ref, yb_ref, g_ref, p_ref, wg_ref, wp_ref, gf_ref, o_ref,
                ybuf, sem, x3_scr, h_scr, p_scr, *, tm, tn, ns, nj):
    m = pl.program_id(0)
    j = pl.program_id(1)
    n_m = pl.num_programs(0)
    n_j = pl.num_programs(1)
    slot = m & 1

    def row_copies(blk, r, s):
        tok = blk * tm + r
        src1 = yb_ref.at[pl.ds(pl.multiple_of(d1_ref[tok] * ns, ns), ns)]
        src2 = yb_ref.at[pl.ds(pl.multiple_of(d2_ref[tok] * ns, ns), ns)]
        rows = pl.ds(pl.multiple_of(r * ns, ns), ns)
        return (pltpu.make_async_copy(src1, ybuf.at[s, 0, rows], sem.at[s]),
                pltpu.make_async_copy(src2, ybuf.at[s, 1, rows], sem.at[s]))

    def drain(s):
        for k in range(2):
            pltpu.make_async_copy(yb_ref.at[pl.ds(0, tm * ns)], ybuf.at[s, k], sem.at[s]).wait()

    @pl.when(j == 0)
    def _():
        @pl.when(m == 0)
        def _():
            def start(r, carry):
                for cp in row_copies(0, r, 0):
                    cp.start()
                return carry

            lax.fori_loop(0, tm, start, 0, unroll=DMA_UNROLL)

        drain(slot)
        mg = mg_ref[...]
        y1 = _unpack_bf16_pairs(_load_row_tiles(ybuf.at[slot, 0], ns))
        y2 = _unpack_bf16_pairs(_load_row_tiles(ybuf.at[slot, 1], ns))
        x3 = x_ref[...] + mg[:, 0:1] * y1 + mg[:, 1:2] * y2
        x3_scr[...] = x3
        h_scr[...] = _rmsnorm_f32(x3, g_ref[...]).astype(BF16)
        p_scr[...] = p_ref[...].astype(BF16)

    nxt = jnp.minimum(m + 1, n_m - 1)
    rows_per_step = tm // nj
    for r in range(rows_per_step):
        for cp in row_copies(nxt, j * rows_per_step + r, 1 - slot):
            cp.start()

    col = pl.multiple_of(j * tn, tn)
    gate = jnp.dot(h_scr[...], wg_ref[:, pl.ds(col, tn)].astype(BF16), preferred_element_type=F32)
    ple = jnp.dot(p_scr[...], wp_ref[:, pl.ds(col, tn)].astype(BF16), preferred_element_type=F32)
    o_ref[:, pl.ds(col, tn)] = x3_scr[:, pl.ds(col, tn)] + jax.nn.sigmoid(gate) * ple

    @pl.when(j == n_j - 1)
    def _():
        o_ref[...] = _rmsnorm_f32(o_ref[...], gf_ref[...])

        @pl.when(m == n_m - 1)
        def _():
            drain(1 - slot)


def _combine_ple_final(x2, moe_gates, yb, dest1, dest2, g_ple, p2d, w_gate, w_proj, g_final, tm, tn, ns):
    t, d = x2.shape
    dp = p2d.shape[1]
    return pl.pallas_call(
        functools.partial(_ple_kernel, tm=tm, tn=tn, ns=ns, nj=d // tn),
        out_shape=jax.ShapeDtypeStruct((t, d), F32),
        grid_spec=pltpu.PrefetchScalarGridSpec(
            num_scalar_prefetch=2,
            grid=(t // tm, d // tn),
            in_specs=[
                pl.BlockSpec((tm, d), lambda m, j, d1, d2: (m, 0)),
                pl.BlockSpec((tm, LANES), lambda m, j, d1, d2: (m, 0)),
                pl.BlockSpec(memory_space=pl.ANY),
                pl.BlockSpec((1, d), lambda m, j, d1, d2: (0, 0)),
                pl.BlockSpec((tm, dp), lambda m, j, d1, d2: (m, 0)),
                pl.BlockSpec((d, d), lambda m, j, d1, d2: (0, 0), **_RESIDENT),
                pl.BlockSpec((dp, d), lambda m, j, d1, d2: (0, 0), **_RESIDENT),
                pl.BlockSpec((1, d), lambda m, j, d1, d2: (0, 0)),
            ],
            out_specs=pl.BlockSpec((tm, d), lambda m, j, d1, d2: (m, 0)),
            scratch_shapes=[
                pltpu.VMEM((2, 2, tm * ns, LANES), yb.dtype),
                pltpu.SemaphoreType.DMA((2,)),
                pltpu.VMEM((tm, d), F32),
                pltpu.VMEM((tm, d), BF16),
                pltpu.VMEM((tm, dp), BF16),
            ],
        ),
        compiler_params=_params(("arbitrary", "arbitrary")),
        name="combine_ple_final",
    )(dest1, dest2, x2, moe_gates, yb, g_ple, p2d, w_gate, w_proj, g_final)


def _tile(n, pref):
    return pref if n % pref == 0 else n


def _layer(x2d, p2d, batch, seq, w_in, b_forget, w_branch_fox, w_branch_sb, w_mix_out, g_mix, g_ffn,
           w_group, w_expert, w_gate, w_up, w_down, g_ple, w_ple_proj, w_ple_gate, g_final):
    t, d = x2d.shape
    n_heads = b_forget.shape[0]
    w_att = n_heads * HEAD_DIM
    n_exp = w_expert.shape[1]

    qs = LOG2E * HEAD_DIM ** -0.5
    w_t, wf_t = _wprep(w_in.T, w_att, n_heads, qs, 512)
    b_row = jnp.pad(b_forget, (0, LANES - n_heads)).reshape(1, LANES)
    col_qa, col_ka, col_qb, col_kb = 0, w_att, 2 * w_att, 3 * w_att
    col_ga, col_gb = 4 * w_att, 4 * w_att + d
    row_va, row_vb = 0, w_att

    tm = _tile(t, 1024)
    tn = _tile(d, 1024)
    proj, vt, f_tok = _inproj(x2d, g_mix.reshape(1, d), w_t, wf_t, 2 * w_att, tm, _tile(d // 2, 1024))
    c = _forget_cumsum(f_tok, b_row, batch, n_heads)

    tq = _tile(seq, 512)
    hp = 4
    o_a = _fox_attention(proj, vt, c, batch, seq, n_heads, col_qa, col_ka, row_va, tq, hp)
    o_b = _sb_attention(proj, vt, batch, seq, n_heads, col_qb, col_kb, row_vb, tq, _tile(seq, 256), hp)

    mixed = _merge(o_a, o_b, w_branch_fox, w_branch_sb, proj, col_ga, col_gb, tm, tn)
    x2 = _mixout(mixed, w_mix_out, x2d, tm, tn)

    w_r = jnp.concatenate([w_group, w_expert], axis=1)
    w_r = jnp.pad(w_r, ((0, 0), (0, LANES - w_r.shape[1])))
    w_r_hi = w_r.astype(BF16)
    w_r_lo = (w_r - w_r_hi.astype(F32)).astype(BF16)
    tr = _tile(t, 512)
    w_r_cat = jnp.concatenate([w_r_hi, w_r_lo], axis=1)
    h2, ids_rows, gates, counts = _router(x2, g_ffn.reshape(1, d), w_r_cat, tr, N_GROUPS, EXPERTS_PER_GROUP)

    bm = 256
    n_assign = 2 * t
    assert n_assign <= 1 << (2 * RANK_BITS), "in-expert ranks must fit two RANK_BITS-bit halves"
    nb = n_assign // bm + n_exp
    dest, block_expert, first, nxt, pad_lo, pad_n, n_used = _plan(counts, ids_rows, n_exp, bm, nb)
    dest1, dest2 = dest[0], dest[1]

    td = _tile(t, 2048)
    ns = d // 2 // LANES
    xb = _dispatch(h2, dest1, dest2, pad_lo, pad_n, n_used, nb, bm, td, ns)
    yb = _experts(xb, w_gate, w_up, w_down, block_expert, first, nxt, n_used, bm, ns)

    tp = _tile(t, 512)
    return _combine_ple_final(x2, gates, yb, dest1, dest2, g_ple.reshape(1, d), p2d, w_ple_gate, w_ple_proj,
                              g_final.reshape(1, d), tp, d, ns)


def kernel(x, p, w_in, b_forget, w_branch_fox, w_branch_sb, w_mix_out, g_mix, g_ffn, w_group, w_expert,
           w_gate, w_up, w_down, g_ple, w_ple_proj, w_ple_gate, g_final):
    b, s, d = x.shape
    depth = w_in.shape[0]
    assert depth == 1, "the final norm is fused into the single layer"
    x2d = x.reshape(b * s, d)
    out = _layer(x2d, p[0].reshape(b * s, -1), b, s, w_in[0], b_forget[0], w_branch_fox[0], w_branch_sb[0],
                 w_mix_out[0], g_mix[0], g_ffn[0], w_group[0], w_expert[0], w_gate[0], w_up[0], w_down[0],
                 g_ple[0], w_ple_proj[0], w_ple_gate[0], g_final)
    return out.reshape(b, s, d)
```

```python
import functools

import jax
import jax.numpy as jnp
from jax import lax
from jax.experimental import pallas as pl
from jax.experimental.pallas import tpu as pltpu

F32 = jnp.float32
BF16 = jnp.bfloat16

HEAD_DIM = 128
N_GROUPS = 4
EXPERTS_PER_GROUP = 8
EPS = 1e-6
LANES = 128
SUBLANES = 8
VMEM_LIMIT = 56 * 1024 * 1024

LOG2E = 1.4426950408889634
DMA_UNROLL = 8
RANK_BITS = 7
R_DONE = 160.0
MASKED_SCORE = -1e4

NT_DIMS = (((1,), (1,)), ((), ()))


def _params(sem):
    return pltpu.CompilerParams(dimension_semantics=sem, vmem_limit_bytes=VMEM_LIMIT)


def _rmsnorm_f32(x, g):
    ms = jnp.mean(x * x, axis=-1, keepdims=True)
    return x * lax.rsqrt(ms + EPS) * g


def _pack_bf16_pairs(x):
    n = x.shape[1] // 2
    bits = lax.bitcast_convert_type(x.astype(BF16).astype(F32), jnp.uint32)
    return (bits[:, :n] >> 16) | (bits[:, n:] & jnp.uint32(0xFFFF0000))


def _unpack_bf16_pairs(w):
    lo = lax.bitcast_convert_type(w << 16, F32)
    hi = lax.bitcast_convert_type(w & jnp.uint32(0xFFFF0000), F32)
    return jnp.concatenate([lo, hi], axis=1)


def _store_row_tiles(ref, words):
    n = words.shape[0]
    ns = words.shape[1] // LANES
    for c in range(ns):
        ref[pl.ds(c, n, stride=ns), :] = words[:, c * LANES:(c + 1) * LANES]


def _load_row_tiles(ref, ns):
    n = ref.shape[0] // ns
    return jnp.concatenate([ref[pl.ds(c, n, stride=ns), :] for c in range(ns)], axis=1)


def _wprep_kernel(off_ref, scale_ref, w_ref, f_ref, wt_ref, wf_ref, *, n_heads):
    i = pl.program_id(0)
    wt_ref[...] = (w_ref[...] * scale_ref[i]).astype(BF16)

    @pl.when(i == 0)
    def _():
        row = lax.broadcasted_iota(jnp.int32, wf_ref.shape, 0)
        f_rows = jnp.concatenate([f_ref[...]] * (wf_ref.shape[0] // f_ref.shape[0]), axis=0)
        wf_ref[...] = jnp.where(row < n_heads, f_rows, 0.0).astype(BF16)


def _wprep(w_t, w_att, n_heads, qs, rb):
    n_in, d = w_t.shape
    off_b = 3 * w_att + n_heads
    segments = [(2 * w_att, w_att, 1.0), (off_b + 2 * w_att, w_att, 1.0),
                (0, w_att, qs), (w_att, w_att, 1.0),
                (off_b, w_att, qs), (off_b + w_att, w_att, 1.0),
                (off_b + 3 * w_att, n_in - off_b - 3 * w_att, 1.0)]
    offs, scales = [], []
    for start, length, scale in segments:
        assert length % rb == 0 and start % SUBLANES == 0
        for k in range(length // rb):
            offs.append((start + k * rb) // SUBLANES)
            scales.append(scale)
    n_out = len(offs) * rb
    f_blk = SUBLANES
    assert n_heads <= f_blk and (3 * w_att) % f_blk == 0
    return pl.pallas_call(
        functools.partial(_wprep_kernel, n_heads=n_heads),
        out_shape=(jax.ShapeDtypeStruct((n_out, d), BF16), jax.ShapeDtypeStruct((LANES, d), BF16)),
        grid_spec=pltpu.PrefetchScalarGridSpec(
            num_scalar_prefetch=2,
            grid=(len(offs),),
            in_specs=[
                pl.BlockSpec((pl.Element(rb), pl.Element(d)), lambda i, off, sc: (off[i] * SUBLANES, 0)),
                pl.BlockSpec((pl.Element(f_blk), pl.Element(d)), lambda i, off, sc: (3 * w_att, 0)),
            ],
            out_specs=(
                pl.BlockSpec((rb, d), lambda i, off, sc: (i, 0)),
                pl.BlockSpec((LANES, d), lambda i, off, sc: (0, 0)),
            ),
        ),
        compiler_params=_params(("arbitrary",)),
        name="inproj_weights",
    )(jnp.asarray(offs, jnp.int32), jnp.asarray(scales, F32), w_t, w_t)


def _inproj_kernel(x_ref, g_ref, w_ref, wf_ref, o_ref, vt_ref, f_ref, h_scr, *, nvt):
    j = pl.program_id(1)

    @pl.when(j == 0)
    def _():
        hb = _rmsnorm_f32(x_ref[...], g_ref[...]).astype(BF16)
        h_scr[...] = hb
        f_ref[...] = lax.dot_general(hb, wf_ref[...], NT_DIMS, preferred_element_type=F32)

    @pl.when(j < nvt)
    def _():
        vt_ref[...] = lax.dot_general(w_ref[...], h_scr[...], NT_DIMS,
                                      preferred_element_type=F32).astype(vt_ref.dtype)

    @pl.when(j >= nvt)
    def _():
        o_ref[...] = lax.dot_general(h_scr[...], w_ref[...], NT_DIMS,
                                     preferred_element_type=F32).astype(o_ref.dtype)


def _inproj(x2d, g, w_t, wf_t, n_v, tm, tn):
    t, d = x2d.shape
    n = w_t.shape[0] - n_v
    nvt = n_v // tn
    return pl.pallas_call(
        functools.partial(_inproj_kernel, nvt=nvt),
        out_shape=(jax.ShapeDtypeStruct((t, n), BF16), jax.ShapeDtypeStruct((n_v, t), BF16),
                   jax.ShapeDtypeStruct((t, LANES), F32)),
        grid=(t // tm, nvt + n // tn),
        in_specs=[
            pl.BlockSpec((tm, d), lambda m, j: (m, 0)),
            pl.BlockSpec((1, d), lambda m, j: (0, 0)),
            pl.BlockSpec((tn, d), lambda m, j: (j, 0)),
            pl.BlockSpec((LANES, d), lambda m, j: (0, 0)),
        ],
        out_specs=(
            pl.BlockSpec((tm, tn), lambda m, j: (m, jnp.maximum(j - nvt, 0))),
            pl.BlockSpec((tn, tm), lambda m, j: (jnp.minimum(j, nvt - 1), m)),
            pl.BlockSpec((tm, LANES), lambda m, j: (m, 0)),
        ),
        scratch_shapes=[pltpu.VMEM((tm, d), BF16)],
        compiler_params=_params(("parallel", "arbitrary")),
        name="inproj",
    )(x2d, g, w_t, wf_t)


def _split3(x):
    p1 = x.astype(BF16)
    r1 = x - p1.astype(F32)
    p2 = r1.astype(BF16)
    p3 = (r1 - p2.astype(F32)).astype(BF16)
    return p1, p2, p3


def _cumsum_kernel(f_ref, b_ref, lt_ref, c_ref, carry_scr, *, chunk, n_heads):
    @pl.when(pl.program_id(1) == 0)
    def _():
        carry_scr[...] = jnp.zeros(carry_scr.shape, F32)

    rows = f_ref.shape[0]
    lt = lt_ref[...]
    carry = carry_scr[...]
    for i in range(rows // chunk):
        lf = jax.nn.log_sigmoid(f_ref[i * chunk:(i + 1) * chunk, :] + b_ref[...])
        p1, p2, p3 = _split3(lf)
        cs = (jnp.dot(lt, p1, preferred_element_type=F32)
              + jnp.dot(lt, p2, preferred_element_type=F32)
              + jnp.dot(lt, p3, preferred_element_type=F32)) + carry
        for h in range(n_heads):
            c_ref[h, i * chunk:(i + 1) * chunk, :] = jnp.broadcast_to(cs[:, h:h + 1] * LOG2E, (chunk, LANES))
        carry = cs[chunk - 1:chunk, :]
    carry_scr[...] = carry


def _forget_cumsum(f_tok, b_row, batch, n_heads):
    t = f_tok.shape[0]
    s = t // batch
    rows = min(1024, s)
    chunk = min(256, s)
    lt = jnp.tril(jnp.ones((chunk, chunk), BF16))
    nr = s // rows
    return pl.pallas_call(
        functools.partial(_cumsum_kernel, chunk=chunk, n_heads=n_heads),
        out_shape=jax.ShapeDtypeStruct((n_heads, t, LANES), F32),
        grid=(batch, nr),
        in_specs=[
            pl.BlockSpec((rows, LANES), lambda b, r: (b * nr + r, 0)),
            pl.BlockSpec((1, LANES), lambda b, r: (0, 0)),
            pl.BlockSpec((chunk, chunk), lambda b, r: (0, 0)),
        ],
        out_specs=pl.BlockSpec((n_heads, rows, LANES), lambda b, r: (0, b * nr + r, 0)),
        scratch_shapes=[pltpu.VMEM((1, LANES), F32)],
        compiler_params=_params(("parallel", "arbitrary")),
        name="forget_cumsum",
    )(f_tok, b_row, lt)


def _fox_kernel(q_ref, k_ref, vt_ref, c_ref, o_ref, m_scr, l_scr, acc_scr, *, tq, tk, hp):
    i = pl.program_id(2)
    m_scr[...] = jnp.full(m_scr.shape, -jnp.inf, F32)
    l_scr[...] = jnp.zeros(l_scr.shape, F32)
    acc_scr[...] = jnp.zeros(acc_scr.shape, F32)

    def tile(j, masked, koff=0, nk=tk, qlo=0):
        start = pl.multiple_of(j * tk + koff, nk)
        nq = tq - qlo
        if masked:
            key = j * tk + koff + lax.broadcasted_iota(jnp.int32, (nk, nq), 0)
            qry = i * tq + qlo + lax.broadcasted_iota(jnp.int32, (nk, nq), 1)
            keep = key <= qry

        def scores(hh):
            lanes = slice(hh * HEAD_DIM, (hh + 1) * HEAD_DIM)
            kj = k_ref[pl.ds(start, nk), lanes]
            return lax.dot_general(kj, q_ref[qlo:tq, lanes], NT_DIMS, preferred_element_type=F32)

        s_all = [scores(hh) for hh in range(hp)]
        for hh in range(hp):
            s = s_all[hh]
            cj = c_ref[hh, pl.ds(start, nk), :]
            s = s - jnp.concatenate([cj] * (nq // LANES), axis=1)
            if masked:
                s = jnp.where(keep, s, -jnp.inf)
            m_prev = m_scr[hh, :, qlo:tq]
            m_new = jnp.maximum(m_prev, jnp.max(s, axis=0, keepdims=True))
            alpha = jnp.exp2(m_prev - m_new)
            p = jnp.exp2(s - m_new)
            l_scr[hh, :, qlo:tq] = alpha * l_scr[hh, :, qlo:tq] + jnp.sum(p, axis=0, keepdims=True)
            vtj = vt_ref[hh * HEAD_DIM:(hh + 1) * HEAD_DIM, pl.ds(start, nk)]
            acc_scr[hh, :, qlo:tq] = (alpha * acc_scr[hh, :, qlo:tq]
                                      + jnp.dot(vtj, p.astype(BF16), preferred_element_type=F32))
            m_scr[hh, :, qlo:tq] = m_new

    def body(pair, carry):
        tile(2 * pair, False)
        tile(2 * pair + 1, False)
        return carry

    lax.fori_loop(0, i // 2, body, 0)

    @pl.when(i % 2 == 1)
    def _():
        tile(i - 1, False)

    half = tk // 2
    tile(i, True, koff=0, nk=half, qlo=0)
    tile(i, True, koff=half, nk=half, qlo=tq - half)
    for hh in range(hp):
        o = acc_scr[hh] / l_scr[hh]
        o_ref[:, hh * HEAD_DIM:(hh + 1) * HEAD_DIM] = o.T.astype(o_ref.dtype)


def _fox_attention(proj, vt, c, batch, seq, n_heads, col_q, col_k, row_v, tq, hp):
    t = proj.shape[0]
    nq = seq // tq
    wb = hp * HEAD_DIM
    cq, ck, rv = col_q // wb, col_k // wb, row_v // wb
    return pl.pallas_call(
        functools.partial(_fox_kernel, tq=tq, tk=tq, hp=hp),
        out_shape=jax.ShapeDtypeStruct((t, n_heads * HEAD_DIM), BF16),
        grid=(batch, n_heads // hp, nq),
        in_specs=[
            pl.BlockSpec((tq, wb), lambda b, g, i: (b * nq + i, cq + g)),
            pl.BlockSpec((seq, wb), lambda b, g, i: (b, ck + g)),
            pl.BlockSpec((wb, seq), lambda b, g, i: (rv + g, b)),
            pl.BlockSpec((hp, seq, LANES), lambda b, g, i: (g, b, 0)),
        ],
        out_specs=pl.BlockSpec((tq, wb), lambda b, g, i: (b * nq + i, g)),
        scratch_shapes=[
            pltpu.VMEM((hp, 1, tq), F32),
            pltpu.VMEM((hp, 1, tq), F32),
            pltpu.VMEM((hp, HEAD_DIM, tq), F32),
        ],
        compiler_params=_params(("parallel", "parallel", "arbitrary")),
        name="fox_attention",
    )(proj, proj, vt, c)


def _sb_kernel(q_ref, k_ref, vt_ref, tri_ref, o_ref, r_scr, acc_scr, *, tq, tk, hp):
    i = pl.program_id(2)
    r_scr[...] = jnp.zeros(r_scr.shape, F32)
    acc_scr[...] = jnp.zeros(acc_scr.shape, F32)

    sign_bit = jnp.uint32(0x80000000)

    def tile(j, masked, qlo=0, qhi=tq):
        start = pl.multiple_of(j * tk, tk)
        nq = qhi - qlo
        if masked:
            key = j * tk + lax.broadcasted_iota(jnp.int32, (tk, nq), 0)
            qry = i * tq + qlo + lax.broadcasted_iota(jnp.int32, (tk, nq), 1)
            mask = key < qry

        def scores(hh):
            lanes = slice(hh * HEAD_DIM, (hh + 1) * HEAD_DIM)
            kj = k_ref[pl.ds(start, tk), lanes]
            return lax.dot_general(kj, q_ref[qlo:qhi, lanes], NT_DIMS, preferred_element_type=F32)

        u_all = [scores(hh) for hh in range(hp)]
        for hh in range(hp):
            u = u_all[hh]
            if masked:
                u = jnp.where(mask, u, MASKED_SCORE)
            minus_abs = lax.bitcast_convert_type(lax.bitcast_convert_type(u, jnp.uint32) | sign_bit, F32)
            sp = jnp.maximum(u, 0.0) + jnp.log2(1.0 + jnp.exp2(minus_abs))
            w = jnp.dot(tri_ref[...], sp.astype(BF16), preferred_element_type=F32)
            a = jnp.exp2(u - ((sp + w) + r_scr[hh, :, qlo:qhi]))
            vtj = vt_ref[hh * HEAD_DIM:(hh + 1) * HEAD_DIM, pl.ds(start, tk)]
            acc_scr[hh, :, qlo:qhi] += jnp.dot(vtj, a.astype(BF16), preferred_element_type=F32)
            r_scr[hh, :, qlo:qhi] += jnp.sum(sp, axis=0, keepdims=True)

    ratio = max(tq // tk, 1)
    n_full = (i * tq) // tk
    for d in reversed(range(ratio)):
        tile(n_full + d, True, qlo=d * tk if tq > tk else 0)

    def live():
        return (jnp.min(r_scr[...]) <= R_DONE).astype(jnp.int32)

    def cond(carry):
        it, alive = carry
        return (it < n_full) & (alive > 0)

    def body(carry):
        it, _ = carry
        late_live = jnp.min(r_scr[:, :, tq // 2:]) <= R_DONE

        @pl.when(late_live)
        def _():
            tile(n_full - 1 - it, False)

        @pl.when(jnp.logical_not(late_live))
        def _():
            tile(n_full - 1 - it, False, qhi=tq // 2)

        return it + 1, live()

    lax.while_loop(cond, body, (jnp.int32(0), live()))
    for hh in range(hp):
        o_ref[:, hh * HEAD_DIM:(hh + 1) * HEAD_DIM] = acc_scr[hh].T.astype(o_ref.dtype)


def _sb_attention(proj, vt, batch, seq, n_heads, col_q, col_k, row_v, tq, tk, hp):
    t = proj.shape[0]
    nq = seq // tq
    wb = hp * HEAD_DIM
    cq, ck, rv = col_q // wb, col_k // wb, row_v // wb
    tri = jnp.triu(jnp.ones((tk, tk), BF16), 1)
    return pl.pallas_call(
        functools.partial(_sb_kernel, tq=tq, tk=tk, hp=hp),
        out_shape=jax.ShapeDtypeStruct((t, n_heads * HEAD_DIM), BF16),
        grid=(batch, n_heads // hp, nq),
        in_specs=[
            pl.BlockSpec((tq, wb), lambda b, g, i: (b * nq + i, cq + g)),
            pl.BlockSpec((seq, wb), lambda b, g, i: (b, ck + g)),
            pl.BlockSpec((wb, seq), lambda b, g, i: (rv + g, b)),
            pl.BlockSpec((tk, tk), lambda b, g, i: (0, 0)),
        ],
        out_specs=pl.BlockSpec((tq, wb), lambda b, g, i: (b * nq + i, g)),
        scratch_shapes=[
            pltpu.VMEM((hp, 1, tq), F32),
            pltpu.VMEM((hp, HEAD_DIM, tq), F32),
        ],
        compiler_params=_params(("parallel", "parallel", "arbitrary")),
        name="sb_attention",
    )(proj, proj, vt, tri)


_RESIDENT = dict(pipeline_mode=pl.Buffered(1))


def _merge_kernel(oa_ref, ob_ref, wa_ref, wb_ref, ga_ref, gb_ref, o_ref, *, tn):
    col = pl.multiple_of(pl.program_id(1) * tn, tn)
    ya = jnp.dot(oa_ref[...], wa_ref[:, pl.ds(col, tn)].astype(BF16), preferred_element_type=F32)
    yb = jnp.dot(ob_ref[...], wb_ref[:, pl.ds(col, tn)].astype(BF16), preferred_element_type=F32)
    ga = jax.nn.sigmoid(ga_ref[...].astype(F32))
    gb = jax.nn.sigmoid(gb_ref[...].astype(F32))
    o_ref[...] = (ga * ya + gb * yb).astype(o_ref.dtype)


def _merge(o_a, o_b, wa, wb, proj, col_ga, col_gb, tm, tn):
    t, ka = o_a.shape
    kb = o_b.shape[1]
    d = wa.shape[1]
    ca, cb = col_ga // tn, col_gb // tn
    return pl.pallas_call(
        functools.partial(_merge_kernel, tn=tn),
        out_shape=jax.ShapeDtypeStruct((t, d), BF16),
        grid=(t // tm, d // tn),
        in_specs=[
            pl.BlockSpec((tm, ka), lambda m, j: (m, 0)),
            pl.BlockSpec((tm, kb), lambda m, j: (m, 0)),
            pl.BlockSpec((ka, d), lambda m, j: (0, 0), **_RESIDENT),
            pl.BlockSpec((kb, d), lambda m, j: (0, 0), **_RESIDENT),
            pl.BlockSpec((tm, tn), lambda m, j: (m, ca + j)),
            pl.BlockSpec((tm, tn), lambda m, j: (m, cb + j)),
        ],
        out_specs=pl.BlockSpec((tm, tn), lambda m, j: (m, j)),
        compiler_params=_params(("parallel", "arbitrary")),
        name="branch_merge",
    )(o_a, o_b, wa, wb, proj, proj)


def _mixout_kernel(a_ref, w_ref, x_ref, o_ref, *, tn):
    col = pl.multiple_of(pl.program_id(1) * tn, tn)
    w = w_ref[:, pl.ds(col, tn)].astype(BF16)
    o_ref[...] = x_ref[...] + jnp.dot(a_ref[...], w, preferred_element_type=F32)


def _mixout(mixed, w, x2d, tm, tn):
    t, k = mixed.shape
    d = w.shape[1]
    return pl.pallas_call(
        functools.partial(_mixout_kernel, tn=tn),
        out_shape=jax.ShapeDtypeStruct((t, d), F32),
        grid=(t // tm, d // tn),
        in_specs=[
            pl.BlockSpec((tm, k), lambda m, j: (m, 0)),
            pl.BlockSpec((k, d), lambda m, j: (0, 0), **_RESIDENT),
            pl.BlockSpec((tm, tn), lambda m, j: (m, j)),
        ],
        out_specs=pl.BlockSpec((tm, tn), lambda m, j: (m, j)),
        compiler_params=_params(("parallel", "arbitrary")),
        name="mix_out",
    )(mixed, w, x2d)


def _router_kernel(x_ref, g_ref, wc_ref, lt_ref, sel_ref, h_ref, ids_ref, gate_ref, cnt_ref, carry_scr,
                   *, n_groups, per_group):
    @pl.when(pl.program_id(0) == 0)
    def _():
        carry_scr[...] = jnp.zeros(carry_scr.shape, F32)

    h = _rmsnorm_f32(x_ref[...], g_ref[...])
    hh = h.astype(BF16)
    _store_row_tiles(h_ref, _pack_bf16_pairs(h))
    hl = (h - hh.astype(F32)).astype(BF16)
    both = jnp.dot(hh, wc_ref[...], preferred_element_type=F32)
    logits = (both[:, :LANES] + jnp.dot(hl, wc_ref[:, :LANES], preferred_element_type=F32)) + both[:, LANES:]
    tm = logits.shape[0]
    lane = lax.broadcasted_iota(jnp.int32, (tm, LANES), 1)
    lane_f = lane.astype(F32)
    neg_inf = -jnp.inf

    def first_argmax(vals, valid):
        vmax = jnp.max(jnp.where(valid, vals, neg_inf), axis=-1, keepdims=True)
        idx = jnp.min(jnp.where(valid & (vals == vmax), lane_f, float(LANES)), axis=-1, keepdims=True)
        return vmax, idx.astype(jnp.int32)

    is_group = lane < n_groups
    gmax, gsel = first_argmax(logits, is_group)
    gsum = jnp.sum(jnp.where(is_group, jnp.exp(logits - gmax), 0.0), axis=-1, keepdims=True)
    p_group = 1.0 / gsum
    lo = n_groups + gsel * per_group
    in_group = (lane >= lo) & (lane < lo + per_group)
    v1, i1 = first_argmax(logits, in_group)
    v2, i2 = first_argmax(logits, in_group & (lane != i1))
    e2w = jnp.exp(v2 - v1)
    gate1 = (1.0 / (1.0 + e2w)) * p_group
    gate2 = (e2w / (1.0 + e2w)) * p_group
    e1 = i1 - n_groups
    e2 = i2 - n_groups

    onehot = ((lane == e1) | (lane == e2)).astype(BF16)
    before = jnp.dot(lt_ref[...], onehot, preferred_element_type=F32) + carry_scr[...]
    rank1 = jnp.sum(jnp.where(lane == e1, before, 0.0), axis=-1, keepdims=True).astype(jnp.int32)
    rank2 = jnp.sum(jnp.where(lane == e2, before, 0.0), axis=-1, keepdims=True).astype(jnp.int32)
    carry_new = carry_scr[...] + jnp.sum(onehot.astype(F32), axis=0, keepdims=True)
    carry_scr[...] = carry_new

    low = (1 << RANK_BITS) - 1
    cols = jnp.where(lane == 0, e1, jnp.where(lane == 1, e2, jnp.where(lane == 2, rank1 >> RANK_BITS,
           jnp.where(lane == 3, rank1 & low, jnp.where(lane == 4, rank2 >> RANK_BITS,
           jnp.where(lane == 5, rank2 & low, 0))))))
    ids_ref[...] = lax.dot_general(sel_ref[...], cols.astype(F32).astype(BF16), NT_DIMS,
                                   preferred_element_type=F32)
    gate_ref[...] = jnp.where(lane == 0, gate1, jnp.where(lane == 1, gate2, 0.0))
    cnt_ref[...] = jnp.broadcast_to(carry_new, cnt_ref.shape).astype(jnp.int32)


def _router(x2, g, w_cat, tm, n_groups, per_group):
    t, d = x2.shape
    lt = jnp.tril(jnp.ones((tm, tm), BF16), -1)
    sel = jnp.eye(8, LANES, dtype=BF16)
    return pl.pallas_call(
        functools.partial(_router_kernel, n_groups=n_groups, per_group=per_group),
        out_shape=(
            jax.ShapeDtypeStruct((t * (d // 2 // LANES), LANES), jnp.uint32),
            jax.ShapeDtypeStruct((8, t), F32),
            jax.ShapeDtypeStruct((t, LANES), F32),
            jax.ShapeDtypeStruct((8, LANES), jnp.int32),
        ),
        grid=(t // tm,),
        in_specs=[
            pl.BlockSpec((tm, d), lambda m: (m, 0)),
            pl.BlockSpec((1, d), lambda m: (0, 0)),
            pl.BlockSpec((d, 2 * LANES), lambda m: (0, 0)),
            pl.BlockSpec((tm, tm), lambda m: (0, 0)),
            pl.BlockSpec((8, LANES), lambda m: (0, 0)),
        ],
        out_specs=(
            pl.BlockSpec((tm * (d // 2 // LANES), LANES), lambda m: (m, 0)),
            pl.BlockSpec((8, tm), lambda m: (0, m)),
            pl.BlockSpec((tm, LANES), lambda m: (m, 0)),
            pl.BlockSpec((8, LANES), lambda m: (0, 0)),
        ),
        scratch_shapes=[pltpu.VMEM((1, LANES), F32)],
        compiler_params=_params(("arbitrary",)),
        name="router",
    )(x2, g, w_cat, lt, sel)


def _plan_kernel(cnt_ref, ids_ref, dest_ref, be_ref, first_ref, nxt_ref, plo_ref, pn_ref, nu_ref, ps_scr,
                 *, n_exp, bm, nb):
    shift = bm.bit_length() - 1

    def clear(k, c):
        first_ref[k] = 0
        nxt_ref[k] = -1
        return c

    lax.fori_loop(0, nb, clear, 0)

    def per_expert(e, carry):
        blk_start, last, prev_start = carry
        cnt = cnt_ref[0, e]
        n = lax.shift_right_logical(cnt + (bm - 1), shift)
        ps_scr[e] = blk_start * bm
        plo_ref[e] = blk_start * bm + cnt
        pn_ref[e] = n * bm - cnt

        def fill(k, c):
            be_ref[blk_start + k] = e
            return c

        lax.fori_loop(0, n, fill, 0)

        @pl.when(n > 0)
        def _():
            first_ref[blk_start] = 1

            @pl.when(prev_start >= 0)
            def _():
                nxt_ref[prev_start] = e

        used = n > 0
        return blk_start + n, jnp.where(used, e, last), jnp.where(used, blk_start, prev_start)

    n_used, last, _ = lax.fori_loop(0, n_exp, per_expert, (jnp.int32(0), jnp.int32(0), jnp.int32(-1)))
    nu_ref[0] = n_used

    def tail(k, c):
        be_ref[k] = last
        return c

    lax.fori_loop(n_used, nb, tail, 0)

    e1 = ids_ref[0:1, :]
    e2 = ids_ref[1:2, :]
    p1 = jnp.zeros_like(e1)
    p2 = jnp.zeros_like(e2)
    for e in range(n_exp):
        ps = ps_scr[e].astype(F32)
        p1 = jnp.where(e1 == e, ps, p1)
        p2 = jnp.where(e2 == e, ps, p2)
    dest_ref[...] = jnp.zeros(dest_ref.shape, jnp.int32)
    hi = float(1 << RANK_BITS)
    dest_ref[0:1, :] = (p1 + ids_ref[2:3, :] * hi + ids_ref[3:4, :]).astype(jnp.int32)
    dest_ref[1:2, :] = (p2 + ids_ref[4:5, :] * hi + ids_ref[5:6, :]).astype(jnp.int32)


def _plan(counts, ids_rows, n_exp, bm, nb):
    t = ids_rows.shape[1]
    return pl.pallas_call(
        functools.partial(_plan_kernel, n_exp=n_exp, bm=bm, nb=nb),
        out_shape=(
            jax.ShapeDtypeStruct((8, t), jnp.int32),
            jax.ShapeDtypeStruct((nb,), jnp.int32),
            jax.ShapeDtypeStruct((nb,), jnp.int32),
            jax.ShapeDtypeStruct((nb,), jnp.int32),
            jax.ShapeDtypeStruct((n_exp,), jnp.int32),
            jax.ShapeDtypeStruct((n_exp,), jnp.int32),
            jax.ShapeDtypeStruct((1,), jnp.int32),
        ),
        in_specs=[
            pl.BlockSpec(memory_space=pltpu.SMEM),
            pl.BlockSpec(memory_space=pltpu.VMEM),
        ],
        out_specs=(
            pl.BlockSpec(memory_space=pltpu.VMEM),
            pl.BlockSpec(memory_space=pltpu.SMEM),
            pl.BlockSpec(memory_space=pltpu.SMEM),
            pl.BlockSpec(memory_space=pltpu.SMEM),
            pl.BlockSpec(memory_space=pltpu.SMEM),
            pl.BlockSpec(memory_space=pltpu.SMEM),
            pl.BlockSpec(memory_space=pltpu.SMEM),
        ),
        scratch_shapes=[pltpu.SMEM((n_exp,), jnp.int32)],
        name="moe_plan",
    )(counts, ids_rows)


def _dispatch_kernel(d1_ref, d2_ref, plo_ref, pn_ref, nu_ref, h_ref, xb_ref, zbuf, sem, zsem,
                     *, tm, ns, bm, nb, n_exp):
    base = pl.program_id(0) * tm

    def zero_copies(fn):
        def per_expert(e, carry):
            pos = plo_ref[e]
            rem = pn_ref[e]
            chunk = bm // 2
            while chunk >= 1:
                hit = (rem & chunk) != 0

                @pl.when(hit)
                def _(pos=pos, chunk=chunk):
                    dst = xb_ref.at[pl.ds(pl.multiple_of(pos * ns, ns), chunk * ns)]
                    fn(pltpu.make_async_copy(zbuf.at[pl.ds(0, chunk * ns)], dst, zsem))

                pos = pos + jnp.where(hit, chunk, 0)
                chunk //= 2
            return carry

        lax.fori_loop(0, n_exp, per_expert, 0)

        def per_tail_block(b, carry):
            dst = xb_ref.at[pl.ds(pl.multiple_of(b * (bm * ns), bm * ns), bm * ns)]
            fn(pltpu.make_async_copy(zbuf, dst, zsem))
            return carry

        lax.fori_loop(nu_ref[0], nb, per_tail_block, 0)

    @pl.when(pl.program_id(0) == 0)
    def _():
        zbuf[...] = jnp.zeros(zbuf.shape, zbuf.dtype)
        zero_copies(lambda cp: cp.start())

    def row_copy(r, dest):
        src = h_ref.at[pl.ds(pl.multiple_of(r * ns, ns), ns)]
        return pltpu.make_async_copy(src, xb_ref.at[pl.ds(pl.multiple_of(dest * ns, ns), ns)], sem)

    def start(r, carry):
        row_copy(r, d1_ref[base + r]).start()
        row_copy(r, d2_ref[base + r]).start()
        return carry

    lax.fori_loop(0, tm, start, 0, unroll=DMA_UNROLL)
    for _ in range(2):
        pltpu.make_async_copy(h_ref, xb_ref.at[pl.ds(0, tm * ns)], sem).wait()

    @pl.when(pl.program_id(0) == pl.num_programs(0) - 1)
    def _():
        zero_copies(lambda cp: cp.wait())


def _dispatch(h2p, dest1, dest2, pad_lo, pad_n, n_used, nb, bm, tm, ns):
    t = h2p.shape[0] // ns
    return pl.pallas_call(
        functools.partial(_dispatch_kernel, tm=tm, ns=ns, bm=bm, nb=nb, n_exp=pad_lo.shape[0]),
        out_shape=jax.ShapeDtypeStruct((nb * bm * ns, LANES), h2p.dtype),
        grid_spec=pltpu.PrefetchScalarGridSpec(
            num_scalar_prefetch=5,
            grid=(t // tm,),
            in_specs=[pl.BlockSpec((tm * ns, LANES), lambda m, d1, d2, plo, pn, nu: (m, 0))],
            out_specs=pl.BlockSpec(memory_space=pl.ANY),
            scratch_shapes=[pltpu.VMEM((bm * ns, LANES), h2p.dtype), pltpu.SemaphoreType.DMA,
                            pltpu.SemaphoreType.DMA],
        ),
        compiler_params=_params(("arbitrary",)),
        name="moe_dispatch",
    )(dest1, dest2, pad_lo, pad_n, n_used, h2p)


def _experts_kernel(be_ref, first_ref, nxt_ref, nu_ref, x_ref, wg_hbm, wu_hbm, wd_hbm, y_ref,
                    wg_buf, wu_buf, wd_buf, sem, run_scr, *, ns):
    b = pl.program_id(0)

    def weight_copies(e, slot):
        return (pltpu.make_async_copy(wg_hbm.at[e], wg_buf.at[slot], sem.at[slot]),
                pltpu.make_async_copy(wu_hbm.at[e], wu_buf.at[slot], sem.at[slot]),
                pltpu.make_async_copy(wd_hbm.at[e], wd_buf.at[slot], sem.at[slot]))

    @pl.when(b == 0)
    def _():
        run_scr[0] = 0
        for cp in weight_copies(be_ref[0], 0):
            cp.start()

    @pl.when((first_ref[b] == 1) & (b < nu_ref[0]))
    def _():
        run = run_scr[0]
        slot = run & 1
        for cp in weight_copies(be_ref[b], slot):
            cp.wait()

        @pl.when(nxt_ref[b] >= 0)
        def _():
            for cp in weight_copies(nxt_ref[b], 1 - slot):
                cp.start()

        run_scr[0] = run + 1

    @pl.when(b < nu_ref[0])
    def _():
        slot = (run_scr[0] - 1) & 1
        x = _unpack_bf16_pairs(_load_row_tiles(x_ref, ns)).astype(BF16)
        g = jnp.dot(x, wg_buf[slot].astype(BF16), preferred_element_type=F32)
        u = jnp.dot(x, wu_buf[slot].astype(BF16), preferred_element_type=F32)
        hid = (g * jax.nn.sigmoid(g) * u).astype(BF16)
        y = jnp.dot(hid, wd_buf[slot].astype(BF16), preferred_element_type=F32)
        _store_row_tiles(y_ref, _pack_bf16_pairs(y))


def _experts(xb, w_gate, w_up, w_down, block_expert, first, nxt, n_used, bm, ns):
    d, de = w_gate.shape[1:]
    nb = xb.shape[0] // (bm * ns)

    def row_map(b, be, fi, nx, nu):
        return (jnp.minimum(b, nu[0] - 1), 0)

    return pl.pallas_call(
        functools.partial(_experts_kernel, ns=ns),
        out_shape=jax.ShapeDtypeStruct(xb.shape, jnp.uint32),
        grid_spec=pltpu.PrefetchScalarGridSpec(
            num_scalar_prefetch=4,
            grid=(nb,),
            in_specs=[
                pl.BlockSpec((bm * ns, LANES), row_map),
                pl.BlockSpec(memory_space=pl.ANY),
                pl.BlockSpec(memory_space=pl.ANY),
                pl.BlockSpec(memory_space=pl.ANY),
            ],
            out_specs=pl.BlockSpec((bm * ns, LANES), row_map),
            scratch_shapes=[
                pltpu.VMEM((2, d, de), F32),
                pltpu.VMEM((2, d, de), F32),
                pltpu.VMEM((2, de, d), F32),
                pltpu.SemaphoreType.DMA((2,)),
                pltpu.SMEM((1,), jnp.int32),
            ],
        ),
        input_output_aliases={4: 0},
        compiler_params=_params(("arbitrary",)),
        name="moe_experts",
    )(block_expert, first, nxt, n_used, xb, w_gate, w_up, w_down)


def _ple_kernel(d1_ref, d2_ref, x_ref, mg_ref, yb_ref, g_ref, p_ref, wg_ref, wp_ref, gf_ref, o_ref,
                ybuf, sem, x3_scr, h_scr, p_scr, *, tm, tn, ns, nj):
    m = pl.program_id(0)
    j = pl.program_id(1)
    n_m = pl.num_programs(0)
    n_j = pl.num_programs(1)
    slot = m & 1

    def row_copies(blk, r, s):
        tok = blk * tm + r
        src1 = yb_ref.at[pl.ds(pl.multiple_of(d1_ref[tok] * ns, ns), ns)]
        src2 = yb_ref.at[pl.ds(pl.multiple_of(d2_ref[tok] * ns, ns), ns)]
        rows = pl.ds(pl.multiple_of(r * ns, ns), ns)
        return (pltpu.make_async_copy(src1, ybuf.at[s, 0, rows], sem.at[s]),
                pltpu.make_async_copy(src2, ybuf.at[s, 1, rows], sem.at[s]))

    def drain(s):
        for k in range(2):
            pltpu.make_async_copy(yb_ref.at[pl.ds(0, tm * ns)], ybuf.at[s, k], sem.at[s]).wait()

    @pl.when(j == 0)
    def _():
        @pl.when(m == 0)
        def _():
            def start(r, carry):
                for cp in row_copies(0, r, 0):
                    cp.start()
                return carry

            lax.fori_loop(0, tm, start, 0, unroll=DMA_UNROLL)

        drain(slot)
        mg = mg_ref[...]
        y1 = _unpack_bf16_pairs(_load_row_tiles(ybuf.at[slot, 0], ns))
        y2 = _unpack_bf16_pairs(_load_row_tiles(ybuf.at[slot, 1], ns))
        x3 = x_ref[...] + mg[:, 0:1] * y1 + mg[:, 1:2] * y2
        x3_scr[...] = x3
        h_scr[...] = _rmsnorm_f32(x3, g_ref[...]).astype(BF16)
        p_scr[...] = p_ref[...].astype(BF16)

    nxt = jnp.minimum(m + 1, n_m - 1)
    rows_per_step = tm // nj
    for r in range(rows_per_step):
        for cp in row_copies(nxt, j * rows_per_step + r, 1 - slot):
            cp.start()

    col = pl.multiple_of(j * tn, tn)
    gate = jnp.dot(h_scr[...], wg_ref[:, pl.ds(col, tn)].astype(BF16), preferred_element_type=F32)
    ple = jnp.dot(p_scr[...], wp_ref[:, pl.ds(col, tn)].astype(BF16), preferred_element_type=F32)
    o_ref[:, pl.ds(col, tn)] = x3_scr[:, pl.ds(col, tn)] + jax.nn.sigmoid(gate) * ple

    @pl.when(j == n_j - 1)
    def _():
        o_ref[...] = _rmsnorm_f32(o_ref[...], gf_ref[...])

        @pl.when(m == n_m - 1)
        def _():
            drain(1 - slot)


def _combine_ple_final(x2, moe_gates, yb, dest1, dest2, g_ple, p2d, w_gate, w_proj, g_final, tm, tn, ns):
    t, d = x2.shape
    dp = p2d.shape[1]
    return pl.pallas_call(
        functools.partial(_ple_kernel, tm=tm, tn=tn, ns=ns, nj=d // tn),
        out_shape=jax.ShapeDtypeStruct((t, d), F32),
        grid_spec=pltpu.PrefetchScalarGridSpec(
            num_scalar_prefetch=2,
            grid=(t // tm, d // tn),
            in_specs=[
                pl.BlockSpec((tm, d), lambda m, j, d1, d2: (m, 0)),
                pl.BlockSpec((tm, LANES), lambda m, j, d1, d2: (m, 0)),
                pl.BlockSpec(memory_space=pl.ANY),
                pl.BlockSpec((1, d), lambda m, j, d1, d2: (0, 0)),
                pl.BlockSpec((tm, dp), lambda m, j, d1, d2: (m, 0)),
                pl.BlockSpec((d, d), lambda m, j, d1, d2: (0, 0), **_RESIDENT),
                pl.BlockSpec((dp, d), lambda m, j, d1, d2: (0, 0), **_RESIDENT),
                pl.BlockSpec((1, d), lambda m, j, d1, d2: (0, 0)),
            ],
            out_specs=pl.BlockSpec((tm, d), lambda m, j, d1, d2: (m, 0)),
            scratch_shapes=[
                pltpu.VMEM((2, 2, tm * ns, LANES), yb.dtype),
                pltpu.SemaphoreType.DMA((2,)),
                pltpu.VMEM((tm, d), F32),
                pltpu.VMEM((tm, d), BF16),
                pltpu.VMEM((tm, dp), BF16),
            ],
        ),
        compiler_params=_params(("arbitrary", "arbitrary")),
        name="combine_ple_final",
    )(dest1, dest2, x2, moe_gates, yb, g_ple, p2d, w_gate, w_proj, g_final)


def _tile(n, pref):
    return pref if n % pref == 0 else n


def _layer(x2d, p2d, batch, seq, w_in, b_forget, w_branch_fox, w_branch_sb, w_mix_out, g_mix, g_ffn,
           w_group, w_expert, w_gate, w_up, w_down, g_ple, w_ple_proj, w_ple_gate, g_final):
    t, d = x2d.shape
    n_heads = b_forget.shape[0]
    w_att = n_heads * HEAD_DIM
    n_exp = w_expert.shape[1]

    qs = LOG2E * HEAD_DIM ** -0.5
    w_t, wf_t = _wprep(w_in.T, w_att, n_heads, qs, 512)
    b_row = jnp.pad(b_forget, (0, LANES - n_heads)).reshape(1, LANES)
    col_qa, col_ka, col_qb, col_kb = 0, w_att, 2 * w_att, 3 * w_att
    col_ga, col_gb = 4 * w_att, 4 * w_att + d
    row_va, row_vb = 0, w_att

    tm = _tile(t, 1024)
    tn = _tile(d, 1024)
    proj, vt, f_tok = _inproj(x2d, g_mix.reshape(1, d), w_t, wf_t, 2 * w_att, tm, _tile(d // 2, 1024))
    c = _forget_cumsum(f_tok, b_row, batch, n_heads)

    tq = _tile(seq, 512)
    hp = 4
    o_a = _fox_attention(proj, vt, c, batch, seq, n_heads, col_qa, col_ka, row_va, tq, hp)
    o_b = _sb_attention(proj, vt, batch, seq, n_heads, col_qb, col_kb, row_vb, tq, _tile(seq, 256), hp)

    mixed = _merge(o_a, o_b, w_branch_fox, w_branch_sb, proj, col_ga, col_gb, tm, tn)
    x2 = _mixout(mixed, w_mix_out, x2d, tm, tn)

    w_r = jnp.concatenate([w_group, w_expert], axis=1)
    w_r = jnp.pad(w_r, ((0, 0), (0, LANES - w_r.shape[1])))
    w_r_hi = w_r.astype(BF16)
    w_r_lo = (w_r - w_r_hi.astype(F32)).astype(BF16)
    tr = _tile(t, 512)
    w_r_cat = jnp.concatenate([w_r_hi, w_r_lo], axis=1)
    h2, ids_rows, gates, counts = _router(x2, g_ffn.reshape(1, d), w_r_cat, tr, N_GROUPS, EXPERTS_PER_GROUP)

    bm = 256
    n_assign = 2 * t
    assert n_assign <= 1 << (2 * RANK_BITS), "in-expert ranks must fit two RANK_BITS-bit halves"
    nb = n_assign // bm + n_exp
    dest, block_expert, first, nxt, pad_lo, pad_n, n_used = _plan(counts, ids_rows, n_exp, bm, nb)
    dest1, dest2 = dest[0], dest[1]

    td = _tile(t, 2048)
    ns = d // 2 // LANES
    xb = _dispatch(h2, dest1, dest2, pad_lo, pad_n, n_used, nb, bm, td, ns)
    yb = _experts(xb, w_gate, w_up, w_down, block_expert, first, nxt, n_used, bm, ns)

    tp = _tile(t, 512)
    return _combine_ple_final(x2, gates, yb, dest1, dest2, g_ple.reshape(1, d), p2d, w_ple_gate, w_ple_proj,
                              g_final.reshape(1, d), tp, d, ns)


def kernel(x, p, w_in, b_forget, w_branch_fox, w_branch_sb, w_mix_out, g_mix, g_ffn, w_group, w_expert,
           w_gate, w_up, w_down, g_ple, w_ple_proj, w_ple_gate, g_final):
    b, s, d = x.shape
    depth = w_in.shape[0]
    assert depth == 1, "the final norm is fused into the single layer"
    x2d = x.reshape(b * s, d)
    out = _layer(x2d, p[0].reshape(b * s, -1), b, s, w_in[0], b_forget[0], w_branch_fox[0], w_branch_sb[0],
                 w_mix_out[0], g_mix[0], g_ffn[0], w_group[0], w_expert[0], w_gate[0], w_up[0], w_down[0],
                 g_ple[0], w_ple_proj[0], w_ple_gate[0], g_final)
    return out.reshape(b, s, d)
```

```python
import functools

import jax
import jax.numpy as jnp
from jax import lax
from jax.experimental import pallas as pl
from jax.experimental.pallas import tpu as pltpu

F32 = jnp.float32
BF16 = jnp.bfloat16

HEAD_DIM = 128
N_GROUPS = 4
EXPERTS_PER_GROUP = 8
EPS = 1e-6
LANES = 128
SUBLANES = 8
VMEM_LIMIT = 56 * 1024 * 1024

LOG2E = 1.4426950408889634
DMA_UNROLL = 8
RANK_BITS = 7
R_DONE = 160.0
MASKED_SCORE = -1e4

NT_DIMS = (((1,), (1,)), ((), ()))


def _params(sem):
    return pltpu.CompilerParams(dimension_semantics=sem, vmem_limit_bytes=VMEM_LIMIT)


def _rmsnorm_f32(x, g):
    ms = jnp.mean(x * x, axis=-1, keepdims=True)
    return x * lax.rsqrt(ms + EPS) * g


def _pack_bf16_pairs(x):
    n = x.shape[1] // 2
    bits = lax.bitcast_convert_type(x.astype(BF16).astype(F32), jnp.uint32)
    return (bits[:, :n] >> 16) | (bits[:, n:] & jnp.uint32(0xFFFF0000))


def _unpack_bf16_pairs(w):
    lo = lax.bitcast_convert_type(w << 16, F32)
    hi = lax.bitcast_convert_type(w & jnp.uint32(0xFFFF0000), F32)
    return jnp.concatenate([lo, hi], axis=1)


def _store_row_tiles(ref, words):
    n = words.shape[0]
    ns = words.shape[1] // LANES
    for c in range(ns):
        ref[pl.ds(c, n, stride=ns), :] = words[:, c * LANES:(c + 1) * LANES]


def _load_row_tiles(ref, ns):
    n = ref.shape[0] // ns
    return jnp.concatenate([ref[pl.ds(c, n, stride=ns), :] for c in range(ns)], axis=1)


def _wprep_kernel(off_ref, scale_ref, w_ref, f_ref, wt_ref, wf_ref, *, n_heads):
    i = pl.program_id(0)
    wt_ref[...] = (w_ref[...] * scale_ref[i]).astype(BF16)

    @pl.when(i == 0)
    def _():
        row = lax.broadcasted_iota(jnp.int32, wf_ref.shape, 0)
        f_rows = jnp.concatenate([f_ref[...]] * (wf_ref.shape[0] // f_ref.shape[0]), axis=0)
        wf_ref[...] = jnp.where(row < n_heads, f_rows, 0.0).astype(BF16)


def _wprep(w_t, w_att, n_heads, qs, rb):
    n_in, d = w_t.shape
    off_b = 3 * w_att + n_heads
    segments = [(2 * w_att, w_att, 1.0), (off_b + 2 * w_att, w_att, 1.0),
                (0, w_att, qs), (w_att, w_att, 1.0),
                (off_b, w_att, qs), (off_b + w_att, w_att, 1.0),
                (off_b + 3 * w_att, n_in - off_b - 3 * w_att, 1.0)]
    offs, scales = [], []
    for start, length, scale in segments:
        assert length % rb == 0 and start % SUBLANES == 0
        for k in range(length // rb):
            offs.append((start + k * rb) // SUBLANES)
            scales.append(scale)
    n_out = len(offs) * rb
    f_blk = SUBLANES
    assert n_heads <= f_blk and (3 * w_att) % f_blk == 0
    return pl.pallas_call(
        functools.partial(_wprep_kernel, n_heads=n_heads),
        out_shape=(jax.ShapeDtypeStruct((n_out, d), BF16), jax.ShapeDtypeStruct((LANES, d), BF16)),
        grid_spec=pltpu.PrefetchScalarGridSpec(
            num_scalar_prefetch=2,
            grid=(len(offs),),
            in_specs=[
                pl.BlockSpec((pl.Element(rb), pl.Element(d)), lambda i, off, sc: (off[i] * SUBLANES, 0)),
                pl.BlockSpec((pl.Element(f_blk), pl.Element(d)), lambda i, off, sc: (3 * w_att, 0)),
            ],
            out_specs=(
                pl.BlockSpec((rb, d), lambda i, off, sc: (i, 0)),
                pl.BlockSpec((LANES, d), lambda i, off, sc: (0, 0)),
            ),
        ),
        compiler_params=_params(("arbitrary",)),
        name="inproj_weights",
    )(jnp.asarray(offs, jnp.int32), jnp.asarray(scales, F32), w_t, w_t)


def _inproj_kernel(x_ref, g_ref, w_ref, wf_ref, o_ref, vt_ref, f_ref, h_scr, *, nvt):
    j = pl.program_id(1)

    @pl.when(j == 0)
    def _():
        hb = _rmsnorm_f32(x_ref[...], g_ref[...]).astype(BF16)
        h_scr[...] = hb
        f_ref[...] = lax.dot_general(hb, wf_ref[...], NT_DIMS, preferred_element_type=F32)

    @pl.when(j < nvt)
    def _():
        vt_ref[...] = lax.dot_general(w_ref[...], h_scr[...], NT_DIMS,
                                      preferred_element_type=F32).astype(vt_ref.dtype)

    @pl.when(j >= nvt)
    def _():
        o_ref[...] = lax.dot_general(h_scr[...], w_ref[...], NT_DIMS,
                                     preferred_element_type=F32).astype(o_ref.dtype)


def _inproj(x2d, g, w_t, wf_t, n_v, tm, tn):
    t, d = x2d.shape
    n = w_t.shape[0] - n_v
    nvt = n_v // tn
    return pl.pallas_call(
        functools.partial(_inproj_kernel, nvt=nvt),
        out_shape=(jax.ShapeDtypeStruct((t, n), BF16), jax.ShapeDtypeStruct((n_v, t), BF16),
                   jax.ShapeDtypeStruct((t, LANES), F32)),
        grid=(t // tm, nvt + n // tn),
        in_specs=[
            pl.BlockSpec((tm, d), lambda m, j: (m, 0)),
            pl.BlockSpec((1, d), lambda m, j: (0, 0)),
            pl.BlockSpec((tn, d), lambda m, j: (j, 0)),
            pl.BlockSpec((LANES, d), lambda m, j: (0, 0)),
        ],
        out_specs=(
            pl.BlockSpec((tm, tn), lambda m, j: (m, jnp.maximum(j - nvt, 0))),
            pl.BlockSpec((tn, tm), lambda m, j: (jnp.minimum(j, nvt - 1), m)),
            pl.BlockSpec((tm, LANES), lambda m, j: (m, 0)),
        ),
        scratch_shapes=[pltpu.VMEM((tm, d), BF16)],
        compiler_params=_params(("parallel", "arbitrary")),
        name="inproj",
    )(x2d, g, w_t, wf_t)


def _split3(x):
    p1 = x.astype(BF16)
    r1 = x - p1.astype(F32)
    p2 = r1.astype(BF16)
    p3 = (r1 - p2.astype(F32)).astype(BF16)
    return p1, p2, p3


def _cumsum_kernel(f_ref, b_ref, lt_ref, c_ref, carry_scr, *, chunk, n_heads):
    @pl.when(pl.program_id(1) == 0)
    def _():
        carry_scr[...] = jnp.zeros(carry_scr.shape, F32)

    rows = f_ref.shape[0]
    lt = lt_ref[...]
    carry = carry_scr[...]
    for i in range(rows // chunk):
        lf = jax.nn.log_sigmoid(f_ref[i * chunk:(i + 1) * chunk, :] + b_ref[...])
        p1, p2, p3 = _split3(lf)
        cs = (jnp.dot(lt, p1, preferred_element_type=F32)
              + jnp.dot(lt, p2, preferred_element_type=F32)
              + jnp.dot(lt, p3, preferred_element_type=F32)) + carry
        for h in range(n_heads):
            c_ref[h, i * chunk:(i + 1) * chunk, :] = jnp.broadcast_to(cs[:, h:h + 1] * LOG2E, (chunk, LANES))
        carry = cs[chunk - 1:chunk, :]
    carry_scr[...] = carry


def _forget_cumsum(f_tok, b_row, batch, n_heads):
    t = f_tok.shape[0]
    s = t // batch
    rows = min(1024, s)
    chunk = min(256, s)
    lt = jnp.tril(jnp.ones((chunk, chunk), BF16))
    nr = s // rows
    return pl.pallas_call(
        functools.partial(_cumsum_kernel, chunk=chunk, n_heads=n_heads),
        out_shape=jax.ShapeDtypeStruct((n_heads, t, LANES), F32),
        grid=(batch, nr),
        in_specs=[
            pl.BlockSpec((rows, LANES), lambda b, r: (b * nr + r, 0)),
            pl.BlockSpec((1, LANES), lambda b, r: (0, 0)),
            pl.BlockSpec((chunk, chunk), lambda b, r: (0, 0)),
        ],
        out_specs=pl.BlockSpec((n_heads, rows, LANES), lambda b, r: (0, b * nr + r, 0)),
        scratch_shapes=[pltpu.VMEM((1, LANES), F32)],
        compiler_params=_params(("parallel", "arbitrary")),
        name="forget_cumsum",
    )(f_tok, b_row, lt)


def _fox_kernel(q_ref, k_ref, vt_ref, c_ref, o_ref, m_scr, l_scr, acc_scr, *, tq, tk, hp):
    i = pl.program_id(2)
    m_scr[...] = jnp.full(m_scr.shape, -jnp.inf, F32)
    l_scr[...] = jnp.zeros(l_scr.shape, F32)
    acc_scr[...] = jnp.zeros(acc_scr.shape, F32)

    def tile(j, masked, koff=0, nk=tk, qlo=0):
        start = pl.multiple_of(j * tk + koff, nk)
        nq = tq - qlo
        if masked:
            key = j * tk + koff + lax.broadcasted_iota(jnp.int32, (nk, nq), 0)
            qry = i * tq + qlo + lax.broadcasted_iota(jnp.int32, (nk, nq), 1)
            keep = key <= qry

        def scores(hh):
            lanes = slice(hh * HEAD_DIM, (hh + 1) * HEAD_DIM)
            kj = k_ref[pl.ds(start, nk), lanes]
            return lax.dot_general(kj, q_ref[qlo:tq, lanes], NT_DIMS, preferred_element_type=F32)

        s_all = [scores(hh) for hh in range(hp)]
        for hh in range(hp):
            s = s_all[hh]
            cj = c_ref[hh, pl.ds(start, nk), :]
            s = s - jnp.concatenate([cj] * (nq // LANES), axis=1)
            if masked:
                s = jnp.where(keep, s, -jnp.inf)
            m_prev = m_scr[hh, :, qlo:tq]
            m_new = jnp.maximum(m_prev, jnp.max(s, axis=0, keepdims=True))
            alpha = jnp.exp2(m_prev - m_new)
            p = jnp.exp2(s - m_new)
            l_scr[hh, :, qlo:tq] = alpha * l_scr[hh, :, qlo:tq] + jnp.sum(p, axis=0, keepdims=True)
            vtj = vt_ref[hh * HEAD_DIM:(hh + 1) * HEAD_DIM, pl.ds(start, nk)]
            acc_scr[hh, :, qlo:tq] = (alpha * acc_scr[hh, :, qlo:tq]
                                      + jnp.dot(vtj, p.astype(BF16), preferred_element_type=F32))
            m_scr[hh, :, qlo:tq] = m_new

    def body(pair, carry):
        tile(2 * pair, False)
        tile(2 * pair + 1, False)
        return carry

    lax.fori_loop(0, i // 2, body, 0)

    @pl.when(i % 2 == 1)
    def _():
        tile(i - 1, False)

    half = tk // 2
    tile(i, True, koff=0, nk=half, qlo=0)
    tile(i, True, koff=half, nk=half, qlo=tq - half)
    for hh in range(hp):
        o = acc_scr[hh] / l_scr[hh]
        o_ref[:, hh * HEAD_DIM:(hh + 1) * HEAD_DIM] = o.T.astype(o_ref.dtype)


def _fox_attention(proj, vt, c, batch, seq, n_heads, col_q, col_k, row_v, tq, hp):
    t = proj.shape[0]
    nq = seq // tq
    wb = hp * HEAD_DIM
    cq, ck, rv = col_q // wb, col_k // wb, row_v // wb
    return pl.pallas_call(
        functools.partial(_fox_kernel, tq=tq, tk=tq, hp=hp),
        out_shape=jax.ShapeDtypeStruct((t, n_heads * HEAD_DIM), BF16),
        grid=(batch, n_heads // hp, nq),
        in_specs=[
            pl.BlockSpec((tq, wb), lambda b, g, i: (b * nq + i, cq + g)),
            pl.BlockSpec((seq, wb), lambda b, g, i: (b, ck + g)),
            pl.BlockSpec((wb, seq), lambda b, g, i: (rv + g, b)),
            pl.BlockSpec((hp, seq, LANES), lambda b, g, i: (g, b, 0)),
        ],
        out_specs=pl.BlockSpec((tq, wb), lambda b, g, i: (b * nq + i, g)),
        scratch_shapes=[
            pltpu.VMEM((hp, 1, tq), F32),
            pltpu.VMEM((hp, 1, tq), F32),
            pltpu.VMEM((hp, HEAD_DIM, tq), F32),
        ],
        compiler_params=_params(("parallel", "parallel", "arbitrary")),
        name="fox_attention",
    )(proj, proj, vt, c)


def _sb_kernel(q_ref, k_ref, vt_ref, tri_ref, o_ref, r_scr, acc_scr, *, tq, tk, hp):
    i = pl.program_id(2)
    r_scr[...] = jnp.zeros(r_scr.shape, F32)
    acc_scr[...] = jnp.zeros(acc_scr.shape, F32)

    sign_bit = jnp.uint32(0x80000000)

    def tile(j, masked, qlo=0, qhi=tq):
        start = pl.multiple_of(j * tk, tk)
        nq = qhi - qlo
        if masked:
            key = j * tk + lax.broadcasted_iota(jnp.int32, (tk, nq), 0)
            qry = i * tq + qlo + lax.broadcasted_iota(jnp.int32, (tk, nq), 1)
            mask = key < qry

        def scores(hh):
            lanes = slice(hh * HEAD_DIM, (hh + 1) * HEAD_DIM)
            kj = k_ref[pl.ds(start, tk), lanes]
            return lax.dot_general(kj, q_ref[qlo:qhi, lanes], NT_DIMS, preferred_element_type=F32)

        u_all = [scores(hh) for hh in range(hp)]
        for hh in range(hp):
            u = u_all[hh]
            if masked:
                u = jnp.where(mask, u, MASKED_SCORE)
            minus_abs = lax.bitcast_convert_type(lax.bitcast_convert_type(u, jnp.uint32) | sign_bit, F32)
            sp = jnp.maximum(u, 0.0) + jnp.log2(1.0 + jnp.exp2(minus_abs))
            w = jnp.dot(tri_ref[...], sp.astype(BF16), preferred_element_type=F32)
            a = jnp.exp2(u - ((sp + w) + r_scr[hh, :, qlo:qhi]))
            vtj = vt_ref[hh * HEAD_DIM:(hh + 1) * HEAD_DIM, pl.ds(start, tk)]
            acc_scr[hh, :, qlo:qhi] += jnp.dot(vtj, a.astype(BF16), preferred_element_type=F32)
            r_scr[hh, :, qlo:qhi] += jnp.sum(sp, axis=0, keepdims=True)

    ratio = max(tq // tk, 1)
    n_full = (i * tq) // tk
    for d in reversed(range(ratio)):
        tile(n_full + d, True, qlo=d * tk if tq > tk else 0)

    def live():
        return (jnp.min(r_scr[...]) <= R_DONE).astype(jnp.int32)

    def cond(carry):
        it, alive = carry
        return (it < n_full) & (alive > 0)

    def body(carry):
        it, _ = carry
        late_live = jnp.min(r_scr[:, :, tq // 2:]) <= R_DONE

        @pl.when(late_live)
        def _():
            tile(n_full - 1 - it, False)

        @pl.when(jnp.logical_not(late_live))
        def _():
            tile(n_full - 1 - it, False, qhi=tq // 2)

        return it + 1, live()

    lax.while_loop(cond, body, (jnp.int32(0), live()))
    for hh in range(hp):
        o_ref[:, hh * HEAD_DIM:(hh + 1) * HEAD_DIM] = acc_scr[hh].T.astype(o_ref.dtype)


def _sb_attention(proj, vt, batch, seq, n_heads, col_q, col_k, row_v, tq, tk, hp):
    t = proj.shape[0]
    nq = seq // tq
    wb = hp * HEAD_DIM
    cq, ck, rv = col_q // wb, col_k // wb, row_v // wb
    tri = jnp.triu(jnp.ones((tk, tk), BF16), 1)
    return pl.pallas_call(
        functools.partial(_sb_kernel, tq=tq, tk=tk, hp=hp),
        out_shape=jax.ShapeDtypeStruct((t, n_heads * HEAD_DIM), BF16),
        grid=(batch, n_heads // hp, nq),
        in_specs=[
            pl.BlockSpec((tq, wb), lambda b, g, i: (b * nq + i, cq + g)),
            pl.BlockSpec((seq, wb), lambda b, g, i: (b, ck + g)),
            pl.BlockSpec((wb, seq), lambda b, g, i: (rv + g, b)),
            pl.BlockSpec((tk, tk), lambda b, g, i: (0, 0)),
        ],
        out_specs=pl.BlockSpec((tq, wb), lambda b, g, i: (b * nq + i, g)),
        scratch_shapes=[
            pltpu.VMEM((hp, 1, tq), F32),
            pltpu.VMEM((hp, HEAD_DIM, tq), F32),
        ],
        compiler_params=_params(("parallel", "parallel", "arbitrary")),
        name="sb_attention",
    )(proj, proj, vt, tri)


_RESIDENT = dict(pipeline_mode=pl.Buffered(1))


def _merge_kernel(oa_ref, ob_ref, wa_ref, wb_ref, ga_ref, gb_ref, o_ref, *, tn):
    col = pl.multiple_of(pl.program_id(1) * tn, tn)
    ya = jnp.dot(oa_ref[...], wa_ref[:, pl.ds(col, tn)].astype(BF16), preferred_element_type=F32)
    yb = jnp.dot(ob_ref[...], wb_ref[:, pl.ds(col, tn)].astype(BF16), preferred_element_type=F32)
    ga = jax.nn.sigmoid(ga_ref[...].astype(F32))
    gb = jax.nn.sigmoid(gb_ref[...].astype(F32))
    o_ref[...] = (ga * ya + gb * yb).astype(o_ref.dtype)


def _merge(o_a, o_b, wa, wb, proj, col_ga, col_gb, tm, tn):
    t, ka = o_a.shape
    kb = o_b.shape[1]
    d = wa.shape[1]
    ca, cb = col_ga // tn, col_gb // tn
    return pl.pallas_call(
        functools.partial(_merge_kernel, tn=tn),
        out_shape=jax.ShapeDtypeStruct((t, d), BF16),
        grid=(t // tm, d // tn),
        in_specs=[
            pl.BlockSpec((tm, ka), lambda m, j: (m, 0)),
            pl.BlockSpec((tm, kb), lambda m, j: (m, 0)),
            pl.BlockSpec((ka, d), lambda m, j: (0, 0), **_RESIDENT),
            pl.BlockSpec((kb, d), lambda m, j: (0, 0), **_RESIDENT),
            pl.BlockSpec((tm, tn), lambda m, j: (m, ca + j)),
            pl.BlockSpec((tm, tn), lambda m, j: (m, cb + j)),
        ],
        out_specs=pl.BlockSpec((tm, tn), lambda m, j: (m, j)),
        compiler_params=_params(("parallel", "arbitrary")),
        name="branch_merge",
    )(o_a, o_b, wa, wb, proj, proj)


def _mixout_kernel(a_ref, w_ref, x_ref, o_ref, *, tn):
    col = pl.multiple_of(pl.program_id(1) * tn, tn)
    w = w_ref[:, pl.ds(col, tn)].astype(BF16)
    o_ref[...] = x_ref[...] + jnp.dot(a_ref[...], w, preferred_element_type=F32)


def _mixout(mixed, w, x2d, tm, tn):
    t, k = mixed.shape
    d = w.shape[1]
    return pl.pallas_call(
        functools.partial(_mixout_kernel, tn=tn),
        out_shape=jax.ShapeDtypeStruct((t, d), F32),
        grid=(t // tm, d // tn),
        in_specs=[
            pl.BlockSpec((tm, k), lambda m, j: (m, 0)),
            pl.BlockSpec((k, d), lambda m, j: (0, 0), **_RESIDENT),
            pl.BlockSpec((tm, tn), lambda m, j: (m, j)),
        ],
        out_specs=pl.BlockSpec((tm, tn), lambda m, j: (m, j)),
        compiler_params=_params(("parallel", "arbitrary")),
        name="mix_out",
    )(mixed, w, x2d)


def _router_kernel(x_ref, g_ref, wc_ref, lt_ref, sel_ref, h_ref, ids_ref, gate_ref, cnt_ref, carry_scr,
                   *, n_groups, per_group):
    @pl.when(pl.program_id(0) == 0)
    def _():
        carry_scr[...] = jnp.zeros(carry_scr.shape, F32)

    h = _rmsnorm_f32(x_ref[...], g_ref[...])
    hh = h.astype(BF16)
    _store_row_tiles(h_ref, _pack_bf16_pairs(h))
    hl = (h - hh.astype(F32)).astype(BF16)
    both = jnp.dot(hh, wc_ref[...], preferred_element_type=F32)
    logits = (both[:, :LANES] + jnp.dot(hl, wc_ref[:, :LANES], preferred_element_type=F32)) + both[:, LANES:]
    tm = logits.shape[0]
    lane = lax.broadcasted_iota(jnp.int32, (tm, LANES), 1)
    lane_f = lane.astype(F32)
    neg_inf = -jnp.inf

    def first_argmax(vals, valid):
        vmax = jnp.max(jnp.where(valid, vals, neg_inf), axis=-1, keepdims=True)
        idx = jnp.min(jnp.where(valid & (vals == vmax), lane_f, float(LANES)), axis=-1, keepdims=True)
        return vmax, idx.astype(jnp.int32)

    is_group = lane < n_groups
    gmax, gsel = first_argmax(logits, is_group)
    gsum = jnp.sum(jnp.where(is_group, jnp.exp(logits - gmax), 0.0), axis=-1, keepdims=True)
    p_group = 1.0 / gsum
    lo = n_groups + gsel * per_group
    in_group = (lane >= lo) & (lane < lo + per_group)
    v1, i1 = first_argmax(logits, in_group)
    v2, i2 = first_argmax(logits, in_group & (lane != i1))
    e2w = jnp.exp(v2 - v1)
    gate1 = (1.0 / (1.0 + e2w)) * p_group
    gate2 = (e2w / (1.0 + e2w)) * p_group
    e1 = i1 - n_groups
    e2 = i2 - n_groups

    onehot = ((lane == e1) | (lane == e2)).astype(BF16)
    before = jnp.dot(lt_ref[...], onehot, preferred_element_type=F32) + carry_scr[...]
    rank1 = jnp.sum(jnp.where(lane == e1, before, 0.0), axis=-1, keepdims=True).astype(jnp.int32)
    rank2 = jnp.sum(jnp.where(lane == e2, before, 0.0), axis=-1, keepdims=True).astype(jnp.int32)
    carry_new = carry_scr[...] + jnp.sum(onehot.astype(F32), axis=0, keepdims=True)
    carry_scr[...] = carry_new

    low = (1 << RANK_BITS) - 1
    cols = jnp.where(lane == 0, e1, jnp.where(lane == 1, e2, jnp.where(lane == 2, rank1 >> RANK_BITS,
           jnp.where(lane == 3, rank1 & low, jnp.where(lane == 4, rank2 >> RANK_BITS,
           jnp.where(lane == 5, rank2 & low, 0))))))
    ids_ref[...] = lax.dot_general(sel_ref[...], cols.astype(F32).astype(BF16), NT_DIMS,
                                   preferred_element_type=F32)
    gate_ref[...] = jnp.where(lane == 0, gate1, jnp.where(lane == 1, gate2, 0.0))
    cnt_ref[...] = jnp.broadcast_to(carry_new, cnt_ref.shape).astype(jnp.int32)


def _router(x2, g, w_cat, tm, n_groups, per_group):
    t, d = x2.shape
    lt = jnp.tril(jnp.ones((tm, tm), BF16), -1)
    sel = jnp.eye(8, LANES, dtype=BF16)
    return pl.pallas_call(
        functools.partial(_router_kernel, n_groups=n_groups, per_group=per_group),
        out_shape=(
            jax.ShapeDtypeStruct((t * (d // 2 // LANES), LANES), jnp.uint32),
            jax.ShapeDtypeStruct((8, t), F32),
            jax.ShapeDtypeStruct((t, LANES), F32),
            jax.ShapeDtypeStruct((8, LANES), jnp.int32),
        ),
        grid=(t // tm,),
        in_specs=[
            pl.BlockSpec((tm, d), lambda m: (m, 0)),
            pl.BlockSpec((1, d), lambda m: (0, 0)),
            pl.BlockSpec((d, 2 * LANES), lambda m: (0, 0)),
            pl.BlockSpec((tm, tm), lambda m: (0, 0)),
            pl.BlockSpec((8, LANES), lambda m: (0, 0)),
        ],
        out_specs=(
            pl.BlockSpec((tm * (d // 2 // LANES), LANES), lambda m: (m, 0)),
            pl.BlockSpec((8, tm), lambda m: (0, m)),
            pl.BlockSpec((tm, LANES), lambda m: (m, 0)),
            pl.BlockSpec((8, LANES), lambda m: (0, 0)),
        ),
        scratch_shapes=[pltpu.VMEM((1, LANES), F32)],
        compiler_params=_params(("arbitrary",)),
        name="router",
    )(x2, g, w_cat, lt, sel)


def _plan_kernel(cnt_ref, ids_ref, dest_ref, be_ref, first_ref, nxt_ref, plo_ref, pn_ref, nu_ref, ps_scr,
                 *, n_exp, bm, nb):
    shift = bm.bit_length() - 1

    def clear(k, c):
        first_ref[k] = 0
        nxt_ref[k] = -1
        return c

    lax.fori_loop(0, nb, clear, 0)

    def per_expert(e, carry):
        blk_start, last, prev_start = carry
        cnt = cnt_ref[0, e]
        n = lax.shift_right_logical(cnt + (bm - 1), shift)
        ps_scr[e] = blk_start * bm
        plo_ref[e] = blk_start * bm + cnt
        pn_ref[e] = n * bm - cnt

        def fill(k, c):
            be_ref[blk_start + k] = e
            return c

        lax.fori_loop(0, n, fill, 0)

        @pl.when(n > 0)
        def _():
            first_ref[blk_start] = 1

            @pl.when(prev_start >= 0)
            def _():
                nxt_ref[prev_start] = e

        used = n > 0
        return blk_start + n, jnp.where(used, e, last), jnp.where(used, blk_start, prev_start)

    n_used, last, _ = lax.fori_loop(0, n_exp, per_expert, (jnp.int32(0), jnp.int32(0), jnp.int32(-1)))
    nu_ref[0] = n_used

    def tail(k, c):
        be_ref[k] = last
        return c

    lax.fori_loop(n_used, nb, tail, 0)

    e1 = ids_ref[0:1, :]
    e2 = ids_ref[1:2, :]
    p1 = jnp.zeros_like(e1)
    p2 = jnp.zeros_like(e2)
    for e in range(n_exp):
        ps = ps_scr[e].astype(F32)
        p1 = jnp.where(e1 == e, ps, p1)
        p2 = jnp.where(e2 == e, ps, p2)
    dest_ref[...] = jnp.zeros(dest_ref.shape, jnp.int32)
    hi = float(1 << RANK_BITS)
    dest_ref[0:1, :] = (p1 + ids_ref[2:3, :] * hi + ids_ref[3:4, :]).astype(jnp.int32)
    dest_ref[1:2, :] = (p2 + ids_ref[4:5, :] * hi + ids_ref[5:6, :]).astype(jnp.int32)


def _plan(counts, ids_rows, n_exp, bm, nb):
    t = ids_rows.shape[1]
    return pl.pallas_call(
        functools.partial(_plan_kernel, n_exp=n_exp, bm=bm, nb=nb),
        out_shape=(
            jax.ShapeDtypeStruct((8, t), jnp.int32),
            jax.ShapeDtypeStruct((nb,), jnp.int32),
            jax.ShapeDtypeStruct((nb,), jnp.int32),
            jax.ShapeDtypeStruct((nb,), jnp.int32),
            jax.ShapeDtypeStruct((n_exp,), jnp.int32),
            jax.ShapeDtypeStruct((n_exp,), jnp.int32),
            jax.ShapeDtypeStruct((1,), jnp.int32),
        ),
        in_specs=[
            pl.BlockSpec(memory_space=pltpu.SMEM),
            pl.BlockSpec(memory_space=pltpu.VMEM),
        ],
        out_specs=(
            pl.BlockSpec(memory_space=pltpu.VMEM),
            pl.BlockSpec(memory_space=pltpu.SMEM),
            pl.BlockSpec(memory_space=pltpu.SMEM),
            pl.BlockSpec(memory_space=pltpu.SMEM),
            pl.BlockSpec(memory_space=pltpu.SMEM),
            pl.BlockSpec(memory_space=pltpu.SMEM),
            pl.BlockSpec(memory_space=pltpu.SMEM),
        ),
        scratch_shapes=[pltpu.SMEM((n_exp,), jnp.int32)],
        name="moe_plan",
    )(counts, ids_rows)


def _dispatch_kernel(d1_ref, d2_ref, plo_ref, pn_ref, nu_ref, h_ref, xb_ref, zbuf, sem, zsem,
                     *, tm, ns, bm, nb, n_exp):
    base = pl.program_id(0) * tm

    def zero_copies(fn):
        def per_expert(e, carry):
            pos = plo_ref[e]
            rem = pn_ref[e]
            chunk = bm // 2
            while chunk >= 1:
                hit = (rem & chunk) != 0

                @pl.when(hit)
                def _(pos=pos, chunk=chunk):
                    dst = xb_ref.at[pl.ds(pl.multiple_of(pos * ns, ns), chunk * ns)]
                    fn(pltpu.make_async_copy(zbuf.at[pl.ds(0, chunk * ns)], dst, zsem))

                pos = pos + jnp.where(hit, chunk, 0)
                chunk //= 2
            return carry

        lax.fori_loop(0, n_exp, per_expert, 0)

        def per_tail_block(b, carry):
            dst = xb_ref.at[pl.ds(pl.multiple_of(b * (bm * ns), bm * ns), bm * ns)]
            fn(pltpu.make_async_copy(zbuf, dst, zsem))
            return carry

        lax.fori_loop(nu_ref[0], nb, per_tail_block, 0)

    @pl.when(pl.program_id(0) == 0)
    def _():
        zbuf[...] = jnp.zeros(zbuf.shape, zbuf.dtype)
        zero_copies(lambda cp: cp.start())

    def row_copy(r, dest):
        src = h_ref.at[pl.ds(pl.multiple_of(r * ns, ns), ns)]
        return pltpu.make_async_copy(src, xb_ref.at[pl.ds(pl.multiple_of(dest * ns, ns), ns)], sem)

    def start(r, carry):
        row_copy(r, d1_ref[base + r]).start(priority=0)
        row_copy(r, d2_ref[base + r]).start(priority=1)
        return carry

    lax.fori_loop(0, tm, start, 0, unroll=DMA_UNROLL)
    for _ in range(2):
        pltpu.make_async_copy(h_ref, xb_ref.at[pl.ds(0, tm * ns)], sem).wait()

    @pl.when(pl.program_id(0) == pl.num_programs(0) - 1)
    def _():
        zero_copies(lambda cp: cp.wait())


def _dispatch(h2p, dest1, dest2, pad_lo, pad_n, n_used, nb, bm, tm, ns):
    t = h2p.shape[0] // ns
    return pl.pallas_call(
        functools.partial(_dispatch_kernel, tm=tm, ns=ns, bm=bm, nb=nb, n_exp=pad_lo.shape[0]),
        out_shape=jax.ShapeDtypeStruct((nb * bm * ns, LANES), h2p.dtype),
        grid_spec=pltpu.PrefetchScalarGridSpec(
            num_scalar_prefetch=5,
            grid=(t // tm,),
            in_specs=[pl.BlockSpec((tm * ns, LANES), lambda m, d1, d2, plo, pn, nu: (m, 0))],
            out_specs=pl.BlockSpec(memory_space=pl.ANY),
            scratch_shapes=[pltpu.VMEM((bm * ns, LANES), h2p.dtype), pltpu.SemaphoreType.DMA,
                            pltpu.SemaphoreType.DMA],
        ),
        compiler_params=_params(("arbitrary",)),
        name="moe_dispatch",
    )(dest1, dest2, pad_lo, pad_n, n_used, h2p)


def _experts_kernel(be_ref, first_ref, nxt_ref, nu_ref, x_ref, wg_hbm, wu_hbm, wd_hbm, y_ref,
                    wg_buf, wu_buf, wd_buf, sem, run_scr, *, ns):
    b = pl.program_id(0)

    def weight_copies(e, slot):
        return (pltpu.make_async_copy(wg_hbm.at[e], wg_buf.at[slot], sem.at[slot]),
                pltpu.make_async_copy(wu_hbm.at[e], wu_buf.at[slot], sem.at[slot]),
                pltpu.make_async_copy(wd_hbm.at[e], wd_buf.at[slot], sem.at[slot]))

    @pl.when(b == 0)
    def _():
        run_scr[0] = 0
        for cp in weight_copies(be_ref[0], 0):
            cp.start()

    @pl.when((first_ref[b] == 1) & (b < nu_ref[0]))
    def _():
        run = run_scr[0]
        slot = run & 1
        for cp in weight_copies(be_ref[b], slot):
            cp.wait()

        @pl.when(nxt_ref[b] >= 0)
        def _():
            for cp in weight_copies(nxt_ref[b], 1 - slot):
                cp.start()

        run_scr[0] = run + 1

    @pl.when(b < nu_ref[0])
    def _():
        slot = (run_scr[0] - 1) & 1
        x = _unpack_bf16_pairs(_load_row_tiles(x_ref, ns)).astype(BF16)
        g = jnp.dot(x, wg_buf[slot].astype(BF16), preferred_element_type=F32)
        u = jnp.dot(x, wu_buf[slot].astype(BF16), preferred_element_type=F32)
        hid = (g * jax.nn.sigmoid(g) * u).astype(BF16)
        y = jnp.dot(hid, wd_buf[slot].astype(BF16), preferred_element_type=F32)
        _store_row_tiles(y_ref, _pack_bf16_pairs(y))


def _experts(xb, w_gate, w_up, w_down, block_expert, first, nxt, n_used, bm, ns):
    d, de = w_gate.shape[1:]
    nb = xb.shape[0] // (bm * ns)

    def row_map(b, be, fi, nx, nu):
        return (jnp.minimum(b, nu[0] - 1), 0)

    return pl.pallas_call(
        functools.partial(_experts_kernel, ns=ns),
        out_shape=jax.ShapeDtypeStruct(xb.shape, jnp.uint32),
        grid_spec=pltpu.PrefetchScalarGridSpec(
            num_scalar_prefetch=4,
            grid=(nb,),
            in_specs=[
                pl.BlockSpec((bm * ns, LANES), row_map),
                pl.BlockSpec(memory_space=pl.ANY),
                pl.BlockSpec(memory_space=pl.ANY),
                pl.BlockSpec(memory_space=pl.ANY),
            ],
            out_specs=pl.BlockSpec((bm * ns, LANES), row_map),
            scratch_shapes=[
                pltpu.VMEM((2, d, de), F32),
                pltpu.VMEM((2, d, de), F32),
                pltpu.VMEM((2, de, d), F32),
                pltpu.SemaphoreType.DMA((2,)),
                pltpu.SMEM((1,), jnp.int32),
            ],
        ),
        input_output_aliases={4: 0},
        compiler_params=_params(("arbitrary",)),
        name="moe_experts",
    )(block_expert, first, nxt, n_used, xb, w_gate, w_up, w_down)


def _ple_kernel(d1_ref, d2_ref, x_ref, mg_ref, yb_ref, g_ref, p_ref, wg_ref, wp_ref, gf_ref, o_ref,
                ybuf, sem, x3_scr, h_scr, p_scr, *, tm, tn, ns, nj):
    m = pl.program_id(0)
    j = pl.program_id(1)
    n_m = pl.num_programs(0)
    n_j = pl.num_programs(1)
    slot = m & 1

    def row_copies(blk, r, s):
        tok = blk * tm + r
        src1 = yb_ref.at[pl.ds(pl.multiple_of(d1_ref[tok] * ns, ns), ns)]
        src2 = yb_ref.at[pl.ds(pl.multiple_of(d2_ref[tok] * ns, ns), ns)]
        rows = pl.ds(pl.multiple_of(r * ns, ns), ns)
        return (pltpu.make_async_copy(src1, ybuf.at[s, 0, rows], sem.at[s]),
                pltpu.make_async_copy(src2, ybuf.at[s, 1, rows], sem.at[s]))

    def drain(s):
        for k in range(2):
            pltpu.make_async_copy(yb_ref.at[pl.ds(0, tm * ns)], ybuf.at[s, k], sem.at[s]).wait()

    @pl.when(j == 0)
    def _():
        @pl.when(m == 0)
        def _():
            def start(r, carry):
                for cp in row_copies(0, r, 0):
                    cp.start()
                return carry

            lax.fori_loop(0, tm, start, 0, unroll=DMA_UNROLL)

        drain(slot)
        mg = mg_ref[...]
        y1 = _unpack_bf16_pairs(_load_row_tiles(ybuf.at[slot, 0], ns))
        y2 = _unpack_bf16_pairs(_load_row_tiles(ybuf.at[slot, 1], ns))
        x3 = x_ref[...] + mg[:, 0:1] * y1 + mg[:, 1:2] * y2
        x3_scr[...] = x3
        h_scr[...] = _rmsnorm_f32(x3, g_ref[...]).astype(BF16)
        p_scr[...] = p_ref[...].astype(BF16)

    nxt = jnp.minimum(m + 1, n_m - 1)
    rows_per_step = tm // nj
    for r in range(rows_per_step):
        for cp in row_copies(nxt, j * rows_per_step + r, 1 - slot):
            cp.start()

    col = pl.multiple_of(j * tn, tn)
    gate = jnp.dot(h_scr[...], wg_ref[:, pl.ds(col, tn)].astype(BF16), preferred_element_type=F32)
    ple = jnp.dot(p_scr[...], wp_ref[:, pl.ds(col, tn)].astype(BF16), preferred_element_type=F32)
    o_ref[:, pl.ds(col, tn)] = x3_scr[:, pl.ds(col, tn)] + jax.nn.sigmoid(gate) * ple

    @pl.when(j == n_j - 1)
    def _():
        o_ref[...] = _rmsnorm_f32(o_ref[...], gf_ref[...])

        @pl.when(m == n_m - 1)
        def _():
            drain(1 - slot)


def _combine_ple_final(x2, moe_gates, yb, dest1, dest2, g_ple, p2d, w_gate, w_proj, g_final, tm, tn, ns):
    t, d = x2.shape
    dp = p2d.shape[1]
    return pl.pallas_call(
        functools.partial(_ple_kernel, tm=tm, tn=tn, ns=ns, nj=d // tn),
        out_shape=jax.ShapeDtypeStruct((t, d), F32),
        grid_spec=pltpu.PrefetchScalarGridSpec(
            num_scalar_prefetch=2,
            grid=(t // tm, d // tn),
            in_specs=[
                pl.BlockSpec((tm, d), lambda m, j, d1, d2: (m, 0)),
                pl.BlockSpec((tm, LANES), lambda m, j, d1, d2: (m, 0)),
                pl.BlockSpec(memory_space=pl.ANY),
                pl.BlockSpec((1, d), lambda m, j, d1, d2: (0, 0)),
                pl.BlockSpec((tm, dp), lambda m, j, d1, d2: (m, 0)),
                pl.BlockSpec((d, d), lambda m, j, d1, d2: (0, 0), **_RESIDENT),
                pl.BlockSpec((dp, d), lambda m, j, d1, d2: (0, 0), **_RESIDENT),
                pl.BlockSpec((1, d), lambda m, j, d1, d2: (0, 0)),
            ],
            out_specs=pl.BlockSpec((tm, d), lambda m, j, d1, d2: (m, 0)),
            scratch_shapes=[
                pltpu.VMEM((2, 2, tm * ns, LANES), yb.dtype),
                pltpu.SemaphoreType.DMA((2,)),
                pltpu.VMEM((tm, d), F32),
                pltpu.VMEM((tm, d), BF16),
                pltpu.VMEM((tm, dp), BF16),
            ],
        ),
        compiler_params=_params(("arbitrary", "arbitrary")),
        name="combine_ple_final",
    )(dest1, dest2, x2, moe_gates, yb, g_ple, p2d, w_gate, w_proj, g_final)


def _tile(n, pref):
    return pref if n % pref == 0 else n


def _layer(x2d, p2d, batch, seq, w_in, b_forget, w_branch_fox, w_branch_sb, w_mix_out, g_mix, g_ffn,
           w_group, w_expert, w_gate, w_up, w_down, g_ple, w_ple_proj, w_ple_gate, g_final):
    t, d = x2d.shape
    n_heads = b_forget.shape[0]
    w_att = n_heads * HEAD_DIM
    n_exp = w_expert.shape[1]

    qs = LOG2E * HEAD_DIM ** -0.5
    w_t, wf_t = _wprep(w_in.T, w_att, n_heads, qs, 512)
    b_row = jnp.pad(b_forget, (0, LANES - n_heads)).reshape(1, LANES)
    col_qa, col_ka, col_qb, col_kb = 0, w_att, 2 * w_att, 3 * w_att
    col_ga, col_gb = 4 * w_att, 4 * w_att + d
    row_va, row_vb = 0, w_att

    tm = _tile(t, 1024)
    tn = _tile(d, 1024)
    proj, vt, f_tok = _inproj(x2d, g_mix.reshape(1, d), w_t, wf_t, 2 * w_att, tm, _tile(d // 2, 1024))
    c = _forget_cumsum(f_tok, b_row, batch, n_heads)

    tq = _tile(seq, 512)
    hp = 4
    o_a = _fox_attention(proj, vt, c, batch, seq, n_heads, col_qa, col_ka, row_va, tq, hp)
    o_b = _sb_attention(proj, vt, batch, seq, n_heads, col_qb, col_kb, row_vb, tq, _tile(seq, 256), hp)

    mixed = _merge(o_a, o_b, w_branch_fox, w_branch_sb, proj, col_ga, col_gb, tm, tn)
    x2 = _mixout(mixed, w_mix_out, x2d, tm, tn)

    w_r = jnp.concatenate([w_group, w_expert], axis=1)
    w_r = jnp.pad(w_r, ((0, 0), (0, LANES - w_r.shape[1])))
    w_r_hi = w_r.astype(BF16)
    w_r_lo = (w_r - w_r_hi.astype(F32)).astype(BF16)
    tr = _tile(t, 512)
    w_r_cat = jnp.concatenate([w_r_hi, w_r_lo], axis=1)
    h2, ids_rows, gates, counts = _router(x2, g_ffn.reshape(1, d), w_r_cat, tr, N_GROUPS, EXPERTS_PER_GROUP)

    bm = 256
    n_assign = 2 * t
    assert n_assign <= 1 << (2 * RANK_BITS), "in-expert ranks must fit two RANK_BITS-bit halves"
    nb = n_assign // bm + n_exp
    dest, block_expert, first, nxt, pad_lo, pad_n, n_used = _plan(counts, ids_rows, n_exp, bm, nb)
    dest1, dest2 = dest[0], dest[1]

    td = _tile(t, 2048)
    ns = d // 2 // LANES
    xb = _dispatch(h2, dest1, dest2, pad_lo, pad_n, n_used, nb, bm, td, ns)
    yb = _experts(xb, w_gate, w_up, w_down, block_expert, first, nxt, n_used, bm, ns)

    tp = _tile(t, 512)
    return _combine_ple_final(x2, gates, yb, dest1, dest2, g_ple.reshape(1, d), p2d, w_ple_gate, w_ple_proj,
                              g_final.reshape(1, d), tp, d, ns)


def kernel(x, p, w_in, b_forget, w_branch_fox, w_branch_sb, w_mix_out, g_mix, g_ffn, w_group, w_expert,
           w_gate, w_up, w_down, g_ple, w_ple_proj, w_ple_gate, g_final):
    b, s, d = x.shape
    depth = w_in.shape[0]
    assert depth == 1, "the final norm is fused into the single layer"
    x2d = x.reshape(b * s, d)
    out = _layer(x2d, p[0].reshape(b * s, -1), b, s, w_in[0], b_forget[0], w_branch_fox[0], w_branch_sb[0],
                 w_mix_out[0], g_mix[0], g_ffn[0], w_group[0], w_expert[0], w_gate[0], w_up[0], w_down[0],
                 g_ple[0], w_ple_proj[0], w_ple_gate[0], g_final)
    return out.reshape(b, s, d)
```
